```python
import math
import jax, jax.numpy as jnp
from jax import lax
import numpy as np

D_MODEL = 1024
BATCH = 4
SEQ = 8192
DEPTH = 2

CHUNK = 64
N_META = 16
N_A_LAYERS = DEPTH // 2
N_B_LAYERS = DEPTH - N_A_LAYERS
CONV_W = 3
HEAD_DIM = 64
N_HEADS = D_MODEL // (2 * HEAD_DIM)
ROT_DIM = HEAD_DIM // 4
ROPE_THETA = 500000.0
Q_BLOCK = 128
N_GROUPS = 4
EXPERTS_PER_GROUP = 8
N_EXPERTS = N_GROUPS * EXPERTS_PER_GROUP
TOP_K = 2
D_EXPERT = 512
ROW_BLOCK = 128
EPS = 1e-6

kernel_name = "yoco_shortconv_diffattn_hiermoe"


def rmsnorm(x, g):
    xf = x.astype(jnp.float32)
    y = xf * lax.rsqrt(jnp.mean(xf * xf, axis=-1, keepdims=True) + EPS) * g.astype(jnp.float32)
    return y.astype(x.dtype)


def partial_rotary(x, pos):
    half = ROT_DIM // 2
    inv_freq = ROPE_THETA ** (-jnp.arange(half, dtype=jnp.float32) * 2.0 / ROT_DIM)
    ang = pos.astype(jnp.float32)[:, None] * inv_freq[None, :]
    ang = ang.reshape((1, ang.shape[0]) + (1,) * (x.ndim - 3) + (half,))
    cos, sin = jnp.cos(ang), jnp.sin(ang)
    xf = x.astype(jnp.float32)
    x1, x2, rest = xf[..., :half], xf[..., half:ROT_DIM], xf[..., ROT_DIM:]
    out = jnp.concatenate([x1 * cos - x2 * sin, x2 * cos + x1 * sin, rest], axis=-1)
    return out.astype(x.dtype)


def short_conv_mixer(hn, w_in, conv_w, w_out):
    d = hn.shape[-1]
    b_gate, c_gate, u = jnp.split(hn @ w_in, 3, axis=-1)
    z = lax.conv_general_dilated(
        c_gate * u, conv_w[:, None, :].astype(hn.dtype), window_strides=(1,),
        padding=[(CONV_W - 1, 0)], dimension_numbers=("NWC", "WIO", "NWC"),
        feature_group_count=d)
    return (b_gate * z) @ w_out


def grouped_experts(xf, expert, gate, w_gate, w_up, w_down):
    t, d = xf.shape
    a = t * TOP_K
    n_blk = a // ROW_BLOCK + N_EXPERTS
    rows = n_blk * ROW_BLOCK
    e_flat = expert.reshape(-1)
    order = jnp.argsort(e_flat)
    e_sorted = e_flat[order]
    counts = jnp.bincount(e_flat, length=N_EXPERTS)
    padded = (counts + ROW_BLOCK - 1) // ROW_BLOCK * ROW_BLOCK
    ends = jnp.cumsum(padded)
    pad_start = ends - padded
    start = jnp.cumsum(counts) - counts
    dest = pad_start[e_sorted] + (jnp.arange(a) - start[e_sorted])
    src_tok = order // TOP_K
    buf = jnp.zeros((rows, d), xf.dtype).at[dest].set(xf[src_tok])
    row_w = jnp.zeros((rows,), jnp.float32).at[dest].set(gate.reshape(-1)[order])
    row_tok = jnp.zeros((rows,), jnp.int32).at[dest].set(src_tok.astype(jnp.int32))
    blk_expert = jnp.minimum(
        jnp.searchsorted(ends, jnp.arange(n_blk) * ROW_BLOCK, side="right"), N_EXPERTS - 1)

    def run_block(args):
        xb, e = args
        return (jax.nn.silu(xb @ w_gate[e]) * (xb @ w_up[e])) @ w_down[e]

    yb = lax.map(run_block, (buf.reshape(n_blk, ROW_BLOCK, d), blk_expert))
    y = yb.reshape(rows, d) * row_w[:, None].astype(xf.dtype)
    return jax.ops.segment_sum(y, row_tok, num_segments=t)


def hier_moe(hn, w_rg, b_rg, w_re, b_re, w_gate, w_up, w_down):
    shp = hn.shape
    xf = hn.reshape(-1, shp[-1])
    t = xf.shape[0]
    lg = (xf @ w_rg).astype(jnp.float32) + b_rg.astype(jnp.float32)
    grp = jnp.argmax(lg, axis=-1)
    p_grp = jnp.take_along_axis(jax.nn.softmax(lg, axis=-1), grp[:, None], axis=-1)
    le = ((xf @ w_re).astype(jnp.float32) + b_re.astype(jnp.float32)).reshape(
        t, N_GROUPS, EXPERTS_PER_GROUP)
    le = jnp.take_along_axis(le, grp[:, None, None], axis=1)[:, 0]
    top_v, top_i = lax.top_k(le, TOP_K)
    gate = p_grp * jax.nn.softmax(top_v, axis=-1)
    expert = grp[:, None].astype(jnp.int32) * EXPERTS_PER_GROUP + top_i.astype(jnp.int32)
    return grouped_experts(xf, expert, gate, w_gate, w_up, w_down).reshape(shp)


def lambda_init(layer_idx):
    return 0.8 - 0.6 * math.exp(-0.3 * layer_idx)


def diff_attention(hn, w_q, lam_p, subln_g, w_out, kk, v, key_chunk, lam_init):
    bsz, seq, _ = hn.shape
    n_blk = seq // Q_BLOCK
    q = (hn @ w_q).reshape(bsz, seq, N_HEADS, 2, HEAD_DIM)
    q = partial_rotary(q, N_META + jnp.arange(seq)) * (HEAD_DIM ** -0.5)
    q = q.reshape(bsz, n_blk, Q_BLOCK, N_HEADS, 2, HEAD_DIM).transpose(1, 4, 0, 3, 2, 5)
    q_chunk = (1 + jnp.arange(seq) // CHUNK).reshape(n_blk, Q_BLOCK)
    lp = lam_p.astype(jnp.float32)
    lam = jnp.exp(jnp.sum(lp[0] * lp[1])) - jnp.exp(jnp.sum(lp[2] * lp[3])) + lam_init

    def attend(args):
        qb, qc = args
        visible = key_chunk[None, :] <= qc[:, None]
        s = jnp.einsum("nbhqd,nbhkd->nbhqk", qb, kk, preferred_element_type=jnp.float32)
        p = jax.nn.softmax(jnp.where(visible, s, jnp.finfo(jnp.float32).min), axis=-1)
        a = p[0] - lam * p[1]
        return jnp.einsum("bhqk,bhkd->bhqd", a.astype(v.dtype), v)

    o = lax.map(attend, (q, q_chunk))
    o = o.transpose(1, 0, 3, 2, 4).reshape(bsz, seq, N_HEADS, 2 * HEAD_DIM)
    o = rmsnorm(o, subln_g) * (1.0 - lam_init)
    return o.reshape(bsz, seq, N_HEADS * 2 * HEAD_DIM) @ w_out


def setup_inputs(seed: int = 0) -> dict:
    key = jax.random.key(seed)
    ks = jax.random.split(key, 24)
    d, f, e = D_MODEL, D_EXPERT, N_EXPERTS
    nrm = lambda k, shp, s: jax.random.normal(k, shp, jnp.float32) * s
    gain = lambda k, shp: 1.0 + 0.02 * jax.random.normal(k, shp, jnp.float32)
    return {
        "x": jax.random.normal(ks[0], (BATCH, SEQ, d), jnp.float32),
        "meta_tokens": nrm(ks[1], (N_META, d), 1.0),
        "a_norm": gain(ks[2], (N_A_LAYERS, d)),
        "a_w_in": nrm(ks[3], (N_A_LAYERS, d, 3 * d), d ** -0.5),
        "a_conv": nrm(ks[4], (N_A_LAYERS, CONV_W, d), CONV_W ** -0.5),
        "a_w_out": nrm(ks[5], (N_A_LAYERS, d, d), d ** -0.5),
        "kv_norm": gain(ks[6], (d,)),
        "w_kv": nrm(ks[7], (d, 2 * d), d ** -0.5),
        "b_norm": gain(ks[8], (N_B_LAYERS, d)),
        "b_w_q": nrm(ks[9], (N_B_LAYERS, d, d), d ** -0.5),
        "b_lambda": nrm(ks[10], (N_B_LAYERS, 4, HEAD_DIM), 0.1),
        "b_subln": gain(ks[11], (N_B_LAYERS, 2 * HEAD_DIM)),
        "b_w_out": nrm(ks[12], (N_B_LAYERS, d, d), d ** -0.5),
        "ffn_norm": gain(ks[13], (DEPTH, d)),
        "r_group": nrm(ks[14], (DEPTH, d, N_GROUPS), d ** -0.5),
        "r_group_b": nrm(ks[15], (DEPTH, N_GROUPS), 0.01),
        "r_expert": nrm(ks[16], (DEPTH, d, e), d ** -0.5),
        "r_expert_b": nrm(ks[17], (DEPTH, e), 0.01),
        "e_gate": nrm(ks[18], (DEPTH, e, d, f), d ** -0.5),
        "e_up": nrm(ks[19], (DEPTH, e, d, f), d ** -0.5),
        "e_down": nrm(ks[20], (DEPTH, e, f, d), f ** -0.5),
        "final_norm": gain(ks[21], (d,)),
    }


def reference(x, meta_tokens, a_norm, a_w_in, a_conv, a_w_out, kv_norm, w_kv, b_norm, b_w_q,
              b_lambda, b_subln, b_w_out, ffn_norm, r_group, r_group_b, r_expert, r_expert_b,
              e_gate, e_up, e_down, final_norm):
    bsz, seq, d = x.shape
    length = seq + N_META
    meta = jnp.broadcast_to(meta_tokens.astype(x.dtype)[None], (bsz, N_META, d))
    h = jnp.concatenate([meta, x], axis=1)
    layer = 0
    for i in range(N_A_LAYERS):
        h = h + short_conv_mixer(rmsnorm(h, a_norm[i]), a_w_in[i], a_conv[i], a_w_out[i])
        h = h + hier_moe(rmsnorm(h, ffn_norm[layer]), r_group[layer], r_group_b[layer],
                         r_expert[layer], r_expert_b[layer], e_gate[layer], e_up[layer],
                         e_down[layer])
        layer += 1
    kv = rmsnorm(h, kv_norm) @ w_kv
    k = kv[..., :d].reshape(bsz, length, N_HEADS, 2, HEAD_DIM)
    k = partial_rotary(k, jnp.arange(length)).transpose(3, 0, 2, 1, 4)
    v = kv[..., d:].reshape(bsz, length, N_HEADS, 2 * HEAD_DIM).transpose(0, 2, 1, 3)
    pos = jnp.arange(length)
    key_chunk = jnp.where(pos < N_META, 0, 1 + (pos - N_META) // CHUNK)
    h = h[:, N_META:]
    for j in range(N_B_LAYERS):
        h = h + diff_attention(rmsnorm(h, b_norm[j]), b_w_q[j], b_lambda[j], b_subln[j],
                               b_w_out[j], k, v, key_chunk, lambda_init(layer))
        h = h + hier_moe(rmsnorm(h, ffn_norm[layer]), r_group[layer], r_group_b[layer],
                         r_expert[layer], r_expert_b[layer], e_gate[layer], e_up[layer],
                         e_down[layer])
        layer += 1
    return rmsnorm(h, final_norm)
```

```python
import functools
import math

import jax
import jax.numpy as jnp
from jax import lax
from jax.experimental import pallas as pl
from jax.experimental.pallas import tpu as pltpu

D_MODEL = 1024
CHUNK = 64
N_META = 16
HEAD_DIM = 64
N_HEADS = 8
ROT_HALF = 8
ROPE_THETA = 500000.0
N_GROUPS = 4
EXPERTS_PER_GROUP = 8
N_EXPERTS = 32
D_EXPERT = 512
EPS = 1e-6

LANES = 128
SUBLANES = 8
ROW_TILES = D_MODEL // LANES
PAD_ROWS = LANES - N_META
FRONT = PAD_ROWS + N_META
TILE_A = 640
TILE_B = 512
TILE_Q = 128
EXPERT_ROWS = 256
ATT_TQ = 256
ATT_TK = 256
VMEM_LIMIT = 56 * 1024 * 1024
NEG = float(jnp.finfo(jnp.float32).min)

BF16 = jnp.bfloat16
F32 = jnp.float32


def _rms(x, g):
    return x * lax.rsqrt(jnp.mean(x * x, axis=-1, keepdims=True) + EPS) * g


def _cparams(sem):
    return pltpu.CompilerParams(dimension_semantics=sem, vmem_limit_bytes=VMEM_LIMIT)


def _mixer_kernel(h_ref, g_ref, win_ref, cw_ref, wout_ref, o_ref, cu_ref, *, tiles_per_batch):
    i = pl.program_id(0)
    tm = h_ref.shape[0]
    x = h_ref[...]
    hn = _rms(x, g_ref[...]).astype(BF16)
    proj = jnp.dot(hn, win_ref[...], preferred_element_type=F32)
    b_gate = proj[:, :D_MODEL]
    cu = proj[:, D_MODEL:2 * D_MODEL] * proj[:, 2 * D_MODEL:]

    @pl.when(i % tiles_per_batch == 0)
    def _():
        cu_ref[0:SUBLANES, :] = jnp.zeros((SUBLANES, D_MODEL), F32)

    cu_ref[SUBLANES:SUBLANES + tm, :] = cu
    cw = cw_ref[...]
    z = (cw[2:3] * cu + cw[1:2] * cu_ref[SUBLANES - 1:SUBLANES - 1 + tm, :]
         + cw[0:1] * cu_ref[SUBLANES - 2:SUBLANES - 2 + tm, :])
    cu_ref[0:SUBLANES, :] = cu_ref[tm:tm + SUBLANES, :]
    y = jnp.dot((b_gate * z).astype(BF16), wout_ref[...], preferred_element_type=F32)
    o_ref[...] = x + y


def _mixer(h, g, w_in, conv_w, w_out, tiles_per_batch):
    t = h.shape[0]
    return pl.pallas_call(
        functools.partial(_mixer_kernel, tiles_per_batch=tiles_per_batch),
        grid=(t // TILE_A,),
        in_specs=[
            pl.BlockSpec((TILE_A, D_MODEL), lambda i: (i, 0)),
            pl.BlockSpec((1, D_MODEL), lambda i: (0, 0)),
            pl.BlockSpec((D_MODEL, 3 * D_MODEL), lambda i: (0, 0)),
            pl.BlockSpec((3, D_MODEL), lambda i: (0, 0)),
            pl.BlockSpec((D_MODEL, D_MODEL), lambda i: (0, 0)),
        ],
        out_specs=pl.BlockSpec((TILE_A, D_MODEL), lambda i: (i, 0)),
        out_shape=jax.ShapeDtypeStruct((t, D_MODEL), F32),
        scratch_shapes=[pltpu.VMEM((TILE_A + 2 * SUBLANES, D_MODEL), F32)],
        compiler_params=_cparams(("arbitrary",)),
        name="mixer_a",
    )(h, g.reshape(1, D_MODEL), w_in.astype(BF16), conv_w, w_out.astype(BF16))


ROUTER_ROWS = SUBLANES + N_EXPERTS


def _router_kernel(h_ref, g_ref, wrt_ref, br_ref, tri_ref, xn_ref, idx_ref, gate_ref, cnt_ref, cnt_sc):
    i = pl.program_id(0)

    @pl.when(i == 0)
    def _():
        cnt_sc[...] = jnp.zeros_like(cnt_sc)

    xn = _rms(h_ref[...], g_ref[...])
    xn_ref[...] = xn
    lt = lax.dot_general(wrt_ref[...], xn, (((1,), (1,)), ((), ())),
                         precision=lax.Precision.HIGHEST, preferred_element_type=F32) + br_ref[...]
    lg = lt[0:N_GROUPS]
    gmax = jnp.max(lg, axis=0, keepdims=True)
    iota_g = lax.broadcasted_iota(jnp.int32, lg.shape, 0)
    grp = jnp.min(jnp.where(lg == gmax, iota_g, N_GROUPS), axis=0, keepdims=True)
    p_grp = 1.0 / jnp.sum(jnp.exp(lg - gmax), axis=0, keepdims=True)
    le = lt[SUBLANES:SUBLANES + EXPERTS_PER_GROUP]
    for g in range(1, N_GROUPS):
        lo = SUBLANES + g * EXPERTS_PER_GROUP
        le = jnp.where(grp == g, lt[lo:lo + EXPERTS_PER_GROUP], le)
    iota_e = lax.broadcasted_iota(jnp.int32, le.shape, 0)
    v1 = jnp.max(le, axis=0, keepdims=True)
    i1 = jnp.min(jnp.where(le == v1, iota_e, EXPERTS_PER_GROUP), axis=0, keepdims=True)
    le2 = jnp.where(iota_e == i1, -jnp.inf, le)
    v2 = jnp.max(le2, axis=0, keepdims=True)
    i2 = jnp.min(jnp.where(le2 == v2, iota_e, EXPERTS_PER_GROUP), axis=0, keepdims=True)
    e2 = jnp.exp(v2 - v1)
    den = 1.0 + e2
    gate_ref[...] = jnp.concatenate([p_grp * (1.0 / den), p_grp * (e2 / den)], axis=0)
    e_a = grp * EXPERTS_PER_GROUP + i1
    e_b = grp * EXPERTS_PER_GROUP + i2

    iota_x = lax.broadcasted_iota(jnp.int32, (N_EXPERTS, xn.shape[0]), 0)
    cnt = cnt_sc[...]
    ranks = []
    for e_k in (e_a, e_b):
        hit = iota_x == e_k
        excl = jnp.dot(jnp.where(hit, 1.0, 0.0).astype(BF16), tri_ref[...], preferred_element_type=F32)
        ranks.append(jnp.sum(jnp.where(hit, excl + cnt, 0.0), axis=0, keepdims=True))
        cnt = cnt + jnp.sum(jnp.where(hit, 1.0, 0.0), axis=1, keepdims=True)
    cnt_sc[...] = cnt
    cnt_ref[...] = jnp.broadcast_to(cnt, cnt_ref.shape)
    idx_ref[...] = jnp.concatenate([e_a, e_b, ranks[0].astype(jnp.int32), ranks[1].astype(jnp.int32)], axis=0)


def _router(h, g, w_rg, b_rg, w_re, b_re, tm):
    t = h.shape[0]
    wrt = jnp.zeros((ROUTER_ROWS, D_MODEL), F32)
    wrt = wrt.at[0:N_GROUPS].set(w_rg.T).at[SUBLANES:].set(w_re.T)
    br = jnp.zeros((ROUTER_ROWS, 1), F32)
    br = br.at[0:N_GROUPS, 0].set(b_rg).at[SUBLANES:, 0].set(b_re)
    tri = (lax.broadcasted_iota(jnp.int32, (tm, tm), 0) < lax.broadcasted_iota(jnp.int32, (tm, tm), 1)).astype(BF16)
    return pl.pallas_call(
        _router_kernel,
        grid=(t // tm,),
        in_specs=[
            pl.BlockSpec((tm, D_MODEL), lambda i: (i, 0)),
            pl.BlockSpec((1, D_MODEL), lambda i: (0, 0)),
            pl.BlockSpec((ROUTER_ROWS, D_MODEL), lambda i: (0, 0)),
            pl.BlockSpec((ROUTER_ROWS, 1), lambda i: (0, 0)),
            pl.BlockSpec((tm, tm), lambda i: (0, 0)),
        ],
        out_specs=[
            pl.BlockSpec((tm, D_MODEL), lambda i: (i, 0)),
            pl.BlockSpec((4, tm), lambda i: (0, i)),
            pl.BlockSpec((2, tm), lambda i: (0, i)),
            pl.BlockSpec((N_EXPERTS, LANES), lambda i: (0, 0)),
        ],
        out_shape=[
            jax.ShapeDtypeStruct((t, D_MODEL), F32),
            jax.ShapeDtypeStruct((4, t), jnp.int32),
            jax.ShapeDtypeStruct((2, t), F32),
            jax.ShapeDtypeStruct((N_EXPERTS, LANES), F32),
        ],
        scratch_shapes=[pltpu.VMEM((N_EXPERTS, 1), F32)],
        compiler_params=_cparams(("arbitrary",)),
        name="router",
    )(h, g.reshape(1, D_MODEL), wrt, br, tri)


def _row_tile(ref, row):
    return ref.at[pl.ds(pl.multiple_of(row * ROW_TILES, ROW_TILES), ROW_TILES), :]


def _dispatch_kernel(dest_ref, xn_ref, zero_ref, buf_ref, stage, sem):
    del zero_ref
    tm = xn_ref.shape[0]
    for s in range(ROW_TILES):
        stage[pl.ds(s, tm, stride=ROW_TILES), :] = xn_ref[:, s * LANES:(s + 1) * LANES]

    def issue(r, c):
        for k in range(2):
            pltpu.make_async_copy(_row_tile(stage, r), _row_tile(buf_ref, dest_ref[0, k, r]), sem).start()
        return c

    lax.fori_loop(0, tm, issue, 0)

    def drain(r, c):
        for k in range(2):
            pltpu.make_async_copy(_row_tile(stage, r), _row_tile(buf_ref, dest_ref[0, k, r]), sem).wait()
        return c

    lax.fori_loop(0, tm, drain, 0)


def _dispatch(xn, dest, rows, tm):
    t = xn.shape[0]
    zeros = jnp.zeros((rows * ROW_TILES, LANES), F32)
    dest3 = dest.reshape(2, t // tm, tm).transpose(1, 0, 2)
    return pl.pallas_call(
        _dispatch_kernel,
        grid=(t // tm,),
        in_specs=[
            pl.BlockSpec((1, 2, tm), lambda i: (i, 0, 0), memory_space=pltpu.SMEM),
            pl.BlockSpec((tm, D_MODEL), lambda i: (i, 0)),
            pl.BlockSpec(memory_space=pl.ANY),
        ],
        out_specs=pl.BlockSpec(memory_space=pl.ANY),
        out_shape=jax.ShapeDtypeStruct((rows * ROW_TILES, LANES), F32),
        scratch_shapes=[pltpu.VMEM((tm * ROW_TILES, LANES), F32), pltpu.SemaphoreType.DMA],
        input_output_aliases={2: 0},
        compiler_params=_cparams(("arbitrary",)),
        name="dispatch",
    )(dest3, xn, zeros)


def _expert_kernel(be_ref, nu_ref, x8_ref, wg_ref, wu_ref, wd_ref, y8_ref, wgu_sc, wd_sc):
    i = pl.program_id(0)
    rb = x8_ref.shape[0] // ROW_TILES
    e = be_ref[i]
    prev = be_ref[jnp.maximum(i - 1, 0)]

    @pl.when(jnp.logical_or(i == 0, e != prev))
    def _():
        wgu_sc[:, :D_EXPERT] = wg_ref[0].astype(BF16)
        wgu_sc[:, D_EXPERT:] = wu_ref[0].astype(BF16)
        wd_sc[...] = wd_ref[0].astype(BF16)

    @pl.when(i < nu_ref[0])
    def _():
        x = jnp.concatenate([x8_ref[pl.ds(s, rb, stride=ROW_TILES), :] for s in range(ROW_TILES)], axis=1)
        gu = jnp.dot(x.astype(BF16), wgu_sc[...], preferred_element_type=F32)
        a = jax.nn.silu(gu[:, :D_EXPERT]) * gu[:, D_EXPERT:]
        y = jnp.dot(a.astype(BF16), wd_sc[...], preferred_element_type=F32)
        for s in range(ROW_TILES):
            y8_ref[pl.ds(s, rb, stride=ROW_TILES), :] = y[:, s * LANES:(s + 1) * LANES]

    @pl.when(i >= nu_ref[0])
    def _():
        y8_ref[...] = jnp.zeros_like(y8_ref)


def _experts(buf8, blk_expert, n_used, w_gate, w_up, w_down):
    n_blk = blk_expert.shape[0]
    blk_rows = EXPERT_ROWS * ROW_TILES

    def xmap(i, be, nu):
        return (jnp.minimum(i, nu[0] - 1), 0)

    grid_spec = pltpu.PrefetchScalarGridSpec(
        num_scalar_prefetch=2,
        grid=(n_blk,),
        in_specs=[
            pl.BlockSpec((blk_rows, LANES), xmap),
            pl.BlockSpec((1, D_MODEL, D_EXPERT), lambda i, be, nu: (be[i], 0, 0)),
            pl.BlockSpec((1, D_MODEL, D_EXPERT), lambda i, be, nu: (be[i], 0, 0)),
            pl.BlockSpec((1, D_EXPERT, D_MODEL), lambda i, be, nu: (be[i], 0, 0)),
        ],
        out_specs=pl.BlockSpec((blk_rows, LANES), lambda i, be, nu: (i, 0)),
        scratch_shapes=[pltpu.VMEM((D_MODEL, 2 * D_EXPERT), BF16), pltpu.VMEM((D_EXPERT, D_MODEL), BF16)],
    )
    return pl.pallas_call(
        _expert_kernel,
        grid_spec=grid_spec,
        out_shape=jax.ShapeDtypeStruct(buf8.shape, F32),
        compiler_params=_cparams(("arbitrary",)),
        name="experts",
    )(blk_expert, n_used, buf8, w_gate, w_up, w_down)


def _combine_kernel(pos_ref, h_ref, gate_ref, y8_ref, fn_ref, o_ref, ga, gb, sem, *, final_norm):
    tm = h_ref.shape[0]
    bufs = (ga, gb)

    def issue(r, c):
        for k in range(2):
            pltpu.make_async_copy(_row_tile(y8_ref, pos_ref[0, k, r]), _row_tile(bufs[k], r), sem).start()
        return c

    lax.fori_loop(0, tm, issue, 0)

    def drain(r, c):
        for k in range(2):
            pltpu.make_async_copy(_row_tile(y8_ref, pos_ref[0, k, r]), _row_tile(bufs[k], r), sem).wait()
        return c

    lax.fori_loop(0, tm, drain, 0)
    g = gate_ref[...]
    ya = jnp.concatenate([ga[pl.ds(s, tm, stride=ROW_TILES), :] for s in range(ROW_TILES)], axis=1)
    yb = jnp.concatenate([gb[pl.ds(s, tm, stride=ROW_TILES), :] for s in range(ROW_TILES)], axis=1)
    out = h_ref[...] + (g[:, 0:1] * ya + g[:, 1:2] * yb)
    if final_norm:
        out = _rms(out, fn_ref[...])
    o_ref[...] = out


def _combine(h, y8, pos, gate, fn, tm, final_norm):
    t = h.shape[0]
    pos3 = pos.reshape(2, t // tm, tm).transpose(1, 0, 2)
    return pl.pallas_call(
        functools.partial(_combine_kernel, final_norm=final_norm),
        grid=(t // tm,),
        in_specs=[
            pl.BlockSpec((1, 2, tm), lambda i: (i, 0, 0), memory_space=pltpu.SMEM),
            pl.BlockSpec((tm, D_MODEL), lambda i: (i, 0)),
            pl.BlockSpec((tm, 2), lambda i: (i, 0)),
            pl.BlockSpec(memory_space=pl.ANY),
            pl.BlockSpec((1, D_MODEL), lambda i: (0, 0)),
        ],
        out_specs=pl.BlockSpec((tm, D_MODEL), lambda i: (i, 0)),
        out_shape=jax.ShapeDtypeStruct((t, D_MODEL), F32),
        scratch_shapes=[pltpu.VMEM((tm * ROW_TILES, LANES), F32), pltpu.VMEM((tm * ROW_TILES, LANES), F32),
                        pltpu.SemaphoreType.DMA],
        compiler_params=_cparams(("arbitrary",)),
        name="combine",
    )(pos3, h, gate.T, y8, fn.reshape(1, D_MODEL))


def _moe(h, g, w_rg, b_rg, w_re, b_re, w_gate, w_up, w_down, fn, tm, final_norm):
    t = h.shape[0]
    xn, idx, gate, cnt = _router(h, g, w_rg, b_rg, w_re, b_re, tm)
    n_blk = 2 * t // EXPERT_ROWS + N_EXPERTS
    counts = cnt[:, 0].astype(jnp.int32)
    padded = (counts + EXPERT_ROWS - 1) // EXPERT_ROWS * EXPERT_ROWS
    ends = jnp.cumsum(padded)
    pad_start = ends - padded
    eid, rank = idx[0:2], idx[2:4]
    sel = eid[:, :, None] == jnp.arange(N_EXPERTS, dtype=jnp.int32)
    dest = jnp.sum(jnp.where(sel, pad_start, 0), axis=-1) + rank
    blk_start = jnp.arange(n_blk, dtype=jnp.int32) * EXPERT_ROWS
    blk_expert = jnp.minimum(jnp.sum(blk_start[:, None] >= ends[None, :], axis=1), N_EXPERTS - 1).astype(jnp.int32)
    n_used = (ends[-1:] // EXPERT_ROWS).astype(jnp.int32)
    buf8 = _dispatch(xn, dest, n_blk * EXPERT_ROWS, tm)
    y8 = _experts(buf8, blk_expert, n_used, w_gate, w_up, w_down)
    return _combine(h, y8, dest, gate, fn, tm, final_norm)


def _kv_kernel(h_ref, g_ref, wk_ref, wvt_ref, cos_ref, sa_ref, sb_ref, k_ref, vt_ref):
    hn = _rms(h_ref[...], g_ref[...]).astype(BF16)
    k = jnp.dot(hn, wk_ref[...], preferred_element_type=F32)
    cos, sa, sb = cos_ref[...], sa_ref[...], sb_ref[...]
    for c in range(D_MODEL // LANES):
        kc = k[:, c * LANES:(c + 1) * LANES]
        rot = kc * cos + pltpu.roll(kc, LANES - ROT_HALF, 1) * sa + pltpu.roll(kc, ROT_HALF, 1) * sb
        k_ref[0, :, c * LANES:(c + 1) * LANES] = rot.astype(BF16)
    vt = lax.dot_general(wvt_ref[...], hn, (((1,), (1,)), ((), ())), preferred_element_type=F32)
    for c in range(vt_ref.shape[1]):
        vt_ref[0, c] = vt[:, c * LANES:(c + 1) * LANES].astype(BF16)


def _kv(h, g, w_kv, bsz, lp):
    tpb = lp // TILE_A
    kblk = TILE_A // LANES
    pos = jnp.maximum(jnp.arange(lp) - PAD_ROWS, 0).astype(F32)
    inv_freq = ROPE_THETA ** (-jnp.arange(ROT_HALF, dtype=F32) * 2.0 / (2 * ROT_HALF))
    ang = pos[:, None] * inv_freq[None, :]
    r = jnp.arange(LANES) % HEAD_DIM
    cos_t = jnp.where(r < 2 * ROT_HALF, jnp.cos(ang)[:, r % ROT_HALF], 1.0)
    sin_t = jnp.sin(ang)[:, r % ROT_HALF]
    sa = jnp.where(r < ROT_HALF, -sin_t, 0.0)
    sb = jnp.where((r >= ROT_HALF) & (r < 2 * ROT_HALF), sin_t, 0.0)
    tab = pl.BlockSpec((TILE_A, LANES), lambda i: (i % tpb, 0))
    return pl.pallas_call(
        _kv_kernel,
        grid=(bsz * tpb,),
        in_specs=[
            pl.BlockSpec((TILE_A, D_MODEL), lambda i: (i, 0)),
            pl.BlockSpec((1, D_MODEL), lambda i: (0, 0)),
            pl.BlockSpec((D_MODEL, D_MODEL), lambda i: (0, 0)),
            pl.BlockSpec((D_MODEL, D_MODEL), lambda i: (0, 0)),
            tab, tab, tab,
        ],
        out_specs=[
            pl.BlockSpec((1, TILE_A, D_MODEL), lambda i: (i // tpb, i % tpb, 0)),
            pl.BlockSpec((1, kblk, D_MODEL, LANES), lambda i: (i // tpb, i % tpb, 0, 0)),
        ],
        out_shape=[
            jax.ShapeDtypeStruct((bsz, lp, D_MODEL), BF16),
            jax.ShapeDtypeStruct((bsz, lp // LANES, D_MODEL, LANES), BF16),
        ],
        compiler_params=_cparams(("arbitrary",)),
        name="kv_proj",
    )(h, g.reshape(1, D_MODEL), w_kv[:, :D_MODEL].astype(BF16), w_kv[:, D_MODEL:].T.astype(BF16), cos_t, sa, sb)


def _q_kernel(h_ref, g_ref, wqt_ref, cos_ref, sin_ref, qt_ref):
    hn = _rms(h_ref[...], g_ref[...]).astype(BF16)
    qt = lax.dot_general(wqt_ref[...], hn, (((1,), (1,)), ((), ())), preferred_element_type=F32)
    cos, sin = cos_ref[...], sin_ref[...]
    scale = HEAD_DIM ** -0.5
    for c in range(D_MODEL // HEAD_DIM):
        lo = c * HEAD_DIM
        x1 = qt[lo:lo + ROT_HALF]
        x2 = qt[lo + ROT_HALF:lo + 2 * ROT_HALF]
        blk = jnp.concatenate([x1 * cos - x2 * sin, x2 * cos + x1 * sin, qt[lo + 2 * ROT_HALF:lo + HEAD_DIM]], axis=0)
        qt_ref[lo:lo + HEAD_DIM, :] = (blk * scale).astype(BF16)


def _real_row_map(tiles_real, tiles_padded):
    front = FRONT // TILE_Q
    return lambda i: ((i // tiles_real) * tiles_padded + front + i % tiles_real, 0)


def _q_proj(h_pad, g, w_q, bsz, seq, lp):
    tiles_real = seq // TILE_Q
    pos = (N_META + jnp.arange(seq)).astype(F32)
    inv_freq = ROPE_THETA ** (-jnp.arange(ROT_HALF, dtype=F32) * 2.0 / (2 * ROT_HALF))
    ang = inv_freq[:, None] * pos[None, :]
    tab = pl.BlockSpec((ROT_HALF, TILE_Q), lambda i: (0, i % tiles_real))
    return pl.pallas_call(
        _q_kernel,
        grid=(bsz * tiles_real,),
        in_specs=[
            pl.BlockSpec((TILE_Q, D_MODEL), _real_row_map(tiles_real, lp // TILE_Q)),
            pl.BlockSpec((1, D_MODEL), lambda i: (0, 0)),
            pl.BlockSpec((D_MODEL, D_MODEL), lambda i: (0, 0)),
            tab, tab,
        ],
        out_specs=pl.BlockSpec((D_MODEL, TILE_Q), lambda i: (0, i)),
        out_shape=jax.ShapeDtypeStruct((D_MODEL, bsz * seq), BF16),
        compiler_params=_cparams(("arbitrary",)),
        name="q_proj",
    )(h_pad, g.reshape(1, D_MODEL), w_q.T.astype(BF16), jnp.cos(ang), jnp.sin(ang))


def _attn_kernel(qt_ref, k_ref, vt_ref, lam_ref, g_ref, o_ref, m_sc, l_sc, acc_sc, *, lam_init):
    i = pl.program_id(2)
    tq = qt_ref.shape[1]
    qt = qt_ref[...]
    zero = jnp.zeros((HEAD_DIM, tq), BF16)
    qs = (jnp.concatenate([qt[0:HEAD_DIM], zero], axis=0), jnp.concatenate([zero, qt[HEAD_DIM:]], axis=0))

    def scores(kt, n):
        return jnp.dot(kt, qs[n], preferred_element_type=F32)

    kt = k_ref[0, 0:FRONT, :]
    vtt = vt_ref[0, 0]
    meta_vis = lax.broadcasted_iota(jnp.int32, (FRONT, tq), 0) >= PAD_ROWS
    for n in range(2):
        s = jnp.where(meta_vis, scores(kt, n), NEG)
        m = jnp.max(s, axis=0, keepdims=True)
        p = jnp.exp(s - m)
        m_sc[n] = m
        l_sc[n] = jnp.sum(p, axis=0, keepdims=True)
        acc_sc[n] = jnp.dot(vtt, p.astype(BF16), preferred_element_type=F32)

    kblk = ATT_TK // LANES
    diag_vis = (lax.broadcasted_iota(jnp.int32, (ATT_TK, tq), 0) // CHUNK
                <= lax.broadcasted_iota(jnp.int32, (ATT_TK, tq), 1) // CHUNK)

    def step(j, masked):
        kt = k_ref[0, pl.ds(pl.multiple_of(FRONT + j * ATT_TK, LANES), ATT_TK), :]
        vtt = jnp.concatenate([vt_ref[0, 1 + j * kblk + c] for c in range(kblk)], axis=1)
        for n in range(2):
            s = scores(kt, n)
            if masked:
                s = jnp.where(diag_vis, s, NEG)
            m_old = m_sc[n]
            m_new = jnp.maximum(m_old, jnp.max(s, axis=0, keepdims=True))
            alpha = jnp.exp(m_old - m_new)
            p = jnp.exp(s - m_new)
            l_sc[n] = alpha * l_sc[n] + jnp.sum(p, axis=0, keepdims=True)
            acc_sc[n] = alpha * acc_sc[n] + jnp.dot(vtt, p.astype(BF16), preferred_element_type=F32)
            m_sc[n] = m_new

    def body(j, c):
        step(j, False)
        return c

    lax.fori_loop(0, i, body, 0)
    step(i, True)

    lp = lam_ref[...]
    lam = (jnp.exp(jnp.sum(lp[0:1] * lp[1:2], axis=1, keepdims=True))
           - jnp.exp(jnp.sum(lp[2:3] * lp[3:4], axis=1, keepdims=True)) + lam_init)
    o = acc_sc[0] / l_sc[0] - lam * (acc_sc[1] / l_sc[1])
    o = o * lax.rsqrt(jnp.mean(o * o, axis=0, keepdims=True) + EPS) * g_ref[...] * (1.0 - lam_init)
    o_ref[...] = o.T.astype(BF16)


def _attention(qt, k, vt4, lam_p, subln_g, bsz, seq, lam_init):
    nq = seq // ATT_TQ
    lp = k.shape[1]
    return pl.pallas_call(
        functools.partial(_attn_kernel, lam_init=lam_init),
        grid=(bsz, N_HEADS, nq),
        in_specs=[
            pl.BlockSpec((2 * HEAD_DIM, ATT_TQ), lambda b, h, i: (h, b * nq + i)),
            pl.BlockSpec((1, lp, 2 * HEAD_DIM), lambda b, h, i: (b, 0, h)),
            pl.BlockSpec((1, lp // LANES, 2 * HEAD_DIM, LANES), lambda b, h, i: (b, 0, h, 0)),
            pl.BlockSpec((4, HEAD_DIM), lambda b, h, i: (0, 0)),
            pl.BlockSpec((2 * HEAD_DIM, 1), lambda b, h, i: (0, 0)),
        ],
        out_specs=pl.BlockSpec((ATT_TQ, 2 * HEAD_DIM), lambda b, h, i: (b * nq + i, h)),
        out_shape=jax.ShapeDtypeStruct((bsz * seq, D_MODEL), BF16),
        scratch_shapes=[pltpu.VMEM((2, 1, ATT_TQ), F32), pltpu.VMEM((2, 1, ATT_TQ), F32),
                        pltpu.VMEM((2, 2 * HEAD_DIM, ATT_TQ), F32)],
        compiler_params=_cparams(("arbitrary", "arbitrary", "arbitrary")),
        name="diff_attention",
    )(qt, k, vt4, lam_p, subln_g.reshape(2 * HEAD_DIM, 1))


def _oproj_kernel(h_ref, o_ref, w_ref, out_ref):
    out_ref[...] = h_ref[...] + jnp.dot(o_ref[...], w_ref[...], preferred_element_type=F32)


def _o_proj(h_pad, o, w_out, bsz, seq, lp):
    tiles_real = seq // TILE_Q
    return pl.pallas_call(
        _oproj_kernel,
        grid=(bsz * tiles_real,),
        in_specs=[
            pl.BlockSpec((TILE_Q, D_MODEL), _real_row_map(tiles_real, lp // TILE_Q)),
            pl.BlockSpec((TILE_Q, D_MODEL), lambda i: (i, 0)),
            pl.BlockSpec((D_MODEL, D_MODEL), lambda i: (0, 0)),
        ],
        out_specs=pl.BlockSpec((TILE_Q, D_MODEL), lambda i: (i, 0)),
        out_shape=jax.ShapeDtypeStruct((bsz * seq, D_MODEL), F32),
        compiler_params=_cparams(("arbitrary",)),
        name="o_proj",
    )(h_pad, o, w_out.astype(BF16))


def kernel(x, meta_tokens, a_norm, a_w_in, a_conv, a_w_out, kv_norm, w_kv, b_norm, b_w_q, b_lambda, b_subln, b_w_out, ffn_norm, r_group, r_group_b, r_expert, r_expert_b, e_gate, e_up, e_down, final_norm):
    bsz, seq, d = x.shape
    assert d == D_MODEL and a_norm.shape[0] == 1 and b_norm.shape[0] == 1
    lp = FRONT + seq
    assert lp % TILE_A == 0 and seq % ATT_TQ == 0 and (bsz * seq) % TILE_B == 0
    front = jnp.concatenate([jnp.zeros((PAD_ROWS, d), x.dtype), meta_tokens.astype(x.dtype)], axis=0)
    h = jnp.concatenate([jnp.broadcast_to(front[None], (bsz, FRONT, d)), x], axis=1).reshape(bsz * lp, d)

    h = _mixer(h, a_norm[0], a_w_in[0], a_conv[0], a_w_out[0], lp // TILE_A)
    h = _moe(h, ffn_norm[0], r_group[0], r_group_b[0], r_expert[0], r_expert_b[0],
             e_gate[0], e_up[0], e_down[0], final_norm, TILE_A, False)

    k, vt4 = _kv(h, kv_norm, w_kv, bsz, lp)
    qt = _q_proj(h, b_norm[0], b_w_q[0], bsz, seq, lp)
    lam_init = 0.8 - 0.6 * math.exp(-0.3 * a_norm.shape[0])
    o = _attention(qt, k, vt4, b_lambda[0], b_subln[0], bsz, seq, lam_init)
    h = _o_proj(h, o, b_w_out[0], bsz, seq, lp)
    h = _moe(h, ffn_norm[1], r_group[1], r_group_b[1], r_expert[1], r_expert_b[1],
             e_gate[1], e_up[1], e_down[1], final_norm, TILE_B, True)
    return h.reshape(bsz, seq, d)
```

```python
import functools
import math

import jax
import jax.numpy as jnp
from jax import lax
from jax.experimental import pallas as pl
from jax.experimental.pallas import tpu as pltpu

D_MODEL = 1024
CHUNK = 64
N_META = 16
HEAD_DIM = 64
N_HEADS = 8
ROT_HALF = 8
ROPE_THETA = 500000.0
N_GROUPS = 4
EXPERTS_PER_GROUP = 8
N_EXPERTS = 32
D_EXPERT = 512
EPS = 1e-6

LANES = 128
SUBLANES = 8
ROW_TILES = D_MODEL // LANES
TILE = 512
FRONT = TILE
PAD_ROWS = FRONT - N_META
META_PAD = LANES - N_META
EXPERT_ROWS = 256
ATT_TQ = 512
ATT_TK = 1024
DMA_UNROLL = 8
VMEM_LIMIT = 56 * 1024 * 1024
NEG = float(jnp.finfo(jnp.float32).min)

BF16 = jnp.bfloat16
F32 = jnp.float32


def _rms(x, g):
    return x * lax.rsqrt(jnp.mean(x * x, axis=-1, keepdims=True) + EPS) * g


def _cparams(sem):
    return pltpu.CompilerParams(dimension_semantics=sem, vmem_limit_bytes=VMEM_LIMIT)


def _mixer_kernel(h_ref, g_ref, win_ref, cw_ref, wout_ref, o_ref, cu_ref, *, tiles_per_batch):
    i = pl.program_id(0)
    tm = h_ref.shape[0]
    x = h_ref[...]
    hn = _rms(x, g_ref[...]).astype(BF16)
    proj = jnp.dot(hn, win_ref[...], preferred_element_type=F32)
    b_gate = proj[:, :D_MODEL]
    cu = proj[:, D_MODEL:2 * D_MODEL] * proj[:, 2 * D_MODEL:]

    @pl.when(i % tiles_per_batch == 0)
    def _():
        cu_ref[0:SUBLANES, :] = jnp.zeros((SUBLANES, D_MODEL), F32)

    cu_ref[SUBLANES:SUBLANES + tm, :] = cu
    cw = cw_ref[...]
    z = (cw[2:3] * cu + cw[1:2] * cu_ref[SUBLANES - 1:SUBLANES - 1 + tm, :]
         + cw[0:1] * cu_ref[SUBLANES - 2:SUBLANES - 2 + tm, :])
    cu_ref[0:SUBLANES, :] = cu_ref[tm:tm + SUBLANES, :]
    y = jnp.dot((b_gate * z).astype(BF16), wout_ref[...], preferred_element_type=F32)
    o_ref[...] = x + y


def _mixer(h, g, w_in, conv_w, w_out, tiles_per_batch):
    t = h.shape[0]
    return pl.pallas_call(
        functools.partial(_mixer_kernel, tiles_per_batch=tiles_per_batch),
        grid=(t // TILE,),
        in_specs=[
            pl.BlockSpec((TILE, D_MODEL), lambda i: (i, 0)),
            pl.BlockSpec((1, D_MODEL), lambda i: (0, 0)),
            pl.BlockSpec((D_MODEL, 3 * D_MODEL), lambda i: (0, 0)),
            pl.BlockSpec((3, D_MODEL), lambda i: (0, 0)),
            pl.BlockSpec((D_MODEL, D_MODEL), lambda i: (0, 0)),
        ],
        out_specs=pl.BlockSpec((TILE, D_MODEL), lambda i: (i, 0)),
        out_shape=jax.ShapeDtypeStruct((t, D_MODEL), F32),
        scratch_shapes=[pltpu.VMEM((TILE + 2 * SUBLANES, D_MODEL), F32)],
        compiler_params=_cparams(("arbitrary",)),
        name="mixer_a",
    )(h, g.reshape(1, D_MODEL), w_in.astype(BF16), conv_w, w_out.astype(BF16))


ROUTER_ROWS = SUBLANES + N_EXPERTS


def _router_kernel(h_ref, g_ref, wrt_ref, br_ref, tri_ref, xn_ref, idx_ref, gate_ref, cnt_ref, cnt_sc):
    i = pl.program_id(0)

    @pl.when(i == 0)
    def _():
        cnt_sc[...] = jnp.zeros_like(cnt_sc)

    xn = _rms(h_ref[...], g_ref[...])
    xn_ref[...] = xn
    lt = lax.dot_general(wrt_ref[...], xn, (((1,), (1,)), ((), ())),
                         precision=lax.Precision.HIGHEST, preferred_element_type=F32) + br_ref[...]
    lg = lt[0:N_GROUPS]
    gmax = jnp.max(lg, axis=0, keepdims=True)
    iota_g = lax.broadcasted_iota(jnp.int32, lg.shape, 0)
    grp = jnp.min(jnp.where(lg == gmax, iota_g, N_GROUPS), axis=0, keepdims=True)
    p_grp = 1.0 / jnp.sum(jnp.exp(lg - gmax), axis=0, keepdims=True)
    le = lt[SUBLANES:SUBLANES + EXPERTS_PER_GROUP]
    for g in range(1, N_GROUPS):
        lo = SUBLANES + g * EXPERTS_PER_GROUP
        le = jnp.where(grp == g, lt[lo:lo + EXPERTS_PER_GROUP], le)
    iota_e = lax.broadcasted_iota(jnp.int32, le.shape, 0)
    v1 = jnp.max(le, axis=0, keepdims=True)
    i1 = jnp.min(jnp.where(le == v1, iota_e, EXPERTS_PER_GROUP), axis=0, keepdims=True)
    le2 = jnp.where(iota_e == i1, -jnp.inf, le)
    v2 = jnp.max(le2, axis=0, keepdims=True)
    i2 = jnp.min(jnp.where(le2 == v2, iota_e, EXPERTS_PER_GROUP), axis=0, keepdims=True)
    e2 = jnp.exp(v2 - v1)
    den = 1.0 + e2
    gate_ref[...] = jnp.concatenate([p_grp * (1.0 / den), p_grp * (e2 / den)], axis=0)
    e_a = grp * EXPERTS_PER_GROUP + i1
    e_b = grp * EXPERTS_PER_GROUP + i2

    iota_x = lax.broadcasted_iota(jnp.int32, (N_EXPERTS, xn.shape[0]), 0)
    cnt = cnt_sc[...]
    ranks = []
    for e_k in (e_a, e_b):
        hit = iota_x == e_k
        excl = jnp.dot(jnp.where(hit, 1.0, 0.0).astype(BF16), tri_ref[...], preferred_element_type=F32)
        ranks.append(jnp.sum(jnp.where(hit, excl + cnt, 0.0), axis=0, keepdims=True))
        cnt = cnt + jnp.sum(jnp.where(hit, 1.0, 0.0), axis=1, keepdims=True)
    cnt_sc[...] = cnt
    cnt_ref[...] = jnp.broadcast_to(cnt, cnt_ref.shape)
    idx_ref[...] = jnp.concatenate([e_a, e_b, ranks[0].astype(jnp.int32), ranks[1].astype(jnp.int32)], axis=0)


def _router(h, g, w_rg, b_rg, w_re, b_re, tm):
    t = h.shape[0]
    wrt = jnp.zeros((ROUTER_ROWS, D_MODEL), F32)
    wrt = wrt.at[0:N_GROUPS].set(w_rg.T).at[SUBLANES:].set(w_re.T)
    br = jnp.zeros((ROUTER_ROWS, 1), F32)
    br = br.at[0:N_GROUPS, 0].set(b_rg).at[SUBLANES:, 0].set(b_re)
    tri = (lax.broadcasted_iota(jnp.int32, (tm, tm), 0) < lax.broadcasted_iota(jnp.int32, (tm, tm), 1)).astype(BF16)
    return pl.pallas_call(
        _router_kernel,
        grid=(t // tm,),
        in_specs=[
            pl.BlockSpec((tm, D_MODEL), lambda i: (i, 0)),
            pl.BlockSpec((1, D_MODEL), lambda i: (0, 0)),
            pl.BlockSpec((ROUTER_ROWS, D_MODEL), lambda i: (0, 0)),
            pl.BlockSpec((ROUTER_ROWS, 1), lambda i: (0, 0)),
            pl.BlockSpec((tm, tm), lambda i: (0, 0)),
        ],
        out_specs=[
            pl.BlockSpec((tm, D_MODEL), lambda i: (i, 0)),
            pl.BlockSpec((4, tm), lambda i: (0, i)),
            pl.BlockSpec((2, tm), lambda i: (0, i)),
            pl.BlockSpec((N_EXPERTS, LANES), lambda i: (0, 0)),
        ],
        out_shape=[
            jax.ShapeDtypeStruct((t, D_MODEL), F32),
            jax.ShapeDtypeStruct((4, t), jnp.int32),
            jax.ShapeDtypeStruct((2, t), F32),
            jax.ShapeDtypeStruct((N_EXPERTS, LANES), F32),
        ],
        scratch_shapes=[pltpu.VMEM((N_EXPERTS, 1), F32)],
        compiler_params=_cparams(("arbitrary",)),
        name="router",
    )(h, g.reshape(1, D_MODEL), wrt, br, tri)


def _row_tile(ref, row):
    return ref.at[pl.ds(pl.multiple_of(row * ROW_TILES, ROW_TILES), ROW_TILES), :]


def _for_each_row(n_rows, fn):
    def body(it, c):
        for u in range(DMA_UNROLL):
            fn(it * DMA_UNROLL + u)
        return c

    lax.fori_loop(0, n_rows // DMA_UNROLL, body, 0)


def _dispatch_kernel(dest_ref, xn_ref, zero_ref, buf_ref, stage, sem):
    del zero_ref
    tm = xn_ref.shape[0]
    for s in range(ROW_TILES):
        stage[pl.ds(s, tm, stride=ROW_TILES), :] = xn_ref[:, s * LANES:(s + 1) * LANES]

    def copies(r):
        return [pltpu.make_async_copy(_row_tile(stage, r), _row_tile(buf_ref, dest_ref[0, k, r]), sem) for k in range(2)]

    _for_each_row(tm, lambda r: [c.start() for c in copies(r)])
    _for_each_row(tm, lambda r: [c.wait() for c in copies(r)])


def _dispatch(xn, dest, rows, tm):
    t = xn.shape[0]
    zeros = jnp.zeros((rows * ROW_TILES, LANES), F32)
    dest3 = dest.reshape(2, t // tm, tm).transpose(1, 0, 2)
    return pl.pallas_call(
        _dispatch_kernel,
        grid=(t // tm,),
        in_specs=[
            pl.BlockSpec((1, 2, tm), lambda i: (i, 0, 0), memory_space=pltpu.SMEM),
            pl.BlockSpec((tm, D_MODEL), lambda i: (i, 0)),
            pl.BlockSpec(memory_space=pl.ANY),
        ],
        out_specs=pl.BlockSpec(memory_space=pl.ANY),
        out_shape=jax.ShapeDtypeStruct((rows * ROW_TILES, LANES), F32),
        scratch_shapes=[pltpu.VMEM((tm * ROW_TILES, LANES), F32), pltpu.SemaphoreType.DMA],
        input_output_aliases={2: 0},
        compiler_params=_cparams(("arbitrary",)),
        name="dispatch",
    )(dest3, xn, zeros)


def _expert_kernel(be_ref, nu_ref, x8_ref, wg_ref, wu_ref, wd_ref, y8_ref, wgu_sc, wd_sc):
    i = pl.program_id(0)
    rb = x8_ref.shape[0] // ROW_TILES
    e = be_ref[i]
    prev = be_ref[jnp.maximum(i - 1, 0)]

    @pl.when(jnp.logical_or(i == 0, e != prev))
    def _():
        wgu_sc[:, :D_EXPERT] = wg_ref[0, 0].astype(BF16)
        wgu_sc[:, D_EXPERT:] = wu_ref[0, 0].astype(BF16)
        wd_sc[...] = wd_ref[0, 0].astype(BF16)

    @pl.when(i < nu_ref[0])
    def _():
        x = jnp.concatenate([x8_ref[pl.ds(s, rb, stride=ROW_TILES), :] for s in range(ROW_TILES)], axis=1)
        gu = jnp.dot(x.astype(BF16), wgu_sc[...], preferred_element_type=F32)
        a = jax.nn.silu(gu[:, :D_EXPERT]) * gu[:, D_EXPERT:]
        y = jnp.dot(a.astype(BF16), wd_sc[...], preferred_element_type=F32)
        for s in range(ROW_TILES):
            y8_ref[pl.ds(s, rb, stride=ROW_TILES), :] = y[:, s * LANES:(s + 1) * LANES]

    @pl.when(i >= nu_ref[0])
    def _():
        y8_ref[...] = jnp.zeros_like(y8_ref)


def _experts(buf8, blk_expert, n_used, w_gate, w_up, w_down, layer):
    n_blk = blk_expert.shape[0]
    blk_rows = EXPERT_ROWS * ROW_TILES

    def xmap(i, be, nu):
        return (jnp.minimum(i, nu[0] - 1), 0)

    grid_spec = pltpu.PrefetchScalarGridSpec(
        num_scalar_prefetch=2,
        grid=(n_blk,),
        in_specs=[
            pl.BlockSpec((blk_rows, LANES), xmap),
            pl.BlockSpec((1, 1, D_MODEL, D_EXPERT), lambda i, be, nu: (layer, be[i], 0, 0)),
            pl.BlockSpec((1, 1, D_MODEL, D_EXPERT), lambda i, be, nu: (layer, be[i], 0, 0)),
            pl.BlockSpec((1, 1, D_EXPERT, D_MODEL), lambda i, be, nu: (layer, be[i], 0, 0)),
        ],
        out_specs=pl.BlockSpec((blk_rows, LANES), lambda i, be, nu: (i, 0)),
        scratch_shapes=[pltpu.VMEM((D_MODEL, 2 * D_EXPERT), BF16), pltpu.VMEM((D_EXPERT, D_MODEL), BF16)],
    )
    return pl.pallas_call(
        _expert_kernel,
        grid_spec=grid_spec,
        out_shape=jax.ShapeDtypeStruct(buf8.shape, F32),
        compiler_params=_cparams(("arbitrary",)),
        name="experts",
    )(blk_expert, n_used, buf8, w_gate, w_up, w_down)


def _combine_kernel(pos_ref, h_ref, gate_ref, y8_ref, fn_ref, o_ref, ga, gb, sem, *, final_norm):
    tm = h_ref.shape[0]
    bufs = (ga, gb)

    def copies(r):
        return [pltpu.make_async_copy(_row_tile(y8_ref, pos_ref[0, k, r]), _row_tile(bufs[k], r), sem) for k in range(2)]

    _for_each_row(tm, lambda r: [c.start() for c in copies(r)])
    _for_each_row(tm, lambda r: [c.wait() for c in copies(r)])
    g = gate_ref[...]
    ya = jnp.concatenate([ga[pl.ds(s, tm, stride=ROW_TILES), :] for s in range(ROW_TILES)], axis=1)
    yb = jnp.concatenate([gb[pl.ds(s, tm, stride=ROW_TILES), :] for s in range(ROW_TILES)], axis=1)
    out = h_ref[...] + (g[:, 0:1] * ya + g[:, 1:2] * yb)
    if final_norm:
        out = _rms(out, fn_ref[...])
    o_ref[...] = out


def _combine(h, y8, pos, gate, fn, tm, final_norm):
    t = h.shape[0]
    pos3 = pos.reshape(2, t // tm, tm).transpose(1, 0, 2)
    return pl.pallas_call(
        functools.partial(_combine_kernel, final_norm=final_norm),
        grid=(t // tm,),
        in_specs=[
            pl.BlockSpec((1, 2, tm), lambda i: (i, 0, 0), memory_space=pltpu.SMEM),
            pl.BlockSpec((tm, D_MODEL), lambda i: (i, 0)),
            pl.BlockSpec((tm, 2), lambda i: (i, 0)),
            pl.BlockSpec(memory_space=pl.ANY),
            pl.BlockSpec((1, D_MODEL), lambda i: (0, 0)),
        ],
        out_specs=pl.BlockSpec((tm, D_MODEL), lambda i: (i, 0)),
        out_shape=jax.ShapeDtypeStruct((t, D_MODEL), F32),
        scratch_shapes=[pltpu.VMEM((tm * ROW_TILES, LANES), F32), pltpu.VMEM((tm * ROW_TILES, LANES), F32),
                        pltpu.SemaphoreType.DMA],
        compiler_params=_cparams(("arbitrary",)),
        name="combine",
    )(pos3, h, gate.T, y8, fn.reshape(1, D_MODEL))


def _moe(h, g, w_rg, b_rg, w_re, b_re, w_gate, w_up, w_down, layer, fn, tm, final_norm):
    t = h.shape[0]
    xn, idx, gate, cnt = _router(h, g, w_rg, b_rg, w_re, b_re, tm)
    n_blk = 2 * t // EXPERT_ROWS + N_EXPERTS
    counts = cnt[:, 0].astype(jnp.int32)
    padded = (counts + EXPERT_ROWS - 1) // EXPERT_ROWS * EXPERT_ROWS
    ends = jnp.cumsum(padded)
    pad_start = ends - padded
    eid, rank = idx[0:2], idx[2:4]
    sel = eid[:, :, None] == jnp.arange(N_EXPERTS, dtype=jnp.int32)
    dest = jnp.sum(jnp.where(sel, pad_start, 0), axis=-1) + rank
    blk_start = jnp.arange(n_blk, dtype=jnp.int32) * EXPERT_ROWS
    blk_expert = jnp.minimum(jnp.sum(blk_start[:, None] >= ends[None, :], axis=1), N_EXPERTS - 1).astype(jnp.int32)
    n_used = (ends[-1:] // EXPERT_ROWS).astype(jnp.int32)
    buf8 = _dispatch(xn, dest, n_blk * EXPERT_ROWS, tm)
    y8 = _experts(buf8, blk_expert, n_used, w_gate, w_up, w_down, layer)
    return _combine(h, y8, dest, gate, fn, tm, final_norm)


def _kv_kernel(h_ref, g_ref, wk_ref, wvt_ref, cos_ref, sa_ref, sb_ref, k_ref, vt_ref):
    hn = _rms(h_ref[...], g_ref[...]).astype(BF16)
    k = jnp.dot(hn, wk_ref[...], preferred_element_type=F32)
    cos, sa, sb = cos_ref[...], sa_ref[...], sb_ref[...]
    for c in range(D_MODEL // LANES):
        kc = k[:, c * LANES:(c + 1) * LANES]
        rot = kc * cos + pltpu.roll(kc, LANES - ROT_HALF, 1) * sa + pltpu.roll(kc, ROT_HALF, 1) * sb
        k_ref[0, :, c * LANES:(c + 1) * LANES] = rot.astype(BF16)
    vt = lax.dot_general(wvt_ref[...], hn, (((1,), (1,)), ((), ())), preferred_element_type=F32)
    for c in range(vt_ref.shape[1]):
        vt_ref[0, c] = vt[:, c * LANES:(c + 1) * LANES].astype(BF16)


def _kv(h, g, w_kv, bsz, lp):
    tpb = lp // TILE
    kblk = TILE // LANES
    pos = jnp.maximum(jnp.arange(lp) - PAD_ROWS, 0).astype(F32)
    inv_freq = ROPE_THETA ** (-jnp.arange(ROT_HALF, dtype=F32) * 2.0 / (2 * ROT_HALF))
    ang = pos[:, None] * inv_freq[None, :]
    r = jnp.arange(LANES) % HEAD_DIM
    cos_t = jnp.where(r < 2 * ROT_HALF, jnp.cos(ang)[:, r % ROT_HALF], 1.0)
    sin_t = jnp.sin(ang)[:, r % ROT_HALF]
    sa = jnp.where(r < ROT_HALF, -sin_t, 0.0)
    sb = jnp.where((r >= ROT_HALF) & (r < 2 * ROT_HALF), sin_t, 0.0)
    tab = pl.BlockSpec((TILE, LANES), lambda i: (i % tpb, 0))
    return pl.pallas_call(
        _kv_kernel,
        grid=(bsz * tpb,),
        in_specs=[
            pl.BlockSpec((TILE, D_MODEL), lambda i: (i, 0)),
            pl.BlockSpec((1, D_MODEL), lambda i: (0, 0)),
            pl.BlockSpec((D_MODEL, D_MODEL), lambda i: (0, 0)),
            pl.BlockSpec((D_MODEL, D_MODEL), lambda i: (0, 0)),
            tab, tab, tab,
        ],
        out_specs=[
            pl.BlockSpec((1, TILE, D_MODEL), lambda i: (i // tpb, i % tpb, 0)),
            pl.BlockSpec((1, kblk, D_MODEL, LANES), lambda i: (i // tpb, i % tpb, 0, 0)),
        ],
        out_shape=[
            jax.ShapeDtypeStruct((bsz, lp, D_MODEL), BF16),
            jax.ShapeDtypeStruct((bsz, lp // LANES, D_MODEL, LANES), BF16),
        ],
        compiler_params=_cparams(("arbitrary",)),
        name="kv_proj",
    )(h, g.reshape(1, D_MODEL), w_kv[:, :D_MODEL].astype(BF16), w_kv[:, D_MODEL:].T.astype(BF16), cos_t, sa, sb)


def _q_kernel(h_ref, g_ref, wqt_ref, cos_ref, sin_ref, qt_ref):
    hn = _rms(h_ref[...], g_ref[...]).astype(BF16)
    qt = lax.dot_general(wqt_ref[...], hn, (((1,), (1,)), ((), ())), preferred_element_type=F32)
    cos, sin = cos_ref[...], sin_ref[...]
    scale = HEAD_DIM ** -0.5
    for c in range(D_MODEL // HEAD_DIM):
        lo = c * HEAD_DIM
        x1 = qt[lo:lo + ROT_HALF]
        x2 = qt[lo + ROT_HALF:lo + 2 * ROT_HALF]
        blk = jnp.concatenate([x1 * cos - x2 * sin, x2 * cos + x1 * sin, qt[lo + 2 * ROT_HALF:lo + HEAD_DIM]], axis=0)
        qt_ref[lo:lo + HEAD_DIM, :] = (blk * scale).astype(BF16)


def _real_row_map(tiles_real, tiles_padded):
    front = FRONT // TILE
    return lambda i: ((i // tiles_real) * tiles_padded + front + i % tiles_real, 0)


def _q_proj(h_pad, g, w_q, bsz, seq, lp):
    tiles_real = seq // TILE
    pos = (N_META + jnp.arange(seq)).astype(F32)
    inv_freq = ROPE_THETA ** (-jnp.arange(ROT_HALF, dtype=F32) * 2.0 / (2 * ROT_HALF))
    ang = inv_freq[:, None] * pos[None, :]
    tab = pl.BlockSpec((ROT_HALF, TILE), lambda i: (0, i % tiles_real))
    return pl.pallas_call(
        _q_kernel,
        grid=(bsz * tiles_real,),
        in_specs=[
            pl.BlockSpec((TILE, D_MODEL), _real_row_map(tiles_real, lp // TILE)),
            pl.BlockSpec((1, D_MODEL), lambda i: (0, 0)),
            pl.BlockSpec((D_MODEL, D_MODEL), lambda i: (0, 0)),
            tab, tab,
        ],
        out_specs=pl.BlockSpec((D_MODEL, TILE), lambda i: (0, i)),
        out_shape=jax.ShapeDtypeStruct((D_MODEL, bsz * seq), BF16),
        compiler_params=_cparams(("arbitrary",)),
        name="q_proj",
    )(h_pad, g.reshape(1, D_MODEL), w_q.T.astype(BF16), jnp.cos(ang), jnp.sin(ang))


def _attn_kernel(qt_ref, k_ref, vt_ref, lam_ref, g_ref, o_ref, m_sc, l_sc, acc_sc, *, lam_init):
    i = pl.program_id(2)
    tq = qt_ref.shape[1]
    qt = qt_ref[...]
    zero = jnp.zeros((HEAD_DIM, tq), BF16)
    qs = (jnp.concatenate([qt[0:HEAD_DIM], zero], axis=0), jnp.concatenate([zero, qt[HEAD_DIM:]], axis=0))
    m_sc[...] = jnp.full(m_sc.shape, NEG, F32)
    l_sc[...] = jnp.zeros_like(l_sc)
    acc_sc[...] = jnp.zeros_like(acc_sc)

    def tile(row0, nkeys, vis):
        kt = k_ref[0, pl.ds(row0, nkeys), :]
        blk0 = row0 // LANES
        vtt = jnp.concatenate([vt_ref[0, blk0 + c] for c in range(nkeys // LANES)], axis=1)
        for n in range(2):
            s = jnp.dot(kt, qs[n], preferred_element_type=F32)
            if vis is not None:
                s = jnp.where(vis, s, NEG)
            m_old = m_sc[n]
            m_new = jnp.maximum(m_old, jnp.max(s, axis=0, keepdims=True))
            alpha = jnp.exp(m_old - m_new)
            p = jnp.exp(s - m_new)
            l_sc[n] = alpha * l_sc[n] + jnp.sum(p, axis=0, keepdims=True)
            acc_sc[n] = alpha * acc_sc[n] + jnp.dot(vtt, p.astype(BF16), preferred_element_type=F32)
            m_sc[n] = m_new

    def chunk_vis(nkeys, shift):
        r = lax.broadcasted_iota(jnp.int32, (nkeys, tq), 0) // CHUNK
        c = lax.broadcasted_iota(jnp.int32, (nkeys, tq), 1) // CHUNK
        return r <= c + shift

    tile(FRONT - LANES, LANES, lax.broadcasted_iota(jnp.int32, (LANES, tq), 0) >= META_PAD)

    def body(j, c):
        tile(pl.multiple_of(FRONT + j * ATT_TK, LANES), ATT_TK, None)
        return c

    lax.fori_loop(0, i // 2, body, 0)
    base = pl.multiple_of(FRONT + (i // 2) * ATT_TK, LANES)

    @pl.when(i % 2 == 0)
    def _():
        tile(base, tq, chunk_vis(tq, 0))

    @pl.when(i % 2 == 1)
    def _():
        tile(base, ATT_TK, chunk_vis(ATT_TK, tq // CHUNK))

    lp = lam_ref[...]
    lam = (jnp.exp(jnp.sum(lp[0:1] * lp[1:2], axis=1, keepdims=True))
           - jnp.exp(jnp.sum(lp[2:3] * lp[3:4], axis=1, keepdims=True)) + lam_init)
    o = acc_sc[0] / l_sc[0] - lam * (acc_sc[1] / l_sc[1])
    o = o * lax.rsqrt(jnp.mean(o * o, axis=0, keepdims=True) + EPS) * g_ref[...] * (1.0 - lam_init)
    o_ref[...] = o.T.astype(BF16)


def _attention(qt, k, vt4, lam_p, subln_g, bsz, seq, lam_init):
    nq = seq // ATT_TQ
    lp = k.shape[1]
    return pl.pallas_call(
        functools.partial(_attn_kernel, lam_init=lam_init),
        grid=(bsz, N_HEADS, nq),
        in_specs=[
            pl.BlockSpec((2 * HEAD_DIM, ATT_TQ), lambda b, h, i: (h, b * nq + i)),
            pl.BlockSpec((1, lp, 2 * HEAD_DIM), lambda b, h, i: (b, 0, h)),
            pl.BlockSpec((1, lp // LANES, 2 * HEAD_DIM, LANES), lambda b, h, i: (b, 0, h, 0)),
            pl.BlockSpec((4, HEAD_DIM), lambda b, h, i: (0, 0)),
            pl.BlockSpec((2 * HEAD_DIM, 1), lambda b, h, i: (0, 0)),
        ],
        out_specs=pl.BlockSpec((ATT_TQ, 2 * HEAD_DIM), lambda b, h, i: (b * nq + i, h)),
        out_shape=jax.ShapeDtypeStruct((bsz * seq, D_MODEL), BF16),
        scratch_shapes=[pltpu.VMEM((2, 1, ATT_TQ), F32), pltpu.VMEM((2, 1, ATT_TQ), F32),
                        pltpu.VMEM((2, 2 * HEAD_DIM, ATT_TQ), F32)],
        compiler_params=_cparams(("arbitrary", "arbitrary", "arbitrary")),
        name="diff_attention",
    )(qt, k, vt4, lam_p, subln_g.reshape(2 * HEAD_DIM, 1))


def _oproj_kernel(h_ref, o_ref, w_ref, out_ref):
    out_ref[...] = h_ref[...] + jnp.dot(o_ref[...], w_ref[...], preferred_element_type=F32)


def _o_proj(h_pad, o, w_out, bsz, seq, lp):
    tiles_real = seq // TILE
    return pl.pallas_call(
        _oproj_kernel,
        grid=(bsz * tiles_real,),
        in_specs=[
            pl.BlockSpec((TILE, D_MODEL), _real_row_map(tiles_real, lp // TILE)),
            pl.BlockSpec((TILE, D_MODEL), lambda i: (i, 0)),
            pl.BlockSpec((D_MODEL, D_MODEL), lambda i: (0, 0)),
        ],
        out_specs=pl.BlockSpec((TILE, D_MODEL), lambda i: (i, 0)),
        out_shape=jax.ShapeDtypeStruct((bsz * seq, D_MODEL), F32),
        compiler_params=_cparams(("arbitrary",)),
        name="o_proj",
    )(h_pad, o, w_out.astype(BF16))


def kernel(x, meta_tokens, a_norm, a_w_in, a_conv, a_w_out, kv_norm, w_kv, b_norm, b_w_q, b_lambda, b_subln, b_w_out, ffn_norm, r_group, r_group_b, r_expert, r_expert_b, e_gate, e_up, e_down, final_norm):
    bsz, seq, d = x.shape
    assert d == D_MODEL and a_norm.shape[0] == 1 and b_norm.shape[0] == 1
    lp = FRONT + seq
    assert seq % ATT_TK == 0 and ATT_TK == 2 * ATT_TQ and ATT_TQ % TILE == 0
    front = jnp.concatenate([jnp.zeros((PAD_ROWS, d), x.dtype), meta_tokens.astype(x.dtype)], axis=0)
    h = jnp.concatenate([jnp.broadcast_to(front[None], (bsz, FRONT, d)), x], axis=1).reshape(bsz * lp, d)

    h = _mixer(h, a_norm[0], a_w_in[0], a_conv[0], a_w_out[0], lp // TILE)
    h = _moe(h, ffn_norm[0], r_group[0], r_group_b[0], r_expert[0], r_expert_b[0],
             e_gate, e_up, e_down, 0, final_norm, TILE, False)

    k, vt4 = _kv(h, kv_norm, w_kv, bsz, lp)
    qt = _q_proj(h, b_norm[0], b_w_q[0], bsz, seq, lp)
    lam_init = 0.8 - 0.6 * math.exp(-0.3 * a_norm.shape[0])
    o = _attention(qt, k, vt4, b_lambda[0], b_subln[0], bsz, seq, lam_init)
    h = _o_proj(h, o, b_w_out[0], bsz, seq, lp)
    h = _moe(h, ffn_norm[1], r_group[1], r_group_b[1], r_expert[1], r_expert_b[1],
             e_gate, e_up, e_down, 1, final_norm, TILE, True)
    return h.reshape(bsz, seq, d)
```

```python
import functools
import math

import jax
import jax.numpy as jnp
from jax import lax
from jax.experimental import pallas as pl
from jax.experimental.pallas import tpu as pltpu

D_MODEL = 1024
CHUNK = 64
N_META = 16
HEAD_DIM = 64
N_HEADS = 8
ROT_HALF = 8
ROPE_THETA = 500000.0
N_GROUPS = 4
EXPERTS_PER_GROUP = 8
N_EXPERTS = 32
D_EXPERT = 512
EPS = 1e-6

LANES = 128
SUBLANES = 8
ROW_TILES = D_MODEL // LANES
TILE = 512
FRONT = TILE
PAD_ROWS = FRONT - N_META
META_PAD = LANES - N_META
EXPERT_ROWS = 256
ATT_TQ = 512
ATT_TK = 1024
ATT_SUB = 512
DMA_UNROLL = 8
VMEM_LIMIT = 56 * 1024 * 1024
NEG = float(jnp.finfo(jnp.float32).min)

BF16 = jnp.bfloat16
F32 = jnp.float32


def _rms(x, g):
    return x * lax.rsqrt(jnp.mean(x * x, axis=-1, keepdims=True) + EPS) * g


def _cparams(sem):
    return pltpu.CompilerParams(dimension_semantics=sem, vmem_limit_bytes=VMEM_LIMIT)


def _mixer_kernel(x_ref, front_ref, g_ref, win_ref, cw_ref, wout_ref, o_ref, cu_ref, *, tiles_per_batch):
    i = pl.program_id(0)
    tm = x_ref.shape[0]
    x = jnp.where(i % tiles_per_batch == 0, front_ref[...], x_ref[...])
    hn = _rms(x, g_ref[...]).astype(BF16)
    proj = jnp.dot(hn, win_ref[...], preferred_element_type=F32)
    b_gate = proj[:, :D_MODEL]
    cu = proj[:, D_MODEL:2 * D_MODEL] * proj[:, 2 * D_MODEL:]

    @pl.when(i % tiles_per_batch == 0)
    def _():
        cu_ref[0:SUBLANES, :] = jnp.zeros((SUBLANES, D_MODEL), F32)

    cu_ref[SUBLANES:SUBLANES + tm, :] = cu
    cw = cw_ref[...]
    z = (cw[2:3] * cu + cw[1:2] * cu_ref[SUBLANES - 1:SUBLANES - 1 + tm, :]
         + cw[0:1] * cu_ref[SUBLANES - 2:SUBLANES - 2 + tm, :])
    cu_ref[0:SUBLANES, :] = cu_ref[tm:tm + SUBLANES, :]
    y = jnp.dot((b_gate * z).astype(BF16), wout_ref[...], preferred_element_type=F32)
    o_ref[...] = x + y


def _mixer(x, front, g, w_in, conv_w, w_out, bsz, tiles_per_batch):
    t = bsz * tiles_per_batch * TILE
    tiles_real = tiles_per_batch - FRONT // TILE

    def x_map(i):
        return ((i // tiles_per_batch) * tiles_real + jnp.maximum(i % tiles_per_batch - FRONT // TILE, 0), 0)

    return pl.pallas_call(
        functools.partial(_mixer_kernel, tiles_per_batch=tiles_per_batch),
        grid=(t // TILE,),
        in_specs=[
            pl.BlockSpec((TILE, D_MODEL), x_map),
            pl.BlockSpec((TILE, D_MODEL), lambda i: (0, 0)),
            pl.BlockSpec((1, D_MODEL), lambda i: (0, 0)),
            pl.BlockSpec((D_MODEL, 3 * D_MODEL), lambda i: (0, 0)),
            pl.BlockSpec((3, D_MODEL), lambda i: (0, 0)),
            pl.BlockSpec((D_MODEL, D_MODEL), lambda i: (0, 0)),
        ],
        out_specs=pl.BlockSpec((TILE, D_MODEL), lambda i: (i, 0)),
        out_shape=jax.ShapeDtypeStruct((t, D_MODEL), F32),
        scratch_shapes=[pltpu.VMEM((TILE + 2 * SUBLANES, D_MODEL), F32)],
        compiler_params=_cparams(("arbitrary",)),
        name="mixer_a",
    )(x, front, g.reshape(1, D_MODEL), w_in.astype(BF16), conv_w, w_out.astype(BF16))


ROUTER_ROWS = SUBLANES + N_EXPERTS


def _router_kernel(h_ref, g_ref, wrt_ref, br_ref, tri_ref, xn_ref, idx_ref, gate_ref, cnt_ref, cnt_sc):
    i = pl.program_id(0)

    @pl.when(i == 0)
    def _():
        cnt_sc[...] = jnp.zeros_like(cnt_sc)

    xn = _rms(h_ref[...], g_ref[...])
    xn_ref[...] = xn
    lt = lax.dot_general(wrt_ref[...], xn, (((1,), (1,)), ((), ())),
                         precision=lax.Precision.HIGHEST, preferred_element_type=F32) + br_ref[...]
    lg = lt[0:N_GROUPS]
    gmax = jnp.max(lg, axis=0, keepdims=True)
    iota_g = lax.broadcasted_iota(jnp.int32, lg.shape, 0)
    grp = jnp.min(jnp.where(lg == gmax, iota_g, N_GROUPS), axis=0, keepdims=True)
    p_grp = 1.0 / jnp.sum(jnp.exp(lg - gmax), axis=0, keepdims=True)
    le = lt[SUBLANES:SUBLANES + EXPERTS_PER_GROUP]
    for g in range(1, N_GROUPS):
        lo = SUBLANES + g * EXPERTS_PER_GROUP
        le = jnp.where(grp == g, lt[lo:lo + EXPERTS_PER_GROUP], le)
    iota_e = lax.broadcasted_iota(jnp.int32, le.shape, 0)
    v1 = jnp.max(le, axis=0, keepdims=True)
    i1 = jnp.min(jnp.where(le == v1, iota_e, EXPERTS_PER_GROUP), axis=0, keepdims=True)
    le2 = jnp.where(iota_e == i1, -jnp.inf, le)
    v2 = jnp.max(le2, axis=0, keepdims=True)
    i2 = jnp.min(jnp.where(le2 == v2, iota_e, EXPERTS_PER_GROUP), axis=0, keepdims=True)
    e2 = jnp.exp(v2 - v1)
    den = 1.0 + e2
    gate_ref[...] = jnp.concatenate([p_grp * (1.0 / den), p_grp * (e2 / den)], axis=0)
    e_a = grp * EXPERTS_PER_GROUP + i1
    e_b = grp * EXPERTS_PER_GROUP + i2

    iota_x = lax.broadcasted_iota(jnp.int32, (N_EXPERTS, xn.shape[0]), 0)
    cnt = cnt_sc[...]
    ranks = []
    for e_k in (e_a, e_b):
        hit = iota_x == e_k
        excl = jnp.dot(jnp.where(hit, 1.0, 0.0).astype(BF16), tri_ref[...], preferred_element_type=F32)
        ranks.append(jnp.sum(jnp.where(hit, excl + cnt, 0.0), axis=0, keepdims=True))
        cnt = cnt + jnp.sum(jnp.where(hit, 1.0, 0.0), axis=1, keepdims=True)
    cnt_sc[...] = cnt
    cnt_ref[...] = jnp.broadcast_to(cnt, cnt_ref.shape)
    idx_ref[...] = jnp.concatenate([e_a, e_b, ranks[0].astype(jnp.int32), ranks[1].astype(jnp.int32)], axis=0)


def _router(h, g, w_rg, b_rg, w_re, b_re, tm):
    t = h.shape[0]
    wrt = jnp.zeros((ROUTER_ROWS, D_MODEL), F32)
    wrt = wrt.at[0:N_GROUPS].set(w_rg.T).at[SUBLANES:].set(w_re.T)
    br = jnp.zeros((ROUTER_ROWS, 1), F32)
    br = br.at[0:N_GROUPS, 0].set(b_rg).at[SUBLANES:, 0].set(b_re)
    tri = (lax.broadcasted_iota(jnp.int32, (tm, tm), 0) < lax.broadcasted_iota(jnp.int32, (tm, tm), 1)).astype(BF16)
    return pl.pallas_call(
        _router_kernel,
        grid=(t // tm,),
        in_specs=[
            pl.BlockSpec((tm, D_MODEL), lambda i: (i, 0)),
            pl.BlockSpec((1, D_MODEL), lambda i: (0, 0)),
            pl.BlockSpec((ROUTER_ROWS, D_MODEL), lambda i: (0, 0)),
            pl.BlockSpec((ROUTER_ROWS, 1), lambda i: (0, 0)),
            pl.BlockSpec((tm, tm), lambda i: (0, 0)),
        ],
        out_specs=[
            pl.BlockSpec((tm, D_MODEL), lambda i: (i, 0)),
            pl.BlockSpec((4, tm), lambda i: (0, i)),
            pl.BlockSpec((2, tm), lambda i: (0, i)),
            pl.BlockSpec((N_EXPERTS, LANES), lambda i: (0, 0)),
        ],
        out_shape=[
            jax.ShapeDtypeStruct((t, D_MODEL), F32),
            jax.ShapeDtypeStruct((4, t), jnp.int32),
            jax.ShapeDtypeStruct((2, t), F32),
            jax.ShapeDtypeStruct((N_EXPERTS, LANES), F32),
        ],
        scratch_shapes=[pltpu.VMEM((N_EXPERTS, 1), F32)],
        compiler_params=_cparams(("arbitrary",)),
        name="router",
    )(h, g.reshape(1, D_MODEL), wrt, br, tri)


def _row_tile(ref, row):
    return ref.at[pl.ds(pl.multiple_of(row * ROW_TILES, ROW_TILES), ROW_TILES), :]


def _for_each_row(n_rows, fn):
    def body(it, c):
        for u in range(DMA_UNROLL):
            fn(it * DMA_UNROLL + u)
        return c

    lax.fori_loop(0, n_rows // DMA_UNROLL, body, 0)


def _dispatch_kernel(dest_ref, xn_ref, zero_ref, buf_ref, stage, sem):
    del zero_ref
    tm = xn_ref.shape[0]
    for s in range(ROW_TILES):
        stage[pl.ds(s, tm, stride=ROW_TILES), :] = xn_ref[:, s * LANES:(s + 1) * LANES]

    def copies(r):
        return [pltpu.make_async_copy(_row_tile(stage, r), _row_tile(buf_ref, dest_ref[0, k, r]), sem) for k in range(2)]

    _for_each_row(tm, lambda r: [c.start() for c in copies(r)])
    _for_each_row(tm, lambda r: [c.wait() for c in copies(r)])


def _dispatch(xn, dest, rows, tm):
    t = xn.shape[0]
    zeros = jnp.zeros((rows * ROW_TILES, LANES), F32)
    dest3 = dest.reshape(2, t // tm, tm).transpose(1, 0, 2)
    return pl.pallas_call(
        _dispatch_kernel,
        grid=(t // tm,),
        in_specs=[
            pl.BlockSpec((1, 2, tm), lambda i: (i, 0, 0), memory_space=pltpu.SMEM),
            pl.BlockSpec((tm, D_MODEL), lambda i: (i, 0)),
            pl.BlockSpec(memory_space=pl.ANY),
        ],
        out_specs=pl.BlockSpec(memory_space=pl.ANY),
        out_shape=jax.ShapeDtypeStruct((rows * ROW_TILES, LANES), F32),
        scratch_shapes=[pltpu.VMEM((tm * ROW_TILES, LANES), F32), pltpu.SemaphoreType.DMA],
        input_output_aliases={2: 0},
        compiler_params=_cparams(("arbitrary",)),
        name="dispatch",
    )(dest3, xn, zeros)


def _expert_kernel(be_ref, nu_ref, x8_ref, wg_ref, wu_ref, wd_ref, y8_ref, wgu_sc, wd_sc):
    i = pl.program_id(0)
    rb = x8_ref.shape[0] // ROW_TILES
    e = be_ref[i]
    prev = be_ref[jnp.maximum(i - 1, 0)]

    @pl.when(jnp.logical_or(i == 0, e != prev))
    def _():
        wgu_sc[:, :D_EXPERT] = wg_ref[0, 0].astype(BF16)
        wgu_sc[:, D_EXPERT:] = wu_ref[0, 0].astype(BF16)
        wd_sc[...] = wd_ref[0, 0].astype(BF16)

    @pl.when(i < nu_ref[0])
    def _():
        x = jnp.concatenate([x8_ref[pl.ds(s, rb, stride=ROW_TILES), :] for s in range(ROW_TILES)], axis=1)
        gu = jnp.dot(x.astype(BF16), wgu_sc[...], preferred_element_type=F32)
        a = jax.nn.silu(gu[:, :D_EXPERT]) * gu[:, D_EXPERT:]
        y = jnp.dot(a.astype(BF16), wd_sc[...], preferred_element_type=F32)
        for s in range(ROW_TILES):
            y8_ref[pl.ds(s, rb, stride=ROW_TILES), :] = y[:, s * LANES:(s + 1) * LANES]

    @pl.when(i >= nu_ref[0])
    def _():
        y8_ref[...] = jnp.zeros_like(y8_ref)


def _experts(buf8, blk_expert, n_used, w_gate, w_up, w_down, layer):
    n_blk = blk_expert.shape[0]
    blk_rows = EXPERT_ROWS * ROW_TILES

    def xmap(i, be, nu):
        return (jnp.maximum(jnp.minimum(i, nu[0] - 1), 0), 0)

    grid_spec = pltpu.PrefetchScalarGridSpec(
        num_scalar_prefetch=2,
        grid=(n_blk,),
        in_specs=[
            pl.BlockSpec((blk_rows, LANES), xmap),
            pl.BlockSpec((1, 1, D_MODEL, D_EXPERT), lambda i, be, nu: (layer, be[i], 0, 0)),
            pl.BlockSpec((1, 1, D_MODEL, D_EXPERT), lambda i, be, nu: (layer, be[i], 0, 0)),
            pl.BlockSpec((1, 1, D_EXPERT, D_MODEL), lambda i, be, nu: (layer, be[i], 0, 0)),
        ],
        out_specs=pl.BlockSpec((blk_rows, LANES), lambda i, be, nu: (i, 0)),
        scratch_shapes=[pltpu.VMEM((D_MODEL, 2 * D_EXPERT), BF16), pltpu.VMEM((D_EXPERT, D_MODEL), BF16)],
    )
    return pl.pallas_call(
        _expert_kernel,
        grid_spec=grid_spec,
        out_shape=jax.ShapeDtypeStruct(buf8.shape, F32),
        compiler_params=_cparams(("arbitrary",)),
        name="experts",
    )(blk_expert, n_used, buf8, w_gate, w_up, w_down)


def _combine_kernel(pos_ref, h_ref, gate_ref, y8_ref, fn_ref, o_ref, ga, gb, sem, *, final_norm):
    tm = h_ref.shape[0]
    bufs = (ga, gb)

    def copies(r):
        return [pltpu.make_async_copy(_row_tile(y8_ref, pos_ref[0, k, r]), _row_tile(bufs[k], r), sem) for k in range(2)]

    _for_each_row(tm, lambda r: [c.start() for c in copies(r)])
    _for_each_row(tm, lambda r: [c.wait() for c in copies(r)])
    g = gate_ref[...]
    ya = jnp.concatenate([ga[pl.ds(s, tm, stride=ROW_TILES), :] for s in range(ROW_TILES)], axis=1)
    yb = jnp.concatenate([gb[pl.ds(s, tm, stride=ROW_TILES), :] for s in range(ROW_TILES)], axis=1)
    out = h_ref[...] + (g[:, 0:1] * ya + g[:, 1:2] * yb)
    if final_norm:
        out = _rms(out, fn_ref[...])
    o_ref[...] = out


def _combine(h, y8, pos, gate, fn, tm, final_norm):
    t = h.shape[0]
    pos3 = pos.reshape(2, t // tm, tm).transpose(1, 0, 2)
    return pl.pallas_call(
        functools.partial(_combine_kernel, final_norm=final_norm),
        grid=(t // tm,),
        in_specs=[
            pl.BlockSpec((1, 2, tm), lambda i: (i, 0, 0), memory_space=pltpu.SMEM),
            pl.BlockSpec((tm, D_MODEL), lambda i: (i, 0)),
            pl.BlockSpec((tm, 2), lambda i: (i, 0)),
            pl.BlockSpec(memory_space=pl.ANY),
            pl.BlockSpec((1, D_MODEL), lambda i: (0, 0)),
        ],
        out_specs=pl.BlockSpec((tm, D_MODEL), lambda i: (i, 0)),
        out_shape=jax.ShapeDtypeStruct((t, D_MODEL), F32),
        scratch_shapes=[pltpu.VMEM((tm * ROW_TILES, LANES), F32), pltpu.VMEM((tm * ROW_TILES, LANES), F32),
                        pltpu.SemaphoreType.DMA],
        compiler_params=_cparams(("arbitrary",)),
        name="combine",
    )(pos3, h, gate.T, y8, fn.reshape(1, D_MODEL))


def _moe(h, g, w_rg, b_rg, w_re, b_re, w_gate, w_up, w_down, layer, fn, tm, final_norm):
    t = h.shape[0]
    xn, idx, gate, cnt = _router(h, g, w_rg, b_rg, w_re, b_re, tm)
    n_blk = 2 * t // EXPERT_ROWS + N_EXPERTS
    counts = cnt[:, 0].astype(jnp.int32)
    padded = (counts + EXPERT_ROWS - 1) // EXPERT_ROWS * EXPERT_ROWS
    ends = jnp.cumsum(padded)
    pad_start = ends - padded
    eid, rank = idx[0:2], idx[2:4]
    sel = eid[:, :, None] == jnp.arange(N_EXPERTS, dtype=jnp.int32)
    dest = jnp.sum(jnp.where(sel, pad_start, 0), axis=-1) + rank
    blk_start = jnp.arange(n_blk, dtype=jnp.int32) * EXPERT_ROWS
    blk_expert = jnp.minimum(jnp.sum(blk_start[:, None] >= ends[None, :], axis=1), N_EXPERTS - 1).astype(jnp.int32)
    n_used = (ends[-1:] // EXPERT_ROWS).astype(jnp.int32)
    buf8 = _dispatch(xn, dest, n_blk * EXPERT_ROWS, tm)
    y8 = _experts(buf8, blk_expert, n_used, w_gate, w_up, w_down, layer)
    return _combine(h, y8, dest, gate, fn, tm, final_norm)


def _kv_kernel(h_ref, g_ref, wk_ref, wvt_ref, cos_ref, sa_ref, sb_ref, k_ref, vt_ref):
    hn = _rms(h_ref[...], g_ref[...]).astype(BF16)
    k = jnp.dot(hn, wk_ref[...], preferred_element_type=F32)
    cos, sa, sb = cos_ref[...], sa_ref[...], sb_ref[...]
    for c in range(D_MODEL // LANES):
        kc = k[:, c * LANES:(c + 1) * LANES]
        rot = kc * cos + pltpu.roll(kc, LANES - ROT_HALF, 1) * sa + pltpu.roll(kc, ROT_HALF, 1) * sb
        k_ref[0, :, c * LANES:(c + 1) * LANES] = rot.astype(BF16)
    vt = lax.dot_general(wvt_ref[...], hn, (((1,), (1,)), ((), ())), preferred_element_type=F32)
    for c in range(vt_ref.shape[1]):
        vt_ref[0, c] = vt[:, c * LANES:(c + 1) * LANES].astype(BF16)


def _kv(h, g, w_kv, bsz, lp):
    tpb = lp // TILE
    kblk = TILE // LANES
    pos = jnp.maximum(jnp.arange(lp) - PAD_ROWS, 0).astype(F32)
    inv_freq = ROPE_THETA ** (-jnp.arange(ROT_HALF, dtype=F32) * 2.0 / (2 * ROT_HALF))
    ang = pos[:, None] * inv_freq[None, :]
    r = jnp.arange(LANES) % HEAD_DIM
    cos_t = jnp.where(r < 2 * ROT_HALF, jnp.cos(ang)[:, r % ROT_HALF], 1.0)
    sin_t = jnp.sin(ang)[:, r % ROT_HALF]
    sa = jnp.where(r < ROT_HALF, -sin_t, 0.0)
    sb = jnp.where((r >= ROT_HALF) & (r < 2 * ROT_HALF), sin_t, 0.0)
    tab = pl.BlockSpec((TILE, LANES), lambda i: (i % tpb, 0))
    return pl.pallas_call(
        _kv_kernel,
        grid=(bsz * tpb,),
        in_specs=[
            pl.BlockSpec((TILE, D_MODEL), lambda i: (i, 0)),
            pl.BlockSpec((1, D_MODEL), lambda i: (0, 0)),
            pl.BlockSpec((D_MODEL, D_MODEL), lambda i: (0, 0)),
            pl.BlockSpec((D_MODEL, D_MODEL), lambda i: (0, 0)),
            tab, tab, tab,
        ],
        out_specs=[
            pl.BlockSpec((1, TILE, D_MODEL), lambda i: (i // tpb, i % tpb, 0)),
            pl.BlockSpec((1, kblk, D_MODEL, LANES), lambda i: (i // tpb, i % tpb, 0, 0)),
        ],
        out_shape=[
            jax.ShapeDtypeStruct((bsz, lp, D_MODEL), BF16),
            jax.ShapeDtypeStruct((bsz, lp // LANES, D_MODEL, LANES), BF16),
        ],
        compiler_params=_cparams(("arbitrary",)),
        name="kv_proj",
    )(h, g.reshape(1, D_MODEL), w_kv[:, :D_MODEL].astype(BF16), w_kv[:, D_MODEL:].T.astype(BF16), cos_t, sa, sb)


def _q_kernel(h_ref, g_ref, wqt_ref, cos_ref, sin_ref, qt_ref):
    hn = _rms(h_ref[...], g_ref[...]).astype(BF16)
    qt = lax.dot_general(wqt_ref[...], hn, (((1,), (1,)), ((), ())), preferred_element_type=F32)
    cos, sin = cos_ref[...], sin_ref[...]
    scale = HEAD_DIM ** -0.5 * math.log2(math.e)
    for c in range(D_MODEL // HEAD_DIM):
        lo = c * HEAD_DIM
        x1 = qt[lo:lo + ROT_HALF]
        x2 = qt[lo + ROT_HALF:lo + 2 * ROT_HALF]
        blk = jnp.concatenate([x1 * cos - x2 * sin, x2 * cos + x1 * sin, qt[lo + 2 * ROT_HALF:lo + HEAD_DIM]], axis=0)
        qt_ref[lo:lo + HEAD_DIM, :] = (blk * scale).astype(BF16)


def _real_row_map(tiles_real, tiles_padded):
    front = FRONT // TILE
    return lambda i: ((i // tiles_real) * tiles_padded + front + i % tiles_real, 0)


def _q_proj(h_pad, g, w_q, bsz, seq, lp):
    tiles_real = seq // TILE
    pos = (N_META + jnp.arange(seq)).astype(F32)
    inv_freq = ROPE_THETA ** (-jnp.arange(ROT_HALF, dtype=F32) * 2.0 / (2 * ROT_HALF))
    ang = inv_freq[:, None] * pos[None, :]
    tab = pl.BlockSpec((ROT_HALF, TILE), lambda i: (0, i % tiles_real))
    return pl.pallas_call(
        _q_kernel,
        grid=(bsz * tiles_real,),
        in_specs=[
            pl.BlockSpec((TILE, D_MODEL), _real_row_map(tiles_real, lp // TILE)),
            pl.BlockSpec((1, D_MODEL), lambda i: (0, 0)),
            pl.BlockSpec((D_MODEL, D_MODEL), lambda i: (0, 0)),
            tab, tab,
        ],
        out_specs=pl.BlockSpec((D_MODEL, TILE), lambda i: (0, i)),
        out_shape=jax.ShapeDtypeStruct((D_MODEL, bsz * seq), BF16),
        compiler_params=_cparams(("arbitrary",)),
        name="q_proj",
    )(h_pad, g.reshape(1, D_MODEL), w_q.T.astype(BF16), jnp.cos(ang), jnp.sin(ang))


def _attn_kernel(qt_ref, k_ref, vt_ref, lam_ref, g_ref, o_ref, m_sc, l_sc, acc_sc, *, lam_init):
    i = pl.program_id(2)
    tq = qt_ref.shape[1]
    qt = qt_ref[...]
    zero = jnp.zeros((HEAD_DIM, tq), BF16)
    qs = (jnp.concatenate([qt[0:HEAD_DIM], zero], axis=0), jnp.concatenate([zero, qt[HEAD_DIM:]], axis=0))
    m_sc[...] = jnp.full(m_sc.shape, NEG, F32)
    l_sc[...] = jnp.zeros_like(l_sc)
    acc_sc[...] = jnp.zeros_like(acc_sc)

    def tile(row0, nkeys, vis):
        sub = min(nkeys, ATT_SUB)
        blk0 = row0 // LANES
        def k_rows(u):
            start = row0 + u * sub
            return k_ref[0, pl.ds(start if isinstance(start, int) else pl.multiple_of(start, LANES), sub), :]

        scores = [[jnp.dot(k_rows(u), qs[n], preferred_element_type=F32) for n in range(2)]
                  for u in range(nkeys // sub)]
        for u in range(nkeys // sub):
            vtt = jnp.concatenate([vt_ref[0, blk0 + u * (sub // LANES) + c] for c in range(sub // LANES)], axis=1)
            for n in range(2):
                s = scores[u][n]
                if vis is not None:
                    s = jnp.where(vis[u * sub:(u + 1) * sub], s, NEG)
                m_old = m_sc[n]
                m_new = jnp.maximum(m_old, jnp.max(s, axis=0, keepdims=True))
                alpha = jnp.exp2(m_old - m_new)
                p = jnp.exp2(s - m_new)
                l_sc[n] = alpha * l_sc[n] + jnp.sum(p, axis=0, keepdims=True)
                acc_sc[n] = alpha * acc_sc[n] + jnp.dot(vtt, p.astype(BF16), preferred_element_type=F32)
                m_sc[n] = m_new

    def chunk_vis(nkeys, shift):
        r = lax.broadcasted_iota(jnp.int32, (nkeys, tq), 0) // CHUNK
        c = lax.broadcasted_iota(jnp.int32, (nkeys, tq), 1) // CHUNK
        return r <= c + shift

    tile(FRONT - LANES, LANES, lax.broadcasted_iota(jnp.int32, (LANES, tq), 0) >= META_PAD)

    def body(j, c):
        tile(pl.multiple_of(FRONT + j * ATT_TK, LANES), ATT_TK, None)
        return c

    lax.fori_loop(0, i // 2, body, 0)
    base = pl.multiple_of(FRONT + (i // 2) * ATT_TK, LANES)

    @pl.when(i % 2 == 0)
    def _():
        tile(base, tq, chunk_vis(tq, 0))

    @pl.when(i % 2 == 1)
    def _():
        tile(base, ATT_TK, chunk_vis(ATT_TK, tq // CHUNK))

    lp = lam_ref[...]
    lam = (jnp.exp(jnp.sum(lp[0:1] * lp[1:2], axis=1, keepdims=True))
           - jnp.exp(jnp.sum(lp[2:3] * lp[3:4], axis=1, keepdims=True)) + lam_init)
    o = acc_sc[0] / l_sc[0] - lam * (acc_sc[1] / l_sc[1])
    o = o * lax.rsqrt(jnp.mean(o * o, axis=0, keepdims=True) + EPS) * g_ref[...] * (1.0 - lam_init)
    o_ref[...] = o.T.astype(BF16)


def _attention(qt, k, vt4, lam_p, subln_g, bsz, seq, lam_init):
    nq = seq // ATT_TQ
    lp = k.shape[1]
    return pl.pallas_call(
        functools.partial(_attn_kernel, lam_init=lam_init),
        grid=(bsz, N_HEADS, nq),
        in_specs=[
            pl.BlockSpec((2 * HEAD_DIM, ATT_TQ), lambda b, h, i: (h, b * nq + i)),
            pl.BlockSpec((1, lp, 2 * HEAD_DIM), lambda b, h, i: (b, 0, h)),
            pl.BlockSpec((1, lp // LANES, 2 * HEAD_DIM, LANES), lambda b, h, i: (b, 0, h, 0)),
            pl.BlockSpec((4, HEAD_DIM), lambda b, h, i: (0, 0)),
            pl.BlockSpec((2 * HEAD_DIM, 1), lambda b, h, i: (0, 0)),
        ],
        out_specs=pl.BlockSpec((ATT_TQ, 2 * HEAD_DIM), lambda b, h, i: (b * nq + i, h)),
        out_shape=jax.ShapeDtypeStruct((bsz * seq, D_MODEL), BF16),
        scratch_shapes=[pltpu.VMEM((2, 1, ATT_TQ), F32), pltpu.VMEM((2, 1, ATT_TQ), F32),
                        pltpu.VMEM((2, 2 * HEAD_DIM, ATT_TQ), F32)],
        compiler_params=_cparams(("arbitrary", "arbitrary", "arbitrary")),
        name="diff_attention",
    )(qt, k, vt4, lam_p, subln_g.reshape(2 * HEAD_DIM, 1))


def _oproj_kernel(h_ref, o_ref, w_ref, out_ref):
    out_ref[...] = h_ref[...] + jnp.dot(o_ref[...], w_ref[...], preferred_element_type=F32)


def _o_proj(h_pad, o, w_out, bsz, seq, lp):
    tiles_real = seq // TILE
    return pl.pallas_call(
        _oproj_kernel,
        grid=(bsz * tiles_real,),
        in_specs=[
            pl.BlockSpec((TILE, D_MODEL), _real_row_map(tiles_real, lp // TILE)),
            pl.BlockSpec((TILE, D_MODEL), lambda i: (i, 0)),
            pl.BlockSpec((D_MODEL, D_MODEL), lambda i: (0, 0)),
        ],
        out_specs=pl.BlockSpec((TILE, D_MODEL), lambda i: (i, 0)),
        out_shape=jax.ShapeDtypeStruct((bsz * seq, D_MODEL), F32),
        compiler_params=_cparams(("arbitrary",)),
        name="o_proj",
    )(h_pad, o, w_out.astype(BF16))


def kernel(x, meta_tokens, a_norm, a_w_in, a_conv, a_w_out, kv_norm, w_kv, b_norm, b_w_q, b_lambda, b_subln, b_w_out, ffn_norm, r_group, r_group_b, r_expert, r_expert_b, e_gate, e_up, e_down, final_norm):
    bsz, seq, d = x.shape
    assert d == D_MODEL and a_norm.shape[0] == 1 and b_norm.shape[0] == 1
    lp = FRONT + seq
    assert seq % ATT_TK == 0 and ATT_TK == 2 * ATT_TQ and ATT_TQ % TILE == 0
    front = jnp.concatenate([jnp.zeros((PAD_ROWS, d), x.dtype), meta_tokens.astype(x.dtype)], axis=0)
    h = _mixer(x.reshape(bsz * seq, d), front, a_norm[0], a_w_in[0], a_conv[0], a_w_out[0], bsz, lp // TILE)
    h = _moe(h, ffn_norm[0], r_group[0], r_group_b[0], r_expert[0], r_expert_b[0],
             e_gate, e_up, e_down, 0, final_norm, TILE, False)

    k, vt4 = _kv(h, kv_norm, w_kv, bsz, lp)
    qt = _q_proj(h, b_norm[0], b_w_q[0], bsz, seq, lp)
    lam_init = 0.8 - 0.6 * math.exp(-0.3 * a_norm.shape[0])
    o = _attention(qt, k, vt4, b_lambda[0], b_subln[0], bsz, seq, lam_init)
    h = _o_proj(h, o, b_w_out[0], bsz, seq, lp)
    h = _moe(h, ffn_norm[1], r_group[1], r_group_b[1], r_expert[1], r_expert_b[1],
             e_gate, e_up, e_down, 1, final_norm, TILE, True)
    return h.reshape(bsz, seq, d)
```

```python
import functools
import math

import jax
import jax.numpy as jnp
from jax import lax
from jax.experimental import pallas as pl
from jax.experimental.pallas import tpu as pltpu

D_MODEL = 1024
CHUNK = 64
N_META = 16
HEAD_DIM = 64
N_HEADS = 8
ROT_HALF = 8
ROPE_THETA = 500000.0
N_GROUPS = 4
EXPERTS_PER_GROUP = 8
N_EXPERTS = 32
D_EXPERT = 512
EPS = 1e-6

LANES = 128
SUBLANES = 8
TILE = 512
FRONT = TILE
PAD_ROWS = FRONT - N_META
META_PAD = LANES - N_META
EXPERT_ROWS = 256
ATT_TQ = 512
ATT_TK = 1024
ATT_SUB = 512
SEG_ALIGN = 16
SEG_SIZES = (512, 256, 128, 64, 32, 16)
TAIL_SIZES = (128, 64, 32, 16)
LOCAL_ROWS = 1536
VMEM_LIMIT = 56 * 1024 * 1024
NEG = float(jnp.finfo(jnp.float32).min)

BF16 = jnp.bfloat16
F32 = jnp.float32


def _rms(x, g):
    return x * lax.rsqrt(jnp.mean(x * x, axis=-1, keepdims=True) + EPS) * g


def _cparams(sem):
    return pltpu.CompilerParams(dimension_semantics=sem, vmem_limit_bytes=VMEM_LIMIT)


def _mixer_kernel(x_ref, front_ref, g_ref, win_ref, cw_ref, wout_ref, o_ref, cu_ref, *, tiles_per_batch):
    i = pl.program_id(0)
    tm = x_ref.shape[0]
    x = jnp.where(i % tiles_per_batch == 0, front_ref[...], x_ref[...])
    hn = _rms(x, g_ref[...]).astype(BF16)
    proj = jnp.dot(hn, win_ref[...], preferred_element_type=F32)
    b_gate = proj[:, :D_MODEL]
    cu = proj[:, D_MODEL:2 * D_MODEL] * proj[:, 2 * D_MODEL:]

    @pl.when(i % tiles_per_batch == 0)
    def _():
        cu_ref[0:SUBLANES, :] = jnp.zeros((SUBLANES, D_MODEL), F32)

    cu_ref[SUBLANES:SUBLANES + tm, :] = cu
    cw = cw_ref[...]
    z = (cw[2:3] * cu + cw[1:2] * cu_ref[SUBLANES - 1:SUBLANES - 1 + tm, :]
         + cw[0:1] * cu_ref[SUBLANES - 2:SUBLANES - 2 + tm, :])
    cu_ref[0:SUBLANES, :] = cu_ref[tm:tm + SUBLANES, :]
    y = jnp.dot((b_gate * z).astype(BF16), wout_ref[...], preferred_element_type=F32)
    o_ref[...] = x + y


def _mixer(x, front, g, w_in, conv_w, w_out, bsz, tiles_per_batch):
    t = bsz * tiles_per_batch * TILE
    tiles_real = tiles_per_batch - FRONT // TILE

    def x_map(i):
        return ((i // tiles_per_batch) * tiles_real + jnp.maximum(i % tiles_per_batch - FRONT // TILE, 0), 0)

    return pl.pallas_call(
        functools.partial(_mixer_kernel, tiles_per_batch=tiles_per_batch),
        grid=(t // TILE,),
        in_specs=[
            pl.BlockSpec((TILE, D_MODEL), x_map),
            pl.BlockSpec((TILE, D_MODEL), lambda i: (0, 0)),
            pl.BlockSpec((1, D_MODEL), lambda i: (0, 0)),
            pl.BlockSpec((D_MODEL, 3 * D_MODEL), lambda i: (0, 0)),
            pl.BlockSpec((3, D_MODEL), lambda i: (0, 0)),
            pl.BlockSpec((D_MODEL, D_MODEL), lambda i: (0, 0)),
        ],
        out_specs=pl.BlockSpec((TILE, D_MODEL), lambda i: (i, 0)),
        out_shape=jax.ShapeDtypeStruct((t, D_MODEL), F32),
        scratch_shapes=[pltpu.VMEM((TILE + 2 * SUBLANES, D_MODEL), F32)],
        compiler_params=_cparams(("arbitrary",)),
        name="mixer_a",
    )(x, front, g.reshape(1, D_MODEL), w_in.astype(BF16), conv_w, w_out.astype(BF16))


ROUTER_ROWS = SUBLANES + N_EXPERTS


def _round_up(v, m):
    return jnp.floor((v + (m - 1)) * (1.0 / m)) * m


def _router_kernel(h_ref, g_ref, wrt_ref, br_ref, tri_ref, sorted_ref, lpos_ref, gate_ref, meta_ref, tot_ref, seg_sc):
    i = pl.program_id(0)
    tm = h_ref.shape[0]

    @pl.when(i == 0)
    def _():
        seg_sc[...] = jnp.zeros_like(seg_sc)

    xn = _rms(h_ref[...], g_ref[...])
    lt = lax.dot_general(wrt_ref[...], xn, (((1,), (1,)), ((), ())),
                         precision=lax.Precision.HIGHEST, preferred_element_type=F32) + br_ref[...]
    lg = lt[0:N_GROUPS]
    gmax = jnp.max(lg, axis=0, keepdims=True)
    iota_g = lax.broadcasted_iota(jnp.int32, lg.shape, 0)
    grp = jnp.min(jnp.where(lg == gmax, iota_g, N_GROUPS), axis=0, keepdims=True)
    p_grp = 1.0 / jnp.sum(jnp.exp(lg - gmax), axis=0, keepdims=True)
    le = lt[SUBLANES:SUBLANES + EXPERTS_PER_GROUP]
    for g in range(1, N_GROUPS):
        lo = SUBLANES + g * EXPERTS_PER_GROUP
        le = jnp.where(grp == g, lt[lo:lo + EXPERTS_PER_GROUP], le)
    iota_e = lax.broadcasted_iota(jnp.int32, le.shape, 0)
    v1 = jnp.max(le, axis=0, keepdims=True)
    i1 = jnp.min(jnp.where(le == v1, iota_e, EXPERTS_PER_GROUP), axis=0, keepdims=True)
    le2 = jnp.where(iota_e == i1, -jnp.inf, le)
    v2 = jnp.max(le2, axis=0, keepdims=True)
    i2 = jnp.min(jnp.where(le2 == v2, iota_e, EXPERTS_PER_GROUP), axis=0, keepdims=True)
    e2 = jnp.exp(v2 - v1)
    den = 1.0 + e2
    gate_ref[...] = jnp.concatenate([p_grp * (1.0 / den), p_grp * (e2 / den)], axis=0)
    experts = (grp * EXPERTS_PER_GROUP + i1, grp * EXPERTS_PER_GROUP + i2)

    iota_x = lax.broadcasted_iota(jnp.int32, (N_EXPERTS, tm), 0)
    hits = [iota_x == e_k for e_k in experts]
    hits_f = [jnp.where(hit, 1.0, 0.0) for hit in hits]
    hits_b = [hf.astype(BF16) for hf in hits_f]
    excl = [jnp.dot(hb, tri_ref[...], preferred_element_type=F32) for hb in hits_b]
    tot_col = [jnp.sum(hf, axis=1, keepdims=True) for hf in hits_f]
    ones = jnp.ones((SUBLANES, tm), BF16)
    n_lane = sum(lax.dot_general(ones, hb, (((1,), (1,)), ((), ())), preferred_element_type=F32) for hb in hits_b)
    seg_len_lane = _round_up(n_lane[0:1], SEG_ALIGN)
    seg_len = _round_up(tot_col[0] + tot_col[1], SEG_ALIGN)
    before = (lax.broadcasted_iota(jnp.int32, (N_EXPERTS, N_EXPERTS), 1)
              < lax.broadcasted_iota(jnp.int32, (N_EXPERTS, N_EXPERTS), 0))
    local_off = jnp.sum(jnp.where(before, seg_len_lane, 0.0), axis=1, keepdims=True)

    lpos_a = jnp.sum(jnp.where(hits[0], excl[0] + local_off, 0.0), axis=0, keepdims=True).astype(jnp.int32)
    lpos_b = jnp.sum(jnp.where(hits[1], excl[1] + tot_col[0] + local_off, 0.0), axis=0, keepdims=True).astype(jnp.int32)
    lpos_ref[...] = jnp.concatenate([lpos_a, lpos_b], axis=0)

    seg_before = seg_sc[...]
    seg_sc[...] = seg_before + seg_len
    lane = lax.broadcasted_iota(jnp.int32, (N_EXPERTS, LANES), 1)
    meta = jnp.where(lane == 0, seg_len, jnp.where(lane == 1, local_off, jnp.where(lane == 2, seg_before, 0.0)))
    meta_ref[0] = meta.astype(jnp.int32)
    tot_ref[...] = jnp.broadcast_to(seg_before + seg_len, tot_ref.shape).astype(jnp.int32)

    row = lax.broadcasted_iota(jnp.int32, (LOCAL_ROWS, tm), 0)
    perm = jnp.where(row == lpos_a, 1.0, jnp.where(row == lpos_b, 1.0, 0.0)).astype(BF16)
    sorted_ref[0] = jnp.dot(perm, xn.astype(BF16), preferred_element_type=F32).astype(BF16)


def _router(h, g, w_rg, b_rg, w_re, b_re, tm):
    t = h.shape[0]
    nt = t // tm
    wrt = jnp.zeros((ROUTER_ROWS, D_MODEL), F32)
    wrt = wrt.at[0:N_GROUPS].set(w_rg.T).at[SUBLANES:].set(w_re.T)
    br = jnp.zeros((ROUTER_ROWS, 1), F32)
    br = br.at[0:N_GROUPS, 0].set(b_rg).at[SUBLANES:, 0].set(b_re)
    tri = (lax.broadcasted_iota(jnp.int32, (tm, tm), 0) < lax.broadcasted_iota(jnp.int32, (tm, tm), 1)).astype(BF16)
    return pl.pallas_call(
        _router_kernel,
        grid=(nt,),
        in_specs=[
            pl.BlockSpec((tm, D_MODEL), lambda i: (i, 0)),
            pl.BlockSpec((1, D_MODEL), lambda i: (0, 0)),
            pl.BlockSpec((ROUTER_ROWS, D_MODEL), lambda i: (0, 0)),
            pl.BlockSpec((ROUTER_ROWS, 1), lambda i: (0, 0)),
            pl.BlockSpec((tm, tm), lambda i: (0, 0)),
        ],
        out_specs=[
            pl.BlockSpec((1, LOCAL_ROWS, D_MODEL), lambda i: (i, 0, 0)),
            pl.BlockSpec((2, tm), lambda i: (0, i)),
            pl.BlockSpec((2, tm), lambda i: (0, i)),
            pl.BlockSpec((1, N_EXPERTS, LANES), lambda i: (i, 0, 0)),
            pl.BlockSpec((N_EXPERTS, LANES), lambda i: (0, 0)),
        ],
        out_shape=[
            jax.ShapeDtypeStruct((nt, LOCAL_ROWS, D_MODEL), BF16),
            jax.ShapeDtypeStruct((2, t), jnp.int32),
            jax.ShapeDtypeStruct((2, t), F32),
            jax.ShapeDtypeStruct((nt, N_EXPERTS, LANES), jnp.int32),
            jax.ShapeDtypeStruct((N_EXPERTS, LANES), jnp.int32),
        ],
        scratch_shapes=[pltpu.VMEM((N_EXPERTS, 1), F32)],
        compiler_params=_cparams(("arbitrary",)),
        name="router",
    )(h, g.reshape(1, D_MODEL), wrt, br, tri)


def _segment_chunks(n, sizes, fn):
    off = 0
    for size in sizes:
        part = n & size

        @pl.when(part != 0)
        def _(off=off, size=size):
            fn(off, size)

        off = off + part


def _rows(ref, start, size):
    return ref.at[pl.ds(pl.multiple_of(start, SEG_ALIGN), size), :]


def _gather_kernel(len_ref, off_ref, dst_ref, zs_ref, zn_ref, nu_ref, sorted_ref, buf_ref, zero_sc, sem, *, n_blk):
    t = pl.program_id(0)
    nt = pl.num_programs(0)

    def tile_copies(t, op):
        def body(e, c):
            idx = t * N_EXPERTS + e
            lo, dst = off_ref[idx], dst_ref[idx]
            _segment_chunks(len_ref[idx], SEG_SIZES, lambda off, size: op(pltpu.make_async_copy(
                _rows(sorted_ref.at[t], lo + off, size), _rows(buf_ref, dst + off, size), sem)))
            return c

        lax.fori_loop(0, N_EXPERTS, body, 0)

    def fill_copies(op):
        def tails(e, c):
            _segment_chunks(zn_ref[e], TAIL_SIZES, lambda off, size: op(pltpu.make_async_copy(
                zero_sc.at[0:size, :], _rows(buf_ref, zs_ref[e] + off, size), sem)))
            return c

        lax.fori_loop(0, N_EXPERTS, tails, 0)

        def blocks(b, c):
            op(pltpu.make_async_copy(zero_sc, _rows(buf_ref, b * EXPERT_ROWS, EXPERT_ROWS), sem))
            return c

        lax.fori_loop(nu_ref[0], n_blk, blocks, 0)

    start = lambda cp: cp.start()
    wait = lambda cp: cp.wait()

    @pl.when(t == 0)
    def _():
        zero_sc[...] = jnp.zeros_like(zero_sc)
        fill_copies(start)

    tile_copies(t, start)

    @pl.when(t > 0)
    def _():
        tile_copies(t - 1, wait)

    @pl.when(t == nt - 1)
    def _():
        tile_copies(t, wait)
        fill_copies(wait)


def _gather_segments(sorted_rows, seg_len, seg_off, seg_dst, tail_start, tail_len, n_used, n_blk):
    grid_spec = pltpu.PrefetchScalarGridSpec(
        num_scalar_prefetch=6,
        grid=(sorted_rows.shape[0],),
        in_specs=[pl.BlockSpec(memory_space=pl.ANY)],
        out_specs=pl.BlockSpec(memory_space=pl.ANY),
        scratch_shapes=[pltpu.VMEM((EXPERT_ROWS, D_MODEL), BF16), pltpu.SemaphoreType.DMA],
    )
    return pl.pallas_call(
        functools.partial(_gather_kernel, n_blk=n_blk),
        grid_spec=grid_spec,
        out_shape=jax.ShapeDtypeStruct((n_blk * EXPERT_ROWS, D_MODEL), BF16),
        compiler_params=_cparams(("arbitrary",)),
        name="gather_segments",
    )(seg_len, seg_off, seg_dst, tail_start, tail_len, n_used, sorted_rows)


def _expert_kernel(be_ref, nu_ref, x_ref, wg_ref, wu_ref, wd_ref, y_ref, wgu_sc, wd_sc):
    i = pl.program_id(0)
    e = be_ref[i]
    prev = be_ref[jnp.maximum(i - 1, 0)]

    @pl.when(jnp.logical_or(i == 0, e != prev))
    def _():
        wgu_sc[:, :D_EXPERT] = wg_ref[0, 0].astype(BF16)
        wgu_sc[:, D_EXPERT:] = wu_ref[0, 0].astype(BF16)
        wd_sc[...] = wd_ref[0, 0].astype(BF16)

    @pl.when(i < nu_ref[0])
    def _():
        gu = jnp.dot(x_ref[...], wgu_sc[...], preferred_element_type=F32)
        a = jax.nn.silu(gu[:, :D_EXPERT]) * gu[:, D_EXPERT:]
        y_ref[...] = jnp.dot(a.astype(BF16), wd_sc[...], preferred_element_type=F32).astype(BF16)

    @pl.when(i >= nu_ref[0])
    def _():
        y_ref[...] = jnp.zeros_like(y_ref)


def _experts(buf, blk_expert, n_used, w_gate, w_up, w_down, layer):
    n_blk = blk_expert.shape[0]

    def xmap(i, be, nu):
        return (jnp.maximum(jnp.minimum(i, nu[0] - 1), 0), 0)

    grid_spec = pltpu.PrefetchScalarGridSpec(
        num_scalar_prefetch=2,
        grid=(n_blk,),
        in_specs=[
            pl.BlockSpec((EXPERT_ROWS, D_MODEL), xmap),
            pl.BlockSpec((1, 1, D_MODEL, D_EXPERT), lambda i, be, nu: (layer, be[i], 0, 0)),
            pl.BlockSpec((1, 1, D_MODEL, D_EXPERT), lambda i, be, nu: (layer, be[i], 0, 0)),
            pl.BlockSpec((1, 1, D_EXPERT, D_MODEL), lambda i, be, nu: (layer, be[i], 0, 0)),
        ],
        out_specs=pl.BlockSpec((EXPERT_ROWS, D_MODEL), lambda i, be, nu: (i, 0)),
        scratch_shapes=[pltpu.VMEM((D_MODEL, 2 * D_EXPERT), BF16), pltpu.VMEM((D_EXPERT, D_MODEL), BF16)],
    )
    return pl.pallas_call(
        _expert_kernel,
        grid_spec=grid_spec,
        out_shape=jax.ShapeDtypeStruct(buf.shape, BF16),
        compiler_params=_cparams(("arbitrary",)),
        name="experts",
    )(blk_expert, n_used, buf, w_gate, w_up, w_down)


def _combine_kernel(len_ref, off_ref, dst_ref, h_ref, gate_ref, lpos_ref, y_ref, fn_ref, o_ref, yl, sem, *, final_norm):
    i = pl.program_id(0)
    nt = pl.num_programs(0)
    tm = h_ref.shape[0]
    slot = i % 2

    def tile_copies(t, s, op):
        def body(e, c):
            idx = t * N_EXPERTS + e
            lo, src = off_ref[idx], dst_ref[idx]
            _segment_chunks(len_ref[idx], SEG_SIZES, lambda off, size: op(pltpu.make_async_copy(
                _rows(y_ref, src + off, size), _rows(yl.at[s], lo + off, size), sem.at[s])))
            return c

        lax.fori_loop(0, N_EXPERTS, body, 0)

    @pl.when(i == 0)
    def _():
        yl[...] = jnp.zeros_like(yl)
        tile_copies(0, 0, lambda cp: cp.start())

    @pl.when(i + 1 < nt)
    def _():
        tile_copies(i + 1, 1 - slot, lambda cp: cp.start())

    tile_copies(i, slot, lambda cp: cp.wait())

    rows = yl[slot]
    lpos = lpos_ref[...]
    col = lax.broadcasted_iota(jnp.int32, (tm, LOCAL_ROWS), 1)
    picked = [jnp.dot(jnp.where(col == lpos[:, k:k + 1], 1.0, 0.0).astype(BF16), rows, preferred_element_type=F32)
              for k in range(2)]
    g = gate_ref[...]
    out = h_ref[...] + (g[:, 0:1] * picked[0] + g[:, 1:2] * picked[1])
    if final_norm:
        out = _rms(out, fn_ref[...])
    o_ref[...] = out


def _combine(h, y, seg_len, seg_off, seg_dst, lpos, gate, fn, tm, final_norm):
    t = h.shape[0]
    grid_spec = pltpu.PrefetchScalarGridSpec(
        num_scalar_prefetch=3,
        grid=(t // tm,),
        in_specs=[
            pl.BlockSpec((tm, D_MODEL), lambda i, *_: (i, 0)),
            pl.BlockSpec((tm, 2), lambda i, *_: (i, 0)),
            pl.BlockSpec((tm, 2), lambda i, *_: (i, 0)),
            pl.BlockSpec(memory_space=pl.ANY),
            pl.BlockSpec((1, D_MODEL), lambda i, *_: (0, 0)),
        ],
        out_specs=pl.BlockSpec((tm, D_MODEL), lambda i, *_: (i, 0)),
        scratch_shapes=[pltpu.VMEM((2, LOCAL_ROWS, D_MODEL), BF16), pltpu.SemaphoreType.DMA((2,))],
    )
    return pl.pallas_call(
        functools.partial(_combine_kernel, final_norm=final_norm),
        grid_spec=grid_spec,
        out_shape=jax.ShapeDtypeStruct((t, D_MODEL), F32),
        compiler_params=_cparams(("arbitrary",)),
        name="combine",
    )(seg_len, seg_off, seg_dst, h, gate.T, lpos.T, y, fn.reshape(1, D_MODEL))


def _moe(h, g, w_rg, b_rg, w_re, b_re, w_gate, w_up, w_down, layer, fn, tm, final_norm):
    t = h.shape[0]
    nt = t // tm
    sorted_rows, lpos, gate, meta, tot = _router(h, g, w_rg, b_rg, w_re, b_re, tm)
    n_blk = (2 * t + nt * N_EXPERTS * (SEG_ALIGN - 1)) // EXPERT_ROWS + N_EXPERTS
    total = tot[:, 0]
    region = (total + EXPERT_ROWS - 1) // EXPERT_ROWS * EXPERT_ROWS
    ends = jnp.cumsum(region)
    starts = ends - region
    seg_len = meta[:, :, 0].reshape(-1)
    seg_off = meta[:, :, 1].reshape(-1)
    seg_dst = (meta[:, :, 2] + starts[None, :]).reshape(-1)
    blk_start = jnp.arange(n_blk, dtype=jnp.int32) * EXPERT_ROWS
    blk_expert = jnp.minimum(jnp.sum(blk_start[:, None] >= ends[None, :], axis=1), N_EXPERTS - 1).astype(jnp.int32)
    n_used = (ends[-1:] // EXPERT_ROWS).astype(jnp.int32)
    buf = _gather_segments(sorted_rows, seg_len, seg_off, seg_dst, starts + total, region - total, n_used, n_blk)
    y = _experts(buf, blk_expert, n_used, w_gate, w_up, w_down, layer)
    return _combine(h, y, seg_len, seg_off, seg_dst, lpos, gate, fn, tm, final_norm)


def _kv_kernel(h_ref, g_ref, wk_ref, wvt_ref, cos_ref, sa_ref, sb_ref, k_ref, vt_ref):
    hn = _rms(h_ref[...], g_ref[...]).astype(BF16)
    k = jnp.dot(hn, wk_ref[...], preferred_element_type=F32)
    cos, sa, sb = cos_ref[...], sa_ref[...], sb_ref[...]
    for c in range(D_MODEL // LANES):
        kc = k[:, c * LANES:(c + 1) * LANES]
        rot = kc * cos + pltpu.roll(kc, LANES - ROT_HALF, 1) * sa + pltpu.roll(kc, ROT_HALF, 1) * sb
        k_ref[0, :, c * LANES:(c + 1) * LANES] = rot.astype(BF16)
    vt = lax.dot_general(wvt_ref[...], hn, (((1,), (1,)), ((), ())), preferred_element_type=F32)
    for c in range(vt_ref.shape[1]):
        vt_ref[0, c] = vt[:, c * LANES:(c + 1) * LANES].astype(BF16)


def _kv(h, g, w_kv, bsz, lp):
    tpb = lp // TILE
    kblk = TILE // LANES
    pos = jnp.maximum(jnp.arange(lp) - PAD_ROWS, 0).astype(F32)
    inv_freq = ROPE_THETA ** (-jnp.arange(ROT_HALF, dtype=F32) * 2.0 / (2 * ROT_HALF))
    ang = pos[:, None] * inv_freq[None, :]
    r = jnp.arange(LANES) % HEAD_DIM
    cos_t = jnp.where(r < 2 * ROT_HALF, jnp.cos(ang)[:, r % ROT_HALF], 1.0)
    sin_t = jnp.sin(ang)[:, r % ROT_HALF]
    sa = jnp.where(r < ROT_HALF, -sin_t, 0.0)
    sb = jnp.where((r >= ROT_HALF) & (r < 2 * ROT_HALF), sin_t, 0.0)
    tab = pl.BlockSpec((TILE, LANES), lambda i: (i % tpb, 0))
    return pl.pallas_call(
        _kv_kernel,
        grid=(bsz * tpb,),
        in_specs=[
            pl.BlockSpec((TILE, D_MODEL), lambda i: (i, 0)),
            pl.BlockSpec((1, D_MODEL), lambda i: (0, 0)),
            pl.BlockSpec((D_MODEL, D_MODEL), lambda i: (0, 0)),
            pl.BlockSpec((D_MODEL, D_MODEL), lambda i: (0, 0)),
            tab, tab, tab,
        ],
        out_specs=[
            pl.BlockSpec((1, TILE, D_MODEL), lambda i: (i // tpb, i % tpb, 0)),
            pl.BlockSpec((1, kblk, D_MODEL, LANES), lambda i: (i // tpb, i % tpb, 0, 0)),
        ],
        out_shape=[
            jax.ShapeDtypeStruct((bsz, lp, D_MODEL), BF16),
            jax.ShapeDtypeStruct((bsz, lp // LANES, D_MODEL, LANES), BF16),
        ],
        compiler_params=_cparams(("arbitrary",)),
        name="kv_proj",
    )(h, g.reshape(1, D_MODEL), w_kv[:, :D_MODEL].astype(BF16), w_kv[:, D_MODEL:].T.astype(BF16), cos_t, sa, sb)


def _q_kernel(h_ref, g_ref, wqt_ref, cos_ref, sin_ref, qt_ref):
    hn = _rms(h_ref[...], g_ref[...]).astype(BF16)
    qt = lax.dot_general(wqt_ref[...], hn, (((1,), (1,)), ((), ())), preferred_element_type=F32)
    cos, sin = cos_ref[...], sin_ref[...]
    scale = HEAD_DIM ** -0.5 * math.log2(math.e)
    for c in range(D_MODEL // HEAD_DIM):
        lo = c * HEAD_DIM
        x1 = qt[lo:lo + ROT_HALF]
        x2 = qt[lo + ROT_HALF:lo + 2 * ROT_HALF]
        blk = jnp.concatenate([x1 * cos - x2 * sin, x2 * cos + x1 * sin, qt[lo + 2 * ROT_HALF:lo + HEAD_DIM]], axis=0)
        qt_ref[lo:lo + HEAD_DIM, :] = (blk * scale).astype(BF16)


def _real_row_map(tiles_real, tiles_padded):
    front = FRONT // TILE
    return lambda i: ((i // tiles_real) * tiles_padded + front + i % tiles_real, 0)


def _q_proj(h_pad, g, w_q, bsz, seq, lp):
    tiles_real = seq // TILE
    pos = (N_META + jnp.arange(seq)).astype(F32)
    inv_freq = ROPE_THETA ** (-jnp.arange(ROT_HALF, dtype=F32) * 2.0 / (2 * ROT_HALF))
    ang = inv_freq[:, None] * pos[None, :]
    tab = pl.BlockSpec((ROT_HALF, TILE), lambda i: (0, i % tiles_real))
    return pl.pallas_call(
        _q_kernel,
        grid=(bsz * tiles_real,),
        in_specs=[
            pl.BlockSpec((TILE, D_MODEL), _real_row_map(tiles_real, lp // TILE)),
            pl.BlockSpec((1, D_MODEL), lambda i: (0, 0)),
            pl.BlockSpec((D_MODEL, D_MODEL), lambda i: (0, 0)),
            tab, tab,
        ],
        out_specs=pl.BlockSpec((D_MODEL, TILE), lambda i: (0, i)),
        out_shape=jax.ShapeDtypeStruct((D_MODEL, bsz * seq), BF16),
        compiler_params=_cparams(("arbitrary",)),
        name="q_proj",
    )(h_pad, g.reshape(1, D_MODEL), w_q.T.astype(BF16), jnp.cos(ang), jnp.sin(ang))


def _attn_kernel(qt_ref, k_ref, vt_ref, lam_ref, g_ref, o_ref, m_sc, l_sc, acc_sc, *, lam_init):
    i = pl.program_id(2)
    tq = qt_ref.shape[1]
    qt = qt_ref[...]
    zero = jnp.zeros((HEAD_DIM, tq), BF16)
    qs = (jnp.concatenate([qt[0:HEAD_DIM], zero], axis=0), jnp.concatenate([zero, qt[HEAD_DIM:]], axis=0))
    m_sc[...] = jnp.full(m_sc.shape, NEG, F32)
    l_sc[...] = jnp.zeros_like(l_sc)
    acc_sc[...] = jnp.zeros_like(acc_sc)

    def tile(row0, nkeys, vis):
        sub = min(nkeys, ATT_SUB)
        blk0 = row0 // LANES
        def k_rows(u):
            start = row0 + u * sub
            return k_ref[0, pl.ds(start if isinstance(start, int) else pl.multiple_of(start, LANES), sub), :]

        scores = [[jnp.dot(k_rows(u), qs[n], preferred_element_type=F32) for n in range(2)]
                  for u in range(nkeys // sub)]
        for u in range(nkeys // sub):
            vtt = jnp.concatenate([vt_ref[0, blk0 + u * (sub // LANES) + c] for c in range(sub // LANES)], axis=1)
            for n in range(2):
                s = scores[u][n]
                if vis is not None:
                    s = jnp.where(vis[u * sub:(u + 1) * sub], s, NEG)
                m_old = m_sc[n]
                m_new = jnp.maximum(m_old, jnp.max(s, axis=0, keepdims=True))
                alpha = jnp.exp2(m_old - m_new)
                p = jnp.exp2(s - m_new)
                l_sc[n] = alpha * l_sc[n] + jnp.sum(p, axis=0, keepdims=True)
                acc_sc[n] = alpha * acc_sc[n] + jnp.dot(vtt, p.astype(BF16), preferred_element_type=F32)
                m_sc[n] = m_new

    def chunk_vis(nkeys, shift):
        r = lax.broadcasted_iota(jnp.int32, (nkeys, tq), 0) // CHUNK
        c = lax.broadcasted_iota(jnp.int32, (nkeys, tq), 1) // CHUNK
        return r <= c + shift

    tile(FRONT - LANES, LANES, lax.broadcasted_iota(jnp.int32, (LANES, tq), 0) >= META_PAD)

    def body(j, c):
        tile(pl.multiple_of(FRONT + j * ATT_TK, LANES), ATT_TK, None)
        return c

    lax.fori_loop(0, i // 2, body, 0)
    base = pl.multiple_of(FRONT + (i // 2) * ATT_TK, LANES)

    @pl.when(i % 2 == 0)
    def _():
        tile(base, tq, chunk_vis(tq, 0))

    @pl.when(i % 2 == 1)
    def _():
        tile(base, ATT_TK, chunk_vis(ATT_TK, tq // CHUNK))

    lp = lam_ref[...]
    lam = (jnp.exp(jnp.sum(lp[0:1] * lp[1:2], axis=1, keepdims=True))
           - jnp.exp(jnp.sum(lp[2:3] * lp[3:4], axis=1, keepdims=True)) + lam_init)
    o = acc_sc[0] / l_sc[0] - lam * (acc_sc[1] / l_sc[1])
    o = o * lax.rsqrt(jnp.mean(o * o, axis=0, keepdims=True) + EPS) * g_ref[...] * (1.0 - lam_init)
    o_ref[...] = o.T.astype(BF16)


def _attention(qt, k, vt4, lam_p, subln_g, bsz, seq, lam_init):
    nq = seq // ATT_TQ
    lp = k.shape[1]
    return pl.pallas_call(
        functools.partial(_attn_kernel, lam_init=lam_init),
        grid=(bsz, N_HEADS, nq),
        in_specs=[
            pl.BlockSpec((2 * HEAD_DIM, ATT_TQ), lambda b, h, i: (h, b * nq + i)),
            pl.BlockSpec((1, lp, 2 * HEAD_DIM), lambda b, h, i: (b, 0, h)),
            pl.BlockSpec((1, lp // LANES, 2 * HEAD_DIM, LANES), lambda b, h, i: (b, 0, h, 0)),
            pl.BlockSpec((4, HEAD_DIM), lambda b, h, i: (0, 0)),
            pl.BlockSpec((2 * HEAD_DIM, 1), lambda b, h, i: (0, 0)),
        ],
        out_specs=pl.BlockSpec((ATT_TQ, 2 * HEAD_DIM), lambda b, h, i: (b * nq + i, h)),
        out_shape=jax.ShapeDtypeStruct((bsz * seq, D_MODEL), BF16),
        scratch_shapes=[pltpu.VMEM((2, 1, ATT_TQ), F32), pltpu.VMEM((2, 1, ATT_TQ), F32),
                        pltpu.VMEM((2, 2 * HEAD_DIM, ATT_TQ), F32)],
        compiler_params=_cparams(("arbitrary", "arbitrary", "arbitrary")),
        name="diff_attention",
    )(qt, k, vt4, lam_p, subln_g.reshape(2 * HEAD_DIM, 1))


def _oproj_kernel(h_ref, o_ref, w_ref, out_ref):
    out_ref[...] = h_ref[...] + jnp.dot(o_ref[...], w_ref[...], preferred_element_type=F32)


def _o_proj(h_pad, o, w_out, bsz, seq, lp):
    tiles_real = seq // TILE
    return pl.pallas_call(
        _oproj_kernel,
        grid=(bsz * tiles_real,),
        in_specs=[
            pl.BlockSpec((TILE, D_MODEL), _real_row_map(tiles_real, lp // TILE)),
            pl.BlockSpec((TILE, D_MODEL), lambda i: (i, 0)),
            pl.BlockSpec((D_MODEL, D_MODEL), lambda i: (0, 0)),
        ],
        out_specs=pl.BlockSpec((TILE, D_MODEL), lambda i: (i, 0)),
        out_shape=jax.ShapeDtypeStruct((bsz * seq, D_MODEL), F32),
        compiler_params=_cparams(("arbitrary",)),
        name="o_proj",
    )(h_pad, o, w_out.astype(BF16))


def kernel(x, meta_tokens, a_norm, a_w_in, a_conv, a_w_out, kv_norm, w_kv, b_norm, b_w_q, b_lambda, b_subln, b_w_out, ffn_norm, r_group, r_group_b, r_expert, r_expert_b, e_gate, e_up, e_down, final_norm):
    bsz, seq, d = x.shape
    assert d == D_MODEL and a_norm.shape[0] == 1 and b_norm.shape[0] == 1
    lp = FRONT + seq
    assert seq % ATT_TK == 0 and ATT_TK == 2 * ATT_TQ and ATT_TQ % TILE == 0
    front = jnp.concatenate([jnp.zeros((PAD_ROWS, d), x.dtype), meta_tokens.astype(x.dtype)], axis=0)
    h = _mixer(x.reshape(bsz * seq, d), front, a_norm[0], a_w_in[0], a_conv[0], a_w_out[0], bsz, lp // TILE)
    h = _moe(h, ffn_norm[0], r_group[0], r_group_b[0], r_expert[0], r_expert_b[0],
             e_gate, e_up, e_down, 0, final_norm, TILE, False)

    k, vt4 = _kv(h, kv_norm, w_kv, bsz, lp)
    qt = _q_proj(h, b_norm[0], b_w_q[0], bsz, seq, lp)
    lam_init = 0.8 - 0.6 * math.exp(-0.3 * a_norm.shape[0])
    o = _attention(qt, k, vt4, b_lambda[0], b_subln[0], bsz, seq, lam_init)
    h = _o_proj(h, o, b_w_out[0], bsz, seq, lp)
    h = _moe(h, ffn_norm[1], r_group[1], r_group_b[1], r_expert[1], r_expert_b[1],
             e_gate, e_up, e_down, 1, final_norm, TILE, True)
    return h.reshape(bsz, seq, d)
```

```python
import functools
import math

import jax
import jax.numpy as jnp
from jax import lax
from jax.experimental import pallas as pl
from jax.experimental.pallas import tpu as pltpu

D_MODEL = 1024
CHUNK = 64
N_META = 16
HEAD_DIM = 64
N_HEADS = 8
ROT_HALF = 8
ROPE_THETA = 500000.0
N_GROUPS = 4
EXPERTS_PER_GROUP = 8
N_EXPERTS = 32
D_EXPERT = 512
EPS = 1e-6

LANES = 128
SUBLANES = 8
TILE = 512
FRONT = TILE
PAD_ROWS = FRONT - N_META
META_PAD = LANES - N_META
EXPERT_ROWS = 256
ATT_TQ = 512
ATT_TK = 1024
ATT_SUB = 512
SEG_ALIGN = 16
SEG_SIZES = (512, 256, 128, 64, 32, 16)
TAIL_SIZES = (128, 64, 32, 16)
LOCAL_ROWS = 1536
VMEM_LIMIT = 56 * 1024 * 1024
NEG = float(jnp.finfo(jnp.float32).min)

BF16 = jnp.bfloat16
F32 = jnp.float32


def _rms(x, g):
    return x * lax.rsqrt(jnp.mean(x * x, axis=-1, keepdims=True) + EPS) * g


def _cparams(sem):
    return pltpu.CompilerParams(dimension_semantics=sem, vmem_limit_bytes=VMEM_LIMIT)


def _mixer_kernel(x_ref, front_ref, g_ref, win_ref, cw_ref, wout_ref, o_ref, cu_ref, *, tiles_per_batch):
    i = pl.program_id(0)
    tm = x_ref.shape[0]
    x = jnp.where(i % tiles_per_batch == 0, front_ref[...], x_ref[...])
    hn = _rms(x, g_ref[...]).astype(BF16)
    proj = jnp.dot(hn, win_ref[...], preferred_element_type=F32)
    b_gate = proj[:, :D_MODEL]
    cu = proj[:, D_MODEL:2 * D_MODEL] * proj[:, 2 * D_MODEL:]

    @pl.when(i % tiles_per_batch == 0)
    def _():
        cu_ref[0:SUBLANES, :] = jnp.zeros((SUBLANES, D_MODEL), F32)

    cu_ref[SUBLANES:SUBLANES + tm, :] = cu
    cw = cw_ref[...]
    z = (cw[2:3] * cu + cw[1:2] * cu_ref[SUBLANES - 1:SUBLANES - 1 + tm, :]
         + cw[0:1] * cu_ref[SUBLANES - 2:SUBLANES - 2 + tm, :])
    cu_ref[0:SUBLANES, :] = cu_ref[tm:tm + SUBLANES, :]
    y = jnp.dot((b_gate * z).astype(BF16), wout_ref[...], preferred_element_type=F32)
    o_ref[...] = x + y


def _mixer(x, front, g, w_in, conv_w, w_out, bsz, tiles_per_batch):
    t = bsz * tiles_per_batch * TILE
    tiles_real = tiles_per_batch - FRONT // TILE

    def x_map(i):
        return ((i // tiles_per_batch) * tiles_real + jnp.maximum(i % tiles_per_batch - FRONT // TILE, 0), 0)

    return pl.pallas_call(
        functools.partial(_mixer_kernel, tiles_per_batch=tiles_per_batch),
        grid=(t // TILE,),
        in_specs=[
            pl.BlockSpec((TILE, D_MODEL), x_map),
            pl.BlockSpec((TILE, D_MODEL), lambda i: (0, 0)),
            pl.BlockSpec((1, D_MODEL), lambda i: (0, 0)),
            pl.BlockSpec((D_MODEL, 3 * D_MODEL), lambda i: (0, 0)),
            pl.BlockSpec((3, D_MODEL), lambda i: (0, 0)),
            pl.BlockSpec((D_MODEL, D_MODEL), lambda i: (0, 0)),
        ],
        out_specs=pl.BlockSpec((TILE, D_MODEL), lambda i: (i, 0)),
        out_shape=jax.ShapeDtypeStruct((t, D_MODEL), F32),
        scratch_shapes=[pltpu.VMEM((TILE + 2 * SUBLANES, D_MODEL), F32)],
        compiler_params=_cparams(("arbitrary",)),
        name="mixer_a",
    )(x, front, g.reshape(1, D_MODEL), w_in.astype(BF16), conv_w, w_out.astype(BF16))


ROUTER_ROWS = SUBLANES + N_EXPERTS


def _round_up(v, m):
    return jnp.floor((v + (m - 1)) * (1.0 / m)) * m


def _router_kernel(h_ref, g_ref, wrt_ref, br_ref, tri_ref, sorted_ref, lpos_ref, gate_ref, meta_ref, tot_ref, seg_sc):
    i = pl.program_id(0)
    tm = h_ref.shape[0]

    @pl.when(i == 0)
    def _():
        seg_sc[...] = jnp.zeros_like(seg_sc)

    xn = _rms(h_ref[...], g_ref[...])
    lt = lax.dot_general(wrt_ref[...], xn, (((1,), (1,)), ((), ())),
                         precision=lax.Precision.HIGHEST, preferred_element_type=F32) + br_ref[...]
    lg = lt[0:N_GROUPS]
    gmax = jnp.max(lg, axis=0, keepdims=True)
    iota_g = lax.broadcasted_iota(jnp.int32, lg.shape, 0)
    grp = jnp.min(jnp.where(lg == gmax, iota_g, N_GROUPS), axis=0, keepdims=True)
    p_grp = 1.0 / jnp.sum(jnp.exp(lg - gmax), axis=0, keepdims=True)
    le = lt[SUBLANES:SUBLANES + EXPERTS_PER_GROUP]
    for g in range(1, N_GROUPS):
        lo = SUBLANES + g * EXPERTS_PER_GROUP
        le = jnp.where(grp == g, lt[lo:lo + EXPERTS_PER_GROUP], le)
    iota_e = lax.broadcasted_iota(jnp.int32, le.shape, 0)
    v1 = jnp.max(le, axis=0, keepdims=True)
    i1 = jnp.min(jnp.where(le == v1, iota_e, EXPERTS_PER_GROUP), axis=0, keepdims=True)
    le2 = jnp.where(iota_e == i1, -jnp.inf, le)
    v2 = jnp.max(le2, axis=0, keepdims=True)
    i2 = jnp.min(jnp.where(le2 == v2, iota_e, EXPERTS_PER_GROUP), axis=0, keepdims=True)
    e2 = jnp.exp(v2 - v1)
    den = 1.0 + e2
    gate_ref[...] = jnp.concatenate([p_grp * (1.0 / den), p_grp * (e2 / den)], axis=0)
    experts = (grp * EXPERTS_PER_GROUP + i1, grp * EXPERTS_PER_GROUP + i2)

    iota_x = lax.broadcasted_iota(jnp.int32, (N_EXPERTS, tm), 0)
    hits = [iota_x == e_k for e_k in experts]
    hits_f = [jnp.where(hit, 1.0, 0.0) for hit in hits]
    hits_b = [hf.astype(BF16) for hf in hits_f]
    excl = [jnp.dot(hb, tri_ref[...], preferred_element_type=F32) for hb in hits_b]
    tot_col = [jnp.sum(hf, axis=1, keepdims=True) for hf in hits_f]
    ones = jnp.ones((SUBLANES, tm), BF16)
    n_lane = sum(lax.dot_general(ones, hb, (((1,), (1,)), ((), ())), preferred_element_type=F32) for hb in hits_b)
    seg_len_lane = _round_up(n_lane[0:1], SEG_ALIGN)
    seg_len = _round_up(tot_col[0] + tot_col[1], SEG_ALIGN)
    before = (lax.broadcasted_iota(jnp.int32, (N_EXPERTS, N_EXPERTS), 1)
              < lax.broadcasted_iota(jnp.int32, (N_EXPERTS, N_EXPERTS), 0))
    local_off = jnp.sum(jnp.where(before, seg_len_lane, 0.0), axis=1, keepdims=True)

    lpos_a = jnp.sum(jnp.where(hits[0], excl[0] + local_off, 0.0), axis=0, keepdims=True).astype(jnp.int32)
    lpos_b = jnp.sum(jnp.where(hits[1], excl[1] + tot_col[0] + local_off, 0.0), axis=0, keepdims=True).astype(jnp.int32)
    lpos_ref[...] = jnp.concatenate([lpos_a, lpos_b], axis=0)

    seg_before = seg_sc[...]
    seg_sc[...] = seg_before + seg_len
    lane = lax.broadcasted_iota(jnp.int32, (N_EXPERTS, LANES), 1)
    meta = jnp.where(lane == 0, seg_len, jnp.where(lane == 1, local_off, jnp.where(lane == 2, seg_before, 0.0)))
    meta_ref[0] = meta.astype(jnp.int32)
    tot_ref[...] = jnp.broadcast_to(seg_before + seg_len, tot_ref.shape).astype(jnp.int32)

    row = lax.broadcasted_iota(jnp.int32, (LOCAL_ROWS, tm), 0)
    perm = jnp.where(row == lpos_a, 1.0, jnp.where(row == lpos_b, 1.0, 0.0)).astype(BF16)
    sorted_ref[0] = jnp.dot(perm, xn.astype(BF16), preferred_element_type=F32).astype(BF16)


def _router(h, g, w_rg, b_rg, w_re, b_re, tm):
    t = h.shape[0]
    nt = t // tm
    wrt = jnp.zeros((ROUTER_ROWS, D_MODEL), F32)
    wrt = wrt.at[0:N_GROUPS].set(w_rg.T).at[SUBLANES:].set(w_re.T)
    br = jnp.zeros((ROUTER_ROWS, 1), F32)
    br = br.at[0:N_GROUPS, 0].set(b_rg).at[SUBLANES:, 0].set(b_re)
    tri = (lax.broadcasted_iota(jnp.int32, (tm, tm), 0) < lax.broadcasted_iota(jnp.int32, (tm, tm), 1)).astype(BF16)
    return pl.pallas_call(
        _router_kernel,
        grid=(nt,),
        in_specs=[
            pl.BlockSpec((tm, D_MODEL), lambda i: (i, 0)),
            pl.BlockSpec((1, D_MODEL), lambda i: (0, 0)),
            pl.BlockSpec((ROUTER_ROWS, D_MODEL), lambda i: (0, 0)),
            pl.BlockSpec((ROUTER_ROWS, 1), lambda i: (0, 0)),
            pl.BlockSpec((tm, tm), lambda i: (0, 0)),
        ],
        out_specs=[
            pl.BlockSpec((1, LOCAL_ROWS, D_MODEL), lambda i: (i, 0, 0)),
            pl.BlockSpec((2, tm), lambda i: (0, i)),
            pl.BlockSpec((2, tm), lambda i: (0, i)),
            pl.BlockSpec((1, N_EXPERTS, LANES), lambda i: (i, 0, 0)),
            pl.BlockSpec((N_EXPERTS, LANES), lambda i: (0, 0)),
        ],
        out_shape=[
            jax.ShapeDtypeStruct((nt, LOCAL_ROWS, D_MODEL), BF16),
            jax.ShapeDtypeStruct((2, t), jnp.int32),
            jax.ShapeDtypeStruct((2, t), F32),
            jax.ShapeDtypeStruct((nt, N_EXPERTS, LANES), jnp.int32),
            jax.ShapeDtypeStruct((N_EXPERTS, LANES), jnp.int32),
        ],
        scratch_shapes=[pltpu.VMEM((N_EXPERTS, 1), F32)],
        compiler_params=_cparams(("arbitrary",)),
        name="router",
    )(h, g.reshape(1, D_MODEL), wrt, br, tri)


def _segment_chunks(n, sizes, fn):
    off = 0
    for size in sizes:
        part = n & size

        @pl.when(part != 0)
        def _(off=off, size=size):
            fn(off, size)

        off = off + part


def _rows(ref, start, size):
    return ref.at[pl.ds(pl.multiple_of(start, SEG_ALIGN), size), :]


def _gather_kernel(len_ref, off_ref, dst_ref, zs_ref, zn_ref, nu_ref, sorted_ref, buf_ref, stage, zero_sc,
                   sem_in, sem_out, sem_fill, *, n_blk):
    t = pl.program_id(0)
    nt = pl.num_programs(0)
    slot = t % 2

    def tile_in(t, s):
        return pltpu.make_async_copy(sorted_ref.at[t], stage.at[s], sem_in.at[s])

    def tile_out(t, s, op):
        def body(e, c):
            idx = t * N_EXPERTS + e
            lo, dst = off_ref[idx], dst_ref[idx]
            _segment_chunks(len_ref[idx], SEG_SIZES, lambda off, size: op(pltpu.make_async_copy(
                _rows(stage.at[s], lo + off, size), _rows(buf_ref, dst + off, size), sem_out.at[s])))
            return c

        lax.fori_loop(0, N_EXPERTS, body, 0)

    def fill_copies(op):
        def tails(e, c):
            _segment_chunks(zn_ref[e], TAIL_SIZES, lambda off, size: op(pltpu.make_async_copy(
                zero_sc.at[0:size, :], _rows(buf_ref, zs_ref[e] + off, size), sem_fill)))
            return c

        lax.fori_loop(0, N_EXPERTS, tails, 0)

        def blocks(b, c):
            op(pltpu.make_async_copy(zero_sc, _rows(buf_ref, b * EXPERT_ROWS, EXPERT_ROWS), sem_fill))
            return c

        lax.fori_loop(nu_ref[0], n_blk, blocks, 0)

    start = lambda cp: cp.start()
    wait = lambda cp: cp.wait()

    @pl.when(t == 0)
    def _():
        tile_in(0, 0).start()
        zero_sc[...] = jnp.zeros_like(zero_sc)
        fill_copies(start)

    tile_in(t, slot).wait()

    @pl.when(t > 0)
    def _():
        tile_out(t - 1, 1 - slot, wait)

    @pl.when(t + 1 < nt)
    def _():
        tile_in(t + 1, 1 - slot).start()

    tile_out(t, slot, start)

    @pl.when(t == nt - 1)
    def _():
        tile_out(t, slot, wait)
        fill_copies(wait)


def _gather_segments(sorted_rows, seg_len, seg_off, seg_dst, tail_start, tail_len, n_used, n_blk):
    grid_spec = pltpu.PrefetchScalarGridSpec(
        num_scalar_prefetch=6,
        grid=(sorted_rows.shape[0],),
        in_specs=[pl.BlockSpec(memory_space=pl.ANY)],
        out_specs=pl.BlockSpec(memory_space=pl.ANY),
        scratch_shapes=[pltpu.VMEM((2, LOCAL_ROWS, D_MODEL), BF16), pltpu.VMEM((EXPERT_ROWS, D_MODEL), BF16),
                        pltpu.SemaphoreType.DMA((2,)), pltpu.SemaphoreType.DMA((2,)), pltpu.SemaphoreType.DMA],
    )
    return pl.pallas_call(
        functools.partial(_gather_kernel, n_blk=n_blk),
        grid_spec=grid_spec,
        out_shape=jax.ShapeDtypeStruct((n_blk * EXPERT_ROWS, D_MODEL), BF16),
        compiler_params=_cparams(("arbitrary",)),
        name="gather_segments",
    )(seg_len, seg_off, seg_dst, tail_start, tail_len, n_used, sorted_rows)


def _expert_kernel(be_ref, nu_ref, x_ref, wg_ref, wu_ref, wd_ref, y_ref, wgu_sc, wd_sc):
    i = pl.program_id(0)
    e = be_ref[i]
    prev = be_ref[jnp.maximum(i - 1, 0)]

    @pl.when(jnp.logical_or(i == 0, e != prev))
    def _():
        wgu_sc[:, :D_EXPERT] = wg_ref[0, 0].astype(BF16)
        wgu_sc[:, D_EXPERT:] = wu_ref[0, 0].astype(BF16)
        wd_sc[...] = wd_ref[0, 0].astype(BF16)

    @pl.when(i < nu_ref[0])
    def _():
        gu = jnp.dot(x_ref[...], wgu_sc[...], preferred_element_type=F32)
        a = jax.nn.silu(gu[:, :D_EXPERT]) * gu[:, D_EXPERT:]
        y_ref[...] = jnp.dot(a.astype(BF16), wd_sc[...], preferred_element_type=F32).astype(BF16)

    @pl.when(i >= nu_ref[0])
    def _():
        y_ref[...] = jnp.zeros_like(y_ref)


def _experts(buf, blk_expert, n_used, w_gate, w_up, w_down, layer):
    n_blk = blk_expert.shape[0]

    def xmap(i, be, nu):
        return (jnp.maximum(jnp.minimum(i, nu[0] - 1), 0), 0)

    grid_spec = pltpu.PrefetchScalarGridSpec(
        num_scalar_prefetch=2,
        grid=(n_blk,),
        in_specs=[
            pl.BlockSpec((EXPERT_ROWS, D_MODEL), xmap),
            pl.BlockSpec((1, 1, D_MODEL, D_EXPERT), lambda i, be, nu: (layer, be[i], 0, 0)),
            pl.BlockSpec((1, 1, D_MODEL, D_EXPERT), lambda i, be, nu: (layer, be[i], 0, 0)),
            pl.BlockSpec((1, 1, D_EXPERT, D_MODEL), lambda i, be, nu: (layer, be[i], 0, 0)),
        ],
        out_specs=pl.BlockSpec((EXPERT_ROWS, D_MODEL), lambda i, be, nu: (i, 0)),
        scratch_shapes=[pltpu.VMEM((D_MODEL, 2 * D_EXPERT), BF16), pltpu.VMEM((D_EXPERT, D_MODEL), BF16)],
    )
    return pl.pallas_call(
        _expert_kernel,
        grid_spec=grid_spec,
        out_shape=jax.ShapeDtypeStruct(buf.shape, BF16),
        compiler_params=_cparams(("arbitrary",)),
        name="experts",
    )(blk_expert, n_used, buf, w_gate, w_up, w_down)


def _combine_kernel(len_ref, off_ref, dst_ref, h_ref, gate_ref, lpos_ref, y_ref, fn_ref, o_ref, yl, sem, *, final_norm):
    i = pl.program_id(0)
    nt = pl.num_programs(0)
    tm = h_ref.shape[0]
    slot = i % 2

    def tile_copies(t, s, op):
        def body(e, c):
            idx = t * N_EXPERTS + e
            lo, src = off_ref[idx], dst_ref[idx]
            _segment_chunks(len_ref[idx], SEG_SIZES, lambda off, size: op(pltpu.make_async_copy(
                _rows(y_ref, src + off, size), _rows(yl.at[s], lo + off, size), sem.at[s])))
            return c

        lax.fori_loop(0, N_EXPERTS, body, 0)

    @pl.when(i == 0)
    def _():
        yl[...] = jnp.zeros_like(yl)
        tile_copies(0, 0, lambda cp: cp.start())

    @pl.when(i + 1 < nt)
    def _():
        tile_copies(i + 1, 1 - slot, lambda cp: cp.start())

    tile_copies(i, slot, lambda cp: cp.wait())

    rows = yl[slot]
    lpos = lpos_ref[...]
    col = lax.broadcasted_iota(jnp.int32, (tm, LOCAL_ROWS), 1)
    picked = [jnp.dot(jnp.where(col == lpos[:, k:k + 1], 1.0, 0.0).astype(BF16), rows, preferred_element_type=F32)
              for k in range(2)]
    g = gate_ref[...]
    out = h_ref[...] + (g[:, 0:1] * picked[0] + g[:, 1:2] * picked[1])
    if final_norm:
        out = _rms(out, fn_ref[...])
    o_ref[...] = out


def _combine(h, y, seg_len, seg_off, seg_dst, lpos, gate, fn, tm, final_norm):
    t = h.shape[0]
    grid_spec = pltpu.PrefetchScalarGridSpec(
        num_scalar_prefetch=3,
        grid=(t // tm,),
        in_specs=[
            pl.BlockSpec((tm, D_MODEL), lambda i, *_: (i, 0)),
            pl.BlockSpec((tm, 2), lambda i, *_: (i, 0)),
            pl.BlockSpec((tm, 2), lambda i, *_: (i, 0)),
            pl.BlockSpec(memory_space=pl.ANY),
            pl.BlockSpec((1, D_MODEL), lambda i, *_: (0, 0)),
        ],
        out_specs=pl.BlockSpec((tm, D_MODEL), lambda i, *_: (i, 0)),
        scratch_shapes=[pltpu.VMEM((2, LOCAL_ROWS, D_MODEL), BF16), pltpu.SemaphoreType.DMA((2,))],
    )
    return pl.pallas_call(
        functools.partial(_combine_kernel, final_norm=final_norm),
        grid_spec=grid_spec,
        out_shape=jax.ShapeDtypeStruct((t, D_MODEL), F32),
        compiler_params=_cparams(("arbitrary",)),
        name="combine",
    )(seg_len, seg_off, seg_dst, h, gate.T, lpos.T, y, fn.reshape(1, D_MODEL))


def _moe(h, g, w_rg, b_rg, w_re, b_re, w_gate, w_up, w_down, layer, fn, tm, final_norm):
    t = h.shape[0]
    nt = t // tm
    sorted_rows, lpos, gate, meta, tot = _router(h, g, w_rg, b_rg, w_re, b_re, tm)
    n_blk = (2 * t + nt * N_EXPERTS * (SEG_ALIGN - 1)) // EXPERT_ROWS + N_EXPERTS
    total = tot[:, 0]
    region = (total + EXPERT_ROWS - 1) // EXPERT_ROWS * EXPERT_ROWS
    ends = jnp.cumsum(region)
    starts = ends - region
    seg_len = meta[:, :, 0].reshape(-1)
    seg_off = meta[:, :, 1].reshape(-1)
    seg_dst = (meta[:, :, 2] + starts[None, :]).reshape(-1)
    blk_start = jnp.arange(n_blk, dtype=jnp.int32) * EXPERT_ROWS
    blk_expert = jnp.minimum(jnp.sum(blk_start[:, None] >= ends[None, :], axis=1), N_EXPERTS - 1).astype(jnp.int32)
    n_used = (ends[-1:] // EXPERT_ROWS).astype(jnp.int32)
    buf = _gather_segments(sorted_rows, seg_len, seg_off, seg_dst, starts + total, region - total, n_used, n_blk)
    y = _experts(buf, blk_expert, n_used, w_gate, w_up, w_down, layer)
    return _combine(h, y, seg_len, seg_off, seg_dst, lpos, gate, fn, tm, final_norm)


def _kv_kernel(h_ref, g_ref, wk_ref, wvt_ref, cos_ref, sa_ref, sb_ref, k_ref, vt_ref):
    hn = _rms(h_ref[...], g_ref[...]).astype(BF16)
    k = jnp.dot(hn, wk_ref[...], preferred_element_type=F32)
    cos, sa, sb = cos_ref[...], sa_ref[...], sb_ref[...]
    for c in range(D_MODEL // LANES):
        kc = k[:, c * LANES:(c + 1) * LANES]
        rot = kc * cos + pltpu.roll(kc, LANES - ROT_HALF, 1) * sa + pltpu.roll(kc, ROT_HALF, 1) * sb
        k_ref[0, :, c * LANES:(c + 1) * LANES] = rot.astype(BF16)
    vt = lax.dot_general(wvt_ref[...], hn, (((1,), (1,)), ((), ())), preferred_element_type=F32)
    for c in range(vt_ref.shape[1]):
        vt_ref[0, c] = vt[:, c * LANES:(c + 1) * LANES].astype(BF16)


def _kv(h, g, w_kv, bsz, lp):
    tpb = lp // TILE
    kblk = TILE // LANES
    pos = jnp.maximum(jnp.arange(lp) - PAD_ROWS, 0).astype(F32)
    inv_freq = ROPE_THETA ** (-jnp.arange(ROT_HALF, dtype=F32) * 2.0 / (2 * ROT_HALF))
    ang = pos[:, None] * inv_freq[None, :]
    r = jnp.arange(LANES) % HEAD_DIM
    cos_t = jnp.where(r < 2 * ROT_HALF, jnp.cos(ang)[:, r % ROT_HALF], 1.0)
    sin_t = jnp.sin(ang)[:, r % ROT_HALF]
    sa = jnp.where(r < ROT_HALF, -sin_t, 0.0)
    sb = jnp.where((r >= ROT_HALF) & (r < 2 * ROT_HALF), sin_t, 0.0)
    tab = pl.BlockSpec((TILE, LANES), lambda i: (i % tpb, 0))
    return pl.pallas_call(
        _kv_kernel,
        grid=(bsz * tpb,),
        in_specs=[
            pl.BlockSpec((TILE, D_MODEL), lambda i: (i, 0)),
            pl.BlockSpec((1, D_MODEL), lambda i: (0, 0)),
            pl.BlockSpec((D_MODEL, D_MODEL), lambda i: (0, 0)),
            pl.BlockSpec((D_MODEL, D_MODEL), lambda i: (0, 0)),
            tab, tab, tab,
        ],
        out_specs=[
            pl.BlockSpec((1, TILE, D_MODEL), lambda i: (i // tpb, i % tpb, 0)),
            pl.BlockSpec((1, kblk, D_MODEL, LANES), lambda i: (i // tpb, i % tpb, 0, 0)),
        ],
        out_shape=[
            jax.ShapeDtypeStruct((bsz, lp, D_MODEL), BF16),
            jax.ShapeDtypeStruct((bsz, lp // LANES, D_MODEL, LANES), BF16),
        ],
        compiler_params=_cparams(("arbitrary",)),
        name="kv_proj",
    )(h, g.reshape(1, D_MODEL), w_kv[:, :D_MODEL].astype(BF16), w_kv[:, D_MODEL:].T.astype(BF16), cos_t, sa, sb)


def _q_kernel(h_ref, g_ref, wqt_ref, cos_ref, sin_ref, qt_ref):
    hn = _rms(h_ref[...], g_ref[...]).astype(BF16)
    qt = lax.dot_general(wqt_ref[...], hn, (((1,), (1,)), ((), ())), preferred_element_type=F32)
    cos, sin = cos_ref[...], sin_ref[...]
    scale = HEAD_DIM ** -0.5 * math.log2(math.e)
    for c in range(D_MODEL // HEAD_DIM):
        lo = c * HEAD_DIM
        x1 = qt[lo:lo + ROT_HALF]
        x2 = qt[lo + ROT_HALF:lo + 2 * ROT_HALF]
        blk = jnp.concatenate([x1 * cos - x2 * sin, x2 * cos + x1 * sin, qt[lo + 2 * ROT_HALF:lo + HEAD_DIM]], axis=0)
        qt_ref[lo:lo + HEAD_DIM, :] = (blk * scale).astype(BF16)


def _real_row_map(tiles_real, tiles_padded):
    front = FRONT // TILE
    return lambda i: ((i // tiles_real) * tiles_padded + front + i % tiles_real, 0)


def _q_proj(h_pad, g, w_q, bsz, seq, lp):
    tiles_real = seq // TILE
    pos = (N_META + jnp.arange(seq)).astype(F32)
    inv_freq = ROPE_THETA ** (-jnp.arange(ROT_HALF, dtype=F32) * 2.0 / (2 * ROT_HALF))
    ang = inv_freq[:, None] * pos[None, :]
    tab = pl.BlockSpec((ROT_HALF, TILE), lambda i: (0, i % tiles_real))
    return pl.pallas_call(
        _q_kernel,
        grid=(bsz * tiles_real,),
        in_specs=[
            pl.BlockSpec((TILE, D_MODEL), _real_row_map(tiles_real, lp // TILE)),
            pl.BlockSpec((1, D_MODEL), lambda i: (0, 0)),
            pl.BlockSpec((D_MODEL, D_MODEL), lambda i: (0, 0)),
            tab, tab,
        ],
        out_specs=pl.BlockSpec((D_MODEL, TILE), lambda i: (0, i)),
        out_shape=jax.ShapeDtypeStruct((D_MODEL, bsz * seq), BF16),
        compiler_params=_cparams(("arbitrary",)),
        name="q_proj",
    )(h_pad, g.reshape(1, D_MODEL), w_q.T.astype(BF16), jnp.cos(ang), jnp.sin(ang))


def _attn_kernel(qt_ref, k_ref, vt_ref, lam_ref, g_ref, o_ref, m_sc, l_sc, acc_sc, *, lam_init):
    i = pl.program_id(2)
    tq = qt_ref.shape[1]
    qt = qt_ref[...]
    zero = jnp.zeros((HEAD_DIM, tq), BF16)
    qs = (jnp.concatenate([qt[0:HEAD_DIM], zero], axis=0), jnp.concatenate([zero, qt[HEAD_DIM:]], axis=0))
    m_sc[...] = jnp.full(m_sc.shape, NEG, F32)
    l_sc[...] = jnp.zeros_like(l_sc)
    acc_sc[...] = jnp.zeros_like(acc_sc)

    def tile(row0, nkeys, vis):
        sub = min(nkeys, ATT_SUB)
        blk0 = row0 // LANES
        def k_rows(u):
            start = row0 + u * sub
            return k_ref[0, pl.ds(start if isinstance(start, int) else pl.multiple_of(start, LANES), sub), :]

        scores = [[jnp.dot(k_rows(u), qs[n], preferred_element_type=F32) for n in range(2)]
                  for u in range(nkeys // sub)]
        for u in range(nkeys // sub):
            vtt = jnp.concatenate([vt_ref[0, blk0 + u * (sub // LANES) + c] for c in range(sub // LANES)], axis=1)
            for n in range(2):
                s = scores[u][n]
                if vis is not None:
                    s = jnp.where(vis[u * sub:(u + 1) * sub], s, NEG)
                m_old = m_sc[n]
                m_new = jnp.maximum(m_old, jnp.max(s, axis=0, keepdims=True))
                alpha = jnp.exp2(m_old - m_new)
                p = jnp.exp2(s - m_new)
                l_sc[n] = alpha * l_sc[n] + jnp.sum(p, axis=0, keepdims=True)
                acc_sc[n] = alpha * acc_sc[n] + jnp.dot(vtt, p.astype(BF16), preferred_element_type=F32)
                m_sc[n] = m_new

    def chunk_vis(nkeys, shift):
        r = lax.broadcasted_iota(jnp.int32, (nkeys, tq), 0) // CHUNK
        c = lax.broadcasted_iota(jnp.int32, (nkeys, tq), 1) // CHUNK
        return r <= c + shift

    tile(FRONT - LANES, LANES, lax.broadcasted_iota(jnp.int32, (LANES, tq), 0) >= META_PAD)

    def body(j, c):
        tile(pl.multiple_of(FRONT + j * ATT_TK, LANES), ATT_TK, None)
        return c

    lax.fori_loop(0, i // 2, body, 0)
    base = pl.multiple_of(FRONT + (i // 2) * ATT_TK, LANES)

    @pl.when(i % 2 == 0)
    def _():
        tile(base, tq, chunk_vis(tq, 0))

    @pl.when(i % 2 == 1)
    def _():
        tile(base, ATT_TK, chunk_vis(ATT_TK, tq // CHUNK))

    lp = lam_ref[...]
    lam = (jnp.exp(jnp.sum(lp[0:1] * lp[1:2], axis=1, keepdims=True))
           - jnp.exp(jnp.sum(lp[2:3] * lp[3:4], axis=1, keepdims=True)) + lam_init)
    o = acc_sc[0] / l_sc[0] - lam * (acc_sc[1] / l_sc[1])
    o = o * lax.rsqrt(jnp.mean(o * o, axis=0, keepdims=True) + EPS) * g_ref[...] * (1.0 - lam_init)
    o_ref[...] = o.T.astype(BF16)


def _attention(qt, k, vt4, lam_p, subln_g, bsz, seq, lam_init):
    nq = seq // ATT_TQ
    lp = k.shape[1]
    return pl.pallas_call(
        functools.partial(_attn_kernel, lam_init=lam_init),
        grid=(bsz, N_HEADS, nq),
        in_specs=[
            pl.BlockSpec((2 * HEAD_DIM, ATT_TQ), lambda b, h, i: (h, b * nq + i)),
            pl.BlockSpec((1, lp, 2 * HEAD_DIM), lambda b, h, i: (b, 0, h)),
            pl.BlockSpec((1, lp // LANES, 2 * HEAD_DIM, LANES), lambda b, h, i: (b, 0, h, 0)),
            pl.BlockSpec((4, HEAD_DIM), lambda b, h, i: (0, 0)),
            pl.BlockSpec((2 * HEAD_DIM, 1), lambda b, h, i: (0, 0)),
        ],
        out_specs=pl.BlockSpec((ATT_TQ, 2 * HEAD_DIM), lambda b, h, i: (b * nq + i, h)),
        out_shape=jax.ShapeDtypeStruct((bsz * seq, D_MODEL), BF16),
        scratch_shapes=[pltpu.VMEM((2, 1, ATT_TQ), F32), pltpu.VMEM((2, 1, ATT_TQ), F32),
                        pltpu.VMEM((2, 2 * HEAD_DIM, ATT_TQ), F32)],
        compiler_params=_cparams(("arbitrary", "arbitrary", "arbitrary")),
        name="diff_attention",
    )(qt, k, vt4, lam_p, subln_g.reshape(2 * HEAD_DIM, 1))


def _oproj_kernel(h_ref, o_ref, w_ref, out_ref):
    out_ref[...] = h_ref[...] + jnp.dot(o_ref[...], w_ref[...], preferred_element_type=F32)


def _o_proj(h_pad, o, w_out, bsz, seq, lp):
    tiles_real = seq // TILE
    return pl.pallas_call(
        _oproj_kernel,
        grid=(bsz * tiles_real,),
        in_specs=[
            pl.BlockSpec((TILE, D_MODEL), _real_row_map(tiles_real, lp // TILE)),
            pl.BlockSpec((TILE, D_MODEL), lambda i: (i, 0)),
            pl.BlockSpec((D_MODEL, D_MODEL), lambda i: (0, 0)),
        ],
        out_specs=pl.BlockSpec((TILE, D_MODEL), lambda i: (i, 0)),
        out_shape=jax.ShapeDtypeStruct((bsz * seq, D_MODEL), F32),
        compiler_params=_cparams(("arbitrary",)),
        name="o_proj",
    )(h_pad, o, w_out.astype(BF16))


def kernel(x, meta_tokens, a_norm, a_w_in, a_conv, a_w_out, kv_norm, w_kv, b_norm, b_w_q, b_lambda, b_subln, b_w_out, ffn_norm, r_group, r_group_b, r_expert, r_expert_b, e_gate, e_up, e_down, final_norm):
    bsz, seq, d = x.shape
    assert d == D_MODEL and a_norm.shape[0] == 1 and b_norm.shape[0] == 1
    lp = FRONT + seq
    assert seq % ATT_TK == 0 and ATT_TK == 2 * ATT_TQ and ATT_TQ % TILE == 0
    front = jnp.concatenate([jnp.zeros((PAD_ROWS, d), x.dtype), meta_tokens.astype(x.dtype)], axis=0)
    h = _mixer(x.reshape(bsz * seq, d), front, a_norm[0], a_w_in[0], a_conv[0], a_w_out[0], bsz, lp // TILE)
    h = _moe(h, ffn_norm[0], r_group[0], r_group_b[0], r_expert[0], r_expert_b[0],
             e_gate, e_up, e_down, 0, final_norm, TILE, False)

    k, vt4 = _kv(h, kv_norm, w_kv, bsz, lp)
    qt = _q_proj(h, b_norm[0], b_w_q[0], bsz, seq, lp)
    lam_init = 0.8 - 0.6 * math.exp(-0.3 * a_norm.shape[0])
    o = _attention(qt, k, vt4, b_lambda[0], b_subln[0], bsz, seq, lam_init)
    h = _o_proj(h, o, b_w_out[0], bsz, seq, lp)
    h = _moe(h, ffn_norm[1], r_group[1], r_group_b[1], r_expert[1], r_expert_b[1],
             e_gate, e_up, e_down, 1, final_norm, TILE, True)
    return h.reshape(bsz, seq, d)
```

```python
import functools
import math

import jax
import jax.numpy as jnp
from jax import lax
from jax.experimental import pallas as pl
from jax.experimental.pallas import tpu as pltpu

D_MODEL = 1024
CHUNK = 64
N_META = 16
HEAD_DIM = 64
N_HEADS = 8
ROT_HALF = 8
ROPE_THETA = 500000.0
N_GROUPS = 4
EXPERTS_PER_GROUP = 8
N_EXPERTS = 32
D_EXPERT = 512
EPS = 1e-6

LANES = 128
SUBLANES = 8
TILE = 512
FRONT = TILE
PAD_ROWS = FRONT - N_META
META_PAD = LANES - N_META
EXPERT_ROWS = 512
ATT_TQ = 512
ATT_TK = 1024
ATT_SUB = 512
SEG_ALIGN = 16
SEG_SIZES = (512, 256, 128, 64, 32, 16)
TAIL_SIZES = (256, 128, 64, 32, 16)
RARE_CHUNK = 128
LOCAL_ROWS = 1536
VMEM_LIMIT = 56 * 1024 * 1024
NEG = float(jnp.finfo(jnp.float32).min)

BF16 = jnp.bfloat16
F32 = jnp.float32


def _rms(x, g):
    return x * lax.rsqrt(jnp.mean(x * x, axis=-1, keepdims=True) + EPS) * g


def _cparams(sem):
    return pltpu.CompilerParams(dimension_semantics=sem, vmem_limit_bytes=VMEM_LIMIT)


def _mixer_kernel(x_ref, front_ref, g_ref, win_ref, cw_ref, wout_ref, o_ref, cu_ref, *, tiles_per_batch):
    i = pl.program_id(0)
    tm = x_ref.shape[0]
    x = jnp.where(i % tiles_per_batch == 0, front_ref[...], x_ref[...])
    hn = _rms(x, g_ref[...]).astype(BF16)
    proj = jnp.dot(hn, win_ref[...], preferred_element_type=F32)
    b_gate = proj[:, :D_MODEL]
    cu = proj[:, D_MODEL:2 * D_MODEL] * proj[:, 2 * D_MODEL:]

    @pl.when(i % tiles_per_batch == 0)
    def _():
        cu_ref[0:SUBLANES, :] = jnp.zeros((SUBLANES, D_MODEL), F32)

    cu_ref[SUBLANES:SUBLANES + tm, :] = cu
    cw = cw_ref[...]
    z = (cw[2:3] * cu + cw[1:2] * cu_ref[SUBLANES - 1:SUBLANES - 1 + tm, :]
         + cw[0:1] * cu_ref[SUBLANES - 2:SUBLANES - 2 + tm, :])
    cu_ref[0:SUBLANES, :] = cu_ref[tm:tm + SUBLANES, :]
    y = jnp.dot((b_gate * z).astype(BF16), wout_ref[...], preferred_element_type=F32)
    o_ref[...] = x + y


def _mixer(x, front, g, w_in, conv_w, w_out, bsz, tiles_per_batch):
    t = bsz * tiles_per_batch * TILE
    tiles_real = tiles_per_batch - FRONT // TILE

    def x_map(i):
        return ((i // tiles_per_batch) * tiles_real + jnp.maximum(i % tiles_per_batch - FRONT // TILE, 0), 0)

    return pl.pallas_call(
        functools.partial(_mixer_kernel, tiles_per_batch=tiles_per_batch),
        grid=(t // TILE,),
        in_specs=[
            pl.BlockSpec((TILE, D_MODEL), x_map),
            pl.BlockSpec((TILE, D_MODEL), lambda i: (0, 0)),
            pl.BlockSpec((1, D_MODEL), lambda i: (0, 0)),
            pl.BlockSpec((D_MODEL, 3 * D_MODEL), lambda i: (0, 0)),
            pl.BlockSpec((3, D_MODEL), lambda i: (0, 0)),
            pl.BlockSpec((D_MODEL, D_MODEL), lambda i: (0, 0)),
        ],
        out_specs=pl.BlockSpec((TILE, D_MODEL), lambda i: (i, 0)),
        out_shape=jax.ShapeDtypeStruct((t, D_MODEL), F32),
        scratch_shapes=[pltpu.VMEM((TILE + 2 * SUBLANES, D_MODEL), F32)],
        compiler_params=_cparams(("arbitrary",)),
        name="mixer_a",
    )(x, front, g.reshape(1, D_MODEL), w_in.astype(BF16), conv_w, w_out.astype(BF16))


ROUTER_ROWS = SUBLANES + N_EXPERTS


def _round_up(v, m):
    return jnp.floor((v + (m - 1)) * (1.0 / m)) * m


def _router_kernel(h_ref, g_ref, wrt_ref, br_ref, tri_ref, sorted_ref, lpos_ref, gate_ref, meta_ref, tot_ref, seg_sc):
    i = pl.program_id(0)
    tm = h_ref.shape[0]

    @pl.when(i == 0)
    def _():
        seg_sc[...] = jnp.zeros_like(seg_sc)

    xn = _rms(h_ref[...], g_ref[...])
    lt = lax.dot_general(wrt_ref[...], xn, (((1,), (1,)), ((), ())),
                         precision=lax.Precision.HIGHEST, preferred_element_type=F32) + br_ref[...]
    lg = lt[0:N_GROUPS]
    gmax = jnp.max(lg, axis=0, keepdims=True)
    iota_g = lax.broadcasted_iota(jnp.int32, lg.shape, 0)
    grp = jnp.min(jnp.where(lg == gmax, iota_g, N_GROUPS), axis=0, keepdims=True)
    p_grp = 1.0 / jnp.sum(jnp.exp(lg - gmax), axis=0, keepdims=True)
    le = lt[SUBLANES:SUBLANES + EXPERTS_PER_GROUP]
    for g in range(1, N_GROUPS):
        lo = SUBLANES + g * EXPERTS_PER_GROUP
        le = jnp.where(grp == g, lt[lo:lo + EXPERTS_PER_GROUP], le)
    iota_e = lax.broadcasted_iota(jnp.int32, le.shape, 0)
    v1 = jnp.max(le, axis=0, keepdims=True)
    i1 = jnp.min(jnp.where(le == v1, iota_e, EXPERTS_PER_GROUP), axis=0, keepdims=True)
    le2 = jnp.where(iota_e == i1, -jnp.inf, le)
    v2 = jnp.max(le2, axis=0, keepdims=True)
    i2 = jnp.min(jnp.where(le2 == v2, iota_e, EXPERTS_PER_GROUP), axis=0, keepdims=True)
    e2 = jnp.exp(v2 - v1)
    den = 1.0 + e2
    gate_ref[...] = jnp.concatenate([p_grp * (1.0 / den), p_grp * (e2 / den)], axis=0)
    experts = (grp * EXPERTS_PER_GROUP + i1, grp * EXPERTS_PER_GROUP + i2)

    iota_x = lax.broadcasted_iota(jnp.int32, (N_EXPERTS, tm), 0)
    hits = [iota_x == e_k for e_k in experts]
    hits_f = [jnp.where(hit, 1.0, 0.0) for hit in hits]
    hits_b = [hf.astype(BF16) for hf in hits_f]
    excl = [jnp.dot(hb, tri_ref[...], preferred_element_type=F32) for hb in hits_b]
    tot_col = [jnp.sum(hf, axis=1, keepdims=True) for hf in hits_f]
    ones = jnp.ones((SUBLANES, tm), BF16)
    n_lane = sum(lax.dot_general(ones, hb, (((1,), (1,)), ((), ())), preferred_element_type=F32) for hb in hits_b)
    seg_len_lane = _round_up(n_lane[0:1], SEG_ALIGN)
    seg_len = _round_up(tot_col[0] + tot_col[1], SEG_ALIGN)
    before = (lax.broadcasted_iota(jnp.int32, (N_EXPERTS, N_EXPERTS), 1)
              < lax.broadcasted_iota(jnp.int32, (N_EXPERTS, N_EXPERTS), 0))
    local_off = jnp.sum(jnp.where(before, seg_len_lane, 0.0), axis=1, keepdims=True)

    lpos_a = jnp.sum(jnp.where(hits[0], excl[0] + local_off, 0.0), axis=0, keepdims=True).astype(jnp.int32)
    lpos_b = jnp.sum(jnp.where(hits[1], excl[1] + tot_col[0] + local_off, 0.0), axis=0, keepdims=True).astype(jnp.int32)
    lpos_ref[...] = jnp.concatenate([lpos_a, lpos_b], axis=0)

    seg_before = seg_sc[...]
    seg_sc[...] = seg_before + seg_len
    lane = lax.broadcasted_iota(jnp.int32, (N_EXPERTS, LANES), 1)
    meta = jnp.where(lane == 0, seg_len, jnp.where(lane == 1, local_off, jnp.where(lane == 2, seg_before, 0.0)))
    meta_ref[0] = meta.astype(jnp.int32)
    tot_ref[...] = jnp.broadcast_to(seg_before + seg_len, tot_ref.shape).astype(jnp.int32)

    row = lax.broadcasted_iota(jnp.int32, (LOCAL_ROWS, tm), 0)
    perm = jnp.where(row == lpos_a, 1.0, jnp.where(row == lpos_b, 1.0, 0.0)).astype(BF16)
    sorted_ref[0] = jnp.dot(perm, xn.astype(BF16), preferred_element_type=F32).astype(BF16)


def _router(h, g, w_rg, b_rg, w_re, b_re, tm):
    t = h.shape[0]
    nt = t // tm
    wrt = jnp.zeros((ROUTER_ROWS, D_MODEL), F32)
    wrt = wrt.at[0:N_GROUPS].set(w_rg.T).at[SUBLANES:].set(w_re.T)
    br = jnp.zeros((ROUTER_ROWS, 1), F32)
    br = br.at[0:N_GROUPS, 0].set(b_rg).at[SUBLANES:, 0].set(b_re)
    tri = (lax.broadcasted_iota(jnp.int32, (tm, tm), 0) < lax.broadcasted_iota(jnp.int32, (tm, tm), 1)).astype(BF16)
    return pl.pallas_call(
        _router_kernel,
        grid=(nt,),
        in_specs=[
            pl.BlockSpec((tm, D_MODEL), lambda i: (i, 0)),
            pl.BlockSpec((1, D_MODEL), lambda i: (0, 0)),
            pl.BlockSpec((ROUTER_ROWS, D_MODEL), lambda i: (0, 0)),
            pl.BlockSpec((ROUTER_ROWS, 1), lambda i: (0, 0)),
            pl.BlockSpec((tm, tm), lambda i: (0, 0)),
        ],
        out_specs=[
            pl.BlockSpec((1, LOCAL_ROWS, D_MODEL), lambda i: (i, 0, 0)),
            pl.BlockSpec((2, tm), lambda i: (0, i)),
            pl.BlockSpec((2, tm), lambda i: (0, i)),
            pl.BlockSpec((1, N_EXPERTS, LANES), lambda i: (i, 0, 0)),
            pl.BlockSpec((N_EXPERTS, LANES), lambda i: (0, 0)),
        ],
        out_shape=[
            jax.ShapeDtypeStruct((nt, LOCAL_ROWS, D_MODEL), BF16),
            jax.ShapeDtypeStruct((2, t), jnp.int32),
            jax.ShapeDtypeStruct((2, t), F32),
            jax.ShapeDtypeStruct((nt, N_EXPERTS, LANES), jnp.int32),
            jax.ShapeDtypeStruct((N_EXPERTS, LANES), jnp.int32),
        ],
        scratch_shapes=[pltpu.VMEM((N_EXPERTS, 1), F32)],
        compiler_params=_cparams(("arbitrary",)),
        name="router",
    )(h, g.reshape(1, D_MODEL), wrt, br, tri)


def _segment_chunks(n, sizes, fn):
    def chunks(group, off):
        for size in group:
            part = n & size

            @pl.when(part != 0)
            def _(off=off, size=size):
                fn(off, size)

            off = off + part

    big = [s for s in sizes if s >= RARE_CHUNK]
    if big:
        pl.when(n >= RARE_CHUNK)(lambda: chunks(big, 0))
    chunks([s for s in sizes if s < RARE_CHUNK], n & -RARE_CHUNK)


def _rows(ref, start, size):
    return ref.at[pl.ds(pl.multiple_of(start, SEG_ALIGN), size), :]


def _gather_kernel(len_ref, off_ref, dst_ref, zs_ref, zn_ref, nu_ref, sorted_ref, buf_ref, stage, zero_sc,
                   sem_in, sem_out, sem_fill, *, n_blk):
    t = pl.program_id(0)
    nt = pl.num_programs(0)
    slot = t % 2

    def tile_in(t, s):
        return pltpu.make_async_copy(sorted_ref.at[t], stage.at[s], sem_in.at[s])

    def tile_out(t, s, op):
        def body(e, c):
            idx = t * N_EXPERTS + e
            lo, dst = off_ref[idx], dst_ref[idx]
            _segment_chunks(len_ref[idx], SEG_SIZES, lambda off, size: op(pltpu.make_async_copy(
                _rows(stage.at[s], lo + off, size), _rows(buf_ref, dst + off, size), sem_out.at[s])))
            return c

        lax.fori_loop(0, N_EXPERTS, body, 0)

    def fill_copies(op):
        def tails(e, c):
            _segment_chunks(zn_ref[e], TAIL_SIZES, lambda off, size: op(pltpu.make_async_copy(
                zero_sc.at[0:size, :], _rows(buf_ref, zs_ref[e] + off, size), sem_fill)))
            return c

        lax.fori_loop(0, N_EXPERTS, tails, 0)

        def blocks(b, c):
            op(pltpu.make_async_copy(zero_sc, _rows(buf_ref, b * EXPERT_ROWS, EXPERT_ROWS), sem_fill))
            return c

        lax.fori_loop(nu_ref[0], n_blk, blocks, 0)

    start = lambda cp: cp.start()
    wait = lambda cp: cp.wait()

    @pl.when(t == 0)
    def _():
        tile_in(0, 0).start()
        zero_sc[...] = jnp.zeros_like(zero_sc)
        fill_copies(start)

    tile_in(t, slot).wait()

    @pl.when(t > 0)
    def _():
        tile_out(t - 1, 1 - slot, wait)

    @pl.when(t + 1 < nt)
    def _():
        tile_in(t + 1, 1 - slot).start()

    tile_out(t, slot, start)

    @pl.when(t == nt - 1)
    def _():
        tile_out(t, slot, wait)
        fill_copies(wait)


def _gather_segments(sorted_rows, seg_len, seg_off, seg_dst, tail_start, tail_len, n_used, n_blk):
    grid_spec = pltpu.PrefetchScalarGridSpec(
        num_scalar_prefetch=6,
        grid=(sorted_rows.shape[0],),
        in_specs=[pl.BlockSpec(memory_space=pl.ANY)],
        out_specs=pl.BlockSpec(memory_space=pl.ANY),
        scratch_shapes=[pltpu.VMEM((2, LOCAL_ROWS, D_MODEL), BF16), pltpu.VMEM((EXPERT_ROWS, D_MODEL), BF16),
                        pltpu.SemaphoreType.DMA((2,)), pltpu.SemaphoreType.DMA((2,)), pltpu.SemaphoreType.DMA],
    )
    return pl.pallas_call(
        functools.partial(_gather_kernel, n_blk=n_blk),
        grid_spec=grid_spec,
        out_shape=jax.ShapeDtypeStruct((n_blk * EXPERT_ROWS, D_MODEL), BF16),
        compiler_params=_cparams(("arbitrary",)),
        name="gather_segments",
    )(seg_len, seg_off, seg_dst, tail_start, tail_len, n_used, sorted_rows)


def _expert_kernel(be_ref, nu_ref, x_ref, wg_ref, wu_ref, wd_ref, y_ref, wgu_sc, wd_sc):
    i = pl.program_id(0)
    e = be_ref[i]
    prev = be_ref[jnp.maximum(i - 1, 0)]

    @pl.when(jnp.logical_or(i == 0, e != prev))
    def _():
        wgu_sc[:, :D_EXPERT] = wg_ref[0, 0].astype(BF16)
        wgu_sc[:, D_EXPERT:] = wu_ref[0, 0].astype(BF16)
        wd_sc[...] = wd_ref[0, 0].astype(BF16)

    @pl.when(i < nu_ref[0])
    def _():
        gu = jnp.dot(x_ref[...], wgu_sc[...], preferred_element_type=F32)
        a = jax.nn.silu(gu[:, :D_EXPERT]) * gu[:, D_EXPERT:]
        y_ref[...] = jnp.dot(a.astype(BF16), wd_sc[...], preferred_element_type=F32).astype(BF16)

    @pl.when(i >= nu_ref[0])
    def _():
        y_ref[...] = jnp.zeros_like(y_ref)


def _experts(buf, blk_expert, n_used, w_gate, w_up, w_down, layer):
    n_blk = blk_expert.shape[0]

    def xmap(i, be, nu):
        return (jnp.maximum(jnp.minimum(i, nu[0] - 1), 0), 0)

    grid_spec = pltpu.PrefetchScalarGridSpec(
        num_scalar_prefetch=2,
        grid=(n_blk,),
        in_specs=[
            pl.BlockSpec((EXPERT_ROWS, D_MODEL), xmap),
            pl.BlockSpec((1, 1, D_MODEL, D_EXPERT), lambda i, be, nu: (layer, be[i], 0, 0)),
            pl.BlockSpec((1, 1, D_MODEL, D_EXPERT), lambda i, be, nu: (layer, be[i], 0, 0)),
            pl.BlockSpec((1, 1, D_EXPERT, D_MODEL), lambda i, be, nu: (layer, be[i], 0, 0)),
        ],
        out_specs=pl.BlockSpec((EXPERT_ROWS, D_MODEL), lambda i, be, nu: (i, 0)),
        scratch_shapes=[pltpu.VMEM((D_MODEL, 2 * D_EXPERT), BF16), pltpu.VMEM((D_EXPERT, D_MODEL), BF16)],
    )
    return pl.pallas_call(
        _expert_kernel,
        grid_spec=grid_spec,
        out_shape=jax.ShapeDtypeStruct(buf.shape, BF16),
        compiler_params=_cparams(("arbitrary",)),
        name="experts",
    )(blk_expert, n_used, buf, w_gate, w_up, w_down)


def _combine_kernel(len_ref, off_ref, dst_ref, h_ref, gate_ref, lpos_ref, y_ref, fn_ref, o_ref, yl, sem, *, final_norm):
    i = pl.program_id(0)
    nt = pl.num_programs(0)
    tm = h_ref.shape[0]
    slot = i % 2

    def tile_copies(t, s, op):
        def body(e, c):
            idx = t * N_EXPERTS + e
            lo, src = off_ref[idx], dst_ref[idx]
            _segment_chunks(len_ref[idx], SEG_SIZES, lambda off, size: op(pltpu.make_async_copy(
                _rows(y_ref, src + off, size), _rows(yl.at[s], lo + off, size), sem.at[s])))
            return c

        lax.fori_loop(0, N_EXPERTS, body, 0)

    @pl.when(i == 0)
    def _():
        yl[...] = jnp.zeros_like(yl)
        tile_copies(0, 0, lambda cp: cp.start())

    @pl.when(i + 1 < nt)
    def _():
        tile_copies(i + 1, 1 - slot, lambda cp: cp.start())

    tile_copies(i, slot, lambda cp: cp.wait())

    rows = yl[slot]
    lpos = lpos_ref[...]
    col = lax.broadcasted_iota(jnp.int32, (tm, LOCAL_ROWS), 1)
    picked = [jnp.dot(jnp.where(col == lpos[:, k:k + 1], 1.0, 0.0).astype(BF16), rows, preferred_element_type=F32)
              for k in range(2)]
    g = gate_ref[...]
    out = h_ref[...] + (g[:, 0:1] * picked[0] + g[:, 1:2] * picked[1])
    if final_norm:
        out = _rms(out, fn_ref[...])
    o_ref[...] = out


def _combine(h, y, seg_len, seg_off, seg_dst, lpos, gate, fn, tm, final_norm):
    t = h.shape[0]
    grid_spec = pltpu.PrefetchScalarGridSpec(
        num_scalar_prefetch=3,
        grid=(t // tm,),
        in_specs=[
            pl.BlockSpec((tm, D_MODEL), lambda i, *_: (i, 0)),
            pl.BlockSpec((tm, 2), lambda i, *_: (i, 0)),
            pl.BlockSpec((tm, 2), lambda i, *_: (i, 0)),
            pl.BlockSpec(memory_space=pl.ANY),
            pl.BlockSpec((1, D_MODEL), lambda i, *_: (0, 0)),
        ],
        out_specs=pl.BlockSpec((tm, D_MODEL), lambda i, *_: (i, 0)),
        scratch_shapes=[pltpu.VMEM((2, LOCAL_ROWS, D_MODEL), BF16), pltpu.SemaphoreType.DMA((2,))],
    )
    return pl.pallas_call(
        functools.partial(_combine_kernel, final_norm=final_norm),
        grid_spec=grid_spec,
        out_shape=jax.ShapeDtypeStruct((t, D_MODEL), F32),
        compiler_params=_cparams(("arbitrary",)),
        name="combine",
    )(seg_len, seg_off, seg_dst, h, gate.T, lpos.T, y, fn.reshape(1, D_MODEL))


def _moe(h, g, w_rg, b_rg, w_re, b_re, w_gate, w_up, w_down, layer, fn, tm, final_norm):
    t = h.shape[0]
    nt = t // tm
    sorted_rows, lpos, gate, meta, tot = _router(h, g, w_rg, b_rg, w_re, b_re, tm)
    n_blk = (2 * t + nt * N_EXPERTS * (SEG_ALIGN - 1)) // EXPERT_ROWS + N_EXPERTS
    total = tot[:, 0]
    region = (total + EXPERT_ROWS - 1) // EXPERT_ROWS * EXPERT_ROWS
    ends = jnp.cumsum(region)
    starts = ends - region
    seg_len = meta[:, :, 0].reshape(-1)
    seg_off = meta[:, :, 1].reshape(-1)
    seg_dst = (meta[:, :, 2] + starts[None, :]).reshape(-1)
    blk_start = jnp.arange(n_blk, dtype=jnp.int32) * EXPERT_ROWS
    blk_expert = jnp.minimum(jnp.sum(blk_start[:, None] >= ends[None, :], axis=1), N_EXPERTS - 1).astype(jnp.int32)
    n_used = (ends[-1:] // EXPERT_ROWS).astype(jnp.int32)
    buf = _gather_segments(sorted_rows, seg_len, seg_off, seg_dst, starts + total, region - total, n_used, n_blk)
    y = _experts(buf, blk_expert, n_used, w_gate, w_up, w_down, layer)
    return _combine(h, y, seg_len, seg_off, seg_dst, lpos, gate, fn, tm, final_norm)


def _kv_kernel(h_ref, g_ref, wk_ref, wvt_ref, cos_ref, sa_ref, sb_ref, k_ref, vt_ref):
    hn = _rms(h_ref[...], g_ref[...]).astype(BF16)
    k = jnp.dot(hn, wk_ref[...], preferred_element_type=F32)
    cos, sa, sb = cos_ref[...], sa_ref[...], sb_ref[...]
    for c in range(D_MODEL // LANES):
        kc = k[:, c * LANES:(c + 1) * LANES]
        rot = kc * cos + pltpu.roll(kc, LANES - ROT_HALF, 1) * sa + pltpu.roll(kc, ROT_HALF, 1) * sb
        k_ref[0, :, c * LANES:(c + 1) * LANES] = rot.astype(BF16)
    vt = lax.dot_general(wvt_ref[...], hn, (((1,), (1,)), ((), ())), preferred_element_type=F32)
    for c in range(vt_ref.shape[1]):
        vt_ref[0, c] = vt[:, c * LANES:(c + 1) * LANES].astype(BF16)


def _kv(h, g, w_kv, bsz, lp):
    tpb = lp // TILE
    kblk = TILE // LANES
    pos = jnp.maximum(jnp.arange(lp) - PAD_ROWS, 0).astype(F32)
    inv_freq = ROPE_THETA ** (-jnp.arange(ROT_HALF, dtype=F32) * 2.0 / (2 * ROT_HALF))
    ang = pos[:, None] * inv_freq[None, :]
    r = jnp.arange(LANES) % HEAD_DIM
    cos_t = jnp.where(r < 2 * ROT_HALF, jnp.cos(ang)[:, r % ROT_HALF], 1.0)
    sin_t = jnp.sin(ang)[:, r % ROT_HALF]
    sa = jnp.where(r < ROT_HALF, -sin_t, 0.0)
    sb = jnp.where((r >= ROT_HALF) & (r < 2 * ROT_HALF), sin_t, 0.0)
    tab = pl.BlockSpec((TILE, LANES), lambda i: (i % tpb, 0))
    return pl.pallas_call(
        _kv_kernel,
        grid=(bsz * tpb,),
        in_specs=[
            pl.BlockSpec((TILE, D_MODEL), lambda i: (i, 0)),
            pl.BlockSpec((1, D_MODEL), lambda i: (0, 0)),
            pl.BlockSpec((D_MODEL, D_MODEL), lambda i: (0, 0)),
            pl.BlockSpec((D_MODEL, D_MODEL), lambda i: (0, 0)),
            tab, tab, tab,
        ],
        out_specs=[
            pl.BlockSpec((1, TILE, D_MODEL), lambda i: (i // tpb, i % tpb, 0)),
            pl.BlockSpec((1, kblk, D_MODEL, LANES), lambda i: (i // tpb, i % tpb, 0, 0)),
        ],
        out_shape=[
            jax.ShapeDtypeStruct((bsz, lp, D_MODEL), BF16),
            jax.ShapeDtypeStruct((bsz, lp // LANES, D_MODEL, LANES), BF16),
        ],
        compiler_params=_cparams(("arbitrary",)),
        name="kv_proj",
    )(h, g.reshape(1, D_MODEL), w_kv[:, :D_MODEL].astype(BF16), w_kv[:, D_MODEL:].T.astype(BF16), cos_t, sa, sb)


def _q_kernel(h_ref, g_ref, wqt_ref, cos_ref, sin_ref, qt_ref):
    hn = _rms(h_ref[...], g_ref[...]).astype(BF16)
    qt = lax.dot_general(wqt_ref[...], hn, (((1,), (1,)), ((), ())), preferred_element_type=F32)
    cos, sin = cos_ref[...], sin_ref[...]
    scale = HEAD_DIM ** -0.5 * math.log2(math.e)
    for c in range(D_MODEL // HEAD_DIM):
        lo = c * HEAD_DIM
        x1 = qt[lo:lo + ROT_HALF]
        x2 = qt[lo + ROT_HALF:lo + 2 * ROT_HALF]
        blk = jnp.concatenate([x1 * cos - x2 * sin, x2 * cos + x1 * sin, qt[lo + 2 * ROT_HALF:lo + HEAD_DIM]], axis=0)
        qt_ref[lo:lo + HEAD_DIM, :] = (blk * scale).astype(BF16)


def _real_row_map(tiles_real, tiles_padded):
    front = FRONT // TILE
    return lambda i: ((i // tiles_real) * tiles_padded + front + i % tiles_real, 0)


def _q_proj(h_pad, g, w_q, bsz, seq, lp):
    tiles_real = seq // TILE
    pos = (N_META + jnp.arange(seq)).astype(F32)
    inv_freq = ROPE_THETA ** (-jnp.arange(ROT_HALF, dtype=F32) * 2.0 / (2 * ROT_HALF))
    ang = inv_freq[:, None] * pos[None, :]
    tab = pl.BlockSpec((ROT_HALF, TILE), lambda i: (0, i % tiles_real))
    return pl.pallas_call(
        _q_kernel,
        grid=(bsz * tiles_real,),
        in_specs=[
            pl.BlockSpec((TILE, D_MODEL), _real_row_map(tiles_real, lp // TILE)),
            pl.BlockSpec((1, D_MODEL), lambda i: (0, 0)),
            pl.BlockSpec((D_MODEL, D_MODEL), lambda i: (0, 0)),
            tab, tab,
        ],
        out_specs=pl.BlockSpec((D_MODEL, TILE), lambda i: (0, i)),
        out_shape=jax.ShapeDtypeStruct((D_MODEL, bsz * seq), BF16),
        compiler_params=_cparams(("arbitrary",)),
        name="q_proj",
    )(h_pad, g.reshape(1, D_MODEL), w_q.T.astype(BF16), jnp.cos(ang), jnp.sin(ang))


def _attn_kernel(qt_ref, k_ref, vt_ref, lam_ref, g_ref, o_ref, m_sc, l_sc, acc_sc, sa_sc, sb_sc, *, lam_init):
    i = pl.program_id(2)
    tq = qt_ref.shape[1]
    qt = qt_ref[...]
    zero = jnp.zeros((HEAD_DIM, tq), BF16)
    qs = (jnp.concatenate([qt[0:HEAD_DIM], zero], axis=0), jnp.concatenate([zero, qt[HEAD_DIM:]], axis=0))
    m_sc[...] = jnp.full(m_sc.shape, NEG, F32)
    l_sc[...] = jnp.zeros_like(l_sc)
    acc_sc[...] = jnp.zeros_like(acc_sc)

    def update(n, s, vtt):
        m_old = m_sc[n]
        m_new = jnp.maximum(m_old, jnp.max(s, axis=0, keepdims=True))
        alpha = jnp.exp2(m_old - m_new)
        p = jnp.exp2(s - m_new)
        l_sc[n] = alpha * l_sc[n] + jnp.sum(p, axis=0, keepdims=True)
        acc_sc[n] = alpha * acc_sc[n] + jnp.dot(vtt, p.astype(BF16), preferred_element_type=F32)
        m_sc[n] = m_new

    def values(row0, nkeys):
        blk0 = row0 // LANES
        return jnp.concatenate([vt_ref[0, blk0 + c] for c in range(nkeys // LANES)], axis=1)

    def key_row(t):
        return pl.multiple_of(FRONT + t * ATT_TK, LANES)

    n_sub = ATT_TK // ATT_SUB

    def scores_into(t, buf):
        for u in range(n_sub):
            kt = k_ref[0, pl.ds(pl.multiple_of(key_row(t) + u * ATT_SUB, LANES), ATT_SUB), :]
            for n in range(2):
                buf[u * 2 + n] = jnp.dot(kt, qs[n], preferred_element_type=F32)

    def consume(t, buf, vis):
        for u in range(n_sub):
            vtt = values(key_row(t) + u * ATT_SUB, ATT_SUB)
            for n in range(2):
                s = buf[u * 2 + n]
                if vis is not None:
                    s = jnp.where(vis[u * ATT_SUB:(u + 1) * ATT_SUB], s, NEG)
                update(n, s, vtt)

    meta_vis = lax.broadcasted_iota(jnp.int32, (LANES, tq), 0) >= META_PAD
    kt = k_ref[0, FRONT - LANES:FRONT, :]
    for n in range(2):
        update(n, jnp.where(meta_vis, jnp.dot(kt, qs[n], preferred_element_type=F32), NEG), values(FRONT - LANES, LANES))

    n_full = i // 2
    odd = n_full % 2

    @pl.when(odd == 1)
    def _():
        scores_into(0, sb_sc)
        consume(0, sb_sc, None)

    scores_into(odd, sa_sc)

    def body(j, c):
        t0 = odd + 2 * j
        scores_into(t0 + 1, sb_sc)
        consume(t0, sa_sc, None)
        scores_into(t0 + 2, sa_sc)
        consume(t0 + 1, sb_sc, None)
        return c

    lax.fori_loop(0, n_full // 2, body, 0)
    r = lax.broadcasted_iota(jnp.int32, (ATT_TK, tq), 0) // CHUNK
    c = lax.broadcasted_iota(jnp.int32, (ATT_TK, tq), 1) // CHUNK
    consume(n_full, sa_sc, r <= c + (i % 2) * (tq // CHUNK))

    lp = lam_ref[...]
    lam = (jnp.exp(jnp.sum(lp[0:1] * lp[1:2], axis=1, keepdims=True))
           - jnp.exp(jnp.sum(lp[2:3] * lp[3:4], axis=1, keepdims=True)) + lam_init)
    o = acc_sc[0] / l_sc[0] - lam * (acc_sc[1] / l_sc[1])
    o = o * lax.rsqrt(jnp.mean(o * o, axis=0, keepdims=True) + EPS) * g_ref[...] * (1.0 - lam_init)
    o_ref[...] = o.T.astype(BF16)


def _attention(qt, k, vt4, lam_p, subln_g, bsz, seq, lam_init):
    nq = seq // ATT_TQ
    lp = k.shape[1]
    return pl.pallas_call(
        functools.partial(_attn_kernel, lam_init=lam_init),
        grid=(bsz, N_HEADS, nq),
        in_specs=[
            pl.BlockSpec((2 * HEAD_DIM, ATT_TQ), lambda b, h, i: (h, b * nq + i)),
            pl.BlockSpec((1, lp, 2 * HEAD_DIM), lambda b, h, i: (b, 0, h)),
            pl.BlockSpec((1, lp // LANES, 2 * HEAD_DIM, LANES), lambda b, h, i: (b, 0, h, 0)),
            pl.BlockSpec((4, HEAD_DIM), lambda b, h, i: (0, 0)),
            pl.BlockSpec((2 * HEAD_DIM, 1), lambda b, h, i: (0, 0)),
        ],
        out_specs=pl.BlockSpec((ATT_TQ, 2 * HEAD_DIM), lambda b, h, i: (b * nq + i, h)),
        out_shape=jax.ShapeDtypeStruct((bsz * seq, D_MODEL), BF16),
        scratch_shapes=[pltpu.VMEM((2, 1, ATT_TQ), F32), pltpu.VMEM((2, 1, ATT_TQ), F32),
                        pltpu.VMEM((2, 2 * HEAD_DIM, ATT_TQ), F32),
                        pltpu.VMEM((2 * ATT_TK // ATT_SUB, ATT_SUB, ATT_TQ), F32),
                        pltpu.VMEM((2 * ATT_TK // ATT_SUB, ATT_SUB, ATT_TQ), F32)],
        compiler_params=_cparams(("arbitrary", "arbitrary", "arbitrary")),
        name="diff_attention",
    )(qt, k, vt4, lam_p, subln_g.reshape(2 * HEAD_DIM, 1))


def _oproj_kernel(h_ref, o_ref, w_ref, out_ref):
    out_ref[...] = h_ref[...] + jnp.dot(o_ref[...], w_ref[...], preferred_element_type=F32)


def _o_proj(h_pad, o, w_out, bsz, seq, lp):
    tiles_real = seq // TILE
    return pl.pallas_call(
        _oproj_kernel,
        grid=(bsz * tiles_real,),
        in_specs=[
            pl.BlockSpec((TILE, D_MODEL), _real_row_map(tiles_real, lp // TILE)),
            pl.BlockSpec((TILE, D_MODEL), lambda i: (i, 0)),
            pl.BlockSpec((D_MODEL, D_MODEL), lambda i: (0, 0)),
        ],
        out_specs=pl.BlockSpec((TILE, D_MODEL), lambda i: (i, 0)),
        out_shape=jax.ShapeDtypeStruct((bsz * seq, D_MODEL), F32),
        compiler_params=_cparams(("arbitrary",)),
        name="o_proj",
    )(h_pad, o, w_out.astype(BF16))


def kernel(x, meta_tokens, a_norm, a_w_in, a_conv, a_w_out, kv_norm, w_kv, b_norm, b_w_q, b_lambda, b_subln, b_w_out, ffn_norm, r_group, r_group_b, r_expert, r_expert_b, e_gate, e_up, e_down, final_norm):
    bsz, seq, d = x.shape
    assert d == D_MODEL and a_norm.shape[0] == 1 and b_norm.shape[0] == 1
    lp = FRONT + seq
    assert seq % ATT_TK == 0 and ATT_TK == 2 * ATT_TQ and ATT_TQ % TILE == 0
    front = jnp.concatenate([jnp.zeros((PAD_ROWS, d), x.dtype), meta_tokens.astype(x.dtype)], axis=0)
    h = _mixer(x.reshape(bsz * seq, d), front, a_norm[0], a_w_in[0], a_conv[0], a_w_out[0], bsz, lp // TILE)
    h = _moe(h, ffn_norm[0], r_group[0], r_group_b[0], r_expert[0], r_expert_b[0],
             e_gate, e_up, e_down, 0, final_norm, TILE, False)

    k, vt4 = _kv(h, kv_norm, w_kv, bsz, lp)
    qt = _q_proj(h, b_norm[0], b_w_q[0], bsz, seq, lp)
    lam_init = 0.8 - 0.6 * math.exp(-0.3 * a_norm.shape[0])
    o = _attention(qt, k, vt4, b_lambda[0], b_subln[0], bsz, seq, lam_init)
    h = _o_proj(h, o, b_w_out[0], bsz, seq, lp)
    h = _moe(h, ffn_norm[1], r_group[1], r_group_b[1], r_expert[1], r_expert_b[1],
             e_gate, e_up, e_down, 1, final_norm, TILE, True)
    return h.reshape(bsz, seq, d)
```

```python
import functools
import math

import jax
import jax.numpy as jnp
from jax import lax
from jax.experimental import pallas as pl
from jax.experimental.pallas import tpu as pltpu

D_MODEL = 1024
CHUNK = 64
N_META = 16
HEAD_DIM = 64
N_HEADS = 8
ROT_HALF = 8
ROPE_THETA = 500000.0
N_GROUPS = 4
EXPERTS_PER_GROUP = 8
N_EXPERTS = 32
D_EXPERT = 512
EPS = 1e-6

LANES = 128
SUBLANES = 8
TILE = 512
FRONT = TILE
PAD_ROWS = FRONT - N_META
META_PAD = LANES - N_META
EXPERT_ROWS = 512
ATT_TQ = 1024
ATT_TK = 1024
ATT_SUB = 512
SEG_ALIGN = 16
LOCAL_ROWS = 1536
VMEM_LIMIT = 56 * 1024 * 1024
NEG = float(jnp.finfo(jnp.float32).min)

BF16 = jnp.bfloat16
F32 = jnp.float32


def _rms(x, g):
    return x * lax.rsqrt(jnp.mean(x * x, axis=-1, keepdims=True) + EPS) * g


def _cparams(sem):
    return pltpu.CompilerParams(dimension_semantics=sem, vmem_limit_bytes=VMEM_LIMIT)


def _mixer_kernel(x_ref, front_ref, g_ref, win_ref, cw_ref, wout_ref, o_ref, cu_ref, *, tiles_per_batch):
    i = pl.program_id(0)
    tm = x_ref.shape[0]
    x = jnp.where(i % tiles_per_batch == 0, front_ref[...], x_ref[...])
    hn = _rms(x, g_ref[...]).astype(BF16)
    proj = jnp.dot(hn, win_ref[...], preferred_element_type=F32)
    b_gate = proj[:, :D_MODEL]
    cu = proj[:, D_MODEL:2 * D_MODEL] * proj[:, 2 * D_MODEL:]

    @pl.when(i % tiles_per_batch == 0)
    def _():
        cu_ref[0:SUBLANES, :] = jnp.zeros((SUBLANES, D_MODEL), F32)

    cu_ref[SUBLANES:SUBLANES + tm, :] = cu
    cw = cw_ref[...]
    z = (cw[2:3] * cu + cw[1:2] * cu_ref[SUBLANES - 1:SUBLANES - 1 + tm, :]
         + cw[0:1] * cu_ref[SUBLANES - 2:SUBLANES - 2 + tm, :])
    cu_ref[0:SUBLANES, :] = cu_ref[tm:tm + SUBLANES, :]
    y = jnp.dot((b_gate * z).astype(BF16), wout_ref[...], preferred_element_type=F32)
    o_ref[...] = x + y


def _mixer(x, front, g, w_in, conv_w, w_out, bsz, tiles_per_batch):
    t = bsz * tiles_per_batch * TILE
    tiles_real = tiles_per_batch - FRONT // TILE

    def x_map(i):
        return ((i // tiles_per_batch) * tiles_real + jnp.maximum(i % tiles_per_batch - FRONT // TILE, 0), 0)

    return pl.pallas_call(
        functools.partial(_mixer_kernel, tiles_per_batch=tiles_per_batch),
        grid=(t // TILE,),
        in_specs=[
            pl.BlockSpec((TILE, D_MODEL), x_map),
            pl.BlockSpec((TILE, D_MODEL), lambda i: (0, 0)),
            pl.BlockSpec((1, D_MODEL), lambda i: (0, 0)),
            pl.BlockSpec((D_MODEL, 3 * D_MODEL), lambda i: (0, 0)),
            pl.BlockSpec((3, D_MODEL), lambda i: (0, 0)),
            pl.BlockSpec((D_MODEL, D_MODEL), lambda i: (0, 0)),
        ],
        out_specs=pl.BlockSpec((TILE, D_MODEL), lambda i: (i, 0)),
        out_shape=jax.ShapeDtypeStruct((t, D_MODEL), F32),
        scratch_shapes=[pltpu.VMEM((TILE + 2 * SUBLANES, D_MODEL), F32)],
        compiler_params=_cparams(("arbitrary",)),
        name="mixer_a",
    )(x, front, g.reshape(1, D_MODEL), w_in.astype(BF16), conv_w, w_out.astype(BF16))


ROUTER_ROWS = SUBLANES + N_EXPERTS


def _round_up(v, m):
    return jnp.floor((v + (m - 1)) * (1.0 / m)) * m


def _router_kernel(h_ref, g_ref, wrt_ref, br_ref, tri_ref, sorted_ref, lpos_ref, gate_ref, meta_ref, tot_ref, seg_sc):
    i = pl.program_id(0)
    tm = h_ref.shape[0]

    @pl.when(i == 0)
    def _():
        seg_sc[...] = jnp.zeros_like(seg_sc)

    xn = _rms(h_ref[...], g_ref[...])
    lt = lax.dot_general(wrt_ref[...], xn, (((1,), (1,)), ((), ())),
                         precision=lax.Precision.HIGHEST, preferred_element_type=F32) + br_ref[...]
    lg = lt[0:N_GROUPS]
    gmax = jnp.max(lg, axis=0, keepdims=True)
    iota_g = lax.broadcasted_iota(jnp.int32, lg.shape, 0)
    grp = jnp.min(jnp.where(lg == gmax, iota_g, N_GROUPS), axis=0, keepdims=True)
    p_grp = 1.0 / jnp.sum(jnp.exp(lg - gmax), axis=0, keepdims=True)
    le = lt[SUBLANES:SUBLANES + EXPERTS_PER_GROUP]
    for g in range(1, N_GROUPS):
        lo = SUBLANES + g * EXPERTS_PER_GROUP
        le = jnp.where(grp == g, lt[lo:lo + EXPERTS_PER_GROUP], le)
    iota_e = lax.broadcasted_iota(jnp.int32, le.shape, 0)
    v1 = jnp.max(le, axis=0, keepdims=True)
    i1 = jnp.min(jnp.where(le == v1, iota_e, EXPERTS_PER_GROUP), axis=0, keepdims=True)
    le2 = jnp.where(iota_e == i1, -jnp.inf, le)
    v2 = jnp.max(le2, axis=0, keepdims=True)
    i2 = jnp.min(jnp.where(le2 == v2, iota_e, EXPERTS_PER_GROUP), axis=0, keepdims=True)
    e2 = jnp.exp(v2 - v1)
    den = 1.0 + e2
    gate_ref[...] = jnp.concatenate([p_grp * (1.0 / den), p_grp * (e2 / den)], axis=0)
    experts = (grp * EXPERTS_PER_GROUP + i1, grp * EXPERTS_PER_GROUP + i2)

    iota_x = lax.broadcasted_iota(jnp.int32, (N_EXPERTS, tm), 0)
    hits = [iota_x == e_k for e_k in experts]
    hits_f = [jnp.where(hit, 1.0, 0.0) for hit in hits]
    hits_b = [hf.astype(BF16) for hf in hits_f]
    excl = [jnp.dot(hb, tri_ref[...], preferred_element_type=F32) for hb in hits_b]
    tot_col = [jnp.sum(hf, axis=1, keepdims=True) for hf in hits_f]
    ones = jnp.ones((SUBLANES, tm), BF16)
    n_lane = sum(lax.dot_general(ones, hb, (((1,), (1,)), ((), ())), preferred_element_type=F32) for hb in hits_b)
    seg_len_lane = _round_up(n_lane[0:1], SEG_ALIGN)
    seg_len = _round_up(tot_col[0] + tot_col[1], SEG_ALIGN)
    before = (lax.broadcasted_iota(jnp.int32, (N_EXPERTS, N_EXPERTS), 1)
              < lax.broadcasted_iota(jnp.int32, (N_EXPERTS, N_EXPERTS), 0))
    local_off = jnp.sum(jnp.where(before, seg_len_lane, 0.0), axis=1, keepdims=True)

    lpos_a = jnp.sum(jnp.where(hits[0], excl[0] + local_off, 0.0), axis=0, keepdims=True).astype(jnp.int32)
    lpos_b = jnp.sum(jnp.where(hits[1], excl[1] + tot_col[0] + local_off, 0.0), axis=0, keepdims=True).astype(jnp.int32)
    lpos_ref[...] = jnp.concatenate([lpos_a, lpos_b], axis=0)

    seg_before = seg_sc[...]
    seg_sc[...] = seg_before + seg_len
    lane = lax.broadcasted_iota(jnp.int32, (N_EXPERTS, LANES), 1)
    meta = jnp.where(lane == 0, seg_len, jnp.where(lane == 1, local_off, jnp.where(lane == 2, seg_before, 0.0)))
    meta_ref[0] = meta.astype(jnp.int32)
    tot_ref[...] = jnp.broadcast_to(seg_before + seg_len, tot_ref.shape).astype(jnp.int32)

    row = lax.broadcasted_iota(jnp.int32, (LOCAL_ROWS, tm), 0)
    perm = jnp.where(row == lpos_a, 1.0, jnp.where(row == lpos_b, 1.0, 0.0)).astype(BF16)
    sorted_ref[0] = jnp.dot(perm, xn.astype(BF16), preferred_element_type=F32).astype(BF16)


def _router(h, g, w_rg, b_rg, w_re, b_re, tm):
    t = h.shape[0]
    nt = t // tm
    wrt = jnp.zeros((ROUTER_ROWS, D_MODEL), F32)
    wrt = wrt.at[0:N_GROUPS].set(w_rg.T).at[SUBLANES:].set(w_re.T)
    br = jnp.zeros((ROUTER_ROWS, 1), F32)
    br = br.at[0:N_GROUPS, 0].set(b_rg).at[SUBLANES:, 0].set(b_re)
    tri = (lax.broadcasted_iota(jnp.int32, (tm, tm), 0) < lax.broadcasted_iota(jnp.int32, (tm, tm), 1)).astype(BF16)
    return pl.pallas_call(
        _router_kernel,
        grid=(nt,),
        in_specs=[
            pl.BlockSpec((tm, D_MODEL), lambda i: (i, 0)),
            pl.BlockSpec((1, D_MODEL), lambda i: (0, 0)),
            pl.BlockSpec((ROUTER_ROWS, D_MODEL), lambda i: (0, 0)),
            pl.BlockSpec((ROUTER_ROWS, 1), lambda i: (0, 0)),
            pl.BlockSpec((tm, tm), lambda i: (0, 0)),
        ],
        out_specs=[
            pl.BlockSpec((1, LOCAL_ROWS, D_MODEL), lambda i: (i, 0, 0)),
            pl.BlockSpec((2, tm), lambda i: (0, i)),
            pl.BlockSpec((2, tm), lambda i: (0, i)),
            pl.BlockSpec((1, N_EXPERTS, LANES), lambda i: (i, 0, 0)),
            pl.BlockSpec((N_EXPERTS, LANES), lambda i: (0, 0)),
        ],
        out_shape=[
            jax.ShapeDtypeStruct((nt, LOCAL_ROWS, D_MODEL), BF16),
            jax.ShapeDtypeStruct((2, t), jnp.int32),
            jax.ShapeDtypeStruct((2, t), F32),
            jax.ShapeDtypeStruct((nt, N_EXPERTS, LANES), jnp.int32),
            jax.ShapeDtypeStruct((N_EXPERTS, LANES), jnp.int32),
        ],
        scratch_shapes=[pltpu.VMEM((N_EXPERTS, 1), F32)],
        compiler_params=_cparams(("arbitrary",)),
        name="router",
    )(h, g.reshape(1, D_MODEL), wrt, br, tri)


def _if_rows(n, fn):
    pl.when(n > 0)(lambda: fn(pl.multiple_of(n, SEG_ALIGN)))


def _rows(ref, start, size):
    return ref.at[pl.ds(pl.multiple_of(start, SEG_ALIGN), size), :]


def _gather_kernel(len_ref, off_ref, dst_ref, zs_ref, zn_ref, nu_ref, sorted_ref, buf_ref, stage, zero_sc,
                   sem_in, sem_out, sem_fill, *, n_blk):
    t = pl.program_id(0)
    nt = pl.num_programs(0)
    slot = t % 2

    def tile_in(t, s):
        return pltpu.make_async_copy(sorted_ref.at[t], stage.at[s], sem_in.at[s])

    def tile_out(t, s, op):
        def body(e, c):
            idx = t * N_EXPERTS + e
            lo, dst = off_ref[idx], dst_ref[idx]
            _if_rows(len_ref[idx], lambda n: op(pltpu.make_async_copy(
                _rows(stage.at[s], lo, n), _rows(buf_ref, dst, n), sem_out.at[s])))
            return c

        lax.fori_loop(0, N_EXPERTS, body, 0)

    def fill_copies(op):
        def tails(e, c):
            _if_rows(zn_ref[e], lambda n: op(pltpu.make_async_copy(
                zero_sc.at[pl.ds(0, n), :], _rows(buf_ref, zs_ref[e], n), sem_fill)))
            return c

        lax.fori_loop(0, N_EXPERTS, tails, 0)

        def blocks(b, c):
            op(pltpu.make_async_copy(zero_sc, _rows(buf_ref, b * EXPERT_ROWS, EXPERT_ROWS), sem_fill))
            return c

        lax.fori_loop(nu_ref[0], n_blk, blocks, 0)

    start = lambda cp: cp.start()
    wait = lambda cp: cp.wait()

    @pl.when(t == 0)
    def _():
        tile_in(0, 0).start()
        zero_sc[...] = jnp.zeros_like(zero_sc)
        fill_copies(start)

    tile_in(t, slot).wait()

    @pl.when(t > 0)
    def _():
        tile_out(t - 1, 1 - slot, wait)

    @pl.when(t + 1 < nt)
    def _():
        tile_in(t + 1, 1 - slot).start()

    tile_out(t, slot, start)

    @pl.when(t == nt - 1)
    def _():
        tile_out(t, slot, wait)
        fill_copies(wait)


def _gather_segments(sorted_rows, seg_len, seg_off, seg_dst, tail_start, tail_len, n_used, n_blk):
    grid_spec = pltpu.PrefetchScalarGridSpec(
        num_scalar_prefetch=6,
        grid=(sorted_rows.shape[0],),
        in_specs=[pl.BlockSpec(memory_space=pl.ANY)],
        out_specs=pl.BlockSpec(memory_space=pl.ANY),
        scratch_shapes=[pltpu.VMEM((2, LOCAL_ROWS, D_MODEL), BF16), pltpu.VMEM((EXPERT_ROWS, D_MODEL), BF16),
                        pltpu.SemaphoreType.DMA((2,)), pltpu.SemaphoreType.DMA((2,)), pltpu.SemaphoreType.DMA],
    )
    return pl.pallas_call(
        functools.partial(_gather_kernel, n_blk=n_blk),
        grid_spec=grid_spec,
        out_shape=jax.ShapeDtypeStruct((n_blk * EXPERT_ROWS, D_MODEL), BF16),
        compiler_params=_cparams(("arbitrary",)),
        name="gather_segments",
    )(seg_len, seg_off, seg_dst, tail_start, tail_len, n_used, sorted_rows)


def _expert_kernel(be_ref, nu_ref, x_ref, wg_ref, wu_ref, wd_ref, y_ref, wgu_sc, wd_sc):
    i = pl.program_id(0)
    e = be_ref[i]
    prev = be_ref[jnp.maximum(i - 1, 0)]

    @pl.when(jnp.logical_or(i == 0, e != prev))
    def _():
        wgu_sc[:, :D_EXPERT] = wg_ref[0, 0].astype(BF16)
        wgu_sc[:, D_EXPERT:] = wu_ref[0, 0].astype(BF16)
        wd_sc[...] = wd_ref[0, 0].astype(BF16)

    @pl.when(i < nu_ref[0])
    def _():
        gu = jnp.dot(x_ref[...], wgu_sc[...], preferred_element_type=F32)
        a = jax.nn.silu(gu[:, :D_EXPERT]) * gu[:, D_EXPERT:]
        y_ref[...] = jnp.dot(a.astype(BF16), wd_sc[...], preferred_element_type=F32).astype(BF16)

    @pl.when(i >= nu_ref[0])
    def _():
        y_ref[...] = jnp.zeros_like(y_ref)


def _experts(buf, blk_expert, n_used, w_gate, w_up, w_down, layer):
    n_blk = blk_expert.shape[0]

    def xmap(i, be, nu):
        return (jnp.maximum(jnp.minimum(i, nu[0] - 1), 0), 0)

    grid_spec = pltpu.PrefetchScalarGridSpec(
        num_scalar_prefetch=2,
        grid=(n_blk,),
        in_specs=[
            pl.BlockSpec((EXPERT_ROWS, D_MODEL), xmap),
            pl.BlockSpec((1, 1, D_MODEL, D_EXPERT), lambda i, be, nu: (layer, be[i], 0, 0)),
            pl.BlockSpec((1, 1, D_MODEL, D_EXPERT), lambda i, be, nu: (layer, be[i], 0, 0)),
            pl.BlockSpec((1, 1, D_EXPERT, D_MODEL), lambda i, be, nu: (layer, be[i], 0, 0)),
        ],
        out_specs=pl.BlockSpec((EXPERT_ROWS, D_MODEL), lambda i, be, nu: (i, 0)),
        scratch_shapes=[pltpu.VMEM((D_MODEL, 2 * D_EXPERT), BF16), pltpu.VMEM((D_EXPERT, D_MODEL), BF16)],
    )
    return pl.pallas_call(
        _expert_kernel,
        grid_spec=grid_spec,
        out_shape=jax.ShapeDtypeStruct(buf.shape, BF16),
        compiler_params=_cparams(("arbitrary",)),
        name="experts",
    )(blk_expert, n_used, buf, w_gate, w_up, w_down)


def _combine_kernel(len_ref, off_ref, dst_ref, h_ref, gate_ref, lpos_ref, y_ref, fn_ref, o_ref, yl, sem, *, final_norm):
    i = pl.program_id(0)
    nt = pl.num_programs(0)
    tm = h_ref.shape[0]
    slot = i % 2

    def tile_copies(t, s, op):
        def body(e, c):
            idx = t * N_EXPERTS + e
            lo, src = off_ref[idx], dst_ref[idx]
            _if_rows(len_ref[idx], lambda n: op(pltpu.make_async_copy(
                _rows(y_ref, src, n), _rows(yl.at[s], lo, n), sem.at[s])))
            return c

        lax.fori_loop(0, N_EXPERTS, body, 0)

    @pl.when(i == 0)
    def _():
        yl[...] = jnp.zeros_like(yl)
        tile_copies(0, 0, lambda cp: cp.start())

    @pl.when(i + 1 < nt)
    def _():
        tile_copies(i + 1, 1 - slot, lambda cp: cp.start())

    tile_copies(i, slot, lambda cp: cp.wait())

    rows = yl[slot]
    lpos = lpos_ref[...]
    col = lax.broadcasted_iota(jnp.int32, (tm, LOCAL_ROWS), 1)
    picked = [jnp.dot(jnp.where(col == lpos[:, k:k + 1], 1.0, 0.0).astype(BF16), rows, preferred_element_type=F32)
              for k in range(2)]
    g = gate_ref[...]
    out = h_ref[...] + (g[:, 0:1] * picked[0] + g[:, 1:2] * picked[1])
    if final_norm:
        out = _rms(out, fn_ref[...])
    o_ref[...] = out


def _combine(h, y, seg_len, seg_off, seg_dst, lpos, gate, fn, tm, final_norm):
    t = h.shape[0]
    grid_spec = pltpu.PrefetchScalarGridSpec(
        num_scalar_prefetch=3,
        grid=(t // tm,),
        in_specs=[
            pl.BlockSpec((tm, D_MODEL), lambda i, *_: (i, 0)),
            pl.BlockSpec((tm, 2), lambda i, *_: (i, 0)),
            pl.BlockSpec((tm, 2), lambda i, *_: (i, 0)),
            pl.BlockSpec(memory_space=pl.ANY),
            pl.BlockSpec((1, D_MODEL), lambda i, *_: (0, 0)),
        ],
        out_specs=pl.BlockSpec((tm, D_MODEL), lambda i, *_: (i, 0)),
        scratch_shapes=[pltpu.VMEM((2, LOCAL_ROWS, D_MODEL), BF16), pltpu.SemaphoreType.DMA((2,))],
    )
    return pl.pallas_call(
        functools.partial(_combine_kernel, final_norm=final_norm),
        grid_spec=grid_spec,
        out_shape=jax.ShapeDtypeStruct((t, D_MODEL), F32),
        compiler_params=_cparams(("arbitrary",)),
        name="combine",
    )(seg_len, seg_off, seg_dst, h, gate.T, lpos.T, y, fn.reshape(1, D_MODEL))


def _moe(h, g, w_rg, b_rg, w_re, b_re, w_gate, w_up, w_down, layer, fn, tm, final_norm):
    t = h.shape[0]
    nt = t // tm
    sorted_rows, lpos, gate, meta, tot = _router(h, g, w_rg, b_rg, w_re, b_re, tm)
    n_blk = (2 * t + nt * N_EXPERTS * (SEG_ALIGN - 1)) // EXPERT_ROWS + N_EXPERTS
    total = tot[:, 0]
    region = (total + EXPERT_ROWS - 1) // EXPERT_ROWS * EXPERT_ROWS
    ends = jnp.cumsum(region)
    starts = ends - region
    seg_len = meta[:, :, 0].reshape(-1)
    seg_off = meta[:, :, 1].reshape(-1)
    seg_dst = (meta[:, :, 2] + starts[None, :]).reshape(-1)
    blk_start = jnp.arange(n_blk, dtype=jnp.int32) * EXPERT_ROWS
    blk_expert = jnp.minimum(jnp.sum(blk_start[:, None] >= ends[None, :], axis=1), N_EXPERTS - 1).astype(jnp.int32)
    n_used = (ends[-1:] // EXPERT_ROWS).astype(jnp.int32)
    buf = _gather_segments(sorted_rows, seg_len, seg_off, seg_dst, starts + total, region - total, n_used, n_blk)
    y = _experts(buf, blk_expert, n_used, w_gate, w_up, w_down, layer)
    return _combine(h, y, seg_len, seg_off, seg_dst, lpos, gate, fn, tm, final_norm)


def _kv_kernel(h_ref, g_ref, wk_ref, wvt_ref, cos_ref, sa_ref, sb_ref, k_ref, vt_ref):
    hn = _rms(h_ref[...], g_ref[...]).astype(BF16)
    k = jnp.dot(hn, wk_ref[...], preferred_element_type=F32)
    cos, sa, sb = cos_ref[...], sa_ref[...], sb_ref[...]
    for c in range(D_MODEL // LANES):
        kc = k[:, c * LANES:(c + 1) * LANES]
        rot = kc * cos + pltpu.roll(kc, LANES - ROT_HALF, 1) * sa + pltpu.roll(kc, ROT_HALF, 1) * sb
        k_ref[0, :, c * LANES:(c + 1) * LANES] = rot.astype(BF16)
    vt = lax.dot_general(wvt_ref[...], hn, (((1,), (1,)), ((), ())), preferred_element_type=F32)
    for c in range(vt_ref.shape[1]):
        vt_ref[0, c] = vt[:, c * LANES:(c + 1) * LANES].astype(BF16)


def _kv(h, g, w_kv, bsz, lp):
    tpb = lp // TILE
    kblk = TILE // LANES
    pos = jnp.maximum(jnp.arange(lp) - PAD_ROWS, 0).astype(F32)
    inv_freq = ROPE_THETA ** (-jnp.arange(ROT_HALF, dtype=F32) * 2.0 / (2 * ROT_HALF))
    ang = pos[:, None] * inv_freq[None, :]
    r = jnp.arange(LANES) % HEAD_DIM
    cos_t = jnp.where(r < 2 * ROT_HALF, jnp.cos(ang)[:, r % ROT_HALF], 1.0)
    sin_t = jnp.sin(ang)[:, r % ROT_HALF]
    sa = jnp.where(r < ROT_HALF, -sin_t, 0.0)
    sb = jnp.where((r >= ROT_HALF) & (r < 2 * ROT_HALF), sin_t, 0.0)
    tab = pl.BlockSpec((TILE, LANES), lambda i: (i % tpb, 0))
    return pl.pallas_call(
        _kv_kernel,
        grid=(bsz * tpb,),
        in_specs=[
            pl.BlockSpec((TILE, D_MODEL), lambda i: (i, 0)),
            pl.BlockSpec((1, D_MODEL), lambda i: (0, 0)),
            pl.BlockSpec((D_MODEL, D_MODEL), lambda i: (0, 0)),
            pl.BlockSpec((D_MODEL, D_MODEL), lambda i: (0, 0)),
            tab, tab, tab,
        ],
        out_specs=[
            pl.BlockSpec((1, TILE, D_MODEL), lambda i: (i // tpb, i % tpb, 0)),
            pl.BlockSpec((1, kblk, D_MODEL, LANES), lambda i: (i // tpb, i % tpb, 0, 0)),
        ],
        out_shape=[
            jax.ShapeDtypeStruct((bsz, lp, D_MODEL), BF16),
            jax.ShapeDtypeStruct((bsz, lp // LANES, D_MODEL, LANES), BF16),
        ],
        compiler_params=_cparams(("arbitrary",)),
        name="kv_proj",
    )(h, g.reshape(1, D_MODEL), w_kv[:, :D_MODEL].astype(BF16), w_kv[:, D_MODEL:].T.astype(BF16), cos_t, sa, sb)


def _q_kernel(h_ref, g_ref, wqt_ref, cos_ref, sin_ref, qt_ref):
    hn = _rms(h_ref[...], g_ref[...]).astype(BF16)
    qt = lax.dot_general(wqt_ref[...], hn, (((1,), (1,)), ((), ())), preferred_element_type=F32)
    cos, sin = cos_ref[...], sin_ref[...]
    scale = HEAD_DIM ** -0.5 * math.log2(math.e)
    for c in range(D_MODEL // HEAD_DIM):
        lo = c * HEAD_DIM
        x1 = qt[lo:lo + ROT_HALF]
        x2 = qt[lo + ROT_HALF:lo + 2 * ROT_HALF]
        blk = jnp.concatenate([x1 * cos - x2 * sin, x2 * cos + x1 * sin, qt[lo + 2 * ROT_HALF:lo + HEAD_DIM]], axis=0)
        qt_ref[0, lo:lo + HEAD_DIM, :] = (blk * scale).astype(BF16)


def _real_row_map(tiles_real, tiles_padded):
    front = FRONT // TILE
    return lambda i: ((i // tiles_real) * tiles_padded + front + i % tiles_real, 0)


def _q_proj(h_pad, g, w_q, bsz, seq, lp):
    tiles_real = seq // TILE
    per_q = ATT_TQ // TILE
    pos = (N_META + jnp.arange(seq)).astype(F32)
    inv_freq = ROPE_THETA ** (-jnp.arange(ROT_HALF, dtype=F32) * 2.0 / (2 * ROT_HALF))
    ang = inv_freq[:, None] * pos[None, :]
    tab = pl.BlockSpec((ROT_HALF, TILE), lambda i: (0, i % tiles_real))
    return pl.pallas_call(
        _q_kernel,
        grid=(bsz * tiles_real,),
        in_specs=[
            pl.BlockSpec((TILE, D_MODEL), _real_row_map(tiles_real, lp // TILE)),
            pl.BlockSpec((1, D_MODEL), lambda i: (0, 0)),
            pl.BlockSpec((D_MODEL, D_MODEL), lambda i: (0, 0)),
            tab, tab,
        ],
        out_specs=pl.BlockSpec((1, D_MODEL, TILE), lambda i: (i // per_q, 0, i % per_q)),
        out_shape=jax.ShapeDtypeStruct((bsz * seq // ATT_TQ, D_MODEL, ATT_TQ), BF16),
        compiler_params=_cparams(("arbitrary",)),
        name="q_proj",
    )(h_pad, g.reshape(1, D_MODEL), w_q.T.astype(BF16), jnp.cos(ang), jnp.sin(ang))


def _attn_schedule(nq):
    steps = [(i, t, int(t == i)) for i in range(nq) for t in range(i + 1)]
    return [jnp.asarray(col, jnp.int32) for col in zip(*steps)]


def _attn_kernel(qi_ref, kt_ref, last_ref, qt_ref, k_ref, vt_ref, lam_ref, g_ref, o_ref, m_sc, l_sc, acc_sc, sa_sc, sb_sc,
                 *, lam_init, n_steps):
    tq = ATT_TQ
    zero = jnp.zeros((HEAD_DIM, tq), BF16)
    n_sub = ATT_TK // ATT_SUB

    def q_maps(s):
        qt = qt_ref[0, qi_ref[s]]
        return (jnp.concatenate([qt[0:HEAD_DIM], zero], axis=0), jnp.concatenate([zero, qt[HEAD_DIM:]], axis=0))

    def update(n, s, vtt):
        m_old = m_sc[n]
        m_new = jnp.maximum(m_old, jnp.max(s, axis=0, keepdims=True))
        alpha = jnp.exp2(m_old - m_new)
        p = jnp.exp2(s - m_new)
        l_sc[n] = alpha * l_sc[n] + jnp.sum(p, axis=0, keepdims=True)
        acc_sc[n] = alpha * acc_sc[n] + jnp.dot(vtt, p.astype(BF16), preferred_element_type=F32)
        m_sc[n] = m_new

    def values(row0, nkeys):
        blk0 = row0 // LANES
        return jnp.concatenate([vt_ref[0, blk0 + c] for c in range(nkeys // LANES)], axis=1)

    def key_row(s, u):
        return pl.multiple_of(FRONT + kt_ref[s] * ATT_TK + u * ATT_SUB, LANES)

    def scores_into(s, buf):
        qs = q_maps(s)
        for u in range(n_sub):
            kt = k_ref[0, pl.ds(key_row(s, u), ATT_SUB), :]
            for n in range(2):
                buf[u * 2 + n] = jnp.dot(kt, qs[n], preferred_element_type=F32)

    def consume(s, buf, vis):
        for u in range(n_sub):
            vtt = values(key_row(s, u), ATT_SUB)
            for n in range(2):
                sc = buf[u * 2 + n]
                if vis is not None:
                    sc = jnp.where(vis[u * ATT_SUB:(u + 1) * ATT_SUB], sc, NEG)
                update(n, sc, vtt)

    def finish(s):
        qs = q_maps(s)
        meta_vis = lax.broadcasted_iota(jnp.int32, (LANES, tq), 0) >= META_PAD
        kt = k_ref[0, FRONT - LANES:FRONT, :]
        for n in range(2):
            update(n, jnp.where(meta_vis, jnp.dot(kt, qs[n], preferred_element_type=F32), NEG), values(FRONT - LANES, LANES))
        lp = lam_ref[...]
        lam = (jnp.exp(jnp.sum(lp[0:1] * lp[1:2], axis=1, keepdims=True))
               - jnp.exp(jnp.sum(lp[2:3] * lp[3:4], axis=1, keepdims=True)) + lam_init)
        o = acc_sc[0] / l_sc[0] - lam * (acc_sc[1] / l_sc[1])
        o = o * lax.rsqrt(jnp.mean(o * o, axis=0, keepdims=True) + EPS) * g_ref[...] * (1.0 - lam_init)
        o_ref[pl.ds(pl.multiple_of(qi_ref[s] * tq, tq), tq), :] = o.T.astype(BF16)

    def half_step(s, cur, nxt):
        @pl.when(kt_ref[s] == 0)
        def _():
            m_sc[...] = jnp.full(m_sc.shape, NEG, F32)
            l_sc[...] = jnp.zeros_like(l_sc)
            acc_sc[...] = jnp.zeros_like(acc_sc)

        nxt_s = jnp.minimum(s + 1, n_steps - 1)

        @pl.when(last_ref[s] == 0)
        def _():
            scores_into(nxt_s, nxt)
            consume(s, cur, None)

        @pl.when(last_ref[s] == 1)
        def _():
            scores_into(nxt_s, nxt)
            r = lax.broadcasted_iota(jnp.int32, (ATT_TK, tq), 0) // CHUNK
            c = lax.broadcasted_iota(jnp.int32, (ATT_TK, tq), 1) // CHUNK
            consume(s, cur, r <= c)
            finish(s)

    scores_into(0, sa_sc)

    def body(p, c):
        half_step(2 * p, sa_sc, sb_sc)
        half_step(2 * p + 1, sb_sc, sa_sc)
        return c

    lax.fori_loop(0, n_steps // 2, body, 0)
    if n_steps % 2:
        half_step(n_steps - 1, sa_sc, sb_sc)


def _attention(qt, k, vt4, lam_p, subln_g, bsz, seq, lam_init):
    nq = seq // ATT_TQ
    lp = k.shape[1]
    q_idx, k_idx, last = _attn_schedule(nq)
    n_steps = int(q_idx.shape[0])
    grid_spec = pltpu.PrefetchScalarGridSpec(
        num_scalar_prefetch=3,
        grid=(bsz, N_HEADS),
        in_specs=[
            pl.BlockSpec((1, nq, 2 * HEAD_DIM, ATT_TQ), lambda b, h, *_: (b, 0, h, 0)),
            pl.BlockSpec((1, lp, 2 * HEAD_DIM), lambda b, h, *_: (b, 0, h)),
            pl.BlockSpec((1, lp // LANES, 2 * HEAD_DIM, LANES), lambda b, h, *_: (b, 0, h, 0)),
            pl.BlockSpec((4, HEAD_DIM), lambda b, h, *_: (0, 0)),
            pl.BlockSpec((2 * HEAD_DIM, 1), lambda b, h, *_: (0, 0)),
        ],
        out_specs=pl.BlockSpec((seq, 2 * HEAD_DIM), lambda b, h, *_: (b, h)),
        scratch_shapes=[pltpu.VMEM((2, 1, ATT_TQ), F32), pltpu.VMEM((2, 1, ATT_TQ), F32),
                        pltpu.VMEM((2, 2 * HEAD_DIM, ATT_TQ), F32),
                        pltpu.VMEM((2 * ATT_TK // ATT_SUB, ATT_SUB, ATT_TQ), F32),
                        pltpu.VMEM((2 * ATT_TK // ATT_SUB, ATT_SUB, ATT_TQ), F32)],
    )
    return pl.pallas_call(
        functools.partial(_attn_kernel, lam_init=lam_init, n_steps=n_steps),
        grid_spec=grid_spec,
        out_shape=jax.ShapeDtypeStruct((bsz * seq, D_MODEL), BF16),
        compiler_params=_cparams(("arbitrary", "arbitrary")),
        name="diff_attention",
    )(q_idx, k_idx, last, qt.reshape(bsz, nq, D_MODEL, ATT_TQ), k, vt4, lam_p, subln_g.reshape(2 * HEAD_DIM, 1))


def _oproj_kernel(h_ref, o_ref, w_ref, out_ref):
    out_ref[...] = h_ref[...] + jnp.dot(o_ref[...], w_ref[...], preferred_element_type=F32)


def _o_proj(h_pad, o, w_out, bsz, seq, lp):
    tiles_real = seq // TILE
    return pl.pallas_call(
        _oproj_kernel,
        grid=(bsz * tiles_real,),
        in_specs=[
            pl.BlockSpec((TILE, D_MODEL), _real_row_map(tiles_real, lp // TILE)),
            pl.BlockSpec((TILE, D_MODEL), lambda i: (i, 0)),
            pl.BlockSpec((D_MODEL, D_MODEL), lambda i: (0, 0)),
        ],
        out_specs=pl.BlockSpec((TILE, D_MODEL), lambda i: (i, 0)),
        out_shape=jax.ShapeDtypeStruct((bsz * seq, D_MODEL), F32),
        compiler_params=_cparams(("arbitrary",)),
        name="o_proj",
    )(h_pad, o, w_out.astype(BF16))


def kernel(x, meta_tokens, a_norm, a_w_in, a_conv, a_w_out, kv_norm, w_kv, b_norm, b_w_q, b_lambda, b_subln, b_w_out, ffn_norm, r_group, r_group_b, r_expert, r_expert_b, e_gate, e_up, e_down, final_norm):
    bsz, seq, d = x.shape
    assert d == D_MODEL and a_norm.shape[0] == 1 and b_norm.shape[0] == 1
    lp = FRONT + seq
    assert seq % ATT_TK == 0 and ATT_TK == ATT_TQ and ATT_TQ % TILE == 0
    front = jnp.concatenate([jnp.zeros((PAD_ROWS, d), x.dtype), meta_tokens.astype(x.dtype)], axis=0)
    h = _mixer(x.reshape(bsz * seq, d), front, a_norm[0], a_w_in[0], a_conv[0], a_w_out[0], bsz, lp // TILE)
    h = _moe(h, ffn_norm[0], r_group[0], r_group_b[0], r_expert[0], r_expert_b[0],
             e_gate, e_up, e_down, 0, final_norm, TILE, False)

    k, vt4 = _kv(h, kv_norm, w_kv, bsz, lp)
    qt = _q_proj(h, b_norm[0], b_w_q[0], bsz, seq, lp)
    lam_init = 0.8 - 0.6 * math.exp(-0.3 * a_norm.shape[0])
    o = _attention(qt, k, vt4, b_lambda[0], b_subln[0], bsz, seq, lam_init)
    h = _o_proj(h, o, b_w_out[0], bsz, seq, lp)
    h = _moe(h, ffn_norm[1], r_group[1], r_group_b[1], r_expert[1], r_expert_b[1],
             e_gate, e_up, e_down, 1, final_norm, TILE, True)
    return h.reshape(bsz, seq, d)
```

```python
import functools
import math

import jax
import jax.numpy as jnp
from jax import lax
from jax.experimental import pallas as pl
from jax.experimental.pallas import tpu as pltpu

D_MODEL = 1024
CHUNK = 64
N_META = 16
HEAD_DIM = 64
N_HEADS = 8
ROT_HALF = 8
ROPE_THETA = 500000.0
N_GROUPS = 4
EXPERTS_PER_GROUP = 8
N_EXPERTS = 32
D_EXPERT = 512
EPS = 1e-6

LANES = 128
SUBLANES = 8
TILE = 512
FRONT = TILE
PAD_ROWS = FRONT - N_META
META_PAD = LANES - N_META
EXPERT_ROWS = 512
ATT_TQ = 512
ATT_TK = 1024
ATT_SUB = 1024
SEG_ALIGN = 16
STAGE_SLOTS = 3
LOCAL_ROWS = 1536
VMEM_LIMIT = 56 * 1024 * 1024
NEG = float(jnp.finfo(jnp.float32).min)

BF16 = jnp.bfloat16
F32 = jnp.float32


def _rms(x, g):
    return x * lax.rsqrt(jnp.mean(x * x, axis=-1, keepdims=True) + EPS) * g


def _cparams(sem):
    return pltpu.CompilerParams(dimension_semantics=sem, vmem_limit_bytes=VMEM_LIMIT)


def _mixer_kernel(x_ref, front_ref, g_ref, win_ref, cw_ref, wout_ref, o_ref, cu_ref, *, tiles_per_batch):
    i = pl.program_id(0)
    tm = x_ref.shape[0]
    x = jnp.where(i % tiles_per_batch == 0, front_ref[...], x_ref[...])
    hn = _rms(x, g_ref[...]).astype(BF16)
    proj = jnp.dot(hn, win_ref[...], preferred_element_type=F32)
    b_gate = proj[:, :D_MODEL]
    cu = proj[:, D_MODEL:2 * D_MODEL] * proj[:, 2 * D_MODEL:]

    @pl.when(i % tiles_per_batch == 0)
    def _():
        cu_ref[0:SUBLANES, :] = jnp.zeros((SUBLANES, D_MODEL), F32)

    cu_ref[SUBLANES:SUBLANES + tm, :] = cu
    cw = cw_ref[...]
    z = (cw[2:3] * cu + cw[1:2] * cu_ref[SUBLANES - 1:SUBLANES - 1 + tm, :]
         + cw[0:1] * cu_ref[SUBLANES - 2:SUBLANES - 2 + tm, :])
    cu_ref[0:SUBLANES, :] = cu_ref[tm:tm + SUBLANES, :]
    y = jnp.dot((b_gate * z).astype(BF16), wout_ref[...], preferred_element_type=F32)
    o_ref[...] = x + y


def _mixer(x, front, g, w_in, conv_w, w_out, bsz, tiles_per_batch):
    t = bsz * tiles_per_batch * TILE
    tiles_real = tiles_per_batch - FRONT // TILE

    def x_map(i):
        return ((i // tiles_per_batch) * tiles_real + jnp.maximum(i % tiles_per_batch - FRONT // TILE, 0), 0)

    return pl.pallas_call(
        functools.partial(_mixer_kernel, tiles_per_batch=tiles_per_batch),
        grid=(t // TILE,),
        in_specs=[
            pl.BlockSpec((TILE, D_MODEL), x_map),
            pl.BlockSpec((TILE, D_MODEL), lambda i: (0, 0)),
            pl.BlockSpec((1, D_MODEL), lambda i: (0, 0)),
            pl.BlockSpec((D_MODEL, 3 * D_MODEL), lambda i: (0, 0)),
            pl.BlockSpec((3, D_MODEL), lambda i: (0, 0)),
            pl.BlockSpec((D_MODEL, D_MODEL), lambda i: (0, 0)),
        ],
        out_specs=pl.BlockSpec((TILE, D_MODEL), lambda i: (i, 0)),
        out_shape=jax.ShapeDtypeStruct((t, D_MODEL), F32),
        scratch_shapes=[pltpu.VMEM((TILE + 2 * SUBLANES, D_MODEL), F32)],
        compiler_params=_cparams(("arbitrary",)),
        name="mixer_a",
    )(x, front, g.reshape(1, D_MODEL), w_in.astype(BF16), conv_w, w_out.astype(BF16))


ROUTER_ROWS = SUBLANES + N_EXPERTS


def _round_up(v, m):
    return jnp.floor((v + (m - 1)) * (1.0 / m)) * m


def _router_kernel(h_ref, g_ref, wrt_ref, br_ref, tri_ref, sorted_ref, lpos_ref, gate_ref, meta_ref, tot_ref, seg_sc):
    i = pl.program_id(0)
    tm = h_ref.shape[0]

    @pl.when(i == 0)
    def _():
        seg_sc[...] = jnp.zeros_like(seg_sc)

    xn = _rms(h_ref[...], g_ref[...])
    lt = lax.dot_general(wrt_ref[...], xn, (((1,), (1,)), ((), ())),
                         precision=lax.Precision.HIGHEST, preferred_element_type=F32) + br_ref[...]
    lg = lt[0:N_GROUPS]
    gmax = jnp.max(lg, axis=0, keepdims=True)
    iota_g = lax.broadcasted_iota(jnp.int32, lg.shape, 0)
    grp = jnp.min(jnp.where(lg == gmax, iota_g, N_GROUPS), axis=0, keepdims=True)
    p_grp = 1.0 / jnp.sum(jnp.exp(lg - gmax), axis=0, keepdims=True)
    le = lt[SUBLANES:SUBLANES + EXPERTS_PER_GROUP]
    for g in range(1, N_GROUPS):
        lo = SUBLANES + g * EXPERTS_PER_GROUP
        le = jnp.where(grp == g, lt[lo:lo + EXPERTS_PER_GROUP], le)
    iota_e = lax.broadcasted_iota(jnp.int32, le.shape, 0)
    v1 = jnp.max(le, axis=0, keepdims=True)
    i1 = jnp.min(jnp.where(le == v1, iota_e, EXPERTS_PER_GROUP), axis=0, keepdims=True)
    le2 = jnp.where(iota_e == i1, -jnp.inf, le)
    v2 = jnp.max(le2, axis=0, keepdims=True)
    i2 = jnp.min(jnp.where(le2 == v2, iota_e, EXPERTS_PER_GROUP), axis=0, keepdims=True)
    e2 = jnp.exp(v2 - v1)
    den = 1.0 + e2
    gate_ref[...] = jnp.concatenate([p_grp * (1.0 / den), p_grp * (e2 / den)], axis=0)
    experts = (grp * EXPERTS_PER_GROUP + i1, grp * EXPERTS_PER_GROUP + i2)

    iota_x = lax.broadcasted_iota(jnp.int32, (N_EXPERTS, tm), 0)
    hits = [iota_x == e_k for e_k in experts]
    hits_f = [jnp.where(hit, 1.0, 0.0) for hit in hits]
    hits_b = [hf.astype(BF16) for hf in hits_f]
    excl = [jnp.dot(hb, tri_ref[...], preferred_element_type=F32) for hb in hits_b]
    tot_col = [jnp.sum(hf, axis=1, keepdims=True) for hf in hits_f]
    ones = jnp.ones((SUBLANES, tm), BF16)
    n_lane = sum(lax.dot_general(ones, hb, (((1,), (1,)), ((), ())), preferred_element_type=F32) for hb in hits_b)
    seg_len_lane = _round_up(n_lane[0:1], SEG_ALIGN)
    seg_len = _round_up(tot_col[0] + tot_col[1], SEG_ALIGN)
    before = (lax.broadcasted_iota(jnp.int32, (N_EXPERTS, N_EXPERTS), 1)
              < lax.broadcasted_iota(jnp.int32, (N_EXPERTS, N_EXPERTS), 0))
    local_off = jnp.sum(jnp.where(before, seg_len_lane, 0.0), axis=1, keepdims=True)

    lpos_a = jnp.sum(jnp.where(hits[0], excl[0] + local_off, 0.0), axis=0, keepdims=True).astype(jnp.int32)
    lpos_b = jnp.sum(jnp.where(hits[1], excl[1] + tot_col[0] + local_off, 0.0), axis=0, keepdims=True).astype(jnp.int32)
    lpos_ref[...] = jnp.concatenate([lpos_a, lpos_b], axis=0)

    seg_before = seg_sc[...]
    seg_sc[...] = seg_before + seg_len
    lane = lax.broadcasted_iota(jnp.int32, (N_EXPERTS, LANES), 1)
    meta = jnp.where(lane == 0, seg_len, jnp.where(lane == 1, local_off, jnp.where(lane == 2, seg_before, 0.0)))
    meta_ref[0] = meta.astype(jnp.int32)
    tot_ref[...] = jnp.broadcast_to(seg_before + seg_len, tot_ref.shape).astype(jnp.int32)

    row = lax.broadcasted_iota(jnp.int32, (LOCAL_ROWS, tm), 0)
    perm = jnp.where(row == lpos_a, 1.0, jnp.where(row == lpos_b, 1.0, 0.0)).astype(BF16)
    sorted_ref[0] = jnp.dot(perm, xn.astype(BF16), preferred_element_type=F32).astype(BF16)


def _router(h, g, w_rg, b_rg, w_re, b_re, tm):
    t = h.shape[0]
    nt = t // tm
    wrt = jnp.zeros((ROUTER_ROWS, D_MODEL), F32)
    wrt = wrt.at[0:N_GROUPS].set(w_rg.T).at[SUBLANES:].set(w_re.T)
    br = jnp.zeros((ROUTER_ROWS, 1), F32)
    br = br.at[0:N_GROUPS, 0].set(b_rg).at[SUBLANES:, 0].set(b_re)
    tri = (lax.broadcasted_iota(jnp.int32, (tm, tm), 0) < lax.broadcasted_iota(jnp.int32, (tm, tm), 1)).astype(BF16)
    return pl.pallas_call(
        _router_kernel,
        grid=(nt,),
        in_specs=[
            pl.BlockSpec((tm, D_MODEL), lambda i: (i, 0)),
            pl.BlockSpec((1, D_MODEL), lambda i: (0, 0)),
            pl.BlockSpec((ROUTER_ROWS, D_MODEL), lambda i: (0, 0)),
            pl.BlockSpec((ROUTER_ROWS, 1), lambda i: (0, 0)),
            pl.BlockSpec((tm, tm), lambda i: (0, 0)),
        ],
        out_specs=[
            pl.BlockSpec((1, LOCAL_ROWS, D_MODEL), lambda i: (i, 0, 0)),
            pl.BlockSpec((2, tm), lambda i: (0, i)),
            pl.BlockSpec((2, tm), lambda i: (0, i)),
            pl.BlockSpec((1, N_EXPERTS, LANES), lambda i: (i, 0, 0)),
            pl.BlockSpec((N_EXPERTS, LANES), lambda i: (0, 0)),
        ],
        out_shape=[
            jax.ShapeDtypeStruct((nt, LOCAL_ROWS, D_MODEL), BF16),
            jax.ShapeDtypeStruct((2, t), jnp.int32),
            jax.ShapeDtypeStruct((2, t), F32),
            jax.ShapeDtypeStruct((nt, N_EXPERTS, LANES), jnp.int32),
            jax.ShapeDtypeStruct((N_EXPERTS, LANES), jnp.int32),
        ],
        scratch_shapes=[pltpu.VMEM((N_EXPERTS, 1), F32)],
        compiler_params=_cparams(("arbitrary",)),
        name="router",
    )(h, g.reshape(1, D_MODEL), wrt, br, tri)


def _if_rows(n, fn):
    pl.when(n > 0)(lambda: fn(pl.multiple_of(n, SEG_ALIGN)))


def _rows(ref, start, size):
    return ref.at[pl.ds(pl.multiple_of(start, SEG_ALIGN), size), :]


def _gather_kernel(len_ref, off_ref, dst_ref, zs_ref, zn_ref, nu_ref, sorted_ref, buf_ref, stage, zero_sc,
                   sem_in, sem_out, sem_fill, *, n_blk):
    t = pl.program_id(0)
    nt = pl.num_programs(0)
    slot = t % STAGE_SLOTS

    def tile_in(t, s):
        return pltpu.make_async_copy(sorted_ref.at[t], stage.at[s], sem_in.at[s])

    def tile_out(t, s, op):
        def body(e, c):
            idx = t * N_EXPERTS + e
            lo, dst = off_ref[idx], dst_ref[idx]
            _if_rows(len_ref[idx], lambda n: op(pltpu.make_async_copy(
                _rows(stage.at[s], lo, n), _rows(buf_ref, dst, n), sem_out.at[s])))
            return c

        lax.fori_loop(0, N_EXPERTS, body, 0)

    def fill_copies(op):
        def tails(e, c):
            _if_rows(zn_ref[e], lambda n: op(pltpu.make_async_copy(
                zero_sc.at[pl.ds(0, n), :], _rows(buf_ref, zs_ref[e], n), sem_fill)))
            return c

        lax.fori_loop(0, N_EXPERTS, tails, 0)

        def blocks(b, c):
            op(pltpu.make_async_copy(zero_sc, _rows(buf_ref, b * EXPERT_ROWS, EXPERT_ROWS), sem_fill))
            return c

        lax.fori_loop(nu_ref[0], n_blk, blocks, 0)

    start = lambda cp: cp.start()
    wait = lambda cp: cp.wait()

    @pl.when(t == 0)
    def _():
        tile_in(0, 0).start()
        zero_sc[...] = jnp.zeros_like(zero_sc)
        fill_copies(start)

    tile_in(t, slot).wait()
    nxt = (t + 1) % STAGE_SLOTS

    @pl.when(t >= STAGE_SLOTS - 1)
    def _():
        tile_out(t + 1 - STAGE_SLOTS, nxt, wait)

    @pl.when(t + 1 < nt)
    def _():
        tile_in(t + 1, nxt).start()

    tile_out(t, slot, start)

    @pl.when(t == nt - 1)
    def _():
        for back in range(STAGE_SLOTS - 2, -1, -1):
            @pl.when(t >= back)
            def _(back=back):
                tile_out(t - back, (t - back) % STAGE_SLOTS, wait)

        fill_copies(wait)


def _gather_segments(sorted_rows, seg_len, seg_off, seg_dst, tail_start, tail_len, n_used, n_blk):
    grid_spec = pltpu.PrefetchScalarGridSpec(
        num_scalar_prefetch=6,
        grid=(sorted_rows.shape[0],),
        in_specs=[pl.BlockSpec(memory_space=pl.ANY)],
        out_specs=pl.BlockSpec(memory_space=pl.ANY),
        scratch_shapes=[pltpu.VMEM((STAGE_SLOTS, LOCAL_ROWS, D_MODEL), BF16), pltpu.VMEM((EXPERT_ROWS, D_MODEL), BF16),
                        pltpu.SemaphoreType.DMA((STAGE_SLOTS,)), pltpu.SemaphoreType.DMA((STAGE_SLOTS,)),
                        pltpu.SemaphoreType.DMA],
    )
    return pl.pallas_call(
        functools.partial(_gather_kernel, n_blk=n_blk),
        grid_spec=grid_spec,
        out_shape=jax.ShapeDtypeStruct((n_blk * EXPERT_ROWS, D_MODEL), BF16),
        compiler_params=_cparams(("arbitrary",)),
        name="gather_segments",
    )(seg_len, seg_off, seg_dst, tail_start, tail_len, n_used, sorted_rows)


def _expert_kernel(be_ref, nu_ref, x_ref, wg_ref, wu_ref, wd_ref, y_ref, wgu_sc, wd_sc):
    i = pl.program_id(0)
    e = be_ref[i]
    prev = be_ref[jnp.maximum(i - 1, 0)]

    @pl.when(jnp.logical_or(i == 0, e != prev))
    def _():
        wgu_sc[:, :D_EXPERT] = wg_ref[0, 0].astype(BF16)
        wgu_sc[:, D_EXPERT:] = wu_ref[0, 0].astype(BF16)
        wd_sc[...] = wd_ref[0, 0].astype(BF16)

    @pl.when(i < nu_ref[0])
    def _():
        gu = jnp.dot(x_ref[...], wgu_sc[...], preferred_element_type=F32)
        a = jax.nn.silu(gu[:, :D_EXPERT]) * gu[:, D_EXPERT:]
        y_ref[...] = jnp.dot(a.astype(BF16), wd_sc[...], preferred_element_type=F32).astype(BF16)

    @pl.when(i >= nu_ref[0])
    def _():
        y_ref[...] = jnp.zeros_like(y_ref)


def _experts(buf, blk_expert, n_used, w_gate, w_up, w_down, layer):
    n_blk = blk_expert.shape[0]

    def xmap(i, be, nu):
        return (jnp.maximum(jnp.minimum(i, nu[0] - 1), 0), 0)

    grid_spec = pltpu.PrefetchScalarGridSpec(
        num_scalar_prefetch=2,
        grid=(n_blk,),
        in_specs=[
            pl.BlockSpec((EXPERT_ROWS, D_MODEL), xmap),
            pl.BlockSpec((1, 1, D_MODEL, D_EXPERT), lambda i, be, nu: (layer, be[i], 0, 0)),
            pl.BlockSpec((1, 1, D_MODEL, D_EXPERT), lambda i, be, nu: (layer, be[i], 0, 0)),
            pl.BlockSpec((1, 1, D_EXPERT, D_MODEL), lambda i, be, nu: (layer, be[i], 0, 0)),
        ],
        out_specs=pl.BlockSpec((EXPERT_ROWS, D_MODEL), lambda i, be, nu: (i, 0)),
        scratch_shapes=[pltpu.VMEM((D_MODEL, 2 * D_EXPERT), BF16), pltpu.VMEM((D_EXPERT, D_MODEL), BF16)],
    )
    return pl.pallas_call(
        _expert_kernel,
        grid_spec=grid_spec,
        out_shape=jax.ShapeDtypeStruct(buf.shape, BF16),
        compiler_params=_cparams(("arbitrary",)),
        name="experts",
    )(blk_expert, n_used, buf, w_gate, w_up, w_down)


def _combine_kernel(len_ref, off_ref, dst_ref, h_ref, gate_ref, lpos_ref, y_ref, fn_ref, o_ref, yl, sem, *, final_norm):
    i = pl.program_id(0)
    nt = pl.num_programs(0)
    tm = h_ref.shape[0]
    slot = i % 2

    def tile_copies(t, s, op):
        def body(e, c):
            idx = t * N_EXPERTS + e
            lo, src = off_ref[idx], dst_ref[idx]
            _if_rows(len_ref[idx], lambda n: op(pltpu.make_async_copy(
                _rows(y_ref, src, n), _rows(yl.at[s], lo, n), sem.at[s])))
            return c

        lax.fori_loop(0, N_EXPERTS, body, 0)

    @pl.when(i == 0)
    def _():
        yl[...] = jnp.zeros_like(yl)
        tile_copies(0, 0, lambda cp: cp.start())

    @pl.when(i + 1 < nt)
    def _():
        tile_copies(i + 1, 1 - slot, lambda cp: cp.start())

    tile_copies(i, slot, lambda cp: cp.wait())

    rows = yl[slot]
    lpos = lpos_ref[...]
    col = lax.broadcasted_iota(jnp.int32, (tm, LOCAL_ROWS), 1)
    picked = [jnp.dot(jnp.where(col == lpos[:, k:k + 1], 1.0, 0.0).astype(BF16), rows, preferred_element_type=F32)
              for k in range(2)]
    g = gate_ref[...]
    out = h_ref[...] + (g[:, 0:1] * picked[0] + g[:, 1:2] * picked[1])
    if final_norm:
        out = _rms(out, fn_ref[...])
    o_ref[...] = out


def _combine(h, y, seg_len, seg_off, seg_dst, lpos, gate, fn, tm, final_norm):
    t = h.shape[0]
    grid_spec = pltpu.PrefetchScalarGridSpec(
        num_scalar_prefetch=3,
        grid=(t // tm,),
        in_specs=[
            pl.BlockSpec((tm, D_MODEL), lambda i, *_: (i, 0)),
            pl.BlockSpec((tm, 2), lambda i, *_: (i, 0)),
            pl.BlockSpec((tm, 2), lambda i, *_: (i, 0)),
            pl.BlockSpec(memory_space=pl.ANY),
            pl.BlockSpec((1, D_MODEL), lambda i, *_: (0, 0)),
        ],
        out_specs=pl.BlockSpec((tm, D_MODEL), lambda i, *_: (i, 0)),
        scratch_shapes=[pltpu.VMEM((2, LOCAL_ROWS, D_MODEL), BF16), pltpu.SemaphoreType.DMA((2,))],
    )
    return pl.pallas_call(
        functools.partial(_combine_kernel, final_norm=final_norm),
        grid_spec=grid_spec,
        out_shape=jax.ShapeDtypeStruct((t, D_MODEL), F32),
        compiler_params=_cparams(("arbitrary",)),
        name="combine",
    )(seg_len, seg_off, seg_dst, h, gate.T, lpos.T, y, fn.reshape(1, D_MODEL))


def _moe(h, g, w_rg, b_rg, w_re, b_re, w_gate, w_up, w_down, layer, fn, tm, final_norm):
    t = h.shape[0]
    nt = t // tm
    sorted_rows, lpos, gate, meta, tot = _router(h, g, w_rg, b_rg, w_re, b_re, tm)
    n_blk = (2 * t + nt * N_EXPERTS * (SEG_ALIGN - 1)) // EXPERT_ROWS + N_EXPERTS
    total = tot[:, 0]
    region = (total + EXPERT_ROWS - 1) // EXPERT_ROWS * EXPERT_ROWS
    ends = jnp.cumsum(region)
    starts = ends - region
    seg_len = meta[:, :, 0].reshape(-1)
    seg_off = meta[:, :, 1].reshape(-1)
    seg_dst = (meta[:, :, 2] + starts[None, :]).reshape(-1)
    blk_start = jnp.arange(n_blk, dtype=jnp.int32) * EXPERT_ROWS
    blk_expert = jnp.minimum(jnp.sum(blk_start[:, None] >= ends[None, :], axis=1), N_EXPERTS - 1).astype(jnp.int32)
    n_used = (ends[-1:] // EXPERT_ROWS).astype(jnp.int32)
    buf = _gather_segments(sorted_rows, seg_len, seg_off, seg_dst, starts + total, region - total, n_used, n_blk)
    y = _experts(buf, blk_expert, n_used, w_gate, w_up, w_down, layer)
    return _combine(h, y, seg_len, seg_off, seg_dst, lpos, gate, fn, tm, final_norm)


def _kv_kernel(h_ref, g_ref, wk_ref, wvt_ref, cos_ref, sa_ref, sb_ref, k_ref, vt_ref):
    hn = _rms(h_ref[...], g_ref[...]).astype(BF16)
    k = jnp.dot(hn, wk_ref[...], preferred_element_type=F32)
    cos, sa, sb = cos_ref[...], sa_ref[...], sb_ref[...]
    for c in range(D_MODEL // LANES):
        kc = k[:, c * LANES:(c + 1) * LANES]
        rot = kc * cos + pltpu.roll(kc, LANES - ROT_HALF, 1) * sa + pltpu.roll(kc, ROT_HALF, 1) * sb
        k_ref[0, :, c * LANES:(c + 1) * LANES] = rot.astype(BF16)
    vt = lax.dot_general(wvt_ref[...], hn, (((1,), (1,)), ((), ())), preferred_element_type=F32)
    for c in range(vt_ref.shape[1]):
        vt_ref[0, c] = vt[:, c * LANES:(c + 1) * LANES].astype(BF16)


def _kv(h, g, w_kv, bsz, lp):
    tpb = lp // TILE
    kblk = TILE // LANES
    pos = jnp.maximum(jnp.arange(lp) - PAD_ROWS, 0).astype(F32)
    inv_freq = ROPE_THETA ** (-jnp.arange(ROT_HALF, dtype=F32) * 2.0 / (2 * ROT_HALF))
    ang = pos[:, None] * inv_freq[None, :]
    r = jnp.arange(LANES) % HEAD_DIM
    cos_t = jnp.where(r < 2 * ROT_HALF, jnp.cos(ang)[:, r % ROT_HALF], 1.0)
    sin_t = jnp.sin(ang)[:, r % ROT_HALF]
    sa = jnp.where(r < ROT_HALF, -sin_t, 0.0)
    sb = jnp.where((r >= ROT_HALF) & (r < 2 * ROT_HALF), sin_t, 0.0)
    tab = pl.BlockSpec((TILE, LANES), lambda i: (i % tpb, 0))
    return pl.pallas_call(
        _kv_kernel,
        grid=(bsz * tpb,),
        in_specs=[
            pl.BlockSpec((TILE, D_MODEL), lambda i: (i, 0)),
            pl.BlockSpec((1, D_MODEL), lambda i: (0, 0)),
            pl.BlockSpec((D_MODEL, D_MODEL), lambda i: (0, 0)),
            pl.BlockSpec((D_MODEL, D_MODEL), lambda i: (0, 0)),
            tab, tab, tab,
        ],
        out_specs=[
            pl.BlockSpec((1, TILE, D_MODEL), lambda i: (i // tpb, i % tpb, 0)),
            pl.BlockSpec((1, kblk, D_MODEL, LANES), lambda i: (i // tpb, i % tpb, 0, 0)),
        ],
        out_shape=[
            jax.ShapeDtypeStruct((bsz, lp, D_MODEL), BF16),
            jax.ShapeDtypeStruct((bsz, lp // LANES, D_MODEL, LANES), BF16),
        ],
        compiler_params=_cparams(("arbitrary",)),
        name="kv_proj",
    )(h, g.reshape(1, D_MODEL), w_kv[:, :D_MODEL].astype(BF16), w_kv[:, D_MODEL:].T.astype(BF16), cos_t, sa, sb)


def _q_kernel(h_ref, g_ref, wqt_ref, cos_ref, sin_ref, qt_ref):
    hn = _rms(h_ref[...], g_ref[...]).astype(BF16)
    qt = lax.dot_general(wqt_ref[...], hn, (((1,), (1,)), ((), ())), preferred_element_type=F32)
    cos, sin = cos_ref[...], sin_ref[...]
    scale = HEAD_DIM ** -0.5 * math.log2(math.e)
    for c in range(D_MODEL // HEAD_DIM):
        lo = c * HEAD_DIM
        x1 = qt[lo:lo + ROT_HALF]
        x2 = qt[lo + ROT_HALF:lo + 2 * ROT_HALF]
        blk = jnp.concatenate([x1 * cos - x2 * sin, x2 * cos + x1 * sin, qt[lo + 2 * ROT_HALF:lo + HEAD_DIM]], axis=0)
        qt_ref[lo:lo + HEAD_DIM, :] = (blk * scale).astype(BF16)


def _real_row_map(tiles_real, tiles_padded):
    front = FRONT // TILE
    return lambda i: ((i // tiles_real) * tiles_padded + front + i % tiles_real, 0)


def _q_proj(h_pad, g, w_q, bsz, seq, lp):
    tiles_real = seq // TILE
    pos = (N_META + jnp.arange(seq)).astype(F32)
    inv_freq = ROPE_THETA ** (-jnp.arange(ROT_HALF, dtype=F32) * 2.0 / (2 * ROT_HALF))
    ang = inv_freq[:, None] * pos[None, :]
    tab = pl.BlockSpec((ROT_HALF, TILE), lambda i: (0, i % tiles_real))
    return pl.pallas_call(
        _q_kernel,
        grid=(bsz * tiles_real,),
        in_specs=[
            pl.BlockSpec((TILE, D_MODEL), _real_row_map(tiles_real, lp // TILE)),
            pl.BlockSpec((1, D_MODEL), lambda i: (0, 0)),
            pl.BlockSpec((D_MODEL, D_MODEL), lambda i: (0, 0)),
            tab, tab,
        ],
        out_specs=pl.BlockSpec((D_MODEL, TILE), lambda i: (0, i)),
        out_shape=jax.ShapeDtypeStruct((D_MODEL, bsz * seq), BF16),
        compiler_params=_cparams(("arbitrary",)),
        name="q_proj",
    )(h_pad, g.reshape(1, D_MODEL), w_q.T.astype(BF16), jnp.cos(ang), jnp.sin(ang))


def _attn_kernel(qt_ref, k_ref, vt_ref, lam_ref, g_ref, o_ref, m_sc, l_sc, acc_sc, sa_sc, sb_sc, *, lam_init):
    i = pl.program_id(2)
    tq = qt_ref.shape[1]
    qt = qt_ref[...]
    zero = jnp.zeros((HEAD_DIM, tq), BF16)
    qs = (jnp.concatenate([qt[0:HEAD_DIM], zero], axis=0), jnp.concatenate([zero, qt[HEAD_DIM:]], axis=0))
    m_sc[...] = jnp.full(m_sc.shape, NEG, F32)
    l_sc[...] = jnp.zeros_like(l_sc)
    acc_sc[...] = jnp.zeros_like(acc_sc)

    def update(n, s, vtt):
        m_old = m_sc[n]
        m_new = jnp.maximum(m_old, jnp.max(s, axis=0, keepdims=True))
        alpha = jnp.exp2(m_old - m_new)
        p = jnp.exp2(s - m_new)
        l_sc[n] = alpha * l_sc[n] + jnp.sum(p, axis=0, keepdims=True)
        acc_sc[n] = alpha * acc_sc[n] + jnp.dot(vtt, p.astype(BF16), preferred_element_type=F32)
        m_sc[n] = m_new

    def values(row0, nkeys):
        blk0 = row0 // LANES
        return jnp.concatenate([vt_ref[0, blk0 + c] for c in range(nkeys // LANES)], axis=1)

    def key_row(t):
        return pl.multiple_of(FRONT + t * ATT_TK, LANES)

    n_sub = ATT_TK // ATT_SUB

    def scores_into(t, buf):
        for u in range(n_sub):
            kt = k_ref[0, pl.ds(pl.multiple_of(key_row(t) + u * ATT_SUB, LANES), ATT_SUB), :]
            for n in range(2):
                buf[u * 2 + n] = jnp.dot(kt, qs[n], preferred_element_type=F32)

    def consume(t, buf, vis):
        for u in range(n_sub):
            vtt = values(key_row(t) + u * ATT_SUB, ATT_SUB)
            for n in range(2):
                s = buf[u * 2 + n]
                if vis is not None:
                    s = jnp.where(vis[u * ATT_SUB:(u + 1) * ATT_SUB], s, NEG)
                update(n, s, vtt)

    meta_vis = lax.broadcasted_iota(jnp.int32, (LANES, tq), 0) >= META_PAD
    kt = k_ref[0, FRONT - LANES:FRONT, :]
    for n in range(2):
        update(n, jnp.where(meta_vis, jnp.dot(kt, qs[n], preferred_element_type=F32), NEG), values(FRONT - LANES, LANES))

    n_full = i // 2
    odd = n_full % 2

    @pl.when(odd == 1)
    def _():
        scores_into(0, sb_sc)
        consume(0, sb_sc, None)

    scores_into(odd, sa_sc)

    def body(j, c):
        t0 = odd + 2 * j
        scores_into(t0 + 1, sb_sc)
        consume(t0, sa_sc, None)
        scores_into(t0 + 2, sa_sc)
        consume(t0 + 1, sb_sc, None)
        return c

    lax.fori_loop(0, n_full // 2, body, 0)
    r = lax.broadcasted_iota(jnp.int32, (ATT_TK, tq), 0) // CHUNK
    c = lax.broadcasted_iota(jnp.int32, (ATT_TK, tq), 1) // CHUNK
    consume(n_full, sa_sc, r <= c + (i % 2) * (tq // CHUNK))

    lp = lam_ref[...]
    lam = (jnp.exp(jnp.sum(lp[0:1] * lp[1:2], axis=1, keepdims=True))
           - jnp.exp(jnp.sum(lp[2:3] * lp[3:4], axis=1, keepdims=True)) + lam_init)
    o = acc_sc[0] / l_sc[0] - lam * (acc_sc[1] / l_sc[1])
    o = o * lax.rsqrt(jnp.mean(o * o, axis=0, keepdims=True) + EPS) * g_ref[...] * (1.0 - lam_init)
    o_ref[...] = o.T.astype(BF16)


def _attention(qt, k, vt4, lam_p, subln_g, bsz, seq, lam_init):
    nq = seq // ATT_TQ
    lp = k.shape[1]
    return pl.pallas_call(
        functools.partial(_attn_kernel, lam_init=lam_init),
        grid=(bsz, N_HEADS, nq),
        in_specs=[
            pl.BlockSpec((2 * HEAD_DIM, ATT_TQ), lambda b, h, i: (h, b * nq + i)),
            pl.BlockSpec((1, lp, 2 * HEAD_DIM), lambda b, h, i: (b, 0, h)),
            pl.BlockSpec((1, lp // LANES, 2 * HEAD_DIM, LANES), lambda b, h, i: (b, 0, h, 0)),
            pl.BlockSpec((4, HEAD_DIM), lambda b, h, i: (0, 0)),
            pl.BlockSpec((2 * HEAD_DIM, 1), lambda b, h, i: (0, 0)),
        ],
        out_specs=pl.BlockSpec((ATT_TQ, 2 * HEAD_DIM), lambda b, h, i: (b * nq + i, h)),
        out_shape=jax.ShapeDtypeStruct((bsz * seq, D_MODEL), BF16),
        scratch_shapes=[pltpu.VMEM((2, 1, ATT_TQ), F32), pltpu.VMEM((2, 1, ATT_TQ), F32),
                        pltpu.VMEM((2, 2 * HEAD_DIM, ATT_TQ), F32),
                        pltpu.VMEM((2 * ATT_TK // ATT_SUB, ATT_SUB, ATT_TQ), F32),
                        pltpu.VMEM((2 * ATT_TK // ATT_SUB, ATT_SUB, ATT_TQ), F32)],
        compiler_params=_cparams(("arbitrary", "arbitrary", "arbitrary")),
        name="diff_attention",
    )(qt, k, vt4, lam_p, subln_g.reshape(2 * HEAD_DIM, 1))


def _oproj_kernel(h_ref, o_ref, w_ref, out_ref):
    out_ref[...] = h_ref[...] + jnp.dot(o_ref[...], w_ref[...], preferred_element_type=F32)


def _o_proj(h_pad, o, w_out, bsz, seq, lp):
    tiles_real = seq // TILE
    return pl.pallas_call(
        _oproj_kernel,
        grid=(bsz * tiles_real,),
        in_specs=[
            pl.BlockSpec((TILE, D_MODEL), _real_row_map(tiles_real, lp // TILE)),
            pl.BlockSpec((TILE, D_MODEL), lambda i: (i, 0)),
            pl.BlockSpec((D_MODEL, D_MODEL), lambda i: (0, 0)),
        ],
        out_specs=pl.BlockSpec((TILE, D_MODEL), lambda i: (i, 0)),
        out_shape=jax.ShapeDtypeStruct((bsz * seq, D_MODEL), F32),
        compiler_params=_cparams(("arbitrary",)),
        name="o_proj",
    )(h_pad, o, w_out.astype(BF16))


def kernel(x, meta_tokens, a_norm, a_w_in, a_conv, a_w_out, kv_norm, w_kv, b_norm, b_w_q, b_lambda, b_subln, b_w_out, ffn_norm, r_group, r_group_b, r_expert, r_expert_b, e_gate, e_up, e_down, final_norm):
    bsz, seq, d = x.shape
    assert d == D_MODEL and a_norm.shape[0] == 1 and b_norm.shape[0] == 1
    lp = FRONT + seq
    assert seq % ATT_TK == 0 and ATT_TK == 2 * ATT_TQ and ATT_TQ % TILE == 0
    front = jnp.concatenate([jnp.zeros((PAD_ROWS, d), x.dtype), meta_tokens.astype(x.dtype)], axis=0)
    h = _mixer(x.reshape(bsz * seq, d), front, a_norm[0], a_w_in[0], a_conv[0], a_w_out[0], bsz, lp // TILE)
    h = _moe(h, ffn_norm[0], r_group[0], r_group_b[0], r_expert[0], r_expert_b[0],
             e_gate, e_up, e_down, 0, final_norm, TILE, False)

    k, vt4 = _kv(h, kv_norm, w_kv, bsz, lp)
    qt = _q_proj(h, b_norm[0], b_w_q[0], bsz, seq, lp)
    lam_init = 0.8 - 0.6 * math.exp(-0.3 * a_norm.shape[0])
    o = _attention(qt, k, vt4, b_lambda[0], b_subln[0], bsz, seq, lam_init)
    h = _o_proj(h, o, b_w_out[0], bsz, seq, lp)
    h = _moe(h, ffn_norm[1], r_group[1], r_group_b[1], r_expert[1], r_expert_b[1],
             e_gate, e_up, e_down, 1, final_norm, TILE, True)
    return h.reshape(bsz, seq, d)
```

```python
import functools
import math

import jax
import jax.numpy as jnp
from jax import lax
from jax.experimental import pallas as pl
from jax.experimental.pallas import tpu as pltpu

D_MODEL = 1024
CHUNK = 64
N_META = 16
HEAD_DIM = 64
N_HEADS = 8
ROT_HALF = 8
ROPE_THETA = 500000.0
N_GROUPS = 4
EXPERTS_PER_GROUP = 8
N_EXPERTS = 32
D_EXPERT = 512
EPS = 1e-6

LANES = 128
SUBLANES = 8
TILE = 512
FRONT = TILE
PAD_ROWS = FRONT - N_META
META_PAD = LANES - N_META
EXPERT_ROWS = 512
ATT_TQ = 512
ATT_TK = 1024
ATT_SUB = 1024
SEG_ALIGN = 16
STAGE_SLOTS = 2
LOCAL_ROWS = 1536
VMEM_LIMIT = 56 * 1024 * 1024
NEG = float(jnp.finfo(jnp.float32).min)

BF16 = jnp.bfloat16
F32 = jnp.float32


def _rms(x, g):
    return x * lax.rsqrt(jnp.mean(x * x, axis=-1, keepdims=True) + EPS) * g


def _cparams(sem):
    return pltpu.CompilerParams(dimension_semantics=sem, vmem_limit_bytes=VMEM_LIMIT)


def _mixer_kernel(x_ref, front_ref, g_ref, win_ref, cw_ref, wout_ref, o_ref, cu_ref, *, tiles_per_batch):
    i = pl.program_id(0)
    tm = x_ref.shape[0]
    x = jnp.where(i % tiles_per_batch == 0, front_ref[...], x_ref[...])
    hn = _rms(x, g_ref[...]).astype(BF16)
    proj = jnp.dot(hn, win_ref[...], preferred_element_type=F32)
    b_gate = proj[:, :D_MODEL]
    cu = proj[:, D_MODEL:2 * D_MODEL] * proj[:, 2 * D_MODEL:]

    @pl.when(i % tiles_per_batch == 0)
    def _():
        cu_ref[0:SUBLANES, :] = jnp.zeros((SUBLANES, D_MODEL), F32)

    cu_ref[SUBLANES:SUBLANES + tm, :] = cu
    cw = cw_ref[...]
    z = (cw[2:3] * cu + cw[1:2] * cu_ref[SUBLANES - 1:SUBLANES - 1 + tm, :]
         + cw[0:1] * cu_ref[SUBLANES - 2:SUBLANES - 2 + tm, :])
    cu_ref[0:SUBLANES, :] = cu_ref[tm:tm + SUBLANES, :]
    y = jnp.dot((b_gate * z).astype(BF16), wout_ref[...], preferred_element_type=F32)
    o_ref[...] = x + y


def _mixer(x, front, g, w_in, conv_w, w_out, bsz, tiles_per_batch):
    t = bsz * tiles_per_batch * TILE
    tiles_real = tiles_per_batch - FRONT // TILE

    def x_map(i):
        return ((i // tiles_per_batch) * tiles_real + jnp.maximum(i % tiles_per_batch - FRONT // TILE, 0), 0)

    return pl.pallas_call(
        functools.partial(_mixer_kernel, tiles_per_batch=tiles_per_batch),
        grid=(t // TILE,),
        in_specs=[
            pl.BlockSpec((TILE, D_MODEL), x_map),
            pl.BlockSpec((TILE, D_MODEL), lambda i: (0, 0)),
            pl.BlockSpec((1, D_MODEL), lambda i: (0, 0)),
            pl.BlockSpec((D_MODEL, 3 * D_MODEL), lambda i: (0, 0)),
            pl.BlockSpec((3, D_MODEL), lambda i: (0, 0)),
            pl.BlockSpec((D_MODEL, D_MODEL), lambda i: (0, 0)),
        ],
        out_specs=pl.BlockSpec((TILE, D_MODEL), lambda i: (i, 0)),
        out_shape=jax.ShapeDtypeStruct((t, D_MODEL), F32),
        scratch_shapes=[pltpu.VMEM((TILE + 2 * SUBLANES, D_MODEL), F32)],
        compiler_params=_cparams(("arbitrary",)),
        name="mixer_a",
    )(x, front, g.reshape(1, D_MODEL), w_in.astype(BF16), conv_w, w_out.astype(BF16))


ROUTER_ROWS = SUBLANES + N_EXPERTS


def _round_up(v, m):
    return jnp.floor((v + (m - 1)) * (1.0 / m)) * m


def _router_kernel(h_ref, g_ref, wrt_ref, br_ref, tri_ref, lpos_ref, gate_ref, meta_ref, tot_ref, seg_sc):
    i = pl.program_id(0)
    tm = h_ref.shape[0]

    @pl.when(i == 0)
    def _():
        seg_sc[...] = jnp.zeros_like(seg_sc)

    xn = _rms(h_ref[...], g_ref[...])
    lt = lax.dot_general(wrt_ref[...], xn, (((1,), (1,)), ((), ())),
                         precision=lax.Precision.HIGHEST, preferred_element_type=F32) + br_ref[...]
    lg = lt[0:N_GROUPS]
    gmax = jnp.max(lg, axis=0, keepdims=True)
    iota_g = lax.broadcasted_iota(jnp.int32, lg.shape, 0)
    grp = jnp.min(jnp.where(lg == gmax, iota_g, N_GROUPS), axis=0, keepdims=True)
    p_grp = 1.0 / jnp.sum(jnp.exp(lg - gmax), axis=0, keepdims=True)
    le = lt[SUBLANES:SUBLANES + EXPERTS_PER_GROUP]
    for g in range(1, N_GROUPS):
        lo = SUBLANES + g * EXPERTS_PER_GROUP
        le = jnp.where(grp == g, lt[lo:lo + EXPERTS_PER_GROUP], le)
    iota_e = lax.broadcasted_iota(jnp.int32, le.shape, 0)
    v1 = jnp.max(le, axis=0, keepdims=True)
    i1 = jnp.min(jnp.where(le == v1, iota_e, EXPERTS_PER_GROUP), axis=0, keepdims=True)
    le2 = jnp.where(iota_e == i1, -jnp.inf, le)
    v2 = jnp.max(le2, axis=0, keepdims=True)
    i2 = jnp.min(jnp.where(le2 == v2, iota_e, EXPERTS_PER_GROUP), axis=0, keepdims=True)
    e2 = jnp.exp(v2 - v1)
    den = 1.0 + e2
    gate_ref[...] = jnp.concatenate([p_grp * (1.0 / den), p_grp * (e2 / den)], axis=0)
    experts = (grp * EXPERTS_PER_GROUP + i1, grp * EXPERTS_PER_GROUP + i2)

    iota_x = lax.broadcasted_iota(jnp.int32, (N_EXPERTS, tm), 0)
    hits = [iota_x == e_k for e_k in experts]
    hits_f = [jnp.where(hit, 1.0, 0.0) for hit in hits]
    hits_b = [hf.astype(BF16) for hf in hits_f]
    excl = [jnp.dot(hb, tri_ref[...], preferred_element_type=F32) for hb in hits_b]
    tot_col = [jnp.sum(hf, axis=1, keepdims=True) for hf in hits_f]
    ones = jnp.ones((SUBLANES, tm), BF16)
    n_lane = sum(lax.dot_general(ones, hb, (((1,), (1,)), ((), ())), preferred_element_type=F32) for hb in hits_b)
    seg_len_lane = _round_up(n_lane[0:1], SEG_ALIGN)
    seg_len = _round_up(tot_col[0] + tot_col[1], SEG_ALIGN)
    before = (lax.broadcasted_iota(jnp.int32, (N_EXPERTS, N_EXPERTS), 1)
              < lax.broadcasted_iota(jnp.int32, (N_EXPERTS, N_EXPERTS), 0))
    local_off = jnp.sum(jnp.where(before, seg_len_lane, 0.0), axis=1, keepdims=True)

    lpos_a = jnp.sum(jnp.where(hits[0], excl[0] + local_off, 0.0), axis=0, keepdims=True).astype(jnp.int32)
    lpos_b = jnp.sum(jnp.where(hits[1], excl[1] + tot_col[0] + local_off, 0.0), axis=0, keepdims=True).astype(jnp.int32)
    lpos_ref[...] = jnp.concatenate([lpos_a, lpos_b], axis=0)

    seg_before = seg_sc[...]
    seg_sc[...] = seg_before + seg_len
    lane = lax.broadcasted_iota(jnp.int32, (N_EXPERTS, LANES), 1)
    meta = jnp.where(lane == 0, seg_len, jnp.where(lane == 1, local_off, jnp.where(lane == 2, seg_before, 0.0)))
    meta_ref[0] = meta.astype(jnp.int32)
    tot_ref[...] = jnp.broadcast_to(seg_before + seg_len, tot_ref.shape).astype(jnp.int32)


def _router(h, g, w_rg, b_rg, w_re, b_re, tm):
    t = h.shape[0]
    nt = t // tm
    wrt = jnp.zeros((ROUTER_ROWS, D_MODEL), F32)
    wrt = wrt.at[0:N_GROUPS].set(w_rg.T).at[SUBLANES:].set(w_re.T)
    br = jnp.zeros((ROUTER_ROWS, 1), F32)
    br = br.at[0:N_GROUPS, 0].set(b_rg).at[SUBLANES:, 0].set(b_re)
    tri = (lax.broadcasted_iota(jnp.int32, (tm, tm), 0) < lax.broadcasted_iota(jnp.int32, (tm, tm), 1)).astype(BF16)
    return pl.pallas_call(
        _router_kernel,
        grid=(nt,),
        in_specs=[
            pl.BlockSpec((tm, D_MODEL), lambda i: (i, 0)),
            pl.BlockSpec((1, D_MODEL), lambda i: (0, 0)),
            pl.BlockSpec((ROUTER_ROWS, D_MODEL), lambda i: (0, 0)),
            pl.BlockSpec((ROUTER_ROWS, 1), lambda i: (0, 0)),
            pl.BlockSpec((tm, tm), lambda i: (0, 0)),
        ],
        out_specs=[
            pl.BlockSpec((2, tm), lambda i: (0, i)),
            pl.BlockSpec((2, tm), lambda i: (0, i)),
            pl.BlockSpec((1, N_EXPERTS, LANES), lambda i: (i, 0, 0)),
            pl.BlockSpec((N_EXPERTS, LANES), lambda i: (0, 0)),
        ],
        out_shape=[
            jax.ShapeDtypeStruct((2, t), jnp.int32),
            jax.ShapeDtypeStruct((2, t), F32),
            jax.ShapeDtypeStruct((nt, N_EXPERTS, LANES), jnp.int32),
            jax.ShapeDtypeStruct((N_EXPERTS, LANES), jnp.int32),
        ],
        scratch_shapes=[pltpu.VMEM((N_EXPERTS, 1), F32)],
        compiler_params=_cparams(("arbitrary",)),
        name="router",
    )(h, g.reshape(1, D_MODEL), wrt, br, tri)


def _if_rows(n, fn):
    pl.when(n > 0)(lambda: fn(pl.multiple_of(n, SEG_ALIGN)))


def _rows(ref, start, size):
    return ref.at[pl.ds(pl.multiple_of(start, SEG_ALIGN), size), :]


def _dispatch_kernel(len_ref, off_ref, dst_ref, zs_ref, zn_ref, nu_ref, h_ref, g_ref, lpos_ref, buf_ref, stage, zero_sc,
                     sem_out, sem_fill, *, n_blk):
    t = pl.program_id(0)
    nt = pl.num_programs(0)
    tm = h_ref.shape[0]
    slot = t % STAGE_SLOTS

    def tile_out(t, s, op):
        def body(e, c):
            idx = t * N_EXPERTS + e
            lo, dst = off_ref[idx], dst_ref[idx]
            _if_rows(len_ref[idx], lambda n: op(pltpu.make_async_copy(
                _rows(stage.at[s], lo, n), _rows(buf_ref, dst, n), sem_out.at[s])))
            return c

        lax.fori_loop(0, N_EXPERTS, body, 0)

    def fill_copies(op):
        def tails(e, c):
            _if_rows(zn_ref[e], lambda n: op(pltpu.make_async_copy(
                zero_sc.at[pl.ds(0, n), :], _rows(buf_ref, zs_ref[e], n), sem_fill)))
            return c

        lax.fori_loop(0, N_EXPERTS, tails, 0)

        def blocks(b, c):
            op(pltpu.make_async_copy(zero_sc, _rows(buf_ref, b * EXPERT_ROWS, EXPERT_ROWS), sem_fill))
            return c

        lax.fori_loop(nu_ref[0], n_blk, blocks, 0)

    start = lambda cp: cp.start()
    wait = lambda cp: cp.wait()

    @pl.when(t == 0)
    def _():
        zero_sc[...] = jnp.zeros_like(zero_sc)
        fill_copies(start)

    @pl.when(t >= STAGE_SLOTS)
    def _():
        tile_out(t - STAGE_SLOTS, slot, wait)

    xn = _rms(h_ref[...], g_ref[...]).astype(BF16)
    lpos = lpos_ref[...]
    row = lax.broadcasted_iota(jnp.int32, (LOCAL_ROWS, tm), 0)
    perm = jnp.where(row == lpos[0:1], 1.0, jnp.where(row == lpos[1:2], 1.0, 0.0)).astype(BF16)
    stage[slot] = jnp.dot(perm, xn, preferred_element_type=F32).astype(BF16)
    tile_out(t, slot, start)

    @pl.when(t == nt - 1)
    def _():
        for back in range(STAGE_SLOTS - 1, -1, -1):
            @pl.when(t >= back)
            def _(back=back):
                tile_out(t - back, (t - back) % STAGE_SLOTS, wait)

        fill_copies(wait)


def _dispatch(h, g, lpos, seg_len, seg_off, seg_dst, tail_start, tail_len, n_used, n_blk, tm):
    t = h.shape[0]
    grid_spec = pltpu.PrefetchScalarGridSpec(
        num_scalar_prefetch=6,
        grid=(t // tm,),
        in_specs=[
            pl.BlockSpec((tm, D_MODEL), lambda i, *_: (i, 0)),
            pl.BlockSpec((1, D_MODEL), lambda i, *_: (0, 0)),
            pl.BlockSpec((2, tm), lambda i, *_: (0, i)),
        ],
        out_specs=pl.BlockSpec(memory_space=pl.ANY),
        scratch_shapes=[pltpu.VMEM((STAGE_SLOTS, LOCAL_ROWS, D_MODEL), BF16), pltpu.VMEM((EXPERT_ROWS, D_MODEL), BF16),
                        pltpu.SemaphoreType.DMA((STAGE_SLOTS,)), pltpu.SemaphoreType.DMA],
    )
    return pl.pallas_call(
        functools.partial(_dispatch_kernel, n_blk=n_blk),
        grid_spec=grid_spec,
        out_shape=jax.ShapeDtypeStruct((n_blk * EXPERT_ROWS, D_MODEL), BF16),
        compiler_params=_cparams(("arbitrary",)),
        name="dispatch",
    )(seg_len, seg_off, seg_dst, tail_start, tail_len, n_used, h, g.reshape(1, D_MODEL), lpos)


def _expert_kernel(be_ref, nu_ref, x_ref, wg_ref, wu_ref, wd_ref, y_ref, wgu_sc, wd_sc):
    i = pl.program_id(0)
    e = be_ref[i]
    prev = be_ref[jnp.maximum(i - 1, 0)]

    @pl.when(jnp.logical_or(i == 0, e != prev))
    def _():
        wgu_sc[:, :D_EXPERT] = wg_ref[0, 0].astype(BF16)
        wgu_sc[:, D_EXPERT:] = wu_ref[0, 0].astype(BF16)
        wd_sc[...] = wd_ref[0, 0].astype(BF16)

    @pl.when(i < nu_ref[0])
    def _():
        gu = jnp.dot(x_ref[...], wgu_sc[...], preferred_element_type=F32)
        a = jax.nn.silu(gu[:, :D_EXPERT]) * gu[:, D_EXPERT:]
        y_ref[...] = jnp.dot(a.astype(BF16), wd_sc[...], preferred_element_type=F32).astype(BF16)

    @pl.when(i >= nu_ref[0])
    def _():
        y_ref[...] = jnp.zeros_like(y_ref)


def _experts(buf, blk_expert, n_used, w_gate, w_up, w_down, layer):
    n_blk = blk_expert.shape[0]

    def xmap(i, be, nu):
        return (jnp.maximum(jnp.minimum(i, nu[0] - 1), 0), 0)

    grid_spec = pltpu.PrefetchScalarGridSpec(
        num_scalar_prefetch=2,
        grid=(n_blk,),
        in_specs=[
            pl.BlockSpec((EXPERT_ROWS, D_MODEL), xmap),
            pl.BlockSpec((1, 1, D_MODEL, D_EXPERT), lambda i, be, nu: (layer, be[i], 0, 0)),
            pl.BlockSpec((1, 1, D_MODEL, D_EXPERT), lambda i, be, nu: (layer, be[i], 0, 0)),
            pl.BlockSpec((1, 1, D_EXPERT, D_MODEL), lambda i, be, nu: (layer, be[i], 0, 0)),
        ],
        out_specs=pl.BlockSpec((EXPERT_ROWS, D_MODEL), lambda i, be, nu: (i, 0)),
        scratch_shapes=[pltpu.VMEM((D_MODEL, 2 * D_EXPERT), BF16), pltpu.VMEM((D_EXPERT, D_MODEL), BF16)],
    )
    return pl.pallas_call(
        _expert_kernel,
        grid_spec=grid_spec,
        out_shape=jax.ShapeDtypeStruct(buf.shape, BF16),
        compiler_params=_cparams(("arbitrary",)),
        name="experts",
    )(blk_expert, n_used, buf, w_gate, w_up, w_down)


def _combine_kernel(len_ref, off_ref, dst_ref, h_ref, gate_ref, lpos_ref, y_ref, fn_ref, o_ref, yl, sem, *, final_norm):
    i = pl.program_id(0)
    nt = pl.num_programs(0)
    tm = h_ref.shape[0]
    slot = i % 2

    def tile_copies(t, s, op):
        def body(e, c):
            idx = t * N_EXPERTS + e
            lo, src = off_ref[idx], dst_ref[idx]
            _if_rows(len_ref[idx], lambda n: op(pltpu.make_async_copy(
                _rows(y_ref, src, n), _rows(yl.at[s], lo, n), sem.at[s])))
            return c

        lax.fori_loop(0, N_EXPERTS, body, 0)

    @pl.when(i == 0)
    def _():
        yl[...] = jnp.zeros_like(yl)
        tile_copies(0, 0, lambda cp: cp.start())

    @pl.when(i + 1 < nt)
    def _():
        tile_copies(i + 1, 1 - slot, lambda cp: cp.start())

    tile_copies(i, slot, lambda cp: cp.wait())

    rows = yl[slot]
    lpos = lpos_ref[...]
    col = lax.broadcasted_iota(jnp.int32, (tm, LOCAL_ROWS), 1)
    picked = [jnp.dot(jnp.where(col == lpos[:, k:k + 1], 1.0, 0.0).astype(BF16), rows, preferred_element_type=F32)
              for k in range(2)]
    g = gate_ref[...]
    out = h_ref[...] + (g[:, 0:1] * picked[0] + g[:, 1:2] * picked[1])
    if final_norm:
        out = _rms(out, fn_ref[...])
    o_ref[...] = out


def _combine(h, y, seg_len, seg_off, seg_dst, lpos, gate, fn, tm, final_norm):
    t = h.shape[0]
    grid_spec = pltpu.PrefetchScalarGridSpec(
        num_scalar_prefetch=3,
        grid=(t // tm,),
        in_specs=[
            pl.BlockSpec((tm, D_MODEL), lambda i, *_: (i, 0)),
            pl.BlockSpec((tm, 2), lambda i, *_: (i, 0)),
            pl.BlockSpec((tm, 2), lambda i, *_: (i, 0)),
            pl.BlockSpec(memory_space=pl.ANY),
            pl.BlockSpec((1, D_MODEL), lambda i, *_: (0, 0)),
        ],
        out_specs=pl.BlockSpec((tm, D_MODEL), lambda i, *_: (i, 0)),
        scratch_shapes=[pltpu.VMEM((2, LOCAL_ROWS, D_MODEL), BF16), pltpu.SemaphoreType.DMA((2,))],
    )
    return pl.pallas_call(
        functools.partial(_combine_kernel, final_norm=final_norm),
        grid_spec=grid_spec,
        out_shape=jax.ShapeDtypeStruct((t, D_MODEL), F32),
        compiler_params=_cparams(("arbitrary",)),
        name="combine",
    )(seg_len, seg_off, seg_dst, h, gate.T, lpos.T, y, fn.reshape(1, D_MODEL))


def _moe(h, g, w_rg, b_rg, w_re, b_re, w_gate, w_up, w_down, layer, fn, tm, final_norm):
    t = h.shape[0]
    nt = t // tm
    lpos, gate, meta, tot = _router(h, g, w_rg, b_rg, w_re, b_re, tm)
    n_blk = (2 * t + nt * N_EXPERTS * (SEG_ALIGN - 1)) // EXPERT_ROWS + N_EXPERTS
    total = tot[:, 0]
    region = (total + EXPERT_ROWS - 1) // EXPERT_ROWS * EXPERT_ROWS
    ends = jnp.cumsum(region)
    starts = ends - region
    seg_len = meta[:, :, 0].reshape(-1)
    seg_off = meta[:, :, 1].reshape(-1)
    seg_dst = (meta[:, :, 2] + starts[None, :]).reshape(-1)
    blk_start = jnp.arange(n_blk, dtype=jnp.int32) * EXPERT_ROWS
    blk_expert = jnp.minimum(jnp.sum(blk_start[:, None] >= ends[None, :], axis=1), N_EXPERTS - 1).astype(jnp.int32)
    n_used = (ends[-1:] // EXPERT_ROWS).astype(jnp.int32)
    buf = _dispatch(h, g, lpos, seg_len, seg_off, seg_dst, starts + total, region - total, n_used, n_blk, tm)
    y = _experts(buf, blk_expert, n_used, w_gate, w_up, w_down, layer)
    return _combine(h, y, seg_len, seg_off, seg_dst, lpos, gate, fn, tm, final_norm)


def _kv_kernel(h_ref, g_ref, wk_ref, wvt_ref, cos_ref, sa_ref, sb_ref, k_ref, vt_ref):
    hn = _rms(h_ref[...], g_ref[...]).astype(BF16)
    k = jnp.dot(hn, wk_ref[...], preferred_element_type=F32)
    cos, sa, sb = cos_ref[...], sa_ref[...], sb_ref[...]
    for c in range(D_MODEL // LANES):
        kc = k[:, c * LANES:(c + 1) * LANES]
        rot = kc * cos + pltpu.roll(kc, LANES - ROT_HALF, 1) * sa + pltpu.roll(kc, ROT_HALF, 1) * sb
        k_ref[0, :, c * LANES:(c + 1) * LANES] = rot.astype(BF16)
    vt = lax.dot_general(wvt_ref[...], hn, (((1,), (1,)), ((), ())), preferred_element_type=F32)
    for c in range(vt_ref.shape[1]):
        vt_ref[0, c] = vt[:, c * LANES:(c + 1) * LANES].astype(BF16)


def _kv(h, g, w_kv, bsz, lp):
    tpb = lp // TILE
    kblk = TILE // LANES
    pos = jnp.maximum(jnp.arange(lp) - PAD_ROWS, 0).astype(F32)
    inv_freq = ROPE_THETA ** (-jnp.arange(ROT_HALF, dtype=F32) * 2.0 / (2 * ROT_HALF))
    ang = pos[:, None] * inv_freq[None, :]
    r = jnp.arange(LANES) % HEAD_DIM
    cos_t = jnp.where(r < 2 * ROT_HALF, jnp.cos(ang)[:, r % ROT_HALF], 1.0)
    sin_t = jnp.sin(ang)[:, r % ROT_HALF]
    sa = jnp.where(r < ROT_HALF, -sin_t, 0.0)
    sb = jnp.where((r >= ROT_HALF) & (r < 2 * ROT_HALF), sin_t, 0.0)
    tab = pl.BlockSpec((TILE, LANES), lambda i: (i % tpb, 0))
    return pl.pallas_call(
        _kv_kernel,
        grid=(bsz * tpb,),
        in_specs=[
            pl.BlockSpec((TILE, D_MODEL), lambda i: (i, 0)),
            pl.BlockSpec((1, D_MODEL), lambda i: (0, 0)),
            pl.BlockSpec((D_MODEL, D_MODEL), lambda i: (0, 0)),
            pl.BlockSpec((D_MODEL, D_MODEL), lambda i: (0, 0)),
            tab, tab, tab,
        ],
        out_specs=[
            pl.BlockSpec((1, TILE, D_MODEL), lambda i: (i // tpb, i % tpb, 0)),
            pl.BlockSpec((1, kblk, D_MODEL, LANES), lambda i: (i // tpb, i % tpb, 0, 0)),
        ],
        out_shape=[
            jax.ShapeDtypeStruct((bsz, lp, D_MODEL), BF16),
            jax.ShapeDtypeStruct((bsz, lp // LANES, D_MODEL, LANES), BF16),
        ],
        compiler_params=_cparams(("arbitrary",)),
        name="kv_proj",
    )(h, g.reshape(1, D_MODEL), w_kv[:, :D_MODEL].astype(BF16), w_kv[:, D_MODEL:].T.astype(BF16), cos_t, sa, sb)


def _q_kernel(h_ref, g_ref, wqt_ref, cos_ref, sin_ref, qt_ref):
    hn = _rms(h_ref[...], g_ref[...]).astype(BF16)
    qt = lax.dot_general(wqt_ref[...], hn, (((1,), (1,)), ((), ())), preferred_element_type=F32)
    cos, sin = cos_ref[...], sin_ref[...]
    scale = HEAD_DIM ** -0.5 * math.log2(math.e)
    for c in range(D_MODEL // HEAD_DIM):
        lo = c * HEAD_DIM
        x1 = qt[lo:lo + ROT_HALF]
        x2 = qt[lo + ROT_HALF:lo + 2 * ROT_HALF]
        blk = jnp.concatenate([x1 * cos - x2 * sin, x2 * cos + x1 * sin, qt[lo + 2 * ROT_HALF:lo + HEAD_DIM]], axis=0)
        qt_ref[lo:lo + HEAD_DIM, :] = (blk * scale).astype(BF16)


def _real_row_map(tiles_real, tiles_padded):
    front = FRONT // TILE
    return lambda i: ((i // tiles_real) * tiles_padded + front + i % tiles_real, 0)


def _q_proj(h_pad, g, w_q, bsz, seq, lp):
    tiles_real = seq // TILE
    pos = (N_META + jnp.arange(seq)).astype(F32)
    inv_freq = ROPE_THETA ** (-jnp.arange(ROT_HALF, dtype=F32) * 2.0 / (2 * ROT_HALF))
    ang = inv_freq[:, None] * pos[None, :]
    tab = pl.BlockSpec((ROT_HALF, TILE), lambda i: (0, i % tiles_real))
    return pl.pallas_call(
        _q_kernel,
        grid=(bsz * tiles_real,),
        in_specs=[
            pl.BlockSpec((TILE, D_MODEL), _real_row_map(tiles_real, lp // TILE)),
            pl.BlockSpec((1, D_MODEL), lambda i: (0, 0)),
            pl.BlockSpec((D_MODEL, D_MODEL), lambda i: (0, 0)),
            tab, tab,
        ],
        out_specs=pl.BlockSpec((D_MODEL, TILE), lambda i: (0, i)),
        out_shape=jax.ShapeDtypeStruct((D_MODEL, bsz * seq), BF16),
        compiler_params=_cparams(("arbitrary",)),
        name="q_proj",
    )(h_pad, g.reshape(1, D_MODEL), w_q.T.astype(BF16), jnp.cos(ang), jnp.sin(ang))


def _attn_kernel(qt_ref, k_ref, vt_ref, lam_ref, g_ref, o_ref, m_sc, l_sc, acc_sc, sa_sc, sb_sc, *, lam_init):
    i = pl.program_id(2)
    tq = qt_ref.shape[1]
    qt = qt_ref[...]
    zero = jnp.zeros((HEAD_DIM, tq), BF16)
    qs = (jnp.concatenate([qt[0:HEAD_DIM], zero], axis=0), jnp.concatenate([zero, qt[HEAD_DIM:]], axis=0))
    m_sc[...] = jnp.full(m_sc.shape, NEG, F32)
    l_sc[...] = jnp.zeros_like(l_sc)
    acc_sc[...] = jnp.zeros_like(acc_sc)

    def update(n, s, vtt):
        m_old = m_sc[n]
        m_new = jnp.maximum(m_old, jnp.max(s, axis=0, keepdims=True))
        alpha = jnp.exp2(m_old - m_new)
        p = jnp.exp2(s - m_new)
        l_sc[n] = alpha * l_sc[n] + jnp.sum(p, axis=0, keepdims=True)
        acc_sc[n] = alpha * acc_sc[n] + jnp.dot(vtt, p.astype(BF16), preferred_element_type=F32)
        m_sc[n] = m_new

    def values(row0, nkeys):
        blk0 = row0 // LANES
        return jnp.concatenate([vt_ref[0, blk0 + c] for c in range(nkeys // LANES)], axis=1)

    def key_row(t):
        return pl.multiple_of(FRONT + t * ATT_TK, LANES)

    n_sub = ATT_TK // ATT_SUB

    def scores_into(t, buf):
        for u in range(n_sub):
            kt = k_ref[0, pl.ds(pl.multiple_of(key_row(t) + u * ATT_SUB, LANES), ATT_SUB), :]
            for n in range(2):
                buf[u * 2 + n] = jnp.dot(kt, qs[n], preferred_element_type=F32)

    def consume(t, buf, vis):
        for u in range(n_sub):
            vtt = values(key_row(t) + u * ATT_SUB, ATT_SUB)
            for n in range(2):
                s = buf[u * 2 + n]
                if vis is not None:
                    s = jnp.where(vis[u * ATT_SUB:(u + 1) * ATT_SUB], s, NEG)
                update(n, s, vtt)

    meta_vis = lax.broadcasted_iota(jnp.int32, (LANES, tq), 0) >= META_PAD
    kt = k_ref[0, FRONT - LANES:FRONT, :]
    for n in range(2):
        update(n, jnp.where(meta_vis, jnp.dot(kt, qs[n], preferred_element_type=F32), NEG), values(FRONT - LANES, LANES))

    n_full = i // 2
    odd = n_full % 2

    @pl.when(odd == 1)
    def _():
        scores_into(0, sb_sc)
        consume(0, sb_sc, None)

    scores_into(odd, sa_sc)

    def body(j, c):
        t0 = odd + 2 * j
        scores_into(t0 + 1, sb_sc)
        consume(t0, sa_sc, None)
        scores_into(t0 + 2, sa_sc)
        consume(t0 + 1, sb_sc, None)
        return c

    lax.fori_loop(0, n_full // 2, body, 0)
    r = lax.broadcasted_iota(jnp.int32, (ATT_TK, tq), 0) // CHUNK
    c = lax.broadcasted_iota(jnp.int32, (ATT_TK, tq), 1) // CHUNK
    consume(n_full, sa_sc, r <= c + (i % 2) * (tq // CHUNK))

    lp = lam_ref[...]
    lam = (jnp.exp(jnp.sum(lp[0:1] * lp[1:2], axis=1, keepdims=True))
           - jnp.exp(jnp.sum(lp[2:3] * lp[3:4], axis=1, keepdims=True)) + lam_init)
    o = acc_sc[0] / l_sc[0] - lam * (acc_sc[1] / l_sc[1])
    o = o * lax.rsqrt(jnp.mean(o * o, axis=0, keepdims=True) + EPS) * g_ref[...] * (1.0 - lam_init)
    o_ref[...] = o.T.astype(BF16)


def _attention(qt, k, vt4, lam_p, subln_g, bsz, seq, lam_init):
    nq = seq // ATT_TQ
    lp = k.shape[1]
    return pl.pallas_call(
        functools.partial(_attn_kernel, lam_init=lam_init),
        grid=(bsz, N_HEADS, nq),
        in_specs=[
            pl.BlockSpec((2 * HEAD_DIM, ATT_TQ), lambda b, h, i: (h, b * nq + i)),
            pl.BlockSpec((1, lp, 2 * HEAD_DIM), lambda b, h, i: (b, 0, h)),
            pl.BlockSpec((1, lp // LANES, 2 * HEAD_DIM, LANES), lambda b, h, i: (b, 0, h, 0)),
            pl.BlockSpec((4, HEAD_DIM), lambda b, h, i: (0, 0)),
            pl.BlockSpec((2 * HEAD_DIM, 1), lambda b, h, i: (0, 0)),
        ],
        out_specs=pl.BlockSpec((ATT_TQ, 2 * HEAD_DIM), lambda b, h, i: (b * nq + i, h)),
        out_shape=jax.ShapeDtypeStruct((bsz * seq, D_MODEL), BF16),
        scratch_shapes=[pltpu.VMEM((2, 1, ATT_TQ), F32), pltpu.VMEM((2, 1, ATT_TQ), F32),
                        pltpu.VMEM((2, 2 * HEAD_DIM, ATT_TQ), F32),
                        pltpu.VMEM((2 * ATT_TK // ATT_SUB, ATT_SUB, ATT_TQ), F32),
                        pltpu.VMEM((2 * ATT_TK // ATT_SUB, ATT_SUB, ATT_TQ), F32)],
        compiler_params=_cparams(("arbitrary", "arbitrary", "arbitrary")),
        name="diff_attention",
    )(qt, k, vt4, lam_p, subln_g.reshape(2 * HEAD_DIM, 1))


def _oproj_kernel(h_ref, o_ref, w_ref, out_ref):
    out_ref[...] = h_ref[...] + jnp.dot(o_ref[...], w_ref[...], preferred_element_type=F32)


def _o_proj(h_pad, o, w_out, bsz, seq, lp):
    tiles_real = seq // TILE
    return pl.pallas_call(
        _oproj_kernel,
        grid=(bsz * tiles_real,),
        in_specs=[
            pl.BlockSpec((TILE, D_MODEL), _real_row_map(tiles_real, lp // TILE)),
            pl.BlockSpec((TILE, D_MODEL), lambda i: (i, 0)),
            pl.BlockSpec((D_MODEL, D_MODEL), lambda i: (0, 0)),
        ],
        out_specs=pl.BlockSpec((TILE, D_MODEL), lambda i: (i, 0)),
        out_shape=jax.ShapeDtypeStruct((bsz * seq, D_MODEL), F32),
        compiler_params=_cparams(("arbitrary",)),
        name="o_proj",
    )(h_pad, o, w_out.astype(BF16))


def kernel(x, meta_tokens, a_norm, a_w_in, a_conv, a_w_out, kv_norm, w_kv, b_norm, b_w_q, b_lambda, b_subln, b_w_out, ffn_norm, r_group, r_group_b, r_expert, r_expert_b, e_gate, e_up, e_down, final_norm):
    bsz, seq, d = x.shape
    assert d == D_MODEL and a_norm.shape[0] == 1 and b_norm.shape[0] == 1
    lp = FRONT + seq
    assert seq % ATT_TK == 0 and ATT_TK == 2 * ATT_TQ and ATT_TQ % TILE == 0
    front = jnp.concatenate([jnp.zeros((PAD_ROWS, d), x.dtype), meta_tokens.astype(x.dtype)], axis=0)
    h = _mixer(x.reshape(bsz * seq, d), front, a_norm[0], a_w_in[0], a_conv[0], a_w_out[0], bsz, lp // TILE)
    h = _moe(h, ffn_norm[0], r_group[0], r_group_b[0], r_expert[0], r_expert_b[0],
             e_gate, e_up, e_down, 0, final_norm, TILE, False)

    k, vt4 = _kv(h, kv_norm, w_kv, bsz, lp)
    qt = _q_proj(h, b_norm[0], b_w_q[0], bsz, seq, lp)
    lam_init = 0.8 - 0.6 * math.exp(-0.3 * a_norm.shape[0])
    o = _attention(qt, k, vt4, b_lambda[0], b_subln[0], bsz, seq, lam_init)
    h = _o_proj(h, o, b_w_out[0], bsz, seq, lp)
    h = _moe(h, ffn_norm[1], r_group[1], r_group_b[1], r_expert[1], r_expert_b[1],
             e_gate, e_up, e_down, 1, final_norm, TILE, True)
    return h.reshape(bsz, seq, d)
```

```python
import functools
import math

import jax
import jax.numpy as jnp
from jax import lax
from jax.experimental import pallas as pl
from jax.experimental.pallas import tpu as pltpu

D_MODEL = 1024
CHUNK = 64
N_META = 16
HEAD_DIM = 64
N_HEADS = 8
ROT_HALF = 8
ROPE_THETA = 500000.0
N_GROUPS = 4
EXPERTS_PER_GROUP = 8
N_EXPERTS = 32
D_EXPERT = 512
EPS = 1e-6

LANES = 128
SUBLANES = 8
TILE = 512
FRONT = TILE
PAD_ROWS = FRONT - N_META
META_PAD = LANES - N_META
EXPERT_ROWS = 512
ATT_TQ = 512
ATT_TK = 1024
ATT_SUB = 1024
SEG_ALIGN = 16
STAGE_SLOTS = 2
LOCAL_ROWS = 1536
VMEM_LIMIT = 56 * 1024 * 1024
NEG = float(jnp.finfo(jnp.float32).min)

BF16 = jnp.bfloat16
F32 = jnp.float32


def _rms(x, g):
    return x * lax.rsqrt(jnp.mean(x * x, axis=-1, keepdims=True) + EPS) * g


def _cparams(sem):
    return pltpu.CompilerParams(dimension_semantics=sem, vmem_limit_bytes=VMEM_LIMIT)


def _mixer_kernel(x_ref, front_ref, g_ref, win_ref, cw_ref, wout_ref, o_ref, cu_ref, *, tiles_per_batch):
    i = pl.program_id(0)
    tm = x_ref.shape[0]
    x = jnp.where(i % tiles_per_batch == 0, front_ref[...], x_ref[...])
    hn = _rms(x, g_ref[...]).astype(BF16)
    proj = jnp.dot(hn, win_ref[...], preferred_element_type=F32)
    b_gate = proj[:, :D_MODEL]
    cu = proj[:, D_MODEL:2 * D_MODEL] * proj[:, 2 * D_MODEL:]

    @pl.when(i % tiles_per_batch == 0)
    def _():
        cu_ref[0:SUBLANES, :] = jnp.zeros((SUBLANES, D_MODEL), F32)

    cu_ref[SUBLANES:SUBLANES + tm, :] = cu
    cw = cw_ref[...]
    z = (cw[2:3] * cu + cw[1:2] * cu_ref[SUBLANES - 1:SUBLANES - 1 + tm, :]
         + cw[0:1] * cu_ref[SUBLANES - 2:SUBLANES - 2 + tm, :])
    cu_ref[0:SUBLANES, :] = cu_ref[tm:tm + SUBLANES, :]
    y = jnp.dot((b_gate * z).astype(BF16), wout_ref[...], preferred_element_type=F32)
    o_ref[...] = x + y


def _mixer(x, front, g, w_in, conv_w, w_out, bsz, tiles_per_batch):
    t = bsz * tiles_per_batch * TILE
    tiles_real = tiles_per_batch - FRONT // TILE

    def x_map(i):
        return ((i // tiles_per_batch) * tiles_real + jnp.maximum(i % tiles_per_batch - FRONT // TILE, 0), 0)

    return pl.pallas_call(
        functools.partial(_mixer_kernel, tiles_per_batch=tiles_per_batch),
        grid=(t // TILE,),
        in_specs=[
            pl.BlockSpec((TILE, D_MODEL), x_map),
            pl.BlockSpec((TILE, D_MODEL), lambda i: (0, 0)),
            pl.BlockSpec((1, D_MODEL), lambda i: (0, 0)),
            pl.BlockSpec((D_MODEL, 3 * D_MODEL), lambda i: (0, 0)),
            pl.BlockSpec((3, D_MODEL), lambda i: (0, 0)),
            pl.BlockSpec((D_MODEL, D_MODEL), lambda i: (0, 0)),
        ],
        out_specs=pl.BlockSpec((TILE, D_MODEL), lambda i: (i, 0)),
        out_shape=jax.ShapeDtypeStruct((t, D_MODEL), F32),
        scratch_shapes=[pltpu.VMEM((TILE + 2 * SUBLANES, D_MODEL), F32)],
        compiler_params=_cparams(("arbitrary",)),
        name="mixer_a",
    )(x, front, g.reshape(1, D_MODEL), w_in.astype(BF16), conv_w, w_out.astype(BF16))


ROUTER_ROWS = SUBLANES + N_EXPERTS


def _round_up(v, m):
    return jnp.floor((v + (m - 1)) * (1.0 / m)) * m


def _router_kernel(h_ref, g_ref, wrt_ref, br_ref, tri_ref, lpos_ref, gate_ref, meta_ref, tot_ref, seg_sc):
    i = pl.program_id(0)
    tm = h_ref.shape[0]

    @pl.when(i == 0)
    def _():
        seg_sc[...] = jnp.zeros_like(seg_sc)

    xn = _rms(h_ref[...], g_ref[...])
    lt = lax.dot_general(wrt_ref[...], xn, (((1,), (1,)), ((), ())),
                         precision=lax.Precision.HIGHEST, preferred_element_type=F32) + br_ref[...]
    lg = lt[0:N_GROUPS]
    gmax = jnp.max(lg, axis=0, keepdims=True)
    iota_g = lax.broadcasted_iota(jnp.int32, lg.shape, 0)
    grp = jnp.min(jnp.where(lg == gmax, iota_g, N_GROUPS), axis=0, keepdims=True)
    p_grp = 1.0 / jnp.sum(jnp.exp(lg - gmax), axis=0, keepdims=True)
    le = lt[SUBLANES:SUBLANES + EXPERTS_PER_GROUP]
    for g in range(1, N_GROUPS):
        lo = SUBLANES + g * EXPERTS_PER_GROUP
        le = jnp.where(grp == g, lt[lo:lo + EXPERTS_PER_GROUP], le)
    iota_e = lax.broadcasted_iota(jnp.int32, le.shape, 0)
    v1 = jnp.max(le, axis=0, keepdims=True)
    i1 = jnp.min(jnp.where(le == v1, iota_e, EXPERTS_PER_GROUP), axis=0, keepdims=True)
    le2 = jnp.where(iota_e == i1, -jnp.inf, le)
    v2 = jnp.max(le2, axis=0, keepdims=True)
    i2 = jnp.min(jnp.where(le2 == v2, iota_e, EXPERTS_PER_GROUP), axis=0, keepdims=True)
    e2 = jnp.exp(v2 - v1)
    den = 1.0 + e2
    gate_ref[...] = jnp.concatenate([p_grp * (1.0 / den), p_grp * (e2 / den)], axis=0)
    experts = (grp * EXPERTS_PER_GROUP + i1, grp * EXPERTS_PER_GROUP + i2)

    iota_x = lax.broadcasted_iota(jnp.int32, (N_EXPERTS, tm), 0)
    hits = [iota_x == e_k for e_k in experts]
    hits_f = [jnp.where(hit, 1.0, 0.0) for hit in hits]
    hits_b = [hf.astype(BF16) for hf in hits_f]
    excl = [jnp.dot(hb, tri_ref[...], preferred_element_type=F32) for hb in hits_b]
    tot_col = [jnp.sum(hf, axis=1, keepdims=True) for hf in hits_f]
    ones = jnp.ones((SUBLANES, tm), BF16)
    n_lane = sum(lax.dot_general(ones, hb, (((1,), (1,)), ((), ())), preferred_element_type=F32) for hb in hits_b)
    seg_len_lane = _round_up(n_lane[0:1], SEG_ALIGN)
    seg_len = _round_up(tot_col[0] + tot_col[1], SEG_ALIGN)
    before = (lax.broadcasted_iota(jnp.int32, (N_EXPERTS, N_EXPERTS), 1)
              < lax.broadcasted_iota(jnp.int32, (N_EXPERTS, N_EXPERTS), 0))
    local_off = jnp.sum(jnp.where(before, seg_len_lane, 0.0), axis=1, keepdims=True)

    lpos_a = jnp.sum(jnp.where(hits[0], excl[0] + local_off, 0.0), axis=0, keepdims=True).astype(jnp.int32)
    lpos_b = jnp.sum(jnp.where(hits[1], excl[1] + tot_col[0] + local_off, 0.0), axis=0, keepdims=True).astype(jnp.int32)
    lpos_ref[...] = jnp.concatenate([lpos_a, lpos_b], axis=0)

    seg_before = seg_sc[...]
    seg_sc[...] = seg_before + seg_len
    lane = lax.broadcasted_iota(jnp.int32, (N_EXPERTS, LANES), 1)
    meta = jnp.where(lane == 0, seg_len, jnp.where(lane == 1, local_off, jnp.where(lane == 2, seg_before, 0.0)))
    meta_ref[0] = meta.astype(jnp.int32)
    tot_ref[...] = jnp.broadcast_to(seg_before + seg_len, tot_ref.shape).astype(jnp.int32)


def _router(h, g, w_rg, b_rg, w_re, b_re, tm):
    t = h.shape[0]
    nt = t // tm
    wrt = jnp.zeros((ROUTER_ROWS, D_MODEL), F32)
    wrt = wrt.at[0:N_GROUPS].set(w_rg.T).at[SUBLANES:].set(w_re.T)
    br = jnp.zeros((ROUTER_ROWS, 1), F32)
    br = br.at[0:N_GROUPS, 0].set(b_rg).at[SUBLANES:, 0].set(b_re)
    tri = (lax.broadcasted_iota(jnp.int32, (tm, tm), 0) < lax.broadcasted_iota(jnp.int32, (tm, tm), 1)).astype(BF16)
    return pl.pallas_call(
        _router_kernel,
        grid=(nt,),
        in_specs=[
            pl.BlockSpec((tm, D_MODEL), lambda i: (i, 0)),
            pl.BlockSpec((1, D_MODEL), lambda i: (0, 0)),
            pl.BlockSpec((ROUTER_ROWS, D_MODEL), lambda i: (0, 0)),
            pl.BlockSpec((ROUTER_ROWS, 1), lambda i: (0, 0)),
            pl.BlockSpec((tm, tm), lambda i: (0, 0)),
        ],
        out_specs=[
            pl.BlockSpec((2, tm), lambda i: (0, i)),
            pl.BlockSpec((2, tm), lambda i: (0, i)),
            pl.BlockSpec((1, N_EXPERTS, LANES), lambda i: (i, 0, 0)),
            pl.BlockSpec((N_EXPERTS, LANES), lambda i: (0, 0)),
        ],
        out_shape=[
            jax.ShapeDtypeStruct((2, t), jnp.int32),
            jax.ShapeDtypeStruct((2, t), F32),
            jax.ShapeDtypeStruct((nt, N_EXPERTS, LANES), jnp.int32),
            jax.ShapeDtypeStruct((N_EXPERTS, LANES), jnp.int32),
        ],
        scratch_shapes=[pltpu.VMEM((N_EXPERTS, 1), F32)],
        compiler_params=_cparams(("arbitrary",)),
        name="router",
    )(h, g.reshape(1, D_MODEL), wrt, br, tri)


def _if_rows(n, fn):
    pl.when(n > 0)(lambda: fn(pl.multiple_of(n, SEG_ALIGN)))


def _rows(ref, start, size):
    return ref.at[pl.ds(pl.multiple_of(start, SEG_ALIGN), size), :]


def _dispatch_kernel(len_ref, off_ref, dst_ref, zs_ref, zn_ref, nu_ref, h_ref, g_ref, lpos_ref, buf_ref, stage, zero_sc,
                     sem_out, sem_fill, *, n_blk):
    t = pl.program_id(0)
    nt = pl.num_programs(0)
    tm = h_ref.shape[0]
    slot = t % STAGE_SLOTS

    def tile_out(t, s, op):
        def body(e, c):
            idx = t * N_EXPERTS + e
            lo, dst = off_ref[idx], dst_ref[idx]
            _if_rows(len_ref[idx], lambda n: op(pltpu.make_async_copy(
                _rows(stage.at[s], lo, n), _rows(buf_ref, dst, n), sem_out.at[s])))
            return c

        lax.fori_loop(0, N_EXPERTS, body, 0)

    def fill_copies(op):
        def tails(e, c):
            _if_rows(zn_ref[e], lambda n: op(pltpu.make_async_copy(
                zero_sc.at[pl.ds(0, n), :], _rows(buf_ref, zs_ref[e], n), sem_fill)))
            return c

        lax.fori_loop(0, N_EXPERTS, tails, 0)

        def blocks(b, c):
            op(pltpu.make_async_copy(zero_sc, _rows(buf_ref, b * EXPERT_ROWS, EXPERT_ROWS), sem_fill))
            return c

        lax.fori_loop(nu_ref[0], n_blk, blocks, 0)

    start = lambda cp: cp.start()
    wait = lambda cp: cp.wait()

    @pl.when(t == 0)
    def _():
        zero_sc[...] = jnp.zeros_like(zero_sc)
        fill_copies(start)

    @pl.when(t >= STAGE_SLOTS)
    def _():
        tile_out(t - STAGE_SLOTS, slot, wait)

    xn = _rms(h_ref[...], g_ref[...]).astype(BF16)
    lpos = lpos_ref[...]
    row = lax.broadcasted_iota(jnp.int32, (LOCAL_ROWS, tm), 0)
    perm = jnp.where(row == lpos[0:1], 1.0, jnp.where(row == lpos[1:2], 1.0, 0.0)).astype(BF16)
    stage[slot] = jnp.dot(perm, xn, preferred_element_type=F32).astype(BF16)
    tile_out(t, slot, start)

    @pl.when(t == nt - 1)
    def _():
        for back in range(STAGE_SLOTS - 1, -1, -1):
            @pl.when(t >= back)
            def _(back=back):
                tile_out(t - back, (t - back) % STAGE_SLOTS, wait)

        fill_copies(wait)


def _dispatch(h, g, lpos, seg_len, seg_off, seg_dst, tail_start, tail_len, n_used, n_blk, tm):
    t = h.shape[0]
    grid_spec = pltpu.PrefetchScalarGridSpec(
        num_scalar_prefetch=6,
        grid=(t // tm,),
        in_specs=[
            pl.BlockSpec((tm, D_MODEL), lambda i, *_: (i, 0)),
            pl.BlockSpec((1, D_MODEL), lambda i, *_: (0, 0)),
            pl.BlockSpec((2, tm), lambda i, *_: (0, i)),
        ],
        out_specs=pl.BlockSpec(memory_space=pl.ANY),
        scratch_shapes=[pltpu.VMEM((STAGE_SLOTS, LOCAL_ROWS, D_MODEL), BF16), pltpu.VMEM((EXPERT_ROWS, D_MODEL), BF16),
                        pltpu.SemaphoreType.DMA((STAGE_SLOTS,)), pltpu.SemaphoreType.DMA],
    )
    return pl.pallas_call(
        functools.partial(_dispatch_kernel, n_blk=n_blk),
        grid_spec=grid_spec,
        out_shape=jax.ShapeDtypeStruct((n_blk * EXPERT_ROWS, D_MODEL), BF16),
        compiler_params=_cparams(("arbitrary",)),
        name="dispatch",
    )(seg_len, seg_off, seg_dst, tail_start, tail_len, n_used, h, g.reshape(1, D_MODEL), lpos)


def _expert_kernel(be_ref, nu_ref, x_ref, wg_ref, wu_ref, wd_ref, y_ref, wgu_sc, wd_sc):
    i = pl.program_id(0)
    e = be_ref[i]
    prev = be_ref[jnp.maximum(i - 1, 0)]

    @pl.when(jnp.logical_or(i == 0, e != prev))
    def _():
        wgu_sc[:, :D_EXPERT] = wg_ref[0, 0].astype(BF16)
        wgu_sc[:, D_EXPERT:] = wu_ref[0, 0].astype(BF16)
        wd_sc[...] = wd_ref[0, 0].astype(BF16)

    @pl.when(i < nu_ref[0])
    def _():
        gu = jnp.dot(x_ref[...], wgu_sc[...], preferred_element_type=F32)
        a = jax.nn.silu(gu[:, :D_EXPERT]) * gu[:, D_EXPERT:]
        y_ref[...] = jnp.dot(a.astype(BF16), wd_sc[...], preferred_element_type=F32).astype(BF16)

    @pl.when(i >= nu_ref[0])
    def _():
        y_ref[...] = jnp.zeros_like(y_ref)


def _experts(buf, blk_expert, n_used, w_gate, w_up, w_down, layer):
    n_blk = blk_expert.shape[0]

    def xmap(i, be, nu):
        return (jnp.maximum(jnp.minimum(i, nu[0] - 1), 0), 0)

    grid_spec = pltpu.PrefetchScalarGridSpec(
        num_scalar_prefetch=2,
        grid=(n_blk,),
        in_specs=[
            pl.BlockSpec((EXPERT_ROWS, D_MODEL), xmap),
            pl.BlockSpec((1, 1, D_MODEL, D_EXPERT), lambda i, be, nu: (layer, be[i], 0, 0)),
            pl.BlockSpec((1, 1, D_MODEL, D_EXPERT), lambda i, be, nu: (layer, be[i], 0, 0)),
            pl.BlockSpec((1, 1, D_EXPERT, D_MODEL), lambda i, be, nu: (layer, be[i], 0, 0)),
        ],
        out_specs=pl.BlockSpec((EXPERT_ROWS, D_MODEL), lambda i, be, nu: (i, 0)),
        scratch_shapes=[pltpu.VMEM((D_MODEL, 2 * D_EXPERT), BF16), pltpu.VMEM((D_EXPERT, D_MODEL), BF16)],
    )
    return pl.pallas_call(
        _expert_kernel,
        grid_spec=grid_spec,
        out_shape=jax.ShapeDtypeStruct(buf.shape, BF16),
        compiler_params=_cparams(("arbitrary",)),
        name="experts",
    )(blk_expert, n_used, buf, w_gate, w_up, w_down)


def _combine_kernel(len_ref, off_ref, dst_ref, h_ref, gate_ref, lpos_ref, y_ref, fn_ref, o_ref, yl, sem, *, final_norm):
    i = pl.program_id(0)
    nt = pl.num_programs(0)
    tm = h_ref.shape[0]
    slot = i % 2

    def tile_copies(t, s, op):
        def body(e, c):
            idx = t * N_EXPERTS + e
            lo, src = off_ref[idx], dst_ref[idx]
            _if_rows(len_ref[idx], lambda n: op(pltpu.make_async_copy(
                _rows(y_ref, src, n), _rows(yl.at[s], lo, n), sem.at[s])))
            return c

        lax.fori_loop(0, N_EXPERTS, body, 0)

    @pl.when(i == 0)
    def _():
        yl[...] = jnp.zeros_like(yl)
        tile_copies(0, 0, lambda cp: cp.start())

    @pl.when(i + 1 < nt)
    def _():
        tile_copies(i + 1, 1 - slot, lambda cp: cp.start())

    tile_copies(i, slot, lambda cp: cp.wait())

    rows = yl[slot]
    lpos = lpos_ref[...]
    col = lax.broadcasted_iota(jnp.int32, (tm, LOCAL_ROWS), 1)
    g = gate_ref[...]
    weights = jnp.where(col == lpos[:, 0:1], g[:, 0:1], jnp.where(col == lpos[:, 1:2], g[:, 1:2], 0.0)).astype(BF16)
    out = h_ref[...] + jnp.dot(weights, rows, preferred_element_type=F32)
    if final_norm:
        out = _rms(out, fn_ref[...])
    o_ref[...] = out


def _combine(h, y, seg_len, seg_off, seg_dst, lpos, gate, fn, tm, final_norm):
    t = h.shape[0]
    grid_spec = pltpu.PrefetchScalarGridSpec(
        num_scalar_prefetch=3,
        grid=(t // tm,),
        in_specs=[
            pl.BlockSpec((tm, D_MODEL), lambda i, *_: (i, 0)),
            pl.BlockSpec((tm, 2), lambda i, *_: (i, 0)),
            pl.BlockSpec((tm, 2), lambda i, *_: (i, 0)),
            pl.BlockSpec(memory_space=pl.ANY),
            pl.BlockSpec((1, D_MODEL), lambda i, *_: (0, 0)),
        ],
        out_specs=pl.BlockSpec((tm, D_MODEL), lambda i, *_: (i, 0)),
        scratch_shapes=[pltpu.VMEM((2, LOCAL_ROWS, D_MODEL), BF16), pltpu.SemaphoreType.DMA((2,))],
    )
    return pl.pallas_call(
        functools.partial(_combine_kernel, final_norm=final_norm),
        grid_spec=grid_spec,
        out_shape=jax.ShapeDtypeStruct((t, D_MODEL), F32),
        compiler_params=_cparams(("arbitrary",)),
        name="combine",
    )(seg_len, seg_off, seg_dst, h, gate.T, lpos.T, y, fn.reshape(1, D_MODEL))


def _moe(h, g, w_rg, b_rg, w_re, b_re, w_gate, w_up, w_down, layer, fn, tm, final_norm):
    t = h.shape[0]
    nt = t // tm
    lpos, gate, meta, tot = _router(h, g, w_rg, b_rg, w_re, b_re, tm)
    n_blk = (2 * t + nt * N_EXPERTS * (SEG_ALIGN - 1)) // EXPERT_ROWS + N_EXPERTS
    total = tot[:, 0]
    region = (total + EXPERT_ROWS - 1) // EXPERT_ROWS * EXPERT_ROWS
    ends = jnp.cumsum(region)
    starts = ends - region
    seg_len = meta[:, :, 0].reshape(-1)
    seg_off = meta[:, :, 1].reshape(-1)
    seg_dst = (meta[:, :, 2] + starts[None, :]).reshape(-1)
    blk_start = jnp.arange(n_blk, dtype=jnp.int32) * EXPERT_ROWS
    blk_expert = jnp.minimum(jnp.sum(blk_start[:, None] >= ends[None, :], axis=1), N_EXPERTS - 1).astype(jnp.int32)
    n_used = (ends[-1:] // EXPERT_ROWS).astype(jnp.int32)
    buf = _dispatch(h, g, lpos, seg_len, seg_off, seg_dst, starts + total, region - total, n_used, n_blk, tm)
    y = _experts(buf, blk_expert, n_used, w_gate, w_up, w_down, layer)
    return _combine(h, y, seg_len, seg_off, seg_dst, lpos, gate, fn, tm, final_norm)


def _kv_kernel(h_ref, g_ref, wk_ref, wvt_ref, cos_ref, sa_ref, sb_ref, k_ref, vt_ref):
    hn = _rms(h_ref[...], g_ref[...]).astype(BF16)
    k = jnp.dot(hn, wk_ref[...], preferred_element_type=F32)
    cos, sa, sb = cos_ref[...], sa_ref[...], sb_ref[...]
    for c in range(D_MODEL // LANES):
        kc = k[:, c * LANES:(c + 1) * LANES]
        rot = kc * cos + pltpu.roll(kc, LANES - ROT_HALF, 1) * sa + pltpu.roll(kc, ROT_HALF, 1) * sb
        k_ref[0, :, c * LANES:(c + 1) * LANES] = rot.astype(BF16)
    vt = lax.dot_general(wvt_ref[...], hn, (((1,), (1,)), ((), ())), preferred_element_type=F32)
    for c in range(vt_ref.shape[1]):
        vt_ref[0, c] = vt[:, c * LANES:(c + 1) * LANES].astype(BF16)


def _kv(h, g, w_kv, bsz, lp):
    tpb = lp // TILE
    kblk = TILE // LANES
    pos = jnp.maximum(jnp.arange(lp) - PAD_ROWS, 0).astype(F32)
    inv_freq = ROPE_THETA ** (-jnp.arange(ROT_HALF, dtype=F32) * 2.0 / (2 * ROT_HALF))
    ang = pos[:, None] * inv_freq[None, :]
    r = jnp.arange(LANES) % HEAD_DIM
    cos_t = jnp.where(r < 2 * ROT_HALF, jnp.cos(ang)[:, r % ROT_HALF], 1.0)
    sin_t = jnp.sin(ang)[:, r % ROT_HALF]
    sa = jnp.where(r < ROT_HALF, -sin_t, 0.0)
    sb = jnp.where((r >= ROT_HALF) & (r < 2 * ROT_HALF), sin_t, 0.0)
    tab = pl.BlockSpec((TILE, LANES), lambda i: (i % tpb, 0))
    return pl.pallas_call(
        _kv_kernel,
        grid=(bsz * tpb,),
        in_specs=[
            pl.BlockSpec((TILE, D_MODEL), lambda i: (i, 0)),
            pl.BlockSpec((1, D_MODEL), lambda i: (0, 0)),
            pl.BlockSpec((D_MODEL, D_MODEL), lambda i: (0, 0)),
            pl.BlockSpec((D_MODEL, D_MODEL), lambda i: (0, 0)),
            tab, tab, tab,
        ],
        out_specs=[
            pl.BlockSpec((1, TILE, D_MODEL), lambda i: (i // tpb, i % tpb, 0)),
            pl.BlockSpec((1, kblk, D_MODEL, LANES), lambda i: (i // tpb, i % tpb, 0, 0)),
        ],
        out_shape=[
            jax.ShapeDtypeStruct((bsz, lp, D_MODEL), BF16),
            jax.ShapeDtypeStruct((bsz, lp // LANES, D_MODEL, LANES), BF16),
        ],
        compiler_params=_cparams(("arbitrary",)),
        name="kv_proj",
    )(h, g.reshape(1, D_MODEL), w_kv[:, :D_MODEL].astype(BF16), w_kv[:, D_MODEL:].T.astype(BF16), cos_t, sa, sb)


def _q_kernel(h_ref, g_ref, wqt_ref, cos_ref, sin_ref, qt_ref):
    hn = _rms(h_ref[...], g_ref[...]).astype(BF16)
    qt = lax.dot_general(wqt_ref[...], hn, (((1,), (1,)), ((), ())), preferred_element_type=F32)
    cos, sin = cos_ref[...], sin_ref[...]
    scale = HEAD_DIM ** -0.5 * math.log2(math.e)
    for c in range(D_MODEL // HEAD_DIM):
        lo = c * HEAD_DIM
        x1 = qt[lo:lo + ROT_HALF]
        x2 = qt[lo + ROT_HALF:lo + 2 * ROT_HALF]
        blk = jnp.concatenate([x1 * cos - x2 * sin, x2 * cos + x1 * sin, qt[lo + 2 * ROT_HALF:lo + HEAD_DIM]], axis=0)
        qt_ref[lo:lo + HEAD_DIM, :] = (blk * scale).astype(BF16)


def _real_row_map(tiles_real, tiles_padded):
    front = FRONT // TILE
    return lambda i: ((i // tiles_real) * tiles_padded + front + i % tiles_real, 0)


def _q_proj(h_pad, g, w_q, bsz, seq, lp):
    tiles_real = seq // TILE
    pos = (N_META + jnp.arange(seq)).astype(F32)
    inv_freq = ROPE_THETA ** (-jnp.arange(ROT_HALF, dtype=F32) * 2.0 / (2 * ROT_HALF))
    ang = inv_freq[:, None] * pos[None, :]
    tab = pl.BlockSpec((ROT_HALF, TILE), lambda i: (0, i % tiles_real))
    return pl.pallas_call(
        _q_kernel,
        grid=(bsz * tiles_real,),
        in_specs=[
            pl.BlockSpec((TILE, D_MODEL), _real_row_map(tiles_real, lp // TILE)),
            pl.BlockSpec((1, D_MODEL), lambda i: (0, 0)),
            pl.BlockSpec((D_MODEL, D_MODEL), lambda i: (0, 0)),
            tab, tab,
        ],
        out_specs=pl.BlockSpec((D_MODEL, TILE), lambda i: (0, i)),
        out_shape=jax.ShapeDtypeStruct((D_MODEL, bsz * seq), BF16),
        compiler_params=_cparams(("arbitrary",)),
        name="q_proj",
    )(h_pad, g.reshape(1, D_MODEL), w_q.T.astype(BF16), jnp.cos(ang), jnp.sin(ang))


def _attn_kernel(qt_ref, k_ref, vt_ref, lam_ref, g_ref, o_ref, m_sc, l_sc, acc_sc, sa_sc, sb_sc, *, lam_init):
    i = pl.program_id(2)
    tq = qt_ref.shape[1]
    qt = qt_ref[...]
    zero = jnp.zeros((HEAD_DIM, tq), BF16)
    qs = (jnp.concatenate([qt[0:HEAD_DIM], zero], axis=0), jnp.concatenate([zero, qt[HEAD_DIM:]], axis=0))
    m_sc[...] = jnp.full(m_sc.shape, NEG, F32)
    l_sc[...] = jnp.zeros_like(l_sc)
    acc_sc[...] = jnp.zeros_like(acc_sc)

    def update(n, s, vtt):
        m_old = m_sc[n]
        m_new = jnp.maximum(m_old, jnp.max(s, axis=0, keepdims=True))
        alpha = jnp.exp2(m_old - m_new)
        p = jnp.exp2(s - m_new)
        l_sc[n] = alpha * l_sc[n] + jnp.sum(p, axis=0, keepdims=True)
        acc_sc[n] = alpha * acc_sc[n] + jnp.dot(vtt, p.astype(BF16), preferred_element_type=F32)
        m_sc[n] = m_new

    def values(row0, nkeys):
        blk0 = row0 // LANES
        return jnp.concatenate([vt_ref[0, blk0 + c] for c in range(nkeys // LANES)], axis=1)

    def key_row(t):
        return pl.multiple_of(FRONT + t * ATT_TK, LANES)

    n_sub = ATT_TK // ATT_SUB

    def scores_into(t, buf):
        for u in range(n_sub):
            kt = k_ref[0, pl.ds(pl.multiple_of(key_row(t) + u * ATT_SUB, LANES), ATT_SUB), :]
            for n in range(2):
                buf[u * 2 + n] = jnp.dot(kt, qs[n], preferred_element_type=F32)

    def consume(t, buf, vis):
        for u in range(n_sub):
            vtt = values(key_row(t) + u * ATT_SUB, ATT_SUB)
            for n in range(2):
                s = buf[u * 2 + n]
                if vis is not None:
                    s = jnp.where(vis[u * ATT_SUB:(u + 1) * ATT_SUB], s, NEG)
                update(n, s, vtt)

    meta_vis = lax.broadcasted_iota(jnp.int32, (LANES, tq), 0) >= META_PAD
    kt = k_ref[0, FRONT - LANES:FRONT, :]
    for n in range(2):
        update(n, jnp.where(meta_vis, jnp.dot(kt, qs[n], preferred_element_type=F32), NEG), values(FRONT - LANES, LANES))

    n_full = i // 2
    odd = n_full % 2

    @pl.when(odd == 1)
    def _():
        scores_into(0, sb_sc)
        consume(0, sb_sc, None)

    scores_into(odd, sa_sc)

    def body(j, c):
        t0 = odd + 2 * j
        scores_into(t0 + 1, sb_sc)
        consume(t0, sa_sc, None)
        scores_into(t0 + 2, sa_sc)
        consume(t0 + 1, sb_sc, None)
        return c

    lax.fori_loop(0, n_full // 2, body, 0)
    r = lax.broadcasted_iota(jnp.int32, (tq, tq), 0) // CHUNK
    c = lax.broadcasted_iota(jnp.int32, (tq, tq), 1) // CHUNK

    def boundary_half(u, vis):
        vtt = values(key_row(n_full) + u * tq, tq)
        for n in range(2):
            update(n, jnp.where(vis, sa_sc[n, u * tq:(u + 1) * tq, :], NEG), vtt)

    boundary_half(0, r <= c + (i % 2) * (tq // CHUNK))
    pl.when(i % 2 == 1)(lambda: boundary_half(1, r <= c))

    lp = lam_ref[...]
    lam = (jnp.exp(jnp.sum(lp[0:1] * lp[1:2], axis=1, keepdims=True))
           - jnp.exp(jnp.sum(lp[2:3] * lp[3:4], axis=1, keepdims=True)) + lam_init)
    o = acc_sc[0] / l_sc[0] - lam * (acc_sc[1] / l_sc[1])
    o = o * lax.rsqrt(jnp.mean(o * o, axis=0, keepdims=True) + EPS) * g_ref[...] * (1.0 - lam_init)
    o_ref[...] = o.T.astype(BF16)


def _attention(qt, k, vt4, lam_p, subln_g, bsz, seq, lam_init):
    nq = seq // ATT_TQ
    lp = k.shape[1]
    return pl.pallas_call(
        functools.partial(_attn_kernel, lam_init=lam_init),
        grid=(bsz, N_HEADS, nq),
        in_specs=[
            pl.BlockSpec((2 * HEAD_DIM, ATT_TQ), lambda b, h, i: (h, b * nq + i)),
            pl.BlockSpec((1, lp, 2 * HEAD_DIM), lambda b, h, i: (b, 0, h)),
            pl.BlockSpec((1, lp // LANES, 2 * HEAD_DIM, LANES), lambda b, h, i: (b, 0, h, 0)),
            pl.BlockSpec((4, HEAD_DIM), lambda b, h, i: (0, 0)),
            pl.BlockSpec((2 * HEAD_DIM, 1), lambda b, h, i: (0, 0)),
        ],
        out_specs=pl.BlockSpec((ATT_TQ, 2 * HEAD_DIM), lambda b, h, i: (b * nq + i, h)),
        out_shape=jax.ShapeDtypeStruct((bsz * seq, D_MODEL), BF16),
        scratch_shapes=[pltpu.VMEM((2, 1, ATT_TQ), F32), pltpu.VMEM((2, 1, ATT_TQ), F32),
                        pltpu.VMEM((2, 2 * HEAD_DIM, ATT_TQ), F32),
                        pltpu.VMEM((2 * ATT_TK // ATT_SUB, ATT_SUB, ATT_TQ), F32),
                        pltpu.VMEM((2 * ATT_TK // ATT_SUB, ATT_SUB, ATT_TQ), F32)],
        compiler_params=_cparams(("arbitrary", "arbitrary", "arbitrary")),
        name="diff_attention",
    )(qt, k, vt4, lam_p, subln_g.reshape(2 * HEAD_DIM, 1))


def _oproj_kernel(h_ref, o_ref, w_ref, out_ref):
    out_ref[...] = h_ref[...] + jnp.dot(o_ref[...], w_ref[...], preferred_element_type=F32)


def _o_proj(h_pad, o, w_out, bsz, seq, lp):
    tiles_real = seq // TILE
    return pl.pallas_call(
        _oproj_kernel,
        grid=(bsz * tiles_real,),
        in_specs=[
            pl.BlockSpec((TILE, D_MODEL), _real_row_map(tiles_real, lp // TILE)),
            pl.BlockSpec((TILE, D_MODEL), lambda i: (i, 0)),
            pl.BlockSpec((D_MODEL, D_MODEL), lambda i: (0, 0)),
        ],
        out_specs=pl.BlockSpec((TILE, D_MODEL), lambda i: (i, 0)),
        out_shape=jax.ShapeDtypeStruct((bsz * seq, D_MODEL), F32),
        compiler_params=_cparams(("arbitrary",)),
        name="o_proj",
    )(h_pad, o, w_out.astype(BF16))


def kernel(x, meta_tokens, a_norm, a_w_in, a_conv, a_w_out, kv_norm, w_kv, b_norm, b_w_q, b_lambda, b_subln, b_w_out, ffn_norm, r_group, r_group_b, r_expert, r_expert_b, e_gate, e_up, e_down, final_norm):
    bsz, seq, d = x.shape
    assert d == D_MODEL and a_norm.shape[0] == 1 and b_norm.shape[0] == 1
    lp = FRONT + seq
    assert seq % ATT_TK == 0 and ATT_TK == 2 * ATT_TQ and ATT_TQ % TILE == 0 and ATT_SUB == ATT_TK
    front = jnp.concatenate([jnp.zeros((PAD_ROWS, d), x.dtype), meta_tokens.astype(x.dtype)], axis=0)
    h = _mixer(x.reshape(bsz * seq, d), front, a_norm[0], a_w_in[0], a_conv[0], a_w_out[0], bsz, lp // TILE)
    h = _moe(h, ffn_norm[0], r_group[0], r_group_b[0], r_expert[0], r_expert_b[0],
             e_gate, e_up, e_down, 0, final_norm, TILE, False)

    k, vt4 = _kv(h, kv_norm, w_kv, bsz, lp)
    qt = _q_proj(h, b_norm[0], b_w_q[0], bsz, seq, lp)
    lam_init = 0.8 - 0.6 * math.exp(-0.3 * a_norm.shape[0])
    o = _attention(qt, k, vt4, b_lambda[0], b_subln[0], bsz, seq, lam_init)
    h = _o_proj(h, o, b_w_out[0], bsz, seq, lp)
    h = _moe(h, ffn_norm[1], r_group[1], r_group_b[1], r_expert[1], r_expert_b[1],
             e_gate, e_up, e_down, 1, final_norm, TILE, True)
    return h.reshape(bsz, seq, d)
```

```python
import functools
import math

import jax
import jax.numpy as jnp
from jax import lax
from jax.experimental import pallas as pl
from jax.experimental.pallas import tpu as pltpu

D_MODEL = 1024
CHUNK = 64
N_META = 16
HEAD_DIM = 64
N_HEADS = 8
ROT_HALF = 8
ROPE_THETA = 500000.0
N_GROUPS = 4
EXPERTS_PER_GROUP = 8
N_EXPERTS = 32
D_EXPERT = 512
EPS = 1e-6

LANES = 128
SUBLANES = 8
TILE = 512
FRONT = TILE
PAD_ROWS = FRONT - N_META
META_PAD = LANES - N_META
EXPERT_ROWS = 512
ATT_TQ = 512
ATT_TK = 1024
ATT_SUB = 512
SEG_ALIGN = 16
STAGE_SLOTS = 2
LOCAL_ROWS = 1536
VMEM_LIMIT = 56 * 1024 * 1024
NEG = float(jnp.finfo(jnp.float32).min)

BF16 = jnp.bfloat16
F32 = jnp.float32


def _rms(x, g):
    return x * lax.rsqrt(jnp.mean(x * x, axis=-1, keepdims=True) + EPS) * g


def _cparams(sem):
    return pltpu.CompilerParams(dimension_semantics=sem, vmem_limit_bytes=VMEM_LIMIT)


def _mixer_kernel(x_ref, front_ref, g_ref, win_ref, cw_ref, wout_ref, o_ref, cu_ref, *, tiles_per_batch):
    i = pl.program_id(0)
    tm = x_ref.shape[0]
    x = jnp.where(i % tiles_per_batch == 0, front_ref[...], x_ref[...])
    hn = _rms(x, g_ref[...]).astype(BF16)
    proj = jnp.dot(hn, win_ref[...], preferred_element_type=F32)
    b_gate = proj[:, :D_MODEL]
    cu = proj[:, D_MODEL:2 * D_MODEL] * proj[:, 2 * D_MODEL:]

    @pl.when(i % tiles_per_batch == 0)
    def _():
        cu_ref[0:SUBLANES, :] = jnp.zeros((SUBLANES, D_MODEL), F32)

    cu_ref[SUBLANES:SUBLANES + tm, :] = cu
    cw = cw_ref[...]
    z = (cw[2:3] * cu + cw[1:2] * cu_ref[SUBLANES - 1:SUBLANES - 1 + tm, :]
         + cw[0:1] * cu_ref[SUBLANES - 2:SUBLANES - 2 + tm, :])
    cu_ref[0:SUBLANES, :] = cu_ref[tm:tm + SUBLANES, :]
    y = jnp.dot((b_gate * z).astype(BF16), wout_ref[...], preferred_element_type=F32)
    o_ref[...] = x + y


def _mixer(x, front, g, w_in, conv_w, w_out, bsz, tiles_per_batch):
    t = bsz * tiles_per_batch * TILE
    tiles_real = tiles_per_batch - FRONT // TILE

    def x_map(i):
        return ((i // tiles_per_batch) * tiles_real + jnp.maximum(i % tiles_per_batch - FRONT // TILE, 0), 0)

    return pl.pallas_call(
        functools.partial(_mixer_kernel, tiles_per_batch=tiles_per_batch),
        grid=(t // TILE,),
        in_specs=[
            pl.BlockSpec((TILE, D_MODEL), x_map),
            pl.BlockSpec((TILE, D_MODEL), lambda i: (0, 0)),
            pl.BlockSpec((1, D_MODEL), lambda i: (0, 0)),
            pl.BlockSpec((D_MODEL, 3 * D_MODEL), lambda i: (0, 0)),
            pl.BlockSpec((3, D_MODEL), lambda i: (0, 0)),
            pl.BlockSpec((D_MODEL, D_MODEL), lambda i: (0, 0)),
        ],
        out_specs=pl.BlockSpec((TILE, D_MODEL), lambda i: (i, 0)),
        out_shape=jax.ShapeDtypeStruct((t, D_MODEL), F32),
        scratch_shapes=[pltpu.VMEM((TILE + 2 * SUBLANES, D_MODEL), F32)],
        compiler_params=_cparams(("arbitrary",)),
        name="mixer_a",
    )(x, front, g.reshape(1, D_MODEL), w_in.astype(BF16), conv_w, w_out.astype(BF16))


ROUTER_ROWS = SUBLANES + N_EXPERTS


def _round_up(v, m):
    return jnp.floor((v + (m - 1)) * (1.0 / m)) * m


def _router_kernel(h_ref, g_ref, wrt_ref, br_ref, tri_ref, lpos_ref, gate_ref, meta_ref, tot_ref, seg_sc):
    i = pl.program_id(0)
    tm = h_ref.shape[0]

    @pl.when(i == 0)
    def _():
        seg_sc[...] = jnp.zeros_like(seg_sc)

    xn = _rms(h_ref[...], g_ref[...])
    lt = lax.dot_general(wrt_ref[...], xn, (((1,), (1,)), ((), ())),
                         precision=lax.Precision.HIGHEST, preferred_element_type=F32) + br_ref[...]
    lg = lt[0:N_GROUPS]
    gmax = jnp.max(lg, axis=0, keepdims=True)
    iota_g = lax.broadcasted_iota(jnp.int32, lg.shape, 0)
    grp = jnp.min(jnp.where(lg == gmax, iota_g, N_GROUPS), axis=0, keepdims=True)
    p_grp = 1.0 / jnp.sum(jnp.exp(lg - gmax), axis=0, keepdims=True)
    le = lt[SUBLANES:SUBLANES + EXPERTS_PER_GROUP]
    for g in range(1, N_GROUPS):
        lo = SUBLANES + g * EXPERTS_PER_GROUP
        le = jnp.where(grp == g, lt[lo:lo + EXPERTS_PER_GROUP], le)
    iota_e = lax.broadcasted_iota(jnp.int32, le.shape, 0)
    v1 = jnp.max(le, axis=0, keepdims=True)
    i1 = jnp.min(jnp.where(le == v1, iota_e, EXPERTS_PER_GROUP), axis=0, keepdims=True)
    le2 = jnp.where(iota_e == i1, -jnp.inf, le)
    v2 = jnp.max(le2, axis=0, keepdims=True)
    i2 = jnp.min(jnp.where(le2 == v2, iota_e, EXPERTS_PER_GROUP), axis=0, keepdims=True)
    e2 = jnp.exp(v2 - v1)
    den = 1.0 + e2
    gate_ref[...] = jnp.concatenate([p_grp * (1.0 / den), p_grp * (e2 / den)], axis=0)
    experts = (grp * EXPERTS_PER_GROUP + i1, grp * EXPERTS_PER_GROUP + i2)

    iota_x = lax.broadcasted_iota(jnp.int32, (N_EXPERTS, tm), 0)
    hits = [iota_x == e_k for e_k in experts]
    hits_f = [jnp.where(hit, 1.0, 0.0) for hit in hits]
    hits_b = [hf.astype(BF16) for hf in hits_f]
    excl = [jnp.dot(hb, tri_ref[...], preferred_element_type=F32) for hb in hits_b]
    tot_col = [jnp.sum(hf, axis=1, keepdims=True) for hf in hits_f]
    ones = jnp.ones((SUBLANES, tm), BF16)
    n_lane = sum(lax.dot_general(ones, hb, (((1,), (1,)), ((), ())), preferred_element_type=F32) for hb in hits_b)
    seg_len_lane = _round_up(n_lane[0:1], SEG_ALIGN)
    seg_len = _round_up(tot_col[0] + tot_col[1], SEG_ALIGN)
    before = (lax.broadcasted_iota(jnp.int32, (N_EXPERTS, N_EXPERTS), 1)
              < lax.broadcasted_iota(jnp.int32, (N_EXPERTS, N_EXPERTS), 0))
    local_off = jnp.sum(jnp.where(before, seg_len_lane, 0.0), axis=1, keepdims=True)

    lpos_a = jnp.sum(jnp.where(hits[0], excl[0] + local_off, 0.0), axis=0, keepdims=True).astype(jnp.int32)
    lpos_b = jnp.sum(jnp.where(hits[1], excl[1] + tot_col[0] + local_off, 0.0), axis=0, keepdims=True).astype(jnp.int32)
    lpos_ref[...] = jnp.concatenate([lpos_a, lpos_b], axis=0)

    seg_before = seg_sc[...]
    seg_sc[...] = seg_before + seg_len
    lane = lax.broadcasted_iota(jnp.int32, (N_EXPERTS, LANES), 1)
    meta = jnp.where(lane == 0, seg_len, jnp.where(lane == 1, local_off, jnp.where(lane == 2, seg_before, 0.0)))
    meta_ref[0] = meta.astype(jnp.int32)
    tot_ref[...] = jnp.broadcast_to(seg_before + seg_len, tot_ref.shape).astype(jnp.int32)


def _router(h, g, w_rg, b_rg, w_re, b_re, tm):
    t = h.shape[0]
    nt = t // tm
    wrt = jnp.zeros((ROUTER_ROWS, D_MODEL), F32)
    wrt = wrt.at[0:N_GROUPS].set(w_rg.T).at[SUBLANES:].set(w_re.T)
    br = jnp.zeros((ROUTER_ROWS, 1), F32)
    br = br.at[0:N_GROUPS, 0].set(b_rg).at[SUBLANES:, 0].set(b_re)
    tri = (lax.broadcasted_iota(jnp.int32, (tm, tm), 0) < lax.broadcasted_iota(jnp.int32, (tm, tm), 1)).astype(BF16)
    return pl.pallas_call(
        _router_kernel,
        grid=(nt,),
        in_specs=[
            pl.BlockSpec((tm, D_MODEL), lambda i: (i, 0)),
            pl.BlockSpec((1, D_MODEL), lambda i: (0, 0)),
            pl.BlockSpec((ROUTER_ROWS, D_MODEL), lambda i: (0, 0)),
            pl.BlockSpec((ROUTER_ROWS, 1), lambda i: (0, 0)),
            pl.BlockSpec((tm, tm), lambda i: (0, 0)),
        ],
        out_specs=[
            pl.BlockSpec((2, tm), lambda i: (0, i)),
            pl.BlockSpec((2, tm), lambda i: (0, i)),
            pl.BlockSpec((1, N_EXPERTS, LANES), lambda i: (i, 0, 0)),
            pl.BlockSpec((N_EXPERTS, LANES), lambda i: (0, 0)),
        ],
        out_shape=[
            jax.ShapeDtypeStruct((2, t), jnp.int32),
            jax.ShapeDtypeStruct((2, t), F32),
            jax.ShapeDtypeStruct((nt, N_EXPERTS, LANES), jnp.int32),
            jax.ShapeDtypeStruct((N_EXPERTS, LANES), jnp.int32),
        ],
        scratch_shapes=[pltpu.VMEM((N_EXPERTS, 1), F32)],
        compiler_params=_cparams(("arbitrary",)),
        name="router",
    )(h, g.reshape(1, D_MODEL), wrt, br, tri)


def _if_rows(n, fn):
    pl.when(n > 0)(lambda: fn(pl.multiple_of(n, SEG_ALIGN)))


def _rows(ref, start, size):
    return ref.at[pl.ds(pl.multiple_of(start, SEG_ALIGN), size), :]


def _dispatch_kernel(len_ref, off_ref, dst_ref, zs_ref, zn_ref, nu_ref, h_ref, g_ref, lpos_ref, buf_ref, stage, zero_sc,
                     sem_out, sem_fill, *, n_blk):
    t = pl.program_id(0)
    nt = pl.num_programs(0)
    tm = h_ref.shape[0]
    slot = t % STAGE_SLOTS

    def tile_out(t, s, op):
        def body(e, c):
            idx = t * N_EXPERTS + e
            lo, dst = off_ref[idx], dst_ref[idx]
            _if_rows(len_ref[idx], lambda n: op(pltpu.make_async_copy(
                _rows(stage.at[s], lo, n), _rows(buf_ref, dst, n), sem_out.at[s])))
            return c

        lax.fori_loop(0, N_EXPERTS, body, 0)

    def fill_copies(op):
        def tails(e, c):
            _if_rows(zn_ref[e], lambda n: op(pltpu.make_async_copy(
                zero_sc.at[pl.ds(0, n), :], _rows(buf_ref, zs_ref[e], n), sem_fill)))
            return c

        lax.fori_loop(0, N_EXPERTS, tails, 0)

        def blocks(b, c):
            op(pltpu.make_async_copy(zero_sc, _rows(buf_ref, b * EXPERT_ROWS, EXPERT_ROWS), sem_fill))
            return c

        lax.fori_loop(nu_ref[0], n_blk, blocks, 0)

    start = lambda cp: cp.start()
    wait = lambda cp: cp.wait()

    @pl.when(t == 0)
    def _():
        zero_sc[...] = jnp.zeros_like(zero_sc)
        fill_copies(start)

    @pl.when(t >= STAGE_SLOTS)
    def _():
        tile_out(t - STAGE_SLOTS, slot, wait)

    xn = _rms(h_ref[...], g_ref[...]).astype(BF16)
    lpos = lpos_ref[...]
    row = lax.broadcasted_iota(jnp.int32, (LOCAL_ROWS, tm), 0)
    perm = jnp.where(row == lpos[0:1], 1.0, jnp.where(row == lpos[1:2], 1.0, 0.0)).astype(BF16)
    stage[slot] = jnp.dot(perm, xn, preferred_element_type=F32).astype(BF16)
    tile_out(t, slot, start)

    @pl.when(t == nt - 1)
    def _():
        for back in range(STAGE_SLOTS - 1, -1, -1):
            @pl.when(t >= back)
            def _(back=back):
                tile_out(t - back, (t - back) % STAGE_SLOTS, wait)

        fill_copies(wait)


def _dispatch(h, g, lpos, seg_len, seg_off, seg_dst, tail_start, tail_len, n_used, n_blk, tm):
    t = h.shape[0]
    grid_spec = pltpu.PrefetchScalarGridSpec(
        num_scalar_prefetch=6,
        grid=(t // tm,),
        in_specs=[
            pl.BlockSpec((tm, D_MODEL), lambda i, *_: (i, 0)),
            pl.BlockSpec((1, D_MODEL), lambda i, *_: (0, 0)),
            pl.BlockSpec((2, tm), lambda i, *_: (0, i)),
        ],
        out_specs=pl.BlockSpec(memory_space=pl.ANY),
        scratch_shapes=[pltpu.VMEM((STAGE_SLOTS, LOCAL_ROWS, D_MODEL), BF16), pltpu.VMEM((EXPERT_ROWS, D_MODEL), BF16),
                        pltpu.SemaphoreType.DMA((STAGE_SLOTS,)), pltpu.SemaphoreType.DMA],
    )
    return pl.pallas_call(
        functools.partial(_dispatch_kernel, n_blk=n_blk),
        grid_spec=grid_spec,
        out_shape=jax.ShapeDtypeStruct((n_blk * EXPERT_ROWS, D_MODEL), BF16),
        compiler_params=_cparams(("arbitrary",)),
        name="dispatch",
    )(seg_len, seg_off, seg_dst, tail_start, tail_len, n_used, h, g.reshape(1, D_MODEL), lpos)


def _expert_kernel(be_ref, nu_ref, x_ref, wg_ref, wu_ref, wd_ref, y_ref, wgu_sc, wd_sc):
    i = pl.program_id(0)
    e = be_ref[i]
    prev = be_ref[jnp.maximum(i - 1, 0)]

    @pl.when(jnp.logical_or(i == 0, e != prev))
    def _():
        wgu_sc[:, :D_EXPERT] = wg_ref[0, 0].astype(BF16)
        wgu_sc[:, D_EXPERT:] = wu_ref[0, 0].astype(BF16)
        wd_sc[...] = wd_ref[0, 0].astype(BF16)

    @pl.when(i < nu_ref[0])
    def _():
        gu = jnp.dot(x_ref[...], wgu_sc[...], preferred_element_type=F32)
        a = jax.nn.silu(gu[:, :D_EXPERT]) * gu[:, D_EXPERT:]
        y_ref[...] = jnp.dot(a.astype(BF16), wd_sc[...], preferred_element_type=F32).astype(BF16)

    @pl.when(i >= nu_ref[0])
    def _():
        y_ref[...] = jnp.zeros_like(y_ref)


def _experts(buf, blk_expert, n_used, w_gate, w_up, w_down, layer):
    n_blk = blk_expert.shape[0]

    def xmap(i, be, nu):
        return (jnp.maximum(jnp.minimum(i, nu[0] - 1), 0), 0)

    grid_spec = pltpu.PrefetchScalarGridSpec(
        num_scalar_prefetch=2,
        grid=(n_blk,),
        in_specs=[
            pl.BlockSpec((EXPERT_ROWS, D_MODEL), xmap),
            pl.BlockSpec((1, 1, D_MODEL, D_EXPERT), lambda i, be, nu: (layer, be[i], 0, 0)),
            pl.BlockSpec((1, 1, D_MODEL, D_EXPERT), lambda i, be, nu: (layer, be[i], 0, 0)),
            pl.BlockSpec((1, 1, D_EXPERT, D_MODEL), lambda i, be, nu: (layer, be[i], 0, 0)),
        ],
        out_specs=pl.BlockSpec((EXPERT_ROWS, D_MODEL), lambda i, be, nu: (i, 0)),
        scratch_shapes=[pltpu.VMEM((D_MODEL, 2 * D_EXPERT), BF16), pltpu.VMEM((D_EXPERT, D_MODEL), BF16)],
    )
    return pl.pallas_call(
        _expert_kernel,
        grid_spec=grid_spec,
        out_shape=jax.ShapeDtypeStruct(buf.shape, BF16),
        compiler_params=_cparams(("arbitrary",)),
        name="experts",
    )(blk_expert, n_used, buf, w_gate, w_up, w_down)


def _combine_kernel(len_ref, off_ref, dst_ref, h_ref, gate_ref, lpos_ref, y_ref, fn_ref, o_ref, yl, sem, *, final_norm):
    i = pl.program_id(0)
    nt = pl.num_programs(0)
    tm = h_ref.shape[0]
    slot = i % 2

    def tile_copies(t, s, op):
        def body(e, c):
            idx = t * N_EXPERTS + e
            lo, src = off_ref[idx], dst_ref[idx]
            _if_rows(len_ref[idx], lambda n: op(pltpu.make_async_copy(
                _rows(y_ref, src, n), _rows(yl.at[s], lo, n), sem.at[s])))
            return c

        lax.fori_loop(0, N_EXPERTS, body, 0)

    @pl.when(i == 0)
    def _():
        yl[...] = jnp.zeros_like(yl)
        tile_copies(0, 0, lambda cp: cp.start())

    @pl.when(i + 1 < nt)
    def _():
        tile_copies(i + 1, 1 - slot, lambda cp: cp.start())

    tile_copies(i, slot, lambda cp: cp.wait())

    rows = yl[slot]
    lpos = lpos_ref[...]
    col = lax.broadcasted_iota(jnp.int32, (tm, LOCAL_ROWS), 1)
    g = gate_ref[...]
    weights = jnp.where(col == lpos[:, 0:1], g[:, 0:1], jnp.where(col == lpos[:, 1:2], g[:, 1:2], 0.0)).astype(BF16)
    out = h_ref[...] + jnp.dot(weights, rows, preferred_element_type=F32)
    if final_norm:
        out = _rms(out, fn_ref[...])
    o_ref[...] = out


def _combine(h, y, seg_len, seg_off, seg_dst, lpos, gate, fn, tm, final_norm):
    t = h.shape[0]
    grid_spec = pltpu.PrefetchScalarGridSpec(
        num_scalar_prefetch=3,
        grid=(t // tm,),
        in_specs=[
            pl.BlockSpec((tm, D_MODEL), lambda i, *_: (i, 0)),
            pl.BlockSpec((tm, 2), lambda i, *_: (i, 0)),
            pl.BlockSpec((tm, 2), lambda i, *_: (i, 0)),
            pl.BlockSpec(memory_space=pl.ANY),
            pl.BlockSpec((1, D_MODEL), lambda i, *_: (0, 0)),
        ],
        out_specs=pl.BlockSpec((tm, D_MODEL), lambda i, *_: (i, 0)),
        scratch_shapes=[pltpu.VMEM((2, LOCAL_ROWS, D_MODEL), BF16), pltpu.SemaphoreType.DMA((2,))],
    )
    return pl.pallas_call(
        functools.partial(_combine_kernel, final_norm=final_norm),
        grid_spec=grid_spec,
        out_shape=jax.ShapeDtypeStruct((t, D_MODEL), F32),
        compiler_params=_cparams(("arbitrary",)),
        name="combine",
    )(seg_len, seg_off, seg_dst, h, gate.T, lpos.T, y, fn.reshape(1, D_MODEL))


def _moe(h, g, w_rg, b_rg, w_re, b_re, w_gate, w_up, w_down, layer, fn, tm, final_norm):
    t = h.shape[0]
    nt = t // tm
    lpos, gate, meta, tot = _router(h, g, w_rg, b_rg, w_re, b_re, tm)
    n_blk = (2 * t + nt * N_EXPERTS * (SEG_ALIGN - 1)) // EXPERT_ROWS + N_EXPERTS
    total = tot[:, 0]
    region = (total + EXPERT_ROWS - 1) // EXPERT_ROWS * EXPERT_ROWS
    ends = jnp.cumsum(region)
    starts = ends - region
    seg_len = meta[:, :, 0].reshape(-1)
    seg_off = meta[:, :, 1].reshape(-1)
    seg_dst = (meta[:, :, 2] + starts[None, :]).reshape(-1)
    blk_start = jnp.arange(n_blk, dtype=jnp.int32) * EXPERT_ROWS
    blk_expert = jnp.minimum(jnp.sum(blk_start[:, None] >= ends[None, :], axis=1), N_EXPERTS - 1).astype(jnp.int32)
    n_used = (ends[-1:] // EXPERT_ROWS).astype(jnp.int32)
    buf = _dispatch(h, g, lpos, seg_len, seg_off, seg_dst, starts + total, region - total, n_used, n_blk, tm)
    y = _experts(buf, blk_expert, n_used, w_gate, w_up, w_down, layer)
    return _combine(h, y, seg_len, seg_off, seg_dst, lpos, gate, fn, tm, final_norm)


def _kv_kernel(h_ref, g_ref, wk_ref, wvt_ref, cos_ref, sa_ref, sb_ref, k_ref, vt_ref):
    hn = _rms(h_ref[...], g_ref[...]).astype(BF16)
    k = jnp.dot(hn, wk_ref[...], preferred_element_type=F32)
    cos, sa, sb = cos_ref[...], sa_ref[...], sb_ref[...]
    for c in range(D_MODEL // LANES):
        kc = k[:, c * LANES:(c + 1) * LANES]
        rot = kc * cos + pltpu.roll(kc, LANES - ROT_HALF, 1) * sa + pltpu.roll(kc, ROT_HALF, 1) * sb
        k_ref[0, :, c * LANES:(c + 1) * LANES] = rot.astype(BF16)
    vt = lax.dot_general(wvt_ref[...], hn, (((1,), (1,)), ((), ())), preferred_element_type=F32)
    for c in range(vt_ref.shape[1]):
        vt_ref[0, c] = vt[:, c * LANES:(c + 1) * LANES].astype(BF16)


def _kv(h, g, w_kv, bsz, lp):
    tpb = lp // TILE
    kblk = TILE // LANES
    pos = jnp.maximum(jnp.arange(lp) - PAD_ROWS, 0).astype(F32)
    inv_freq = ROPE_THETA ** (-jnp.arange(ROT_HALF, dtype=F32) * 2.0 / (2 * ROT_HALF))
    ang = pos[:, None] * inv_freq[None, :]
    r = jnp.arange(LANES) % HEAD_DIM
    cos_t = jnp.where(r < 2 * ROT_HALF, jnp.cos(ang)[:, r % ROT_HALF], 1.0)
    sin_t = jnp.sin(ang)[:, r % ROT_HALF]
    sa = jnp.where(r < ROT_HALF, -sin_t, 0.0)
    sb = jnp.where((r >= ROT_HALF) & (r < 2 * ROT_HALF), sin_t, 0.0)
    tab = pl.BlockSpec((TILE, LANES), lambda i: (i % tpb, 0))
    return pl.pallas_call(
        _kv_kernel,
        grid=(bsz * tpb,),
        in_specs=[
            pl.BlockSpec((TILE, D_MODEL), lambda i: (i, 0)),
            pl.BlockSpec((1, D_MODEL), lambda i: (0, 0)),
            pl.BlockSpec((D_MODEL, D_MODEL), lambda i: (0, 0)),
            pl.BlockSpec((D_MODEL, D_MODEL), lambda i: (0, 0)),
            tab, tab, tab,
        ],
        out_specs=[
            pl.BlockSpec((1, TILE, D_MODEL), lambda i: (i // tpb, i % tpb, 0)),
            pl.BlockSpec((1, kblk, D_MODEL, LANES), lambda i: (i // tpb, i % tpb, 0, 0)),
        ],
        out_shape=[
            jax.ShapeDtypeStruct((bsz, lp, D_MODEL), BF16),
            jax.ShapeDtypeStruct((bsz, lp // LANES, D_MODEL, LANES), BF16),
        ],
        compiler_params=_cparams(("arbitrary",)),
        name="kv_proj",
    )(h, g.reshape(1, D_MODEL), w_kv[:, :D_MODEL].astype(BF16), w_kv[:, D_MODEL:].T.astype(BF16), cos_t, sa, sb)


def _q_kernel(h_ref, g_ref, wqt_ref, cos_ref, sin_ref, qt_ref):
    hn = _rms(h_ref[...], g_ref[...]).astype(BF16)
    qt = lax.dot_general(wqt_ref[...], hn, (((1,), (1,)), ((), ())), preferred_element_type=F32)
    cos, sin = cos_ref[...], sin_ref[...]
    scale = HEAD_DIM ** -0.5 * math.log2(math.e)
    for c in range(D_MODEL // HEAD_DIM):
        lo = c * HEAD_DIM
        x1 = qt[lo:lo + ROT_HALF]
        x2 = qt[lo + ROT_HALF:lo + 2 * ROT_HALF]
        blk = jnp.concatenate([x1 * cos - x2 * sin, x2 * cos + x1 * sin, qt[lo + 2 * ROT_HALF:lo + HEAD_DIM]], axis=0)
        qt_ref[0, lo:lo + HEAD_DIM, :] = (blk * scale).astype(BF16)


def _real_row_map(tiles_real, tiles_padded):
    front = FRONT // TILE
    return lambda i: ((i // tiles_real) * tiles_padded + front + i % tiles_real, 0)


def _q_proj(h_pad, g, w_q, bsz, seq, lp):
    tiles_real = seq // TILE
    pos = (N_META + jnp.arange(seq)).astype(F32)
    inv_freq = ROPE_THETA ** (-jnp.arange(ROT_HALF, dtype=F32) * 2.0 / (2 * ROT_HALF))
    ang = inv_freq[:, None] * pos[None, :]
    tab = pl.BlockSpec((ROT_HALF, TILE), lambda i: (0, i % tiles_real))
    return pl.pallas_call(
        _q_kernel,
        grid=(bsz * tiles_real,),
        in_specs=[
            pl.BlockSpec((TILE, D_MODEL), _real_row_map(tiles_real, lp // TILE)),
            pl.BlockSpec((1, D_MODEL), lambda i: (0, 0)),
            pl.BlockSpec((D_MODEL, D_MODEL), lambda i: (0, 0)),
            tab, tab,
        ],
        out_specs=pl.BlockSpec((1, D_MODEL, TILE), lambda i: (i, 0, 0)),
        out_shape=jax.ShapeDtypeStruct((bsz * tiles_real, D_MODEL, TILE), BF16),
        compiler_params=_cparams(("arbitrary",)),
        name="q_proj",
    )(h_pad, g.reshape(1, D_MODEL), w_q.T.astype(BF16), jnp.cos(ang), jnp.sin(ang))


def _attn_schedule(nq):
    steps = [(i, t, int(t == i // 2)) for i in range(nq) for t in range(i // 2 + 1)]
    return [jnp.asarray(col, jnp.int32) for col in zip(*steps)]


def _attn_kernel(qi_ref, kt_ref, last_ref, qt_ref, k_ref, vt_ref, lam_ref, g_ref, o_ref, m_sc, l_sc, acc_sc, sa_sc, sb_sc,
                 *, lam_init, n_steps):
    tq = ATT_TQ
    zero = jnp.zeros((HEAD_DIM, tq), BF16)
    n_sub = ATT_TK // ATT_SUB

    def q_maps(s):
        qt = qt_ref[0, qi_ref[s]]
        return (jnp.concatenate([qt[0:HEAD_DIM], zero], axis=0), jnp.concatenate([zero, qt[HEAD_DIM:]], axis=0))

    def update(n, sc, vtt, fresh=None):
        m_old, l_old, acc_old = m_sc[n], l_sc[n], acc_sc[n]
        if fresh is not None:
            m_old = jnp.where(fresh, NEG, m_old)
            l_old = jnp.where(fresh, 0.0, l_old)
            acc_old = jnp.where(fresh, 0.0, acc_old)
        m_new = jnp.maximum(m_old, jnp.max(sc, axis=0, keepdims=True))
        alpha = jnp.exp2(m_old - m_new)
        p = jnp.exp2(sc - m_new)
        l_sc[n] = alpha * l_old + jnp.sum(p, axis=0, keepdims=True)
        acc_sc[n] = alpha * acc_old + jnp.dot(vtt, p.astype(BF16), preferred_element_type=F32)
        m_sc[n] = m_new

    def values(row0, nkeys):
        blk0 = row0 // LANES
        return jnp.concatenate([vt_ref[0, blk0 + c] for c in range(nkeys // LANES)], axis=1)

    def key_row(s, u):
        return pl.multiple_of(FRONT + kt_ref[s] * ATT_TK + u * ATT_SUB, LANES)

    def scores_into(s, buf):
        qs = q_maps(s)
        for u in range(n_sub):
            kt = k_ref[0, pl.ds(key_row(s, u), ATT_SUB), :]
            for n in range(2):
                buf[u * 2 + n] = jnp.dot(kt, qs[n], preferred_element_type=F32)

    def consume(s, buf, boundary):
        fresh = kt_ref[s] == 0
        if boundary:
            r = lax.broadcasted_iota(jnp.int32, (ATT_SUB, tq), 0) // CHUNK
            c = lax.broadcasted_iota(jnp.int32, (ATT_SUB, tq), 1) // CHUNK
            shift = (qi_ref[s] % (ATT_TK // tq)) * (tq // CHUNK)
        for u in range(n_sub):
            vtt = values(key_row(s, u), ATT_SUB)
            for n in range(2):
                sc = buf[u * 2 + n]
                if boundary:
                    sc = jnp.where(r <= c + (shift - u * (ATT_SUB // CHUNK)), sc, NEG)
                update(n, sc, vtt, fresh if u == 0 else None)

    def finish(s):
        qs = q_maps(s)
        meta_vis = lax.broadcasted_iota(jnp.int32, (LANES, tq), 0) >= META_PAD
        kt = k_ref[0, FRONT - LANES:FRONT, :]
        for n in range(2):
            update(n, jnp.where(meta_vis, jnp.dot(kt, qs[n], preferred_element_type=F32), NEG), values(FRONT - LANES, LANES))
        lp = lam_ref[...]
        lam = (jnp.exp(jnp.sum(lp[0:1] * lp[1:2], axis=1, keepdims=True))
               - jnp.exp(jnp.sum(lp[2:3] * lp[3:4], axis=1, keepdims=True)) + lam_init)
        o = acc_sc[0] / l_sc[0] - lam * (acc_sc[1] / l_sc[1])
        o = o * lax.rsqrt(jnp.mean(o * o, axis=0, keepdims=True) + EPS) * g_ref[...] * (1.0 - lam_init)
        o_ref[pl.ds(pl.multiple_of(qi_ref[s] * tq, tq), tq), :] = o.T.astype(BF16)

    def block(p, boundary_a, boundary_b):
        s_a, s_b = 2 * p, 2 * p + 1
        scores_into(s_b, sb_sc)
        consume(s_a, sa_sc, boundary_a)
        if boundary_a:
            finish(s_a)
        scores_into(jnp.minimum(s_b + 1, n_steps - 1), sa_sc)
        consume(s_b, sb_sc, boundary_b)
        if boundary_b:
            finish(s_b)

    scores_into(0, sa_sc)

    def body(p, c):
        last_a, last_b = last_ref[2 * p], last_ref[2 * p + 1]
        for va in (0, 1):
            for vb in (0, 1):
                pl.when(jnp.logical_and(last_a == va, last_b == vb))(functools.partial(block, p, va == 1, vb == 1))
        return c

    lax.fori_loop(0, n_steps // 2, body, 0)


def _attention(qt, k, vt4, lam_p, subln_g, bsz, seq, lam_init):
    nq = seq // ATT_TQ
    lp = k.shape[1]
    q_idx, k_idx, last = _attn_schedule(nq)
    n_steps = int(q_idx.shape[0])
    assert n_steps % 2 == 0
    grid_spec = pltpu.PrefetchScalarGridSpec(
        num_scalar_prefetch=3,
        grid=(bsz, N_HEADS),
        in_specs=[
            pl.BlockSpec((1, nq, 2 * HEAD_DIM, ATT_TQ), lambda b, h, *_: (b, 0, h, 0)),
            pl.BlockSpec((1, lp, 2 * HEAD_DIM), lambda b, h, *_: (b, 0, h)),
            pl.BlockSpec((1, lp // LANES, 2 * HEAD_DIM, LANES), lambda b, h, *_: (b, 0, h, 0)),
            pl.BlockSpec((4, HEAD_DIM), lambda b, h, *_: (0, 0)),
            pl.BlockSpec((2 * HEAD_DIM, 1), lambda b, h, *_: (0, 0)),
        ],
        out_specs=pl.BlockSpec((seq, 2 * HEAD_DIM), lambda b, h, *_: (b, h)),
        scratch_shapes=[pltpu.VMEM((2, 1, ATT_TQ), F32), pltpu.VMEM((2, 1, ATT_TQ), F32),
                        pltpu.VMEM((2, 2 * HEAD_DIM, ATT_TQ), F32),
                        pltpu.VMEM((2 * ATT_TK // ATT_SUB, ATT_SUB, ATT_TQ), F32),
                        pltpu.VMEM((2 * ATT_TK // ATT_SUB, ATT_SUB, ATT_TQ), F32)],
    )
    return pl.pallas_call(
        functools.partial(_attn_kernel, lam_init=lam_init, n_steps=n_steps),
        grid_spec=grid_spec,
        out_shape=jax.ShapeDtypeStruct((bsz * seq, D_MODEL), BF16),
        compiler_params=_cparams(("arbitrary", "arbitrary")),
        name="diff_attention",
    )(q_idx, k_idx, last, qt.reshape(bsz, nq, D_MODEL, ATT_TQ), k, vt4, lam_p, subln_g.reshape(2 * HEAD_DIM, 1))


def _oproj_kernel(h_ref, o_ref, w_ref, out_ref):
    out_ref[...] = h_ref[...] + jnp.dot(o_ref[...], w_ref[...], preferred_element_type=F32)


def _o_proj(h_pad, o, w_out, bsz, seq, lp):
    tiles_real = seq // TILE
    return pl.pallas_call(
        _oproj_kernel,
        grid=(bsz * tiles_real,),
        in_specs=[
            pl.BlockSpec((TILE, D_MODEL), _real_row_map(tiles_real, lp // TILE)),
            pl.BlockSpec((TILE, D_MODEL), lambda i: (i, 0)),
            pl.BlockSpec((D_MODEL, D_MODEL), lambda i: (0, 0)),
        ],
        out_specs=pl.BlockSpec((TILE, D_MODEL), lambda i: (i, 0)),
        out_shape=jax.ShapeDtypeStruct((bsz * seq, D_MODEL), F32),
        compiler_params=_cparams(("arbitrary",)),
        name="o_proj",
    )(h_pad, o, w_out.astype(BF16))


def kernel(x, meta_tokens, a_norm, a_w_in, a_conv, a_w_out, kv_norm, w_kv, b_norm, b_w_q, b_lambda, b_subln, b_w_out, ffn_norm, r_group, r_group_b, r_expert, r_expert_b, e_gate, e_up, e_down, final_norm):
    bsz, seq, d = x.shape
    assert d == D_MODEL and a_norm.shape[0] == 1 and b_norm.shape[0] == 1
    lp = FRONT + seq
    assert seq % ATT_TK == 0 and ATT_TK == 2 * ATT_TQ and ATT_TQ == TILE and ATT_TK % ATT_SUB == 0
    front = jnp.concatenate([jnp.zeros((PAD_ROWS, d), x.dtype), meta_tokens.astype(x.dtype)], axis=0)
    h = _mixer(x.reshape(bsz * seq, d), front, a_norm[0], a_w_in[0], a_conv[0], a_w_out[0], bsz, lp // TILE)
    h = _moe(h, ffn_norm[0], r_group[0], r_group_b[0], r_expert[0], r_expert_b[0],
             e_gate, e_up, e_down, 0, final_norm, TILE, False)

    k, vt4 = _kv(h, kv_norm, w_kv, bsz, lp)
    qt = _q_proj(h, b_norm[0], b_w_q[0], bsz, seq, lp)
    lam_init = 0.8 - 0.6 * math.exp(-0.3 * a_norm.shape[0])
    o = _attention(qt, k, vt4, b_lambda[0], b_subln[0], bsz, seq, lam_init)
    h = _o_proj(h, o, b_w_out[0], bsz, seq, lp)
    h = _moe(h, ffn_norm[1], r_group[1], r_group_b[1], r_expert[1], r_expert_b[1],
             e_gate, e_up, e_down, 1, final_norm, TILE, True)
    return h.reshape(bsz, seq, d)
```

```python
import functools
import math

import jax
import jax.numpy as jnp
from jax import lax
from jax.experimental import pallas as pl
from jax.experimental.pallas import tpu as pltpu

D_MODEL = 1024
CHUNK = 64
N_META = 16
HEAD_DIM = 64
N_HEADS = 8
ROT_HALF = 8
ROPE_THETA = 500000.0
N_GROUPS = 4
EXPERTS_PER_GROUP = 8
N_EXPERTS = 32
D_EXPERT = 512
EPS = 1e-6

LANES = 128
SUBLANES = 8
TILE = 512
FRONT = TILE
PAD_ROWS = FRONT - N_META
EXPERT_ROWS = 512
ATT_TQ = 512
ATT_TK = 1024
ATT_SUB = 512
SEG_ALIGN = 16
STAGE_SLOTS = 2
LOCAL_ROWS = 1536
VMEM_LIMIT = 56 * 1024 * 1024
NEG = float(jnp.finfo(jnp.float32).min)

BF16 = jnp.bfloat16
F32 = jnp.float32


def _rms(x, g):
    return x * lax.rsqrt(jnp.mean(x * x, axis=-1, keepdims=True) + EPS) * g


def _cparams(sem):
    return pltpu.CompilerParams(dimension_semantics=sem, vmem_limit_bytes=VMEM_LIMIT)


def _mixer_kernel(x_ref, front_ref, g_ref, win_ref, cw_ref, wout_ref, o_ref, cu_ref, *, tiles_per_batch):
    i = pl.program_id(0)
    tm = x_ref.shape[0]
    x = jnp.where(i % tiles_per_batch == 0, front_ref[...], x_ref[...])
    hn = _rms(x, g_ref[...]).astype(BF16)
    proj = jnp.dot(hn, win_ref[...], preferred_element_type=F32)
    b_gate = proj[:, :D_MODEL]
    cu = proj[:, D_MODEL:2 * D_MODEL] * proj[:, 2 * D_MODEL:]

    @pl.when(i % tiles_per_batch == 0)
    def _():
        cu_ref[0:SUBLANES, :] = jnp.zeros((SUBLANES, D_MODEL), F32)

    cu_ref[SUBLANES:SUBLANES + tm, :] = cu
    cw = cw_ref[...]
    z = (cw[2:3] * cu + cw[1:2] * cu_ref[SUBLANES - 1:SUBLANES - 1 + tm, :]
         + cw[0:1] * cu_ref[SUBLANES - 2:SUBLANES - 2 + tm, :])
    cu_ref[0:SUBLANES, :] = cu_ref[tm:tm + SUBLANES, :]
    y = jnp.dot((b_gate * z).astype(BF16), wout_ref[...], preferred_element_type=F32)
    o_ref[...] = x + y


def _mixer(x, front, g, w_in, conv_w, w_out, bsz, tiles_per_batch):
    t = bsz * tiles_per_batch * TILE
    tiles_real = tiles_per_batch - FRONT // TILE

    def x_map(i):
        return ((i // tiles_per_batch) * tiles_real + jnp.maximum(i % tiles_per_batch - FRONT // TILE, 0), 0)

    return pl.pallas_call(
        functools.partial(_mixer_kernel, tiles_per_batch=tiles_per_batch),
        grid=(t // TILE,),
        in_specs=[
            pl.BlockSpec((TILE, D_MODEL), x_map),
            pl.BlockSpec((TILE, D_MODEL), lambda i: (0, 0)),
            pl.BlockSpec((1, D_MODEL), lambda i: (0, 0)),
            pl.BlockSpec((D_MODEL, 3 * D_MODEL), lambda i: (0, 0)),
            pl.BlockSpec((3, D_MODEL), lambda i: (0, 0)),
            pl.BlockSpec((D_MODEL, D_MODEL), lambda i: (0, 0)),
        ],
        out_specs=pl.BlockSpec((TILE, D_MODEL), lambda i: (i, 0)),
        out_shape=jax.ShapeDtypeStruct((t, D_MODEL), F32),
        scratch_shapes=[pltpu.VMEM((TILE + 2 * SUBLANES, D_MODEL), F32)],
        compiler_params=_cparams(("arbitrary",)),
        name="mixer_a",
    )(x, front, g.reshape(1, D_MODEL), w_in.astype(BF16), conv_w, w_out.astype(BF16))


ROUTER_ROWS = SUBLANES + N_EXPERTS


def _round_up(v, m):
    return jnp.floor((v + (m - 1)) * (1.0 / m)) * m


def _router_kernel(h_ref, g_ref, wrt_ref, br_ref, tri_ref, lpos_ref, gate_ref, meta_ref, tot_ref, seg_sc):
    i = pl.program_id(0)
    tm = h_ref.shape[0]

    @pl.when(i == 0)
    def _():
        seg_sc[...] = jnp.zeros_like(seg_sc)

    xn = _rms(h_ref[...], g_ref[...])
    lt = lax.dot_general(wrt_ref[...], xn, (((1,), (1,)), ((), ())),
                         precision=lax.Precision.HIGHEST, preferred_element_type=F32) + br_ref[...]
    lg = lt[0:N_GROUPS]
    gmax = jnp.max(lg, axis=0, keepdims=True)
    iota_g = lax.broadcasted_iota(jnp.int32, lg.shape, 0)
    grp = jnp.min(jnp.where(lg == gmax, iota_g, N_GROUPS), axis=0, keepdims=True)
    p_grp = 1.0 / jnp.sum(jnp.exp(lg - gmax), axis=0, keepdims=True)
    le = lt[SUBLANES:SUBLANES + EXPERTS_PER_GROUP]
    for g in range(1, N_GROUPS):
        lo = SUBLANES + g * EXPERTS_PER_GROUP
        le = jnp.where(grp == g, lt[lo:lo + EXPERTS_PER_GROUP], le)
    iota_e = lax.broadcasted_iota(jnp.int32, le.shape, 0)
    v1 = jnp.max(le, axis=0, keepdims=True)
    i1 = jnp.min(jnp.where(le == v1, iota_e, EXPERTS_PER_GROUP), axis=0, keepdims=True)
    le2 = jnp.where(iota_e == i1, -jnp.inf, le)
    v2 = jnp.max(le2, axis=0, keepdims=True)
    i2 = jnp.min(jnp.where(le2 == v2, iota_e, EXPERTS_PER_GROUP), axis=0, keepdims=True)
    e2 = jnp.exp(v2 - v1)
    den = 1.0 + e2
    gate_ref[...] = jnp.concatenate([p_grp * (1.0 / den), p_grp * (e2 / den)], axis=0)
    experts = (grp * EXPERTS_PER_GROUP + i1, grp * EXPERTS_PER_GROUP + i2)

    iota_x = lax.broadcasted_iota(jnp.int32, (N_EXPERTS, tm), 0)
    hits = [iota_x == e_k for e_k in experts]
    hits_f = [jnp.where(hit, 1.0, 0.0) for hit in hits]
    hits_b = [hf.astype(BF16) for hf in hits_f]
    excl = [jnp.dot(hb, tri_ref[...], preferred_element_type=F32) for hb in hits_b]
    tot_col = [jnp.sum(hf, axis=1, keepdims=True) for hf in hits_f]
    ones = jnp.ones((SUBLANES, tm), BF16)
    n_lane = sum(lax.dot_general(ones, hb, (((1,), (1,)), ((), ())), preferred_element_type=F32) for hb in hits_b)
    seg_len_lane = _round_up(n_lane[0:1], SEG_ALIGN)
    seg_len = _round_up(tot_col[0] + tot_col[1], SEG_ALIGN)
    before = (lax.broadcasted_iota(jnp.int32, (N_EXPERTS, N_EXPERTS), 1)
              < lax.broadcasted_iota(jnp.int32, (N_EXPERTS, N_EXPERTS), 0))
    local_off = jnp.sum(jnp.where(before, seg_len_lane, 0.0), axis=1, keepdims=True)

    lpos_a = jnp.sum(jnp.where(hits[0], excl[0] + local_off, 0.0), axis=0, keepdims=True).astype(jnp.int32)
    lpos_b = jnp.sum(jnp.where(hits[1], excl[1] + tot_col[0] + local_off, 0.0), axis=0, keepdims=True).astype(jnp.int32)
    lpos_ref[...] = jnp.concatenate([lpos_a, lpos_b], axis=0)

    seg_before = seg_sc[...]
    seg_sc[...] = seg_before + seg_len
    lane = lax.broadcasted_iota(jnp.int32, (N_EXPERTS, LANES), 1)
    meta = jnp.where(lane == 0, seg_len, jnp.where(lane == 1, local_off, jnp.where(lane == 2, seg_before, 0.0)))
    meta_ref[0] = meta.astype(jnp.int32)
    tot_ref[...] = jnp.broadcast_to(seg_before + seg_len, tot_ref.shape).astype(jnp.int32)


def _router(h, g, w_rg, b_rg, w_re, b_re, tm):
    t = h.shape[0]
    nt = t // tm
    wrt = jnp.zeros((ROUTER_ROWS, D_MODEL), F32)
    wrt = wrt.at[0:N_GROUPS].set(w_rg.T).at[SUBLANES:].set(w_re.T)
    br = jnp.zeros((ROUTER_ROWS, 1), F32)
    br = br.at[0:N_GROUPS, 0].set(b_rg).at[SUBLANES:, 0].set(b_re)
    tri = (lax.broadcasted_iota(jnp.int32, (tm, tm), 0) < lax.broadcasted_iota(jnp.int32, (tm, tm), 1)).astype(BF16)
    return pl.pallas_call(
        _router_kernel,
        grid=(nt,),
        in_specs=[
            pl.BlockSpec((tm, D_MODEL), lambda i: (i, 0)),
            pl.BlockSpec((1, D_MODEL), lambda i: (0, 0)),
            pl.BlockSpec((ROUTER_ROWS, D_MODEL), lambda i: (0, 0)),
            pl.BlockSpec((ROUTER_ROWS, 1), lambda i: (0, 0)),
            pl.BlockSpec((tm, tm), lambda i: (0, 0)),
        ],
        out_specs=[
            pl.BlockSpec((2, tm), lambda i: (0, i)),
            pl.BlockSpec((2, tm), lambda i: (0, i)),
            pl.BlockSpec((1, N_EXPERTS, LANES), lambda i: (i, 0, 0)),
            pl.BlockSpec((N_EXPERTS, LANES), lambda i: (0, 0)),
        ],
        out_shape=[
            jax.ShapeDtypeStruct((2, t), jnp.int32),
            jax.ShapeDtypeStruct((2, t), F32),
            jax.ShapeDtypeStruct((nt, N_EXPERTS, LANES), jnp.int32),
            jax.ShapeDtypeStruct((N_EXPERTS, LANES), jnp.int32),
        ],
        scratch_shapes=[pltpu.VMEM((N_EXPERTS, 1), F32)],
        compiler_params=_cparams(("arbitrary",)),
        name="router",
    )(h, g.reshape(1, D_MODEL), wrt, br, tri)


def _if_rows(n, fn):
    pl.when(n > 0)(lambda: fn(pl.multiple_of(n, SEG_ALIGN)))


def _rows(ref, start, size):
    return ref.at[pl.ds(pl.multiple_of(start, SEG_ALIGN), size), :]


def _dispatch_kernel(len_ref, off_ref, dst_ref, zs_ref, zn_ref, nu_ref, h_ref, g_ref, lpos_ref, buf_ref, stage, zero_sc,
                     sem_out, sem_fill, *, n_blk):
    t = pl.program_id(0)
    nt = pl.num_programs(0)
    tm = h_ref.shape[0]
    slot = t % STAGE_SLOTS

    def tile_out(t, s, op):
        def body(e, c):
            idx = t * N_EXPERTS + e
            lo, dst = off_ref[idx], dst_ref[idx]
            _if_rows(len_ref[idx], lambda n: op(pltpu.make_async_copy(
                _rows(stage.at[s], lo, n), _rows(buf_ref, dst, n), sem_out.at[s])))
            return c

        lax.fori_loop(0, N_EXPERTS, body, 0)

    def fill_copies(op):
        def tails(e, c):
            _if_rows(zn_ref[e], lambda n: op(pltpu.make_async_copy(
                zero_sc.at[pl.ds(0, n), :], _rows(buf_ref, zs_ref[e], n), sem_fill)))
            return c

        lax.fori_loop(0, N_EXPERTS, tails, 0)

        def blocks(b, c):
            op(pltpu.make_async_copy(zero_sc, _rows(buf_ref, b * EXPERT_ROWS, EXPERT_ROWS), sem_fill))
            return c

        lax.fori_loop(nu_ref[0], n_blk, blocks, 0)

    start = lambda cp: cp.start()
    wait = lambda cp: cp.wait()

    @pl.when(t == 0)
    def _():
        zero_sc[...] = jnp.zeros_like(zero_sc)
        fill_copies(start)

    @pl.when(t >= STAGE_SLOTS)
    def _():
        tile_out(t - STAGE_SLOTS, slot, wait)

    xn = _rms(h_ref[...], g_ref[...]).astype(BF16)
    lpos = lpos_ref[...]
    row = lax.broadcasted_iota(jnp.int32, (LOCAL_ROWS, tm), 0)
    perm = jnp.where(row == lpos[0:1], 1.0, jnp.where(row == lpos[1:2], 1.0, 0.0)).astype(BF16)
    stage[slot] = jnp.dot(perm, xn, preferred_element_type=F32).astype(BF16)
    tile_out(t, slot, start)

    @pl.when(t == nt - 1)
    def _():
        for back in range(STAGE_SLOTS - 1, -1, -1):
            @pl.when(t >= back)
            def _(back=back):
                tile_out(t - back, (t - back) % STAGE_SLOTS, wait)

        fill_copies(wait)


def _dispatch(h, g, lpos, seg_len, seg_off, seg_dst, tail_start, tail_len, n_used, n_blk, tm):
    t = h.shape[0]
    grid_spec = pltpu.PrefetchScalarGridSpec(
        num_scalar_prefetch=6,
        grid=(t // tm,),
        in_specs=[
            pl.BlockSpec((tm, D_MODEL), lambda i, *_: (i, 0)),
            pl.BlockSpec((1, D_MODEL), lambda i, *_: (0, 0)),
            pl.BlockSpec((2, tm), lambda i, *_: (0, i)),
        ],
        out_specs=pl.BlockSpec(memory_space=pl.ANY),
        scratch_shapes=[pltpu.VMEM((STAGE_SLOTS, LOCAL_ROWS, D_MODEL), BF16), pltpu.VMEM((EXPERT_ROWS, D_MODEL), BF16),
                        pltpu.SemaphoreType.DMA((STAGE_SLOTS,)), pltpu.SemaphoreType.DMA],
    )
    return pl.pallas_call(
        functools.partial(_dispatch_kernel, n_blk=n_blk),
        grid_spec=grid_spec,
        out_shape=jax.ShapeDtypeStruct((n_blk * EXPERT_ROWS, D_MODEL), BF16),
        compiler_params=_cparams(("arbitrary",)),
        name="dispatch",
    )(seg_len, seg_off, seg_dst, tail_start, tail_len, n_used, h, g.reshape(1, D_MODEL), lpos)


def _expert_kernel(be_ref, nu_ref, x_ref, wg_ref, wu_ref, wd_ref, y_ref, wgu_sc, wd_sc):
    i = pl.program_id(0)
    e = be_ref[i]
    prev = be_ref[jnp.maximum(i - 1, 0)]

    @pl.when(jnp.logical_or(i == 0, e != prev))
    def _():
        wgu_sc[:, :D_EXPERT] = wg_ref[0, 0].astype(BF16)
        wgu_sc[:, D_EXPERT:] = wu_ref[0, 0].astype(BF16)
        wd_sc[...] = wd_ref[0, 0].astype(BF16)

    @pl.when(i < nu_ref[0])
    def _():
        gu = jnp.dot(x_ref[...], wgu_sc[...], preferred_element_type=F32)
        a = jax.nn.silu(gu[:, :D_EXPERT]) * gu[:, D_EXPERT:]
        y_ref[...] = jnp.dot(a.astype(BF16), wd_sc[...], preferred_element_type=F32).astype(BF16)

    @pl.when(i >= nu_ref[0])
    def _():
        y_ref[...] = jnp.zeros_like(y_ref)


def _experts(buf, blk_expert, n_used, w_gate, w_up, w_down, layer):
    n_blk = blk_expert.shape[0]

    def xmap(i, be, nu):
        return (jnp.maximum(jnp.minimum(i, nu[0] - 1), 0), 0)

    grid_spec = pltpu.PrefetchScalarGridSpec(
        num_scalar_prefetch=2,
        grid=(n_blk,),
        in_specs=[
            pl.BlockSpec((EXPERT_ROWS, D_MODEL), xmap),
            pl.BlockSpec((1, 1, D_MODEL, D_EXPERT), lambda i, be, nu: (layer, be[i], 0, 0)),
            pl.BlockSpec((1, 1, D_MODEL, D_EXPERT), lambda i, be, nu: (layer, be[i], 0, 0)),
            pl.BlockSpec((1, 1, D_EXPERT, D_MODEL), lambda i, be, nu: (layer, be[i], 0, 0)),
        ],
        out_specs=pl.BlockSpec((EXPERT_ROWS, D_MODEL), lambda i, be, nu: (i, 0)),
        scratch_shapes=[pltpu.VMEM((D_MODEL, 2 * D_EXPERT), BF16), pltpu.VMEM((D_EXPERT, D_MODEL), BF16)],
    )
    return pl.pallas_call(
        _expert_kernel,
        grid_spec=grid_spec,
        out_shape=jax.ShapeDtypeStruct(buf.shape, BF16),
        compiler_params=_cparams(("arbitrary",)),
        name="experts",
    )(blk_expert, n_used, buf, w_gate, w_up, w_down)


def _combine_kernel(len_ref, off_ref, dst_ref, h_ref, gate_ref, lpos_ref, y_ref, fn_ref, o_ref, yl, sem, *, final_norm):
    i = pl.program_id(0)
    nt = pl.num_programs(0)
    tm = h_ref.shape[0]
    slot = i % 2

    def tile_copies(t, s, op):
        def body(e, c):
            idx = t * N_EXPERTS + e
            lo, src = off_ref[idx], dst_ref[idx]
            _if_rows(len_ref[idx], lambda n: op(pltpu.make_async_copy(
                _rows(y_ref, src, n), _rows(yl.at[s], lo, n), sem.at[s])))
            return c

        lax.fori_loop(0, N_EXPERTS, body, 0)

    @pl.when(i == 0)
    def _():
        yl[...] = jnp.zeros_like(yl)
        tile_copies(0, 0, lambda cp: cp.start())

    @pl.when(i + 1 < nt)
    def _():
        tile_copies(i + 1, 1 - slot, lambda cp: cp.start())

    tile_copies(i, slot, lambda cp: cp.wait())

    rows = yl[slot]
    lpos = lpos_ref[...]
    col = lax.broadcasted_iota(jnp.int32, (tm, LOCAL_ROWS), 1)
    g = gate_ref[...]
    weights = jnp.where(col == lpos[:, 0:1], g[:, 0:1], jnp.where(col == lpos[:, 1:2], g[:, 1:2], 0.0)).astype(BF16)
    out = h_ref[...] + jnp.dot(weights, rows, preferred_element_type=F32)
    if final_norm:
        out = _rms(out, fn_ref[...])
    o_ref[...] = out


def _combine(h, y, seg_len, seg_off, seg_dst, lpos, gate, fn, tm, final_norm):
    t = h.shape[0]
    grid_spec = pltpu.PrefetchScalarGridSpec(
        num_scalar_prefetch=3,
        grid=(t // tm,),
        in_specs=[
            pl.BlockSpec((tm, D_MODEL), lambda i, *_: (i, 0)),
            pl.BlockSpec((tm, 2), lambda i, *_: (i, 0)),
            pl.BlockSpec((tm, 2), lambda i, *_: (i, 0)),
            pl.BlockSpec(memory_space=pl.ANY),
            pl.BlockSpec((1, D_MODEL), lambda i, *_: (0, 0)),
        ],
        out_specs=pl.BlockSpec((tm, D_MODEL), lambda i, *_: (i, 0)),
        scratch_shapes=[pltpu.VMEM((2, LOCAL_ROWS, D_MODEL), BF16), pltpu.SemaphoreType.DMA((2,))],
    )
    return pl.pallas_call(
        functools.partial(_combine_kernel, final_norm=final_norm),
        grid_spec=grid_spec,
        out_shape=jax.ShapeDtypeStruct((t, D_MODEL), F32),
        compiler_params=_cparams(("arbitrary",)),
        name="combine",
    )(seg_len, seg_off, seg_dst, h, gate.T, lpos.T, y, fn.reshape(1, D_MODEL))


def _moe(h, g, w_rg, b_rg, w_re, b_re, w_gate, w_up, w_down, layer, fn, tm, final_norm):
    t = h.shape[0]
    nt = t // tm
    lpos, gate, meta, tot = _router(h, g, w_rg, b_rg, w_re, b_re, tm)
    n_blk = (2 * t + nt * N_EXPERTS * (SEG_ALIGN - 1)) // EXPERT_ROWS + N_EXPERTS
    total = tot[:, 0]
    region = (total + EXPERT_ROWS - 1) // EXPERT_ROWS * EXPERT_ROWS
    ends = jnp.cumsum(region)
    starts = ends - region
    seg_len = meta[:, :, 0].reshape(-1)
    seg_off = meta[:, :, 1].reshape(-1)
    seg_dst = (meta[:, :, 2] + starts[None, :]).reshape(-1)
    blk_start = jnp.arange(n_blk, dtype=jnp.int32) * EXPERT_ROWS
    blk_expert = jnp.minimum(jnp.sum(blk_start[:, None] >= ends[None, :], axis=1), N_EXPERTS - 1).astype(jnp.int32)
    n_used = (ends[-1:] // EXPERT_ROWS).astype(jnp.int32)
    buf = _dispatch(h, g, lpos, seg_len, seg_off, seg_dst, starts + total, region - total, n_used, n_blk, tm)
    y = _experts(buf, blk_expert, n_used, w_gate, w_up, w_down, layer)
    return _combine(h, y, seg_len, seg_off, seg_dst, lpos, gate, fn, tm, final_norm)


def _kv_kernel(h_ref, g_ref, wk_ref, wvt_ref, cos_ref, sa_ref, sb_ref, k_ref, vt_ref):
    hn = _rms(h_ref[...], g_ref[...]).astype(BF16)
    k = jnp.dot(hn, wk_ref[...], preferred_element_type=F32)
    cos, sa, sb = cos_ref[...], sa_ref[...], sb_ref[...]
    for c in range(D_MODEL // LANES):
        kc = k[:, c * LANES:(c + 1) * LANES]
        rot = kc * cos + pltpu.roll(kc, LANES - ROT_HALF, 1) * sa + pltpu.roll(kc, ROT_HALF, 1) * sb
        k_ref[0, :, c * LANES:(c + 1) * LANES] = rot.astype(BF16)
    vt = lax.dot_general(wvt_ref[...], hn, (((1,), (1,)), ((), ())), preferred_element_type=F32)
    for c in range(vt_ref.shape[1]):
        vt_ref[0, c] = vt[:, c * LANES:(c + 1) * LANES].astype(BF16)


def _kv(h, g, w_kv, bsz, lp):
    tpb = lp // TILE
    kblk = TILE // LANES
    pos = jnp.maximum(jnp.arange(lp) - PAD_ROWS, 0).astype(F32)
    inv_freq = ROPE_THETA ** (-jnp.arange(ROT_HALF, dtype=F32) * 2.0 / (2 * ROT_HALF))
    ang = pos[:, None] * inv_freq[None, :]
    r = jnp.arange(LANES) % HEAD_DIM
    cos_t = jnp.where(r < 2 * ROT_HALF, jnp.cos(ang)[:, r % ROT_HALF], 1.0)
    sin_t = jnp.sin(ang)[:, r % ROT_HALF]
    sa = jnp.where(r < ROT_HALF, -sin_t, 0.0)
    sb = jnp.where((r >= ROT_HALF) & (r < 2 * ROT_HALF), sin_t, 0.0)
    tab = pl.BlockSpec((TILE, LANES), lambda i: (i % tpb, 0))
    return pl.pallas_call(
        _kv_kernel,
        grid=(bsz * tpb,),
        in_specs=[
            pl.BlockSpec((TILE, D_MODEL), lambda i: (i, 0)),
            pl.BlockSpec((1, D_MODEL), lambda i: (0, 0)),
            pl.BlockSpec((D_MODEL, D_MODEL), lambda i: (0, 0)),
            pl.BlockSpec((D_MODEL, D_MODEL), lambda i: (0, 0)),
            tab, tab, tab,
        ],
        out_specs=[
            pl.BlockSpec((1, TILE, D_MODEL), lambda i: (i // tpb, i % tpb, 0)),
            pl.BlockSpec((1, kblk, D_MODEL, LANES), lambda i: (i // tpb, i % tpb, 0, 0)),
        ],
        out_shape=[
            jax.ShapeDtypeStruct((bsz, lp, D_MODEL), BF16),
            jax.ShapeDtypeStruct((bsz, lp // LANES, D_MODEL, LANES), BF16),
        ],
        compiler_params=_cparams(("arbitrary",)),
        name="kv_proj",
    )(h, g.reshape(1, D_MODEL), w_kv[:, :D_MODEL].astype(BF16), w_kv[:, D_MODEL:].T.astype(BF16), cos_t, sa, sb)


def _q_kernel(h_ref, g_ref, wqt_ref, cos_ref, sin_ref, qt_ref):
    hn = _rms(h_ref[...], g_ref[...]).astype(BF16)
    qt = lax.dot_general(wqt_ref[...], hn, (((1,), (1,)), ((), ())), preferred_element_type=F32)
    cos, sin = cos_ref[...], sin_ref[...]
    scale = HEAD_DIM ** -0.5 * math.log2(math.e)
    for c in range(D_MODEL // HEAD_DIM):
        lo = c * HEAD_DIM
        x1 = qt[lo:lo + ROT_HALF]
        x2 = qt[lo + ROT_HALF:lo + 2 * ROT_HALF]
        blk = jnp.concatenate([x1 * cos - x2 * sin, x2 * cos + x1 * sin, qt[lo + 2 * ROT_HALF:lo + HEAD_DIM]], axis=0)
        qt_ref[0, lo:lo + HEAD_DIM, :] = (blk * scale).astype(BF16)


def _real_row_map(tiles_real, tiles_padded):
    front = FRONT // TILE
    return lambda i: ((i // tiles_real) * tiles_padded + front + i % tiles_real, 0)


def _q_proj(h_pad, g, w_q, bsz, seq, lp):
    tiles_real = seq // TILE
    pos = (N_META + jnp.arange(seq)).astype(F32)
    inv_freq = ROPE_THETA ** (-jnp.arange(ROT_HALF, dtype=F32) * 2.0 / (2 * ROT_HALF))
    ang = inv_freq[:, None] * pos[None, :]
    tab = pl.BlockSpec((ROT_HALF, TILE), lambda i: (0, i % tiles_real))
    return pl.pallas_call(
        _q_kernel,
        grid=(bsz * tiles_real,),
        in_specs=[
            pl.BlockSpec((TILE, D_MODEL), _real_row_map(tiles_real, lp // TILE)),
            pl.BlockSpec((1, D_MODEL), lambda i: (0, 0)),
            pl.BlockSpec((D_MODEL, D_MODEL), lambda i: (0, 0)),
            tab, tab,
        ],
        out_specs=pl.BlockSpec((1, D_MODEL, TILE), lambda i: (i, 0, 0)),
        out_shape=jax.ShapeDtypeStruct((bsz * tiles_real, D_MODEL, TILE), BF16),
        compiler_params=_cparams(("arbitrary",)),
        name="q_proj",
    )(h_pad, g.reshape(1, D_MODEL), w_q.T.astype(BF16), jnp.cos(ang), jnp.sin(ang))


STEP_FULL = 0
STEP_EDGE = (1, 2)


def _attn_schedule(nq):
    per_k = ATT_TK // ATT_TQ
    steps = [(i, t, STEP_FULL if t < i // per_k else STEP_EDGE[i % per_k]) for i in range(nq) for t in range(i // per_k + 1)]
    return [list(col) for col in zip(*steps)]


def _attn_kernel(qi_ref, kt_ref, kind_ref, qt_ref, k_ref, vt_ref, lam_ref, g_ref, o_ref, m_sc, l_sc, acc_sc, sa_sc, sb_sc,
                 *, lam_init, n_steps, block_kinds):
    tq = ATT_TQ
    zero = jnp.zeros((HEAD_DIM, tq), BF16)
    n_sub = ATT_TK // ATT_SUB

    def q_maps(s):
        qt = qt_ref[0, qi_ref[s]]
        return (jnp.concatenate([qt[0:HEAD_DIM], zero], axis=0), jnp.concatenate([zero, qt[HEAD_DIM:]], axis=0))

    def update(n, sc, vtt, fresh=None):
        m_old, l_old, acc_old = m_sc[n], l_sc[n], acc_sc[n]
        if fresh is not None:
            m_old = jnp.where(fresh, NEG, m_old)
            l_old = jnp.where(fresh, 0.0, l_old)
            acc_old = jnp.where(fresh, 0.0, acc_old)
        m_new = jnp.maximum(m_old, jnp.max(sc, axis=0, keepdims=True))
        alpha = jnp.exp2(m_old - m_new)
        p = jnp.exp2(sc - m_new)
        l_sc[n] = alpha * l_old + jnp.sum(p, axis=0, keepdims=True)
        acc_sc[n] = alpha * acc_old + jnp.dot(vtt, p.astype(BF16), preferred_element_type=F32)
        m_sc[n] = m_new

    def values(row0, nkeys):
        blk0 = row0 // LANES
        return jnp.concatenate([vt_ref[0, blk0 + c] for c in range(nkeys // LANES)], axis=1)

    def key_row(s, u):
        return pl.multiple_of(FRONT + kt_ref[s] * ATT_TK + u * ATT_SUB, LANES)

    def scores_into(s, buf):
        qs = q_maps(s)
        for u in range(n_sub):
            kt = k_ref[0, pl.ds(key_row(s, u), ATT_SUB), :]
            for n in range(2):
                buf[u * 2 + n] = jnp.dot(kt, qs[n], preferred_element_type=F32)

    def consume(s, buf, kind):
        fresh = kt_ref[s] == 0
        edge = STEP_EDGE.index(kind) if kind in STEP_EDGE else None
        for u in range(n_sub) if edge is None else range(edge + 1):
            vtt = values(key_row(s, u), ATT_SUB)
            for n in range(2):
                sc = buf[u * 2 + n]
                if u == edge:
                    r = lax.broadcasted_iota(jnp.int32, (ATT_SUB, tq), 0) // CHUNK
                    c = lax.broadcasted_iota(jnp.int32, (ATT_SUB, tq), 1) // CHUNK
                    sc = jnp.where(r <= c, sc, NEG)
                update(n, sc, vtt, fresh if u == 0 else None)

    def finish(s):
        qs = q_maps(s)
        kt = k_ref[0, FRONT - N_META:FRONT, :]
        vtt = vt_ref[0, FRONT // LANES - 1][:, LANES - N_META:]
        for n in range(2):
            update(n, jnp.dot(kt, qs[n], preferred_element_type=F32), vtt)
        lp = lam_ref[...]
        lam = (jnp.exp(jnp.sum(lp[0:1] * lp[1:2], axis=1, keepdims=True))
               - jnp.exp(jnp.sum(lp[2:3] * lp[3:4], axis=1, keepdims=True)) + lam_init)
        o = acc_sc[0] / l_sc[0] - lam * (acc_sc[1] / l_sc[1])
        o = o * lax.rsqrt(jnp.mean(o * o, axis=0, keepdims=True) + EPS) * g_ref[...] * (1.0 - lam_init)
        o_ref[pl.ds(pl.multiple_of(qi_ref[s] * tq, tq), tq), :] = o.T.astype(BF16)

    def block(p, kind_a, kind_b):
        s_a, s_b = 2 * p, 2 * p + 1
        scores_into(s_b, sb_sc)
        consume(s_a, sa_sc, kind_a)
        if kind_a in STEP_EDGE:
            finish(s_a)
        scores_into(jnp.minimum(s_b + 1, n_steps - 1), sa_sc)
        consume(s_b, sb_sc, kind_b)
        if kind_b in STEP_EDGE:
            finish(s_b)

    scores_into(0, sa_sc)

    def body(p, c):
        kind_a, kind_b = kind_ref[2 * p], kind_ref[2 * p + 1]
        for va, vb in block_kinds:
            pl.when(jnp.logical_and(kind_a == va, kind_b == vb))(functools.partial(block, p, va, vb))
        return c

    lax.fori_loop(0, n_steps // 2, body, 0)


def _attention(qt, k, vt4, lam_p, subln_g, bsz, seq, lam_init):
    nq = seq // ATT_TQ
    lp = k.shape[1]
    q_idx, k_idx, kinds = _attn_schedule(nq)
    n_steps = len(kinds)
    assert n_steps % 2 == 0
    block_kinds = sorted(set(zip(kinds[0::2], kinds[1::2])))
    grid_spec = pltpu.PrefetchScalarGridSpec(
        num_scalar_prefetch=3,
        grid=(bsz, N_HEADS),
        in_specs=[
            pl.BlockSpec((1, nq, 2 * HEAD_DIM, ATT_TQ), lambda b, h, *_: (b, 0, h, 0)),
            pl.BlockSpec((1, lp, 2 * HEAD_DIM), lambda b, h, *_: (b, 0, h)),
            pl.BlockSpec((1, lp // LANES, 2 * HEAD_DIM, LANES), lambda b, h, *_: (b, 0, h, 0)),
            pl.BlockSpec((4, HEAD_DIM), lambda b, h, *_: (0, 0)),
            pl.BlockSpec((2 * HEAD_DIM, 1), lambda b, h, *_: (0, 0)),
        ],
        out_specs=pl.BlockSpec((seq, 2 * HEAD_DIM), lambda b, h, *_: (b, h)),
        scratch_shapes=[pltpu.VMEM((2, 1, ATT_TQ), F32), pltpu.VMEM((2, 1, ATT_TQ), F32),
                        pltpu.VMEM((2, 2 * HEAD_DIM, ATT_TQ), F32),
                        pltpu.VMEM((2 * ATT_TK // ATT_SUB, ATT_SUB, ATT_TQ), F32),
                        pltpu.VMEM((2 * ATT_TK // ATT_SUB, ATT_SUB, ATT_TQ), F32)],
    )
    return pl.pallas_call(
        functools.partial(_attn_kernel, lam_init=lam_init, n_steps=n_steps, block_kinds=block_kinds),
        grid_spec=grid_spec,
        out_shape=jax.ShapeDtypeStruct((bsz * seq, D_MODEL), BF16),
        compiler_params=_cparams(("arbitrary", "arbitrary")),
        name="diff_attention",
    )(*(jnp.asarray(col, jnp.int32) for col in (q_idx, k_idx, kinds)),
      qt.reshape(bsz, nq, D_MODEL, ATT_TQ), k, vt4, lam_p, subln_g.reshape(2 * HEAD_DIM, 1))


def _oproj_kernel(h_ref, o_ref, w_ref, out_ref):
    out_ref[...] = h_ref[...] + jnp.dot(o_ref[...], w_ref[...], preferred_element_type=F32)


def _o_proj(h_pad, o, w_out, bsz, seq, lp):
    tiles_real = seq // TILE
    return pl.pallas_call(
        _oproj_kernel,
        grid=(bsz * tiles_real,),
        in_specs=[
            pl.BlockSpec((TILE, D_MODEL), _real_row_map(tiles_real, lp // TILE)),
            pl.BlockSpec((TILE, D_MODEL), lambda i: (i, 0)),
            pl.BlockSpec((D_MODEL, D_MODEL), lambda i: (0, 0)),
        ],
        out_specs=pl.BlockSpec((TILE, D_MODEL), lambda i: (i, 0)),
        out_shape=jax.ShapeDtypeStruct((bsz * seq, D_MODEL), F32),
        compiler_params=_cparams(("arbitrary",)),
        name="o_proj",
    )(h_pad, o, w_out.astype(BF16))


def kernel(x, meta_tokens, a_norm, a_w_in, a_conv, a_w_out, kv_norm, w_kv, b_norm, b_w_q, b_lambda, b_subln, b_w_out, ffn_norm, r_group, r_group_b, r_expert, r_expert_b, e_gate, e_up, e_down, final_norm):
    bsz, seq, d = x.shape
    assert d == D_MODEL and a_norm.shape[0] == 1 and b_norm.shape[0] == 1
    lp = FRONT + seq
    assert seq % ATT_TK == 0 and ATT_TK == 2 * ATT_TQ and ATT_TQ == TILE and ATT_TK % ATT_SUB == 0
    front = jnp.concatenate([jnp.zeros((PAD_ROWS, d), x.dtype), meta_tokens.astype(x.dtype)], axis=0)
    h = _mixer(x.reshape(bsz * seq, d), front, a_norm[0], a_w_in[0], a_conv[0], a_w_out[0], bsz, lp // TILE)
    h = _moe(h, ffn_norm[0], r_group[0], r_group_b[0], r_expert[0], r_expert_b[0],
             e_gate, e_up, e_down, 0, final_norm, TILE, False)

    k, vt4 = _kv(h, kv_norm, w_kv, bsz, lp)
    qt = _q_proj(h, b_norm[0], b_w_q[0], bsz, seq, lp)
    lam_init = 0.8 - 0.6 * math.exp(-0.3 * a_norm.shape[0])
    o = _attention(qt, k, vt4, b_lambda[0], b_subln[0], bsz, seq, lam_init)
    h = _o_proj(h, o, b_w_out[0], bsz, seq, lp)
    h = _moe(h, ffn_norm[1], r_group[1], r_group_b[1], r_expert[1], r_expert_b[1],
             e_gate, e_up, e_down, 1, final_norm, TILE, True)
    return h.reshape(bsz, seq, d)
```

```python
import functools
import math

import jax
import jax.numpy as jnp
from jax import lax
from jax.experimental import pallas as pl
from jax.experimental.pallas import tpu as pltpu

D_MODEL = 1024
CHUNK = 64
N_META = 16
HEAD_DIM = 64
N_HEADS = 8
ROT_HALF = 8
ROPE_THETA = 500000.0
N_GROUPS = 4
EXPERTS_PER_GROUP = 8
N_EXPERTS = 32
D_EXPERT = 512
EPS = 1e-6

LANES = 128
SUBLANES = 8
TILE = 512
FRONT = TILE
PAD_ROWS = FRONT - N_META
EXPERT_ROWS = 512
ATT_TQ = 512
ATT_TK = 1024
ATT_SUB = 512
SEG_ALIGN = 16
STAGE_SLOTS = 2
LOCAL_ROWS = 1536
VMEM_LIMIT = 56 * 1024 * 1024
NEG = float(jnp.finfo(jnp.float32).min)

BF16 = jnp.bfloat16
F32 = jnp.float32


def _rms(x, g):
    return x * lax.rsqrt(jnp.mean(x * x, axis=-1, keepdims=True) + EPS) * g


def _cparams(sem):
    return pltpu.CompilerParams(dimension_semantics=sem, vmem_limit_bytes=VMEM_LIMIT)


def _mixer_kernel(x_ref, front_ref, g_ref, win_ref, cw_ref, wout_ref, rg_ref, wrt_ref, br_ref, tri_ref,
                  o_ref, lpos_ref, gate_ref, meta_ref, tot_ref, cu_ref, seg_sc, *, tiles_per_batch):
    i = pl.program_id(0)
    tm = x_ref.shape[0]
    x = jnp.where(i % tiles_per_batch == 0, front_ref[...], x_ref[...])
    hn = _rms(x, g_ref[...]).astype(BF16)
    proj = jnp.dot(hn, win_ref[...], preferred_element_type=F32)
    b_gate = proj[:, :D_MODEL]
    cu = proj[:, D_MODEL:2 * D_MODEL] * proj[:, 2 * D_MODEL:]

    @pl.when(i % tiles_per_batch == 0)
    def _():
        cu_ref[0:SUBLANES, :] = jnp.zeros((SUBLANES, D_MODEL), F32)

    cu_ref[SUBLANES:SUBLANES + tm, :] = cu
    cw = cw_ref[...]
    z = (cw[2:3] * cu + cw[1:2] * cu_ref[SUBLANES - 1:SUBLANES - 1 + tm, :]
         + cw[0:1] * cu_ref[SUBLANES - 2:SUBLANES - 2 + tm, :])
    cu_ref[0:SUBLANES, :] = cu_ref[tm:tm + SUBLANES, :]
    y = jnp.dot((b_gate * z).astype(BF16), wout_ref[...], preferred_element_type=F32)
    h = x + y
    o_ref[...] = h
    _route(h, rg_ref, wrt_ref, br_ref, tri_ref, lpos_ref, gate_ref, meta_ref, tot_ref, seg_sc)


def _mixer(x, front, g, w_in, conv_w, w_out, router, bsz, tiles_per_batch):
    t = bsz * tiles_per_batch * TILE
    r_args, r_specs = _router_operands(*router, TILE)
    ro_specs, ro_shapes, r_scratch = _router_outputs(t, TILE)
    tiles_real = tiles_per_batch - FRONT // TILE

    def x_map(i):
        return ((i // tiles_per_batch) * tiles_real + jnp.maximum(i % tiles_per_batch - FRONT // TILE, 0), 0)

    return pl.pallas_call(
        functools.partial(_mixer_kernel, tiles_per_batch=tiles_per_batch),
        grid=(t // TILE,),
        in_specs=[
            pl.BlockSpec((TILE, D_MODEL), x_map),
            pl.BlockSpec((TILE, D_MODEL), lambda i: (0, 0)),
            pl.BlockSpec((1, D_MODEL), lambda i: (0, 0)),
            pl.BlockSpec((D_MODEL, 3 * D_MODEL), lambda i: (0, 0)),
            pl.BlockSpec((3, D_MODEL), lambda i: (0, 0)),
            pl.BlockSpec((D_MODEL, D_MODEL), lambda i: (0, 0)),
        ] + r_specs,
        out_specs=[pl.BlockSpec((TILE, D_MODEL), lambda i: (i, 0))] + ro_specs,
        out_shape=[jax.ShapeDtypeStruct((t, D_MODEL), F32)] + ro_shapes,
        scratch_shapes=[pltpu.VMEM((TILE + 2 * SUBLANES, D_MODEL), F32), r_scratch],
        compiler_params=_cparams(("arbitrary",)),
        name="mixer_a",
    )(x, front, g.reshape(1, D_MODEL), w_in.astype(BF16), conv_w, w_out.astype(BF16), *r_args)


ROUTER_ROWS = SUBLANES + N_EXPERTS


def _round_up(v, m):
    return jnp.floor((v + (m - 1)) * (1.0 / m)) * m


def _route(h, g_ref, wrt_ref, br_ref, tri_ref, lpos_ref, gate_ref, meta_ref, tot_ref, seg_sc):
    i = pl.program_id(0)
    tm = h.shape[0]

    @pl.when(i == 0)
    def _():
        seg_sc[...] = jnp.zeros_like(seg_sc)

    xn = _rms(h, g_ref[...])
    lt = lax.dot_general(wrt_ref[...], xn, (((1,), (1,)), ((), ())),
                         precision=lax.Precision.HIGHEST, preferred_element_type=F32) + br_ref[...]
    lg = lt[0:N_GROUPS]
    gmax = jnp.max(lg, axis=0, keepdims=True)
    iota_g = lax.broadcasted_iota(jnp.int32, lg.shape, 0)
    grp = jnp.min(jnp.where(lg == gmax, iota_g, N_GROUPS), axis=0, keepdims=True)
    p_grp = 1.0 / jnp.sum(jnp.exp(lg - gmax), axis=0, keepdims=True)
    le = lt[SUBLANES:SUBLANES + EXPERTS_PER_GROUP]
    for g in range(1, N_GROUPS):
        lo = SUBLANES + g * EXPERTS_PER_GROUP
        le = jnp.where(grp == g, lt[lo:lo + EXPERTS_PER_GROUP], le)
    iota_e = lax.broadcasted_iota(jnp.int32, le.shape, 0)
    v1 = jnp.max(le, axis=0, keepdims=True)
    i1 = jnp.min(jnp.where(le == v1, iota_e, EXPERTS_PER_GROUP), axis=0, keepdims=True)
    le2 = jnp.where(iota_e == i1, -jnp.inf, le)
    v2 = jnp.max(le2, axis=0, keepdims=True)
    i2 = jnp.min(jnp.where(le2 == v2, iota_e, EXPERTS_PER_GROUP), axis=0, keepdims=True)
    e2 = jnp.exp(v2 - v1)
    den = 1.0 + e2
    gate_ref[...] = jnp.concatenate([p_grp * (1.0 / den), p_grp * (e2 / den)], axis=0)
    experts = (grp * EXPERTS_PER_GROUP + i1, grp * EXPERTS_PER_GROUP + i2)

    iota_x = lax.broadcasted_iota(jnp.int32, (N_EXPERTS, tm), 0)
    hits = [iota_x == e_k for e_k in experts]
    hits_f = [jnp.where(hit, 1.0, 0.0) for hit in hits]
    hits_b = [hf.astype(BF16) for hf in hits_f]
    excl = [jnp.dot(hb, tri_ref[...], preferred_element_type=F32) for hb in hits_b]
    tot_col = [jnp.sum(hf, axis=1, keepdims=True) for hf in hits_f]
    ones = jnp.ones((SUBLANES, tm), BF16)
    n_lane = sum(lax.dot_general(ones, hb, (((1,), (1,)), ((), ())), preferred_element_type=F32) for hb in hits_b)
    seg_len_lane = _round_up(n_lane[0:1], SEG_ALIGN)
    seg_len = _round_up(tot_col[0] + tot_col[1], SEG_ALIGN)
    before = (lax.broadcasted_iota(jnp.int32, (N_EXPERTS, N_EXPERTS), 1)
              < lax.broadcasted_iota(jnp.int32, (N_EXPERTS, N_EXPERTS), 0))
    local_off = jnp.sum(jnp.where(before, seg_len_lane, 0.0), axis=1, keepdims=True)

    lpos_a = jnp.sum(jnp.where(hits[0], excl[0] + local_off, 0.0), axis=0, keepdims=True).astype(jnp.int32)
    lpos_b = jnp.sum(jnp.where(hits[1], excl[1] + tot_col[0] + local_off, 0.0), axis=0, keepdims=True).astype(jnp.int32)
    lpos_ref[...] = jnp.concatenate([lpos_a, lpos_b], axis=0)

    seg_before = seg_sc[...]
    seg_sc[...] = seg_before + seg_len
    lane = lax.broadcasted_iota(jnp.int32, (N_EXPERTS, LANES), 1)
    meta = jnp.where(lane == 0, seg_len, jnp.where(lane == 1, local_off, jnp.where(lane == 2, seg_before, 0.0)))
    meta_ref[0] = meta.astype(jnp.int32)
    tot_ref[...] = jnp.broadcast_to(seg_before + seg_len, tot_ref.shape).astype(jnp.int32)


def _router_operands(g, w_rg, b_rg, w_re, b_re, tm):
    wrt = jnp.zeros((ROUTER_ROWS, D_MODEL), F32)
    wrt = wrt.at[0:N_GROUPS].set(w_rg.T).at[SUBLANES:].set(w_re.T)
    br = jnp.zeros((ROUTER_ROWS, 1), F32)
    br = br.at[0:N_GROUPS, 0].set(b_rg).at[SUBLANES:, 0].set(b_re)
    tri = (lax.broadcasted_iota(jnp.int32, (tm, tm), 0) < lax.broadcasted_iota(jnp.int32, (tm, tm), 1)).astype(BF16)
    specs = [
        pl.BlockSpec((1, D_MODEL), lambda i: (0, 0)),
        pl.BlockSpec((ROUTER_ROWS, D_MODEL), lambda i: (0, 0)),
        pl.BlockSpec((ROUTER_ROWS, 1), lambda i: (0, 0)),
        pl.BlockSpec((tm, tm), lambda i: (0, 0)),
    ]
    return [g.reshape(1, D_MODEL), wrt, br, tri], specs


def _router_outputs(t, tm):
    nt = t // tm
    specs = [
        pl.BlockSpec((2, tm), lambda i: (0, i)),
        pl.BlockSpec((2, tm), lambda i: (0, i)),
        pl.BlockSpec((1, N_EXPERTS, LANES), lambda i: (i, 0, 0)),
        pl.BlockSpec((N_EXPERTS, LANES), lambda i: (0, 0)),
    ]
    shapes = [
        jax.ShapeDtypeStruct((2, t), jnp.int32),
        jax.ShapeDtypeStruct((2, t), F32),
        jax.ShapeDtypeStruct((nt, N_EXPERTS, LANES), jnp.int32),
        jax.ShapeDtypeStruct((N_EXPERTS, LANES), jnp.int32),
    ]
    return specs, shapes, pltpu.VMEM((N_EXPERTS, 1), F32)


def _if_rows(n, fn):
    pl.when(n > 0)(lambda: fn(pl.multiple_of(n, SEG_ALIGN)))


def _rows(ref, start, size):
    return ref.at[pl.ds(pl.multiple_of(start, SEG_ALIGN), size), :]


def _dispatch_kernel(len_ref, off_ref, dst_ref, zs_ref, zn_ref, nu_ref, h_ref, g_ref, lpos_ref, buf_ref, stage, zero_sc,
                     sem_out, sem_fill, *, n_blk):
    t = pl.program_id(0)
    nt = pl.num_programs(0)
    tm = h_ref.shape[0]
    slot = t % STAGE_SLOTS

    def tile_out(t, s, op):
        def body(e, c):
            idx = t * N_EXPERTS + e
            lo, dst = off_ref[idx], dst_ref[idx]
            _if_rows(len_ref[idx], lambda n: op(pltpu.make_async_copy(
                _rows(stage.at[s], lo, n), _rows(buf_ref, dst, n), sem_out.at[s])))
            return c

        lax.fori_loop(0, N_EXPERTS, body, 0)

    def fill_copies(op):
        def tails(e, c):
            _if_rows(zn_ref[e], lambda n: op(pltpu.make_async_copy(
                zero_sc.at[pl.ds(0, n), :], _rows(buf_ref, zs_ref[e], n), sem_fill)))
            return c

        lax.fori_loop(0, N_EXPERTS, tails, 0)

        def blocks(b, c):
            op(pltpu.make_async_copy(zero_sc, _rows(buf_ref, b * EXPERT_ROWS, EXPERT_ROWS), sem_fill))
            return c

        lax.fori_loop(nu_ref[0], n_blk, blocks, 0)

    start = lambda cp: cp.start()
    wait = lambda cp: cp.wait()

    @pl.when(t == 0)
    def _():
        zero_sc[...] = jnp.zeros_like(zero_sc)
        fill_copies(start)

    @pl.when(t >= STAGE_SLOTS)
    def _():
        tile_out(t - STAGE_SLOTS, slot, wait)

    xn = _rms(h_ref[...], g_ref[...]).astype(BF16)
    lpos = lpos_ref[...]
    row = lax.broadcasted_iota(jnp.int32, (LOCAL_ROWS, tm), 0)
    perm = jnp.where(row == lpos[0:1], 1.0, jnp.where(row == lpos[1:2], 1.0, 0.0)).astype(BF16)
    stage[slot] = jnp.dot(perm, xn, preferred_element_type=F32).astype(BF16)
    tile_out(t, slot, start)

    @pl.when(t == nt - 1)
    def _():
        for back in range(STAGE_SLOTS - 1, -1, -1):
            @pl.when(t >= back)
            def _(back=back):
                tile_out(t - back, (t - back) % STAGE_SLOTS, wait)

        fill_copies(wait)


def _dispatch(h, g, lpos, seg_len, seg_off, seg_dst, tail_start, tail_len, n_used, n_blk, tm):
    t = h.shape[0]
    grid_spec = pltpu.PrefetchScalarGridSpec(
        num_scalar_prefetch=6,
        grid=(t // tm,),
        in_specs=[
            pl.BlockSpec((tm, D_MODEL), lambda i, *_: (i, 0)),
            pl.BlockSpec((1, D_MODEL), lambda i, *_: (0, 0)),
            pl.BlockSpec((2, tm), lambda i, *_: (0, i)),
        ],
        out_specs=pl.BlockSpec(memory_space=pl.ANY),
        scratch_shapes=[pltpu.VMEM((STAGE_SLOTS, LOCAL_ROWS, D_MODEL), BF16), pltpu.VMEM((EXPERT_ROWS, D_MODEL), BF16),
                        pltpu.SemaphoreType.DMA((STAGE_SLOTS,)), pltpu.SemaphoreType.DMA],
    )
    return pl.pallas_call(
        functools.partial(_dispatch_kernel, n_blk=n_blk),
        grid_spec=grid_spec,
        out_shape=jax.ShapeDtypeStruct((n_blk * EXPERT_ROWS, D_MODEL), BF16),
        compiler_params=_cparams(("arbitrary",)),
        name="dispatch",
    )(seg_len, seg_off, seg_dst, tail_start, tail_len, n_used, h, g.reshape(1, D_MODEL), lpos)


def _expert_kernel(be_ref, nu_ref, x_ref, wg_ref, wu_ref, wd_ref, y_ref, wgu_sc, wd_sc):
    i = pl.program_id(0)
    e = be_ref[i]
    prev = be_ref[jnp.maximum(i - 1, 0)]

    @pl.when(jnp.logical_or(i == 0, e != prev))
    def _():
        wgu_sc[:, :D_EXPERT] = wg_ref[0, 0].astype(BF16)
        wgu_sc[:, D_EXPERT:] = wu_ref[0, 0].astype(BF16)
        wd_sc[...] = wd_ref[0, 0].astype(BF16)

    @pl.when(i < nu_ref[0])
    def _():
        gu = jnp.dot(x_ref[...], wgu_sc[...], preferred_element_type=F32)
        a = jax.nn.silu(gu[:, :D_EXPERT]) * gu[:, D_EXPERT:]
        y_ref[...] = jnp.dot(a.astype(BF16), wd_sc[...], preferred_element_type=F32).astype(BF16)

    @pl.when(i >= nu_ref[0])
    def _():
        y_ref[...] = jnp.zeros_like(y_ref)


def _experts(buf, blk_expert, n_used, w_gate, w_up, w_down, layer):
    n_blk = blk_expert.shape[0]

    def xmap(i, be, nu):
        return (jnp.maximum(jnp.minimum(i, nu[0] - 1), 0), 0)

    grid_spec = pltpu.PrefetchScalarGridSpec(
        num_scalar_prefetch=2,
        grid=(n_blk,),
        in_specs=[
            pl.BlockSpec((EXPERT_ROWS, D_MODEL), xmap),
            pl.BlockSpec((1, 1, D_MODEL, D_EXPERT), lambda i, be, nu: (layer, be[i], 0, 0)),
            pl.BlockSpec((1, 1, D_MODEL, D_EXPERT), lambda i, be, nu: (layer, be[i], 0, 0)),
            pl.BlockSpec((1, 1, D_EXPERT, D_MODEL), lambda i, be, nu: (layer, be[i], 0, 0)),
        ],
        out_specs=pl.BlockSpec((EXPERT_ROWS, D_MODEL), lambda i, be, nu: (i, 0)),
        scratch_shapes=[pltpu.VMEM((D_MODEL, 2 * D_EXPERT), BF16), pltpu.VMEM((D_EXPERT, D_MODEL), BF16)],
    )
    return pl.pallas_call(
        _expert_kernel,
        grid_spec=grid_spec,
        out_shape=jax.ShapeDtypeStruct(buf.shape, BF16),
        compiler_params=_cparams(("arbitrary",)),
        name="experts",
    )(blk_expert, n_used, buf, w_gate, w_up, w_down)


def _combine_kernel(len_ref, off_ref, dst_ref, h_ref, gate_ref, lpos_ref, y_ref, fn_ref, o_ref, yl, sem, *, final_norm):
    i = pl.program_id(0)
    nt = pl.num_programs(0)
    tm = h_ref.shape[0]
    slot = i % 2

    def tile_copies(t, s, op):
        def body(e, c):
            idx = t * N_EXPERTS + e
            lo, src = off_ref[idx], dst_ref[idx]
            _if_rows(len_ref[idx], lambda n: op(pltpu.make_async_copy(
                _rows(y_ref, src, n), _rows(yl.at[s], lo, n), sem.at[s])))
            return c

        lax.fori_loop(0, N_EXPERTS, body, 0)

    @pl.when(i == 0)
    def _():
        yl[...] = jnp.zeros_like(yl)
        tile_copies(0, 0, lambda cp: cp.start())

    @pl.when(i + 1 < nt)
    def _():
        tile_copies(i + 1, 1 - slot, lambda cp: cp.start())

    tile_copies(i, slot, lambda cp: cp.wait())

    rows = yl[slot]
    lpos = lpos_ref[...]
    col = lax.broadcasted_iota(jnp.int32, (tm, LOCAL_ROWS), 1)
    g = gate_ref[...]
    weights = jnp.where(col == lpos[:, 0:1], g[:, 0:1], jnp.where(col == lpos[:, 1:2], g[:, 1:2], 0.0)).astype(BF16)
    out = h_ref[...] + jnp.dot(weights, rows, preferred_element_type=F32)
    if final_norm:
        out = _rms(out, fn_ref[...])
    o_ref[...] = out


def _combine(h, y, seg_len, seg_off, seg_dst, lpos, gate, fn, tm, final_norm):
    t = h.shape[0]
    grid_spec = pltpu.PrefetchScalarGridSpec(
        num_scalar_prefetch=3,
        grid=(t // tm,),
        in_specs=[
            pl.BlockSpec((tm, D_MODEL), lambda i, *_: (i, 0)),
            pl.BlockSpec((tm, 2), lambda i, *_: (i, 0)),
            pl.BlockSpec((tm, 2), lambda i, *_: (i, 0)),
            pl.BlockSpec(memory_space=pl.ANY),
            pl.BlockSpec((1, D_MODEL), lambda i, *_: (0, 0)),
        ],
        out_specs=pl.BlockSpec((tm, D_MODEL), lambda i, *_: (i, 0)),
        scratch_shapes=[pltpu.VMEM((2, LOCAL_ROWS, D_MODEL), BF16), pltpu.SemaphoreType.DMA((2,))],
    )
    return pl.pallas_call(
        functools.partial(_combine_kernel, final_norm=final_norm),
        grid_spec=grid_spec,
        out_shape=jax.ShapeDtypeStruct((t, D_MODEL), F32),
        compiler_params=_cparams(("arbitrary",)),
        name="combine",
    )(seg_len, seg_off, seg_dst, h, gate.T, lpos.T, y, fn.reshape(1, D_MODEL))


def _moe(h, routing, g, w_gate, w_up, w_down, layer, fn, tm, final_norm):
    t = h.shape[0]
    nt = t // tm
    lpos, gate, meta, tot = routing
    n_blk = (2 * t + nt * N_EXPERTS * (SEG_ALIGN - 1)) // EXPERT_ROWS + N_EXPERTS
    total = tot[:, 0]
    region = (total + EXPERT_ROWS - 1) // EXPERT_ROWS * EXPERT_ROWS
    ends = jnp.cumsum(region)
    starts = ends - region
    seg_len = meta[:, :, 0].reshape(-1)
    seg_off = meta[:, :, 1].reshape(-1)
    seg_dst = (meta[:, :, 2] + starts[None, :]).reshape(-1)
    blk_start = jnp.arange(n_blk, dtype=jnp.int32) * EXPERT_ROWS
    blk_expert = jnp.minimum(jnp.sum(blk_start[:, None] >= ends[None, :], axis=1), N_EXPERTS - 1).astype(jnp.int32)
    n_used = (ends[-1:] // EXPERT_ROWS).astype(jnp.int32)
    buf = _dispatch(h, g, lpos, seg_len, seg_off, seg_dst, starts + total, region - total, n_used, n_blk, tm)
    y = _experts(buf, blk_expert, n_used, w_gate, w_up, w_down, layer)
    return _combine(h, y, seg_len, seg_off, seg_dst, lpos, gate, fn, tm, final_norm)


def _kv_kernel(h_ref, g_ref, wk_ref, wvt_ref, cos_ref, sa_ref, sb_ref, k_ref, vt_ref):
    hn = _rms(h_ref[...], g_ref[...]).astype(BF16)
    k = jnp.dot(hn, wk_ref[...], preferred_element_type=F32)
    cos, sa, sb = cos_ref[...], sa_ref[...], sb_ref[...]
    for c in range(D_MODEL // LANES):
        kc = k[:, c * LANES:(c + 1) * LANES]
        rot = kc * cos + pltpu.roll(kc, LANES - ROT_HALF, 1) * sa + pltpu.roll(kc, ROT_HALF, 1) * sb
        k_ref[0, :, c * LANES:(c + 1) * LANES] = rot.astype(BF16)
    vt = lax.dot_general(wvt_ref[...], hn, (((1,), (1,)), ((), ())), preferred_element_type=F32)
    for c in range(vt_ref.shape[1]):
        vt_ref[0, c] = vt[:, c * LANES:(c + 1) * LANES].astype(BF16)


def _kv(h, g, w_kv, bsz, lp):
    tpb = lp // TILE
    kblk = TILE // LANES
    pos = jnp.maximum(jnp.arange(lp) - PAD_ROWS, 0).astype(F32)
    inv_freq = ROPE_THETA ** (-jnp.arange(ROT_HALF, dtype=F32) * 2.0 / (2 * ROT_HALF))
    ang = pos[:, None] * inv_freq[None, :]
    r = jnp.arange(LANES) % HEAD_DIM
    cos_t = jnp.where(r < 2 * ROT_HALF, jnp.cos(ang)[:, r % ROT_HALF], 1.0)
    sin_t = jnp.sin(ang)[:, r % ROT_HALF]
    sa = jnp.where(r < ROT_HALF, -sin_t, 0.0)
    sb = jnp.where((r >= ROT_HALF) & (r < 2 * ROT_HALF), sin_t, 0.0)
    tab = pl.BlockSpec((TILE, LANES), lambda i: (i % tpb, 0))
    return pl.pallas_call(
        _kv_kernel,
        grid=(bsz * tpb,),
        in_specs=[
            pl.BlockSpec((TILE, D_MODEL), lambda i: (i, 0)),
            pl.BlockSpec((1, D_MODEL), lambda i: (0, 0)),
            pl.BlockSpec((D_MODEL, D_MODEL), lambda i: (0, 0)),
            pl.BlockSpec((D_MODEL, D_MODEL), lambda i: (0, 0)),
            tab, tab, tab,
        ],
        out_specs=[
            pl.BlockSpec((1, TILE, D_MODEL), lambda i: (i // tpb, i % tpb, 0)),
            pl.BlockSpec((1, kblk, D_MODEL, LANES), lambda i: (i // tpb, i % tpb, 0, 0)),
        ],
        out_shape=[
            jax.ShapeDtypeStruct((bsz, lp, D_MODEL), BF16),
            jax.ShapeDtypeStruct((bsz, lp // LANES, D_MODEL, LANES), BF16),
        ],
        compiler_params=_cparams(("arbitrary",)),
        name="kv_proj",
    )(h, g.reshape(1, D_MODEL), w_kv[:, :D_MODEL].astype(BF16), w_kv[:, D_MODEL:].T.astype(BF16), cos_t, sa, sb)


def _q_kernel(h_ref, g_ref, wqt_ref, cos_ref, sin_ref, qt_ref):
    hn = _rms(h_ref[...], g_ref[...]).astype(BF16)
    qt = lax.dot_general(wqt_ref[...], hn, (((1,), (1,)), ((), ())), preferred_element_type=F32)
    cos, sin = cos_ref[...], sin_ref[...]
    scale = HEAD_DIM ** -0.5 * math.log2(math.e)
    for c in range(D_MODEL // HEAD_DIM):
        lo = c * HEAD_DIM
        x1 = qt[lo:lo + ROT_HALF]
        x2 = qt[lo + ROT_HALF:lo + 2 * ROT_HALF]
        blk = jnp.concatenate([x1 * cos - x2 * sin, x2 * cos + x1 * sin, qt[lo + 2 * ROT_HALF:lo + HEAD_DIM]], axis=0)
        qt_ref[0, lo:lo + HEAD_DIM, :] = (blk * scale).astype(BF16)


def _real_row_map(tiles_real, tiles_padded):
    front = FRONT // TILE
    return lambda i: ((i // tiles_real) * tiles_padded + front + i % tiles_real, 0)


def _q_proj(h_pad, g, w_q, bsz, seq, lp):
    tiles_real = seq // TILE
    pos = (N_META + jnp.arange(seq)).astype(F32)
    inv_freq = ROPE_THETA ** (-jnp.arange(ROT_HALF, dtype=F32) * 2.0 / (2 * ROT_HALF))
    ang = inv_freq[:, None] * pos[None, :]
    tab = pl.BlockSpec((ROT_HALF, TILE), lambda i: (0, i % tiles_real))
    return pl.pallas_call(
        _q_kernel,
        grid=(bsz * tiles_real,),
        in_specs=[
            pl.BlockSpec((TILE, D_MODEL), _real_row_map(tiles_real, lp // TILE)),
            pl.BlockSpec((1, D_MODEL), lambda i: (0, 0)),
            pl.BlockSpec((D_MODEL, D_MODEL), lambda i: (0, 0)),
            tab, tab,
        ],
        out_specs=pl.BlockSpec((1, D_MODEL, TILE), lambda i: (i, 0, 0)),
        out_shape=jax.ShapeDtypeStruct((bsz * tiles_real, D_MODEL, TILE), BF16),
        compiler_params=_cparams(("arbitrary",)),
        name="q_proj",
    )(h_pad, g.reshape(1, D_MODEL), w_q.T.astype(BF16), jnp.cos(ang), jnp.sin(ang))


STEP_FULL = 0
STEP_EDGE = (1, 2)


def _attn_schedule(nq):
    per_k = ATT_TK // ATT_TQ
    steps = [(i, t, STEP_FULL if t < i // per_k else STEP_EDGE[i % per_k]) for i in range(nq) for t in range(i // per_k + 1)]
    return [list(col) for col in zip(*steps)]


def _attn_kernel(qi_ref, kt_ref, kind_ref, qt_ref, k_ref, vt_ref, lam_ref, g_ref, o_ref, m_sc, l_sc, acc_sc, sa_sc, sb_sc,
                 *, lam_init, n_steps, block_kinds):
    tq = ATT_TQ
    zero = jnp.zeros((HEAD_DIM, tq), BF16)
    n_sub = ATT_TK // ATT_SUB

    def q_maps(s):
        qt = qt_ref[0, qi_ref[s]]
        return (jnp.concatenate([qt[0:HEAD_DIM], zero], axis=0), jnp.concatenate([zero, qt[HEAD_DIM:]], axis=0))

    def update(n, sc, vtt, fresh=None):
        m_old, l_old, acc_old = m_sc[n], l_sc[n], acc_sc[n]
        if fresh is not None:
            m_old = jnp.where(fresh, NEG, m_old)
            l_old = jnp.where(fresh, 0.0, l_old)
            acc_old = jnp.where(fresh, 0.0, acc_old)
        m_new = jnp.maximum(m_old, jnp.max(sc, axis=0, keepdims=True))
        alpha = jnp.exp2(m_old - m_new)
        p = jnp.exp2(sc - m_new)
        l_sc[n] = alpha * l_old + jnp.sum(p, axis=0, keepdims=True)
        acc_sc[n] = alpha * acc_old + jnp.dot(vtt, p.astype(BF16), preferred_element_type=F32)
        m_sc[n] = m_new

    def values(row0, nkeys):
        blk0 = row0 // LANES
        return jnp.concatenate([vt_ref[0, blk0 + c] for c in range(nkeys // LANES)], axis=1)

    def key_row(s, u):
        return pl.multiple_of(FRONT + kt_ref[s] * ATT_TK + u * ATT_SUB, LANES)

    def scores_into(s, buf):
        qs = q_maps(s)
        for u in range(n_sub):
            kt = k_ref[0, pl.ds(key_row(s, u), ATT_SUB), :]
            for n in range(2):
                buf[u * 2 + n] = jnp.dot(kt, qs[n], preferred_element_type=F32)

    def consume(s, buf, kind):
        fresh = kt_ref[s] == 0
        edge = STEP_EDGE.index(kind) if kind in STEP_EDGE else None
        for u in range(n_sub) if edge is None else range(edge + 1):
            vtt = values(key_row(s, u), ATT_SUB)
            for n in range(2):
                sc = buf[u * 2 + n]
                if u == edge:
                    r = lax.broadcasted_iota(jnp.int32, (ATT_SUB, tq), 0) // CHUNK
                    c = lax.broadcasted_iota(jnp.int32, (ATT_SUB, tq), 1) // CHUNK
                    sc = jnp.where(r <= c, sc, NEG)
                update(n, sc, vtt, fresh if u == 0 else None)

    def finish(s):
        qs = q_maps(s)
        kt = k_ref[0, FRONT - N_META:FRONT, :]
        vtt = vt_ref[0, FRONT // LANES - 1][:, LANES - N_META:]
        for n in range(2):
            update(n, jnp.dot(kt, qs[n], preferred_element_type=F32), vtt)
        lp = lam_ref[...]
        lam = (jnp.exp(jnp.sum(lp[0:1] * lp[1:2], axis=1, keepdims=True))
               - jnp.exp(jnp.sum(lp[2:3] * lp[3:4], axis=1, keepdims=True)) + lam_init)
        o = acc_sc[0] / l_sc[0] - lam * (acc_sc[1] / l_sc[1])
        o = o * lax.rsqrt(jnp.mean(o * o, axis=0, keepdims=True) + EPS) * g_ref[...] * (1.0 - lam_init)
        o_ref[pl.ds(pl.multiple_of(qi_ref[s] * tq, tq), tq), :] = o.T.astype(BF16)

    def block(p, kind_a, kind_b):
        s_a, s_b = 2 * p, 2 * p + 1
        scores_into(s_b, sb_sc)
        consume(s_a, sa_sc, kind_a)
        if kind_a in STEP_EDGE:
            finish(s_a)
        scores_into(jnp.minimum(s_b + 1, n_steps - 1), sa_sc)
        consume(s_b, sb_sc, kind_b)
        if kind_b in STEP_EDGE:
            finish(s_b)

    scores_into(0, sa_sc)

    def body(p, c):
        kind_a, kind_b = kind_ref[2 * p], kind_ref[2 * p + 1]
        for va, vb in block_kinds:
            pl.when(jnp.logical_and(kind_a == va, kind_b == vb))(functools.partial(block, p, va, vb))
        return c

    lax.fori_loop(0, n_steps // 2, body, 0)


def _attention(qt, k, vt4, lam_p, subln_g, bsz, seq, lam_init):
    nq = seq // ATT_TQ
    lp = k.shape[1]
    q_idx, k_idx, kinds = _attn_schedule(nq)
    n_steps = len(kinds)
    assert n_steps % 2 == 0
    block_kinds = sorted(set(zip(kinds[0::2], kinds[1::2])))
    grid_spec = pltpu.PrefetchScalarGridSpec(
        num_scalar_prefetch=3,
        grid=(bsz, N_HEADS),
        in_specs=[
            pl.BlockSpec((1, nq, 2 * HEAD_DIM, ATT_TQ), lambda b, h, *_: (b, 0, h, 0)),
            pl.BlockSpec((1, lp, 2 * HEAD_DIM), lambda b, h, *_: (b, 0, h)),
            pl.BlockSpec((1, lp // LANES, 2 * HEAD_DIM, LANES), lambda b, h, *_: (b, 0, h, 0)),
            pl.BlockSpec((4, HEAD_DIM), lambda b, h, *_: (0, 0)),
            pl.BlockSpec((2 * HEAD_DIM, 1), lambda b, h, *_: (0, 0)),
        ],
        out_specs=pl.BlockSpec((seq, 2 * HEAD_DIM), lambda b, h, *_: (b, h)),
        scratch_shapes=[pltpu.VMEM((2, 1, ATT_TQ), F32), pltpu.VMEM((2, 1, ATT_TQ), F32),
                        pltpu.VMEM((2, 2 * HEAD_DIM, ATT_TQ), F32),
                        pltpu.VMEM((2 * ATT_TK // ATT_SUB, ATT_SUB, ATT_TQ), F32),
                        pltpu.VMEM((2 * ATT_TK // ATT_SUB, ATT_SUB, ATT_TQ), F32)],
    )
    return pl.pallas_call(
        functools.partial(_attn_kernel, lam_init=lam_init, n_steps=n_steps, block_kinds=block_kinds),
        grid_spec=grid_spec,
        out_shape=jax.ShapeDtypeStruct((bsz * seq, D_MODEL), BF16),
        compiler_params=_cparams(("arbitrary", "arbitrary")),
        name="diff_attention",
    )(*(jnp.asarray(col, jnp.int32) for col in (q_idx, k_idx, kinds)),
      qt.reshape(bsz, nq, D_MODEL, ATT_TQ), k, vt4, lam_p, subln_g.reshape(2 * HEAD_DIM, 1))


def _oproj_kernel(h_ref, o_ref, w_ref, rg_ref, wrt_ref, br_ref, tri_ref, out_ref, lpos_ref, gate_ref, meta_ref, tot_ref,
                  seg_sc):
    h = h_ref[...] + jnp.dot(o_ref[...], w_ref[...], preferred_element_type=F32)
    out_ref[...] = h
    _route(h, rg_ref, wrt_ref, br_ref, tri_ref, lpos_ref, gate_ref, meta_ref, tot_ref, seg_sc)


def _o_proj(h_pad, o, w_out, router, bsz, seq, lp):
    tiles_real = seq // TILE
    r_args, r_specs = _router_operands(*router, TILE)
    ro_specs, ro_shapes, r_scratch = _router_outputs(bsz * seq, TILE)
    return pl.pallas_call(
        _oproj_kernel,
        grid=(bsz * tiles_real,),
        in_specs=[
            pl.BlockSpec((TILE, D_MODEL), _real_row_map(tiles_real, lp // TILE)),
            pl.BlockSpec((TILE, D_MODEL), lambda i: (i, 0)),
            pl.BlockSpec((D_MODEL, D_MODEL), lambda i: (0, 0)),
        ] + r_specs,
        out_specs=[pl.BlockSpec((TILE, D_MODEL), lambda i: (i, 0))] + ro_specs,
        out_shape=[jax.ShapeDtypeStruct((bsz * seq, D_MODEL), F32)] + ro_shapes,
        scratch_shapes=[r_scratch],
        compiler_params=_cparams(("arbitrary",)),
        name="o_proj",
    )(h_pad, o, w_out.astype(BF16), *r_args)


def kernel(x, meta_tokens, a_norm, a_w_in, a_conv, a_w_out, kv_norm, w_kv, b_norm, b_w_q, b_lambda, b_subln, b_w_out, ffn_norm, r_group, r_group_b, r_expert, r_expert_b, e_gate, e_up, e_down, final_norm):
    bsz, seq, d = x.shape
    assert d == D_MODEL and a_norm.shape[0] == 1 and b_norm.shape[0] == 1
    lp = FRONT + seq
    assert seq % ATT_TK == 0 and ATT_TK == 2 * ATT_TQ and ATT_TQ == TILE and ATT_TK % ATT_SUB == 0
    front = jnp.concatenate([jnp.zeros((PAD_ROWS, d), x.dtype), meta_tokens.astype(x.dtype)], axis=0)
    routers = [(ffn_norm[j], r_group[j], r_group_b[j], r_expert[j], r_expert_b[j]) for j in range(2)]
    h, *routing = _mixer(x.reshape(bsz * seq, d), front, a_norm[0], a_w_in[0], a_conv[0], a_w_out[0], routers[0],
                         bsz, lp // TILE)
    h = _moe(h, routing, ffn_norm[0], e_gate, e_up, e_down, 0, final_norm, TILE, False)

    k, vt4 = _kv(h, kv_norm, w_kv, bsz, lp)
    qt = _q_proj(h, b_norm[0], b_w_q[0], bsz, seq, lp)
    lam_init = 0.8 - 0.6 * math.exp(-0.3 * a_norm.shape[0])
    o = _attention(qt, k, vt4, b_lambda[0], b_subln[0], bsz, seq, lam_init)
    h, *routing = _o_proj(h, o, b_w_out[0], routers[1], bsz, seq, lp)
    h = _moe(h, routing, ffn_norm[1], e_gate, e_up, e_down, 1, final_norm, TILE, True)
    return h.reshape(bsz, seq, d)
```

```python
import functools
import math

import jax
import jax.numpy as jnp
from jax import lax
from jax.experimental import pallas as pl
from jax.experimental.pallas import tpu as pltpu

D_MODEL = 1024
CHUNK = 64
N_META = 16
HEAD_DIM = 64
N_HEADS = 8
ROT_HALF = 8
ROPE_THETA = 500000.0
N_GROUPS = 4
EXPERTS_PER_GROUP = 8
N_EXPERTS = 32
D_EXPERT = 512
EPS = 1e-6

LANES = 128
SUBLANES = 8
TILE = 512
FRONT = TILE
PAD_ROWS = FRONT - N_META
EXPERT_ROWS = 512
ATT_TQ = 512
ATT_TK = 1024
ATT_SUB = 512
SEG_ALIGN = 16
STAGE_SLOTS = 2
LOCAL_ROWS = 1536
VMEM_LIMIT = 56 * 1024 * 1024
NEG = float(jnp.finfo(jnp.float32).min)

BF16 = jnp.bfloat16
F32 = jnp.float32


def _rms(x, g):
    return x * lax.rsqrt(jnp.mean(x * x, axis=-1, keepdims=True) + EPS) * g


def _cparams(sem):
    return pltpu.CompilerParams(dimension_semantics=sem, vmem_limit_bytes=VMEM_LIMIT)


def _mixer_kernel(x_ref, front_ref, g_ref, win_ref, cw_ref, wout_ref, rg_ref, wrt_ref, br_ref, tri_ref,
                  o_ref, lpos_ref, gate_ref, meta_ref, tot_ref, cu_ref, seg_sc, *, tiles_per_batch):
    i = pl.program_id(0)
    tm = x_ref.shape[0]
    _route_start(seg_sc)

    @pl.when(i % tiles_per_batch == 0)
    def _():
        cu_ref[0:SUBLANES, :] = jnp.zeros((SUBLANES, D_MODEL), F32)

    x = jnp.where(i % tiles_per_batch == 0, front_ref[...], x_ref[...])
    hn = _rms(x, g_ref[...]).astype(BF16)
    proj = jnp.dot(hn, win_ref[...], preferred_element_type=F32)
    b_gate = proj[:, :D_MODEL]
    cu = proj[:, D_MODEL:2 * D_MODEL] * proj[:, 2 * D_MODEL:]
    cu_ref[SUBLANES:SUBLANES + tm, :] = cu
    cw = cw_ref[...]
    z = (cw[2:3] * cu + cw[1:2] * cu_ref[SUBLANES - 1:SUBLANES - 1 + tm, :]
         + cw[0:1] * cu_ref[SUBLANES - 2:SUBLANES - 2 + tm, :])
    cu_ref[0:SUBLANES, :] = cu_ref[tm:tm + SUBLANES, :]
    y = jnp.dot((b_gate * z).astype(BF16), wout_ref[...], preferred_element_type=F32)
    h = x + y
    o_ref[...] = h
    _route(h, rg_ref, wrt_ref, br_ref, tri_ref, lpos_ref, gate_ref, meta_ref, tot_ref, seg_sc)


def _mixer(x, front, g, w_in, conv_w, w_out, router, bsz, tiles_per_batch):
    t = bsz * tiles_per_batch * TILE
    r_args, r_specs = _router_operands(*router, TILE)
    ro_specs, ro_shapes, r_scratch = _router_outputs(t, TILE)
    tiles_real = tiles_per_batch - FRONT // TILE

    def x_map(i):
        return ((i // tiles_per_batch) * tiles_real + jnp.maximum(i % tiles_per_batch - FRONT // TILE, 0), 0)

    return pl.pallas_call(
        functools.partial(_mixer_kernel, tiles_per_batch=tiles_per_batch),
        grid=(t // TILE,),
        in_specs=[
            pl.BlockSpec((TILE, D_MODEL), x_map),
            pl.BlockSpec((TILE, D_MODEL), lambda i: (0, 0)),
            pl.BlockSpec((1, D_MODEL), lambda i: (0, 0)),
            pl.BlockSpec((D_MODEL, 3 * D_MODEL), lambda i: (0, 0)),
            pl.BlockSpec((3, D_MODEL), lambda i: (0, 0)),
            pl.BlockSpec((D_MODEL, D_MODEL), lambda i: (0, 0)),
        ] + r_specs,
        out_specs=[pl.BlockSpec((TILE, D_MODEL), lambda i: (i, 0))] + ro_specs,
        out_shape=[jax.ShapeDtypeStruct((t, D_MODEL), F32)] + ro_shapes,
        scratch_shapes=[pltpu.VMEM((TILE + 2 * SUBLANES, D_MODEL), F32), r_scratch],
        compiler_params=_cparams(("arbitrary",)),
        name="mixer_a",
    )(x, front, g.reshape(1, D_MODEL), w_in.astype(BF16), conv_w, w_out.astype(BF16), *r_args)


ROUTER_ROWS = SUBLANES + N_EXPERTS


def _round_up(v, m):
    return jnp.floor((v + (m - 1)) * (1.0 / m)) * m


def _route_start(seg_sc):
    @pl.when(pl.program_id(0) == 0)
    def _():
        seg_sc[...] = jnp.zeros_like(seg_sc)


def _route(h, g_ref, wrt_ref, br_ref, tri_ref, lpos_ref, gate_ref, meta_ref, tot_ref, seg_sc):
    i = pl.program_id(0)
    tm = h.shape[0]
    xn = _rms(h, g_ref[...])
    lt = lax.dot_general(wrt_ref[...], xn, (((1,), (1,)), ((), ())),
                         precision=lax.Precision.HIGHEST, preferred_element_type=F32) + br_ref[...]
    lg = lt[0:N_GROUPS]
    gmax = jnp.max(lg, axis=0, keepdims=True)
    iota_g = lax.broadcasted_iota(jnp.int32, lg.shape, 0)
    grp = jnp.min(jnp.where(lg == gmax, iota_g, N_GROUPS), axis=0, keepdims=True)
    p_grp = 1.0 / jnp.sum(jnp.exp(lg - gmax), axis=0, keepdims=True)
    le = lt[SUBLANES:SUBLANES + EXPERTS_PER_GROUP]
    for g in range(1, N_GROUPS):
        lo = SUBLANES + g * EXPERTS_PER_GROUP
        le = jnp.where(grp == g, lt[lo:lo + EXPERTS_PER_GROUP], le)
    iota_e = lax.broadcasted_iota(jnp.int32, le.shape, 0)
    v1 = jnp.max(le, axis=0, keepdims=True)
    i1 = jnp.min(jnp.where(le == v1, iota_e, EXPERTS_PER_GROUP), axis=0, keepdims=True)
    le2 = jnp.where(iota_e == i1, -jnp.inf, le)
    v2 = jnp.max(le2, axis=0, keepdims=True)
    i2 = jnp.min(jnp.where(le2 == v2, iota_e, EXPERTS_PER_GROUP), axis=0, keepdims=True)
    e2 = jnp.exp(v2 - v1)
    den = 1.0 + e2
    gate_ref[...] = jnp.concatenate([p_grp * (1.0 / den), p_grp * (e2 / den)], axis=0)
    experts = (grp * EXPERTS_PER_GROUP + i1, grp * EXPERTS_PER_GROUP + i2)

    iota_x = lax.broadcasted_iota(jnp.int32, (N_EXPERTS, tm), 0)
    hits = [iota_x == e_k for e_k in experts]
    hits_f = [jnp.where(hit, 1.0, 0.0) for hit in hits]
    hits_b = [hf.astype(BF16) for hf in hits_f]
    excl = [jnp.dot(hb, tri_ref[...], preferred_element_type=F32) for hb in hits_b]
    tot_col = [jnp.sum(hf, axis=1, keepdims=True) for hf in hits_f]
    ones = jnp.ones((SUBLANES, tm), BF16)
    n_lane = sum(lax.dot_general(ones, hb, (((1,), (1,)), ((), ())), preferred_element_type=F32) for hb in hits_b)
    seg_len_lane = _round_up(n_lane[0:1], SEG_ALIGN)
    seg_len = _round_up(tot_col[0] + tot_col[1], SEG_ALIGN)
    before = (lax.broadcasted_iota(jnp.int32, (N_EXPERTS, N_EXPERTS), 1)
              < lax.broadcasted_iota(jnp.int32, (N_EXPERTS, N_EXPERTS), 0))
    local_off = jnp.sum(jnp.where(before, seg_len_lane, 0.0), axis=1, keepdims=True)

    lpos_a = jnp.sum(jnp.where(hits[0], excl[0] + local_off, 0.0), axis=0, keepdims=True).astype(jnp.int32)
    lpos_b = jnp.sum(jnp.where(hits[1], excl[1] + tot_col[0] + local_off, 0.0), axis=0, keepdims=True).astype(jnp.int32)
    lpos_ref[...] = jnp.concatenate([lpos_a, lpos_b], axis=0)

    seg_before = seg_sc[...]
    seg_sc[...] = seg_before + seg_len
    lane = lax.broadcasted_iota(jnp.int32, (N_EXPERTS, LANES), 1)
    meta = jnp.where(lane == 0, seg_len, jnp.where(lane == 1, local_off, jnp.where(lane == 2, seg_before, 0.0)))
    meta_ref[0] = meta.astype(jnp.int32)
    tot_ref[...] = jnp.broadcast_to(seg_before + seg_len, tot_ref.shape).astype(jnp.int32)


def _router_operands(g, w_rg, b_rg, w_re, b_re, tm):
    wrt = jnp.zeros((ROUTER_ROWS, D_MODEL), F32)
    wrt = wrt.at[0:N_GROUPS].set(w_rg.T).at[SUBLANES:].set(w_re.T)
    br = jnp.zeros((ROUTER_ROWS, 1), F32)
    br = br.at[0:N_GROUPS, 0].set(b_rg).at[SUBLANES:, 0].set(b_re)
    tri = (lax.broadcasted_iota(jnp.int32, (tm, tm), 0) < lax.broadcasted_iota(jnp.int32, (tm, tm), 1)).astype(BF16)
    specs = [
        pl.BlockSpec((1, D_MODEL), lambda i: (0, 0)),
        pl.BlockSpec((ROUTER_ROWS, D_MODEL), lambda i: (0, 0)),
        pl.BlockSpec((ROUTER_ROWS, 1), lambda i: (0, 0)),
        pl.BlockSpec((tm, tm), lambda i: (0, 0)),
    ]
    return [g.reshape(1, D_MODEL), wrt, br, tri], specs


def _router_outputs(t, tm):
    nt = t // tm
    specs = [
        pl.BlockSpec((2, tm), lambda i: (0, i)),
        pl.BlockSpec((2, tm), lambda i: (0, i)),
        pl.BlockSpec((1, N_EXPERTS, LANES), lambda i: (i, 0, 0)),
        pl.BlockSpec((N_EXPERTS, LANES), lambda i: (0, 0)),
    ]
    shapes = [
        jax.ShapeDtypeStruct((2, t), jnp.int32),
        jax.ShapeDtypeStruct((2, t), F32),
        jax.ShapeDtypeStruct((nt, N_EXPERTS, LANES), jnp.int32),
        jax.ShapeDtypeStruct((N_EXPERTS, LANES), jnp.int32),
    ]
    return specs, shapes, pltpu.VMEM((N_EXPERTS, 1), F32)


def _if_rows(n, fn):
    pl.when(n > 0)(lambda: fn(pl.multiple_of(n, SEG_ALIGN)))


def _rows(ref, start, size):
    return ref.at[pl.ds(pl.multiple_of(start, SEG_ALIGN), size), :]


def _dispatch_kernel(len_ref, off_ref, dst_ref, zs_ref, zn_ref, nu_ref, h_ref, g_ref, lpos_ref, buf_ref, stage, zero_sc,
                     sem_out, sem_fill, *, n_blk):
    t = pl.program_id(0)
    nt = pl.num_programs(0)
    tm = h_ref.shape[0]
    slot = t % STAGE_SLOTS

    def tile_out(t, s, op):
        def body(e, c):
            idx = t * N_EXPERTS + e
            lo, dst = off_ref[idx], dst_ref[idx]
            _if_rows(len_ref[idx], lambda n: op(pltpu.make_async_copy(
                _rows(stage.at[s], lo, n), _rows(buf_ref, dst, n), sem_out.at[s])))
            return c

        lax.fori_loop(0, N_EXPERTS, body, 0)

    def fill_copies(op):
        def tails(e, c):
            _if_rows(zn_ref[e], lambda n: op(pltpu.make_async_copy(
                zero_sc.at[pl.ds(0, n), :], _rows(buf_ref, zs_ref[e], n), sem_fill)))
            return c

        lax.fori_loop(0, N_EXPERTS, tails, 0)

        def blocks(b, c):
            op(pltpu.make_async_copy(zero_sc, _rows(buf_ref, b * EXPERT_ROWS, EXPERT_ROWS), sem_fill))
            return c

        lax.fori_loop(nu_ref[0], n_blk, blocks, 0)

    start = lambda cp: cp.start()
    wait = lambda cp: cp.wait()

    @pl.when(t == 0)
    def _():
        zero_sc[...] = jnp.zeros_like(zero_sc)
        fill_copies(start)

    @pl.when(t >= STAGE_SLOTS)
    def _():
        tile_out(t - STAGE_SLOTS, slot, wait)

    xn = _rms(h_ref[...], g_ref[...]).astype(BF16)
    lpos = lpos_ref[...]
    row = lax.broadcasted_iota(jnp.int32, (LOCAL_ROWS, tm), 0)
    perm = jnp.where(row == lpos[0:1], 1.0, jnp.where(row == lpos[1:2], 1.0, 0.0)).astype(BF16)
    stage[slot] = jnp.dot(perm, xn, preferred_element_type=F32).astype(BF16)
    tile_out(t, slot, start)

    @pl.when(t == nt - 1)
    def _():
        for back in range(STAGE_SLOTS - 1, -1, -1):
            @pl.when(t >= back)
            def _(back=back):
                tile_out(t - back, (t - back) % STAGE_SLOTS, wait)

        fill_copies(wait)


def _dispatch(h, g, lpos, seg_len, seg_off, seg_dst, tail_start, tail_len, n_used, n_blk, tm):
    t = h.shape[0]
    grid_spec = pltpu.PrefetchScalarGridSpec(
        num_scalar_prefetch=6,
        grid=(t // tm,),
        in_specs=[
            pl.BlockSpec((tm, D_MODEL), lambda i, *_: (i, 0)),
            pl.BlockSpec((1, D_MODEL), lambda i, *_: (0, 0)),
            pl.BlockSpec((2, tm), lambda i, *_: (0, i)),
        ],
        out_specs=pl.BlockSpec(memory_space=pl.ANY),
        scratch_shapes=[pltpu.VMEM((STAGE_SLOTS, LOCAL_ROWS, D_MODEL), BF16), pltpu.VMEM((EXPERT_ROWS, D_MODEL), BF16),
                        pltpu.SemaphoreType.DMA((STAGE_SLOTS,)), pltpu.SemaphoreType.DMA],
    )
    return pl.pallas_call(
        functools.partial(_dispatch_kernel, n_blk=n_blk),
        grid_spec=grid_spec,
        out_shape=jax.ShapeDtypeStruct((n_blk * EXPERT_ROWS, D_MODEL), BF16),
        compiler_params=_cparams(("arbitrary",)),
        name="dispatch",
    )(seg_len, seg_off, seg_dst, tail_start, tail_len, n_used, h, g.reshape(1, D_MODEL), lpos)


def _expert_kernel(be_ref, nu_ref, rows_ref, x_ref, wg_ref, wu_ref, wd_ref, y_ref, wgu_sc, wd_sc):
    del nu_ref
    i = pl.program_id(0)
    half = EXPERT_ROWS // 2
    rows = rows_ref[i]
    e = be_ref[i]
    prev = be_ref[jnp.maximum(i - 1, 0)]

    @pl.when(jnp.logical_or(i == 0, e != prev))
    def _():
        wgu_sc[:, :D_EXPERT] = wg_ref[0, 0].astype(BF16)
        wgu_sc[:, D_EXPERT:] = wu_ref[0, 0].astype(BF16)
        wd_sc[...] = wd_ref[0, 0].astype(BF16)

    def ffn(x):
        gu = jnp.dot(x, wgu_sc[...], preferred_element_type=F32)
        a = jax.nn.silu(gu[:, :D_EXPERT]) * gu[:, D_EXPERT:]
        return jnp.dot(a.astype(BF16), wd_sc[...], preferred_element_type=F32).astype(BF16)

    @pl.when(rows > half)
    def _():
        y_ref[...] = ffn(x_ref[...])

    @pl.when(jnp.logical_and(rows > 0, rows <= half))
    def _():
        y_ref[0:half, :] = ffn(x_ref[0:half, :])
        y_ref[half:, :] = jnp.zeros((half, D_MODEL), BF16)

    @pl.when(rows == 0)
    def _():
        y_ref[...] = jnp.zeros_like(y_ref)


def _experts(buf, blk_expert, n_used, blk_rows, w_gate, w_up, w_down, layer):
    n_blk = blk_expert.shape[0]

    def xmap(i, be, nu, rows):
        return (jnp.maximum(jnp.minimum(i, nu[0] - 1), 0), 0)

    def wmap(i, be, nu, rows):
        return (layer, be[i], 0, 0)

    grid_spec = pltpu.PrefetchScalarGridSpec(
        num_scalar_prefetch=3,
        grid=(n_blk,),
        in_specs=[
            pl.BlockSpec((EXPERT_ROWS, D_MODEL), xmap),
            pl.BlockSpec((1, 1, D_MODEL, D_EXPERT), wmap),
            pl.BlockSpec((1, 1, D_MODEL, D_EXPERT), wmap),
            pl.BlockSpec((1, 1, D_EXPERT, D_MODEL), wmap),
        ],
        out_specs=pl.BlockSpec((EXPERT_ROWS, D_MODEL), lambda i, be, nu, rows: (i, 0)),
        scratch_shapes=[pltpu.VMEM((D_MODEL, 2 * D_EXPERT), BF16), pltpu.VMEM((D_EXPERT, D_MODEL), BF16)],
    )
    return pl.pallas_call(
        _expert_kernel,
        grid_spec=grid_spec,
        out_shape=jax.ShapeDtypeStruct(buf.shape, BF16),
        compiler_params=_cparams(("arbitrary",)),
        name="experts",
    )(blk_expert, n_used, blk_rows, buf, w_gate, w_up, w_down)


def _combine_kernel(len_ref, off_ref, dst_ref, h_ref, gate_ref, lpos_ref, y_ref, fn_ref, o_ref, yl, sem, *, final_norm):
    i = pl.program_id(0)
    nt = pl.num_programs(0)
    tm = h_ref.shape[0]
    slot = i % 2

    def tile_copies(t, s, op):
        def body(e, c):
            idx = t * N_EXPERTS + e
            lo, src = off_ref[idx], dst_ref[idx]
            _if_rows(len_ref[idx], lambda n: op(pltpu.make_async_copy(
                _rows(y_ref, src, n), _rows(yl.at[s], lo, n), sem.at[s])))
            return c

        lax.fori_loop(0, N_EXPERTS, body, 0)

    @pl.when(i == 0)
    def _():
        yl[...] = jnp.zeros_like(yl)
        tile_copies(0, 0, lambda cp: cp.start())

    @pl.when(i + 1 < nt)
    def _():
        tile_copies(i + 1, 1 - slot, lambda cp: cp.start())

    tile_copies(i, slot, lambda cp: cp.wait())

    rows = yl[slot]
    lpos = lpos_ref[...]
    col = lax.broadcasted_iota(jnp.int32, (tm, LOCAL_ROWS), 1)
    g = gate_ref[...]
    weights = jnp.where(col == lpos[:, 0:1], g[:, 0:1], jnp.where(col == lpos[:, 1:2], g[:, 1:2], 0.0)).astype(BF16)
    out = h_ref[...] + jnp.dot(weights, rows, preferred_element_type=F32)
    if final_norm:
        out = _rms(out, fn_ref[...])
    o_ref[...] = out


def _combine(h, y, seg_len, seg_off, seg_dst, lpos, gate, fn, tm, final_norm):
    t = h.shape[0]
    grid_spec = pltpu.PrefetchScalarGridSpec(
        num_scalar_prefetch=3,
        grid=(t // tm,),
        in_specs=[
            pl.BlockSpec((tm, D_MODEL), lambda i, *_: (i, 0)),
            pl.BlockSpec((tm, 2), lambda i, *_: (i, 0)),
            pl.BlockSpec((tm, 2), lambda i, *_: (i, 0)),
            pl.BlockSpec(memory_space=pl.ANY),
            pl.BlockSpec((1, D_MODEL), lambda i, *_: (0, 0)),
        ],
        out_specs=pl.BlockSpec((tm, D_MODEL), lambda i, *_: (i, 0)),
        scratch_shapes=[pltpu.VMEM((2, LOCAL_ROWS, D_MODEL), BF16), pltpu.SemaphoreType.DMA((2,))],
    )
    return pl.pallas_call(
        functools.partial(_combine_kernel, final_norm=final_norm),
        grid_spec=grid_spec,
        out_shape=jax.ShapeDtypeStruct((t, D_MODEL), F32),
        compiler_params=_cparams(("arbitrary",)),
        name="combine",
    )(seg_len, seg_off, seg_dst, h, gate.T, lpos.T, y, fn.reshape(1, D_MODEL))


def _moe(h, routing, g, w_gate, w_up, w_down, layer, fn, tm, final_norm):
    t = h.shape[0]
    nt = t // tm
    lpos, gate, meta, tot = routing
    n_blk = (2 * t + nt * N_EXPERTS * (SEG_ALIGN - 1)) // EXPERT_ROWS + N_EXPERTS
    total = tot[:, 0]
    region = (total + EXPERT_ROWS - 1) // EXPERT_ROWS * EXPERT_ROWS
    ends = jnp.cumsum(region)
    starts = ends - region
    seg_len = meta[:, :, 0].reshape(-1)
    seg_off = meta[:, :, 1].reshape(-1)
    seg_dst = (meta[:, :, 2] + starts[None, :]).reshape(-1)
    blk_start = jnp.arange(n_blk, dtype=jnp.int32) * EXPERT_ROWS
    blk_expert = jnp.minimum(jnp.sum(blk_start[:, None] >= ends[None, :], axis=1), N_EXPERTS - 1).astype(jnp.int32)
    n_used = (ends[-1:] // EXPERT_ROWS).astype(jnp.int32)
    blk_rows = jnp.clip((starts + total)[blk_expert] - blk_start, 0, EXPERT_ROWS).astype(jnp.int32)
    buf = _dispatch(h, g, lpos, seg_len, seg_off, seg_dst, starts + total, region - total, n_used, n_blk, tm)
    y = _experts(buf, blk_expert, n_used, blk_rows, w_gate, w_up, w_down, layer)
    return _combine(h, y, seg_len, seg_off, seg_dst, lpos, gate, fn, tm, final_norm)


def _kv_kernel(h_ref, g_ref, wk_ref, wvt_ref, cos_ref, sa_ref, sb_ref, k_ref, vt_ref):
    hn = _rms(h_ref[...], g_ref[...]).astype(BF16)
    k = jnp.dot(hn, wk_ref[...], preferred_element_type=F32)
    cos, sa, sb = cos_ref[...], sa_ref[...], sb_ref[...]
    for c in range(D_MODEL // LANES):
        kc = k[:, c * LANES:(c + 1) * LANES]
        rot = kc * cos + pltpu.roll(kc, LANES - ROT_HALF, 1) * sa + pltpu.roll(kc, ROT_HALF, 1) * sb
        k_ref[0, :, c * LANES:(c + 1) * LANES] = rot.astype(BF16)
    vt = lax.dot_general(wvt_ref[...], hn, (((1,), (1,)), ((), ())), preferred_element_type=F32)
    for c in range(vt_ref.shape[1]):
        vt_ref[0, c] = vt[:, c * LANES:(c + 1) * LANES].astype(BF16)


def _kv(h, g, w_kv, bsz, lp):
    tpb = lp // TILE
    kblk = TILE // LANES
    pos = jnp.maximum(jnp.arange(lp) - PAD_ROWS, 0).astype(F32)
    inv_freq = ROPE_THETA ** (-jnp.arange(ROT_HALF, dtype=F32) * 2.0 / (2 * ROT_HALF))
    ang = pos[:, None] * inv_freq[None, :]
    r = jnp.arange(LANES) % HEAD_DIM
    cos_t = jnp.where(r < 2 * ROT_HALF, jnp.cos(ang)[:, r % ROT_HALF], 1.0)
    sin_t = jnp.sin(ang)[:, r % ROT_HALF]
    sa = jnp.where(r < ROT_HALF, -sin_t, 0.0)
    sb = jnp.where((r >= ROT_HALF) & (r < 2 * ROT_HALF), sin_t, 0.0)
    tab = pl.BlockSpec((TILE, LANES), lambda i: (i % tpb, 0))
    return pl.pallas_call(
        _kv_kernel,
        grid=(bsz * tpb,),
        in_specs=[
            pl.BlockSpec((TILE, D_MODEL), lambda i: (i, 0)),
            pl.BlockSpec((1, D_MODEL), lambda i: (0, 0)),
            pl.BlockSpec((D_MODEL, D_MODEL), lambda i: (0, 0)),
            pl.BlockSpec((D_MODEL, D_MODEL), lambda i: (0, 0)),
            tab, tab, tab,
        ],
        out_specs=[
            pl.BlockSpec((1, TILE, D_MODEL), lambda i: (i // tpb, i % tpb, 0)),
            pl.BlockSpec((1, kblk, D_MODEL, LANES), lambda i: (i // tpb, i % tpb, 0, 0)),
        ],
        out_shape=[
            jax.ShapeDtypeStruct((bsz, lp, D_MODEL), BF16),
            jax.ShapeDtypeStruct((bsz, lp // LANES, D_MODEL, LANES), BF16),
        ],
        compiler_params=_cparams(("arbitrary",)),
        name="kv_proj",
    )(h, g.reshape(1, D_MODEL), w_kv[:, :D_MODEL].astype(BF16), w_kv[:, D_MODEL:].T.astype(BF16), cos_t, sa, sb)


def _q_kernel(h_ref, g_ref, wqt_ref, cos_ref, sin_ref, qt_ref):
    hn = _rms(h_ref[...], g_ref[...]).astype(BF16)
    qt = lax.dot_general(wqt_ref[...], hn, (((1,), (1,)), ((), ())), preferred_element_type=F32)
    cos, sin = cos_ref[...], sin_ref[...]
    scale = HEAD_DIM ** -0.5 * math.log2(math.e)
    for c in range(D_MODEL // HEAD_DIM):
        lo = c * HEAD_DIM
        x1 = qt[lo:lo + ROT_HALF]
        x2 = qt[lo + ROT_HALF:lo + 2 * ROT_HALF]
        blk = jnp.concatenate([x1 * cos - x2 * sin, x2 * cos + x1 * sin, qt[lo + 2 * ROT_HALF:lo + HEAD_DIM]], axis=0)
        qt_ref[0, lo:lo + HEAD_DIM, :] = (blk * scale).astype(BF16)


def _real_row_map(tiles_real, tiles_padded):
    front = FRONT // TILE
    return lambda i: ((i // tiles_real) * tiles_padded + front + i % tiles_real, 0)


def _q_proj(h_pad, g, w_q, bsz, seq, lp):
    tiles_real = seq // TILE
    pos = (N_META + jnp.arange(seq)).astype(F32)
    inv_freq = ROPE_THETA ** (-jnp.arange(ROT_HALF, dtype=F32) * 2.0 / (2 * ROT_HALF))
    ang = inv_freq[:, None] * pos[None, :]
    tab = pl.BlockSpec((ROT_HALF, TILE), lambda i: (0, i % tiles_real))
    return pl.pallas_call(
        _q_kernel,
        grid=(bsz * tiles_real,),
        in_specs=[
            pl.BlockSpec((TILE, D_MODEL), _real_row_map(tiles_real, lp // TILE)),
            pl.BlockSpec((1, D_MODEL), lambda i: (0, 0)),
            pl.BlockSpec((D_MODEL, D_MODEL), lambda i: (0, 0)),
            tab, tab,
        ],
        out_specs=pl.BlockSpec((1, D_MODEL, TILE), lambda i: (i, 0, 0)),
        out_shape=jax.ShapeDtypeStruct((bsz * tiles_real, D_MODEL, TILE), BF16),
        compiler_params=_cparams(("arbitrary",)),
        name="q_proj",
    )(h_pad, g.reshape(1, D_MODEL), w_q.T.astype(BF16), jnp.cos(ang), jnp.sin(ang))


STEP_FULL = 0
STEP_EDGE = (1, 2)


def _attn_schedule(nq):
    per_k = ATT_TK // ATT_TQ
    steps = [(i, t, STEP_FULL if t < i // per_k else STEP_EDGE[i % per_k]) for i in range(nq) for t in range(i // per_k + 1)]
    return [list(col) for col in zip(*steps)]


def _attn_kernel(qi_ref, kt_ref, kind_ref, qt_ref, k_ref, vt_ref, lam_ref, g_ref, o_ref, m_sc, l_sc, acc_sc, sa_sc, sb_sc,
                 *, lam_init, n_steps, block_kinds):
    tq = ATT_TQ
    zero = jnp.zeros((HEAD_DIM, tq), BF16)
    n_sub = ATT_TK // ATT_SUB

    def q_maps(s):
        qt = qt_ref[0, qi_ref[s]]
        return (jnp.concatenate([qt[0:HEAD_DIM], zero], axis=0), jnp.concatenate([zero, qt[HEAD_DIM:]], axis=0))

    def update(n, sc, vtt, fresh=None):
        m_old, l_old, acc_old = m_sc[n], l_sc[n], acc_sc[n]
        if fresh is not None:
            m_old = jnp.where(fresh, NEG, m_old)
            l_old = jnp.where(fresh, 0.0, l_old)
            acc_old = jnp.where(fresh, 0.0, acc_old)
        m_new = jnp.maximum(m_old, jnp.max(sc, axis=0, keepdims=True))
        alpha = jnp.exp2(m_old - m_new)
        p = jnp.exp2(sc - m_new)
        l_sc[n] = alpha * l_old + jnp.sum(p, axis=0, keepdims=True)
        acc_sc[n] = alpha * acc_old + jnp.dot(vtt, p.astype(BF16), preferred_element_type=F32)
        m_sc[n] = m_new

    def values(row0, nkeys):
        blk0 = row0 // LANES
        return jnp.concatenate([vt_ref[0, blk0 + c] for c in range(nkeys // LANES)], axis=1)

    def key_row(s, u):
        return pl.multiple_of(FRONT + kt_ref[s] * ATT_TK + u * ATT_SUB, LANES)

    def scores_into(s, buf):
        qs = q_maps(s)
        for u in range(n_sub):
            kt = k_ref[0, pl.ds(key_row(s, u), ATT_SUB), :]
            for n in range(2):
                buf[u * 2 + n] = jnp.dot(kt, qs[n], preferred_element_type=F32)

    def consume(s, buf, kind):
        fresh = kt_ref[s] == 0
        edge = STEP_EDGE.index(kind) if kind in STEP_EDGE else None
        for u in range(n_sub) if edge is None else range(edge + 1):
            vtt = values(key_row(s, u), ATT_SUB)
            for n in range(2):
                sc = buf[u * 2 + n]
                if u == edge:
                    r = lax.broadcasted_iota(jnp.int32, (ATT_SUB, tq), 0) // CHUNK
                    c = lax.broadcasted_iota(jnp.int32, (ATT_SUB, tq), 1) // CHUNK
                    sc = jnp.where(r <= c, sc, NEG)
                update(n, sc, vtt, fresh if u == 0 else None)

    def finish(s):
        qs = q_maps(s)
        kt = k_ref[0, FRONT - N_META:FRONT, :]
        vtt = vt_ref[0, FRONT // LANES - 1][:, LANES - N_META:]
        for n in range(2):
            update(n, jnp.dot(kt, qs[n], preferred_element_type=F32), vtt)
        lp = lam_ref[...]
        lam = (jnp.exp(jnp.sum(lp[0:1] * lp[1:2], axis=1, keepdims=True))
               - jnp.exp(jnp.sum(lp[2:3] * lp[3:4], axis=1, keepdims=True)) + lam_init)
        o = acc_sc[0] / l_sc[0] - lam * (acc_sc[1] / l_sc[1])
        o = o * lax.rsqrt(jnp.mean(o * o, axis=0, keepdims=True) + EPS) * g_ref[...] * (1.0 - lam_init)
        o_ref[pl.ds(pl.multiple_of(qi_ref[s] * tq, tq), tq), :] = o.T.astype(BF16)

    def block(p, kind_a, kind_b):
        s_a, s_b = 2 * p, 2 * p + 1
        scores_into(s_b, sb_sc)
        consume(s_a, sa_sc, kind_a)
        if kind_a in STEP_EDGE:
            finish(s_a)
        scores_into(jnp.minimum(s_b + 1, n_steps - 1), sa_sc)
        consume(s_b, sb_sc, kind_b)
        if kind_b in STEP_EDGE:
            finish(s_b)

    scores_into(0, sa_sc)

    def body(p, c):
        kind_a, kind_b = kind_ref[2 * p], kind_ref[2 * p + 1]
        for va, vb in block_kinds:
            pl.when(jnp.logical_and(kind_a == va, kind_b == vb))(functools.partial(block, p, va, vb))
        return c

    lax.fori_loop(0, n_steps // 2, body, 0)


def _attention(qt, k, vt4, lam_p, subln_g, bsz, seq, lam_init):
    nq = seq // ATT_TQ
    lp = k.shape[1]
    q_idx, k_idx, kinds = _attn_schedule(nq)
    n_steps = len(kinds)
    assert n_steps % 2 == 0
    block_kinds = sorted(set(zip(kinds[0::2], kinds[1::2])))
    grid_spec = pltpu.PrefetchScalarGridSpec(
        num_scalar_prefetch=3,
        grid=(bsz, N_HEADS),
        in_specs=[
            pl.BlockSpec((1, nq, 2 * HEAD_DIM, ATT_TQ), lambda b, h, *_: (b, 0, h, 0)),
            pl.BlockSpec((1, lp, 2 * HEAD_DIM), lambda b, h, *_: (b, 0, h)),
            pl.BlockSpec((1, lp // LANES, 2 * HEAD_DIM, LANES), lambda b, h, *_: (b, 0, h, 0)),
            pl.BlockSpec((4, HEAD_DIM), lambda b, h, *_: (0, 0)),
            pl.BlockSpec((2 * HEAD_DIM, 1), lambda b, h, *_: (0, 0)),
        ],
        out_specs=pl.BlockSpec((seq, 2 * HEAD_DIM), lambda b, h, *_: (b, h)),
        scratch_shapes=[pltpu.VMEM((2, 1, ATT_TQ), F32), pltpu.VMEM((2, 1, ATT_TQ), F32),
                        pltpu.VMEM((2, 2 * HEAD_DIM, ATT_TQ), F32),
                        pltpu.VMEM((2 * ATT_TK // ATT_SUB, ATT_SUB, ATT_TQ), F32),
                        pltpu.VMEM((2 * ATT_TK // ATT_SUB, ATT_SUB, ATT_TQ), F32)],
    )
    return pl.pallas_call(
        functools.partial(_attn_kernel, lam_init=lam_init, n_steps=n_steps, block_kinds=block_kinds),
        grid_spec=grid_spec,
        out_shape=jax.ShapeDtypeStruct((bsz * seq, D_MODEL), BF16),
        compiler_params=_cparams(("arbitrary", "arbitrary")),
        name="diff_attention",
    )(*(jnp.asarray(col, jnp.int32) for col in (q_idx, k_idx, kinds)),
      qt.reshape(bsz, nq, D_MODEL, ATT_TQ), k, vt4, lam_p, subln_g.reshape(2 * HEAD_DIM, 1))


def _oproj_kernel(h_ref, o_ref, w_ref, rg_ref, wrt_ref, br_ref, tri_ref, out_ref, lpos_ref, gate_ref, meta_ref, tot_ref,
                  seg_sc):
    _route_start(seg_sc)
    h = h_ref[...] + jnp.dot(o_ref[...], w_ref[...], preferred_element_type=F32)
    out_ref[...] = h
    _route(h, rg_ref, wrt_ref, br_ref, tri_ref, lpos_ref, gate_ref, meta_ref, tot_ref, seg_sc)


def _o_proj(h_pad, o, w_out, router, bsz, seq, lp):
    tiles_real = seq // TILE
    r_args, r_specs = _router_operands(*router, TILE)
    ro_specs, ro_shapes, r_scratch = _router_outputs(bsz * seq, TILE)
    return pl.pallas_call(
        _oproj_kernel,
        grid=(bsz * tiles_real,),
        in_specs=[
            pl.BlockSpec((TILE, D_MODEL), _real_row_map(tiles_real, lp // TILE)),
            pl.BlockSpec((TILE, D_MODEL), lambda i: (i, 0)),
            pl.BlockSpec((D_MODEL, D_MODEL), lambda i: (0, 0)),
        ] + r_specs,
        out_specs=[pl.BlockSpec((TILE, D_MODEL), lambda i: (i, 0))] + ro_specs,
        out_shape=[jax.ShapeDtypeStruct((bsz * seq, D_MODEL), F32)] + ro_shapes,
        scratch_shapes=[r_scratch],
        compiler_params=_cparams(("arbitrary",)),
        name="o_proj",
    )(h_pad, o, w_out.astype(BF16), *r_args)


def kernel(x, meta_tokens, a_norm, a_w_in, a_conv, a_w_out, kv_norm, w_kv, b_norm, b_w_q, b_lambda, b_subln, b_w_out, ffn_norm, r_group, r_group_b, r_expert, r_expert_b, e_gate, e_up, e_down, final_norm):
    bsz, seq, d = x.shape
    assert d == D_MODEL and a_norm.shape[0] == 1 and b_norm.shape[0] == 1
    lp = FRONT + seq
    assert seq % ATT_TK == 0 and ATT_TK == 2 * ATT_TQ and ATT_TQ == TILE and ATT_TK % ATT_SUB == 0
    front = jnp.concatenate([jnp.zeros((PAD_ROWS, d), x.dtype), meta_tokens.astype(x.dtype)], axis=0)
    routers = [(ffn_norm[j], r_group[j], r_group_b[j], r_expert[j], r_expert_b[j]) for j in range(2)]
    h, *routing = _mixer(x.reshape(bsz * seq, d), front, a_norm[0], a_w_in[0], a_conv[0], a_w_out[0], routers[0],
                         bsz, lp // TILE)
    h = _moe(h, routing, ffn_norm[0], e_gate, e_up, e_down, 0, final_norm, TILE, False)

    k, vt4 = _kv(h, kv_norm, w_kv, bsz, lp)
    qt = _q_proj(h, b_norm[0], b_w_q[0], bsz, seq, lp)
    lam_init = 0.8 - 0.6 * math.exp(-0.3 * a_norm.shape[0])
    o = _attention(qt, k, vt4, b_lambda[0], b_subln[0], bsz, seq, lam_init)
    h, *routing = _o_proj(h, o, b_w_out[0], routers[1], bsz, seq, lp)
    h = _moe(h, routing, ffn_norm[1], e_gate, e_up, e_down, 1, final_norm, TILE, True)
    return h.reshape(bsz, seq, d)
```

```python
import functools
import math

import jax
import jax.numpy as jnp
from jax import lax
from jax.experimental import pallas as pl
from jax.experimental.pallas import tpu as pltpu

D_MODEL = 1024
CHUNK = 64
N_META = 16
HEAD_DIM = 64
N_HEADS = 8
ROT_HALF = 8
ROPE_THETA = 500000.0
N_GROUPS = 4
EXPERTS_PER_GROUP = 8
N_EXPERTS = 32
D_EXPERT = 512
EPS = 1e-6

LANES = 128
SUBLANES = 8
TILE = 512
FRONT = TILE
PAD_ROWS = FRONT - N_META
EXPERT_ROWS = 512
ATT_TQ = 512
ATT_TK = 1024
ATT_SUB = 512
SEG_ALIGN = 16
STAGE_SLOTS = 2
LOCAL_ROWS = 1536
VMEM_LIMIT = 56 * 1024 * 1024
NEG = float(jnp.finfo(jnp.float32).min)

BF16 = jnp.bfloat16
F32 = jnp.float32


def _rms(x, g):
    return x * lax.rsqrt(jnp.mean(x * x, axis=-1, keepdims=True) + EPS) * g


def _cparams(sem):
    return pltpu.CompilerParams(dimension_semantics=sem, vmem_limit_bytes=VMEM_LIMIT)


def _mixer_kernel(x_ref, front_ref, g_ref, win_ref, cw_ref, wout_ref, rg_ref, wrt_ref, br_ref, tri_ref,
                  o_ref, lpos_ref, gate_ref, meta_ref, tot_ref, cu_ref, seg_sc, *, tiles_per_batch):
    i = pl.program_id(0)
    tm = x_ref.shape[0]
    _route_start(seg_sc)

    @pl.when(i % tiles_per_batch == 0)
    def _():
        cu_ref[0:SUBLANES, :] = jnp.zeros((SUBLANES, D_MODEL), F32)

    x = jnp.where(i % tiles_per_batch == 0, front_ref[...], x_ref[...])
    hn = _rms(x, g_ref[...]).astype(BF16)
    proj = jnp.dot(hn, win_ref[...], preferred_element_type=F32)
    b_gate = proj[:, :D_MODEL]
    cu = proj[:, D_MODEL:2 * D_MODEL] * proj[:, 2 * D_MODEL:]
    cu_ref[SUBLANES:SUBLANES + tm, :] = cu
    cw = cw_ref[...]
    z = (cw[2:3] * cu + cw[1:2] * cu_ref[SUBLANES - 1:SUBLANES - 1 + tm, :]
         + cw[0:1] * cu_ref[SUBLANES - 2:SUBLANES - 2 + tm, :])
    cu_ref[0:SUBLANES, :] = cu_ref[tm:tm + SUBLANES, :]
    y = jnp.dot((b_gate * z).astype(BF16), wout_ref[...], preferred_element_type=F32)
    h = x + y
    o_ref[...] = h
    _route(h, rg_ref, wrt_ref, br_ref, tri_ref, lpos_ref, gate_ref, meta_ref, tot_ref, seg_sc)


def _mixer(x, front, g, w_in, conv_w, w_out, router, bsz, tiles_per_batch):
    t = bsz * tiles_per_batch * TILE
    r_args, r_specs = _router_operands(*router, TILE)
    ro_specs, ro_shapes, r_scratch = _router_outputs(t, TILE)
    tiles_real = tiles_per_batch - FRONT // TILE

    def x_map(i):
        return ((i // tiles_per_batch) * tiles_real + jnp.maximum(i % tiles_per_batch - FRONT // TILE, 0), 0)

    return pl.pallas_call(
        functools.partial(_mixer_kernel, tiles_per_batch=tiles_per_batch),
        grid=(t // TILE,),
        in_specs=[
            pl.BlockSpec((TILE, D_MODEL), x_map),
            pl.BlockSpec((TILE, D_MODEL), lambda i: (0, 0)),
            pl.BlockSpec((1, D_MODEL), lambda i: (0, 0)),
            pl.BlockSpec((D_MODEL, 3 * D_MODEL), lambda i: (0, 0)),
            pl.BlockSpec((3, D_MODEL), lambda i: (0, 0)),
            pl.BlockSpec((D_MODEL, D_MODEL), lambda i: (0, 0)),
        ] + r_specs,
        out_specs=[pl.BlockSpec((TILE, D_MODEL), lambda i: (i, 0))] + ro_specs,
        out_shape=[jax.ShapeDtypeStruct((t, D_MODEL), F32)] + ro_shapes,
        scratch_shapes=[pltpu.VMEM((TILE + 2 * SUBLANES, D_MODEL), F32), r_scratch],
        compiler_params=_cparams(("arbitrary",)),
        name="mixer_a",
    )(x, front, g.reshape(1, D_MODEL), w_in.astype(BF16), conv_w, w_out.astype(BF16), *r_args)


ROUTER_ROWS = SUBLANES + N_EXPERTS


def _round_up(v, m):
    return jnp.floor((v + (m - 1)) * (1.0 / m)) * m


def _route_start(seg_sc):
    @pl.when(pl.program_id(0) == 0)
    def _():
        seg_sc[...] = jnp.zeros_like(seg_sc)


def _route(h, g_ref, wrt_ref, br_ref, tri_ref, lpos_ref, gate_ref, meta_ref, tot_ref, seg_sc):
    i = pl.program_id(0)
    tm = h.shape[0]
    xn = _rms(h, g_ref[...])
    lt = lax.dot_general(wrt_ref[...], xn, (((1,), (1,)), ((), ())),
                         precision=lax.Precision.HIGHEST, preferred_element_type=F32) + br_ref[...]
    lg = lt[0:N_GROUPS]
    gmax = jnp.max(lg, axis=0, keepdims=True)
    iota_g = lax.broadcasted_iota(jnp.int32, lg.shape, 0)
    grp = jnp.min(jnp.where(lg == gmax, iota_g, N_GROUPS), axis=0, keepdims=True)
    p_grp = 1.0 / jnp.sum(jnp.exp(lg - gmax), axis=0, keepdims=True)
    le = lt[SUBLANES:SUBLANES + EXPERTS_PER_GROUP]
    for g in range(1, N_GROUPS):
        lo = SUBLANES + g * EXPERTS_PER_GROUP
        le = jnp.where(grp == g, lt[lo:lo + EXPERTS_PER_GROUP], le)
    iota_e = lax.broadcasted_iota(jnp.int32, le.shape, 0)
    v1 = jnp.max(le, axis=0, keepdims=True)
    i1 = jnp.min(jnp.where(le == v1, iota_e, EXPERTS_PER_GROUP), axis=0, keepdims=True)
    le2 = jnp.where(iota_e == i1, -jnp.inf, le)
    v2 = jnp.max(le2, axis=0, keepdims=True)
    i2 = jnp.min(jnp.where(le2 == v2, iota_e, EXPERTS_PER_GROUP), axis=0, keepdims=True)
    e2 = jnp.exp(v2 - v1)
    den = 1.0 + e2
    gate_ref[...] = jnp.concatenate([p_grp * (1.0 / den), p_grp * (e2 / den)], axis=0)
    experts = (grp * EXPERTS_PER_GROUP + i1, grp * EXPERTS_PER_GROUP + i2)

    iota_x = lax.broadcasted_iota(jnp.int32, (N_EXPERTS, tm), 0)
    hits = [iota_x == e_k for e_k in experts]
    hits_f = [jnp.where(hit, 1.0, 0.0) for hit in hits]
    hits_b = [hf.astype(BF16) for hf in hits_f]
    excl = [jnp.dot(hb, tri_ref[...], preferred_element_type=F32) for hb in hits_b]
    tot_col = [jnp.sum(hf, axis=1, keepdims=True) for hf in hits_f]
    ones = jnp.ones((SUBLANES, tm), BF16)
    n_lane = sum(lax.dot_general(ones, hb, (((1,), (1,)), ((), ())), preferred_element_type=F32) for hb in hits_b)
    seg_len_lane = _round_up(n_lane[0:1], SEG_ALIGN)
    seg_len = _round_up(tot_col[0] + tot_col[1], SEG_ALIGN)
    before = (lax.broadcasted_iota(jnp.int32, (N_EXPERTS, N_EXPERTS), 1)
              < lax.broadcasted_iota(jnp.int32, (N_EXPERTS, N_EXPERTS), 0))
    local_off = jnp.sum(jnp.where(before, seg_len_lane, 0.0), axis=1, keepdims=True)

    lpos_a = jnp.sum(jnp.where(hits[0], excl[0] + local_off, 0.0), axis=0, keepdims=True).astype(jnp.int32)
    lpos_b = jnp.sum(jnp.where(hits[1], excl[1] + tot_col[0] + local_off, 0.0), axis=0, keepdims=True).astype(jnp.int32)
    lpos_ref[...] = jnp.concatenate([lpos_a, lpos_b], axis=0)

    seg_before = seg_sc[...]
    seg_sc[...] = seg_before + seg_len
    lane = lax.broadcasted_iota(jnp.int32, (N_EXPERTS, LANES), 1)
    meta = jnp.where(lane == 0, seg_len, jnp.where(lane == 1, local_off, jnp.where(lane == 2, seg_before, 0.0)))
    meta_ref[0] = meta.astype(jnp.int32)
    tot_ref[...] = jnp.broadcast_to(seg_before + seg_len, tot_ref.shape).astype(jnp.int32)


def _router_operands(g, w_rg, b_rg, w_re, b_re, tm):
    wrt = jnp.zeros((ROUTER_ROWS, D_MODEL), F32)
    wrt = wrt.at[0:N_GROUPS].set(w_rg.T).at[SUBLANES:].set(w_re.T)
    br = jnp.zeros((ROUTER_ROWS, 1), F32)
    br = br.at[0:N_GROUPS, 0].set(b_rg).at[SUBLANES:, 0].set(b_re)
    tri = (lax.broadcasted_iota(jnp.int32, (tm, tm), 0) < lax.broadcasted_iota(jnp.int32, (tm, tm), 1)).astype(BF16)
    specs = [
        pl.BlockSpec((1, D_MODEL), lambda i: (0, 0)),
        pl.BlockSpec((ROUTER_ROWS, D_MODEL), lambda i: (0, 0)),
        pl.BlockSpec((ROUTER_ROWS, 1), lambda i: (0, 0)),
        pl.BlockSpec((tm, tm), lambda i: (0, 0)),
    ]
    return [g.reshape(1, D_MODEL), wrt, br, tri], specs


def _router_outputs(t, tm):
    nt = t // tm
    specs = [
        pl.BlockSpec((2, tm), lambda i: (0, i)),
        pl.BlockSpec((2, tm), lambda i: (0, i)),
        pl.BlockSpec((1, N_EXPERTS, LANES), lambda i: (i, 0, 0)),
        pl.BlockSpec((N_EXPERTS, LANES), lambda i: (0, 0)),
    ]
    shapes = [
        jax.ShapeDtypeStruct((2, t), jnp.int32),
        jax.ShapeDtypeStruct((2, t), F32),
        jax.ShapeDtypeStruct((nt, N_EXPERTS, LANES), jnp.int32),
        jax.ShapeDtypeStruct((N_EXPERTS, LANES), jnp.int32),
    ]
    return specs, shapes, pltpu.VMEM((N_EXPERTS, 1), F32)


def _if_rows(n, fn):
    pl.when(n > 0)(lambda: fn(pl.multiple_of(n, SEG_ALIGN)))


def _rows(ref, start, size):
    return ref.at[pl.ds(pl.multiple_of(start, SEG_ALIGN), size), :]


def _dispatch_kernel(len_ref, off_ref, dst_ref, zs_ref, zn_ref, nu_ref, h_ref, g_ref, lpos_ref, buf_ref, stage, zero_sc,
                     sem_out, sem_fill, *, n_blk):
    t = pl.program_id(0)
    nt = pl.num_programs(0)
    tm = h_ref.shape[0]
    slot = t % STAGE_SLOTS

    def tile_out(t, s, op):
        def body(e, c):
            idx = t * N_EXPERTS + e
            lo, dst = off_ref[idx], dst_ref[idx]
            _if_rows(len_ref[idx], lambda n: op(pltpu.make_async_copy(
                _rows(stage.at[s], lo, n), _rows(buf_ref, dst, n), sem_out.at[s])))
            return c

        lax.fori_loop(0, N_EXPERTS, body, 0)

    def fill_copies(op):
        def tails(e, c):
            _if_rows(zn_ref[e], lambda n: op(pltpu.make_async_copy(
                zero_sc.at[pl.ds(0, n), :], _rows(buf_ref, zs_ref[e], n), sem_fill)))
            return c

        lax.fori_loop(0, N_EXPERTS, tails, 0)

        def blocks(b, c):
            op(pltpu.make_async_copy(zero_sc, _rows(buf_ref, b * EXPERT_ROWS, EXPERT_ROWS), sem_fill))
            return c

        lax.fori_loop(nu_ref[0], n_blk, blocks, 0)

    start = lambda cp: cp.start()
    wait = lambda cp: cp.wait()

    @pl.when(t == 0)
    def _():
        zero_sc[...] = jnp.zeros_like(zero_sc)
        fill_copies(start)

    @pl.when(t >= STAGE_SLOTS)
    def _():
        tile_out(t - STAGE_SLOTS, slot, wait)

    xn = _rms(h_ref[...], g_ref[...]).astype(BF16)
    lpos = lpos_ref[...]
    row = lax.broadcasted_iota(jnp.int32, (LOCAL_ROWS, tm), 0)
    perm = jnp.where(row == lpos[0:1], 1.0, jnp.where(row == lpos[1:2], 1.0, 0.0)).astype(BF16)
    stage[slot] = jnp.dot(perm, xn, preferred_element_type=F32).astype(BF16)
    tile_out(t, slot, start)

    @pl.when(t == nt - 1)
    def _():
        for back in range(STAGE_SLOTS - 1, -1, -1):
            @pl.when(t >= back)
            def _(back=back):
                tile_out(t - back, (t - back) % STAGE_SLOTS, wait)

        fill_copies(wait)


def _dispatch(h, g, lpos, seg_len, seg_off, seg_dst, tail_start, tail_len, n_used, n_blk, tm):
    t = h.shape[0]
    grid_spec = pltpu.PrefetchScalarGridSpec(
        num_scalar_prefetch=6,
        grid=(t // tm,),
        in_specs=[
            pl.BlockSpec((tm, D_MODEL), lambda i, *_: (i, 0)),
            pl.BlockSpec((1, D_MODEL), lambda i, *_: (0, 0)),
            pl.BlockSpec((2, tm), lambda i, *_: (0, i)),
        ],
        out_specs=pl.BlockSpec(memory_space=pl.ANY),
        scratch_shapes=[pltpu.VMEM((STAGE_SLOTS, LOCAL_ROWS, D_MODEL), BF16), pltpu.VMEM((EXPERT_ROWS, D_MODEL), BF16),
                        pltpu.SemaphoreType.DMA((STAGE_SLOTS,)), pltpu.SemaphoreType.DMA],
    )
    return pl.pallas_call(
        functools.partial(_dispatch_kernel, n_blk=n_blk),
        grid_spec=grid_spec,
        out_shape=jax.ShapeDtypeStruct((n_blk * EXPERT_ROWS, D_MODEL), BF16),
        compiler_params=_cparams(("arbitrary",)),
        name="dispatch",
    )(seg_len, seg_off, seg_dst, tail_start, tail_len, n_used, h, g.reshape(1, D_MODEL), lpos)


def _expert_kernel(be_ref, nu_ref, x_ref, wg_ref, wu_ref, wd_ref, y_ref, wgu_sc, wd_sc):
    i = pl.program_id(0)
    e = be_ref[i]
    prev = be_ref[jnp.maximum(i - 1, 0)]

    @pl.when(jnp.logical_or(i == 0, e != prev))
    def _():
        wgu_sc[:, :D_EXPERT] = wg_ref[0, 0].astype(BF16)
        wgu_sc[:, D_EXPERT:] = wu_ref[0, 0].astype(BF16)
        wd_sc[...] = wd_ref[0, 0].astype(BF16)

    @pl.when(i < nu_ref[0])
    def _():
        gu = jnp.dot(x_ref[...], wgu_sc[...], preferred_element_type=F32)
        a = jax.nn.silu(gu[:, :D_EXPERT]) * gu[:, D_EXPERT:]
        y_ref[...] = jnp.dot(a.astype(BF16), wd_sc[...], preferred_element_type=F32).astype(BF16)

    @pl.when(i >= nu_ref[0])
    def _():
        y_ref[...] = jnp.zeros_like(y_ref)


def _experts(buf, blk_expert, n_used, w_gate, w_up, w_down, layer):
    n_blk = blk_expert.shape[0]

    def xmap(i, be, nu):
        return (jnp.maximum(jnp.minimum(i, nu[0] - 1), 0), 0)

    grid_spec = pltpu.PrefetchScalarGridSpec(
        num_scalar_prefetch=2,
        grid=(n_blk,),
        in_specs=[
            pl.BlockSpec((EXPERT_ROWS, D_MODEL), xmap),
            pl.BlockSpec((1, 1, D_MODEL, D_EXPERT), lambda i, be, nu: (layer, be[i], 0, 0)),
            pl.BlockSpec((1, 1, D_MODEL, D_EXPERT), lambda i, be, nu: (layer, be[i], 0, 0)),
            pl.BlockSpec((1, 1, D_EXPERT, D_MODEL), lambda i, be, nu: (layer, be[i], 0, 0)),
        ],
        out_specs=pl.BlockSpec((EXPERT_ROWS, D_MODEL), lambda i, be, nu: (i, 0)),
        scratch_shapes=[pltpu.VMEM((D_MODEL, 2 * D_EXPERT), BF16), pltpu.VMEM((D_EXPERT, D_MODEL), BF16)],
    )
    return pl.pallas_call(
        _expert_kernel,
        grid_spec=grid_spec,
        out_shape=jax.ShapeDtypeStruct(buf.shape, BF16),
        compiler_params=_cparams(("arbitrary",)),
        name="experts",
    )(blk_expert, n_used, buf, w_gate, w_up, w_down)


def _combine_kernel(len_ref, off_ref, dst_ref, h_ref, gate_ref, lpos_ref, y_ref, fn_ref, o_ref, yl, sem, *, final_norm):
    i = pl.program_id(0)
    nt = pl.num_programs(0)
    tm = h_ref.shape[0]
    slot = i % 2

    def tile_copies(t, s, op):
        def body(e, c):
            idx = t * N_EXPERTS + e
            lo, src = off_ref[idx], dst_ref[idx]
            _if_rows(len_ref[idx], lambda n: op(pltpu.make_async_copy(
                _rows(y_ref, src, n), _rows(yl.at[s], lo, n), sem.at[s])))
            return c

        lax.fori_loop(0, N_EXPERTS, body, 0)

    @pl.when(i == 0)
    def _():
        yl[...] = jnp.zeros_like(yl)
        tile_copies(0, 0, lambda cp: cp.start())

    @pl.when(i + 1 < nt)
    def _():
        tile_copies(i + 1, 1 - slot, lambda cp: cp.start())

    tile_copies(i, slot, lambda cp: cp.wait())

    rows = yl[slot]
    lpos = lpos_ref[...]
    col = lax.broadcasted_iota(jnp.int32, (tm, LOCAL_ROWS), 1)
    g = gate_ref[...]
    weights = jnp.where(col == lpos[:, 0:1], g[:, 0:1], jnp.where(col == lpos[:, 1:2], g[:, 1:2], 0.0)).astype(BF16)
    out = h_ref[...] + jnp.dot(weights, rows, preferred_element_type=F32)
    if final_norm:
        out = _rms(out, fn_ref[...])
    o_ref[...] = out


def _combine(h, y, seg_len, seg_off, seg_dst, lpos, gate, fn, tm, final_norm):
    t = h.shape[0]
    grid_spec = pltpu.PrefetchScalarGridSpec(
        num_scalar_prefetch=3,
        grid=(t // tm,),
        in_specs=[
            pl.BlockSpec((tm, D_MODEL), lambda i, *_: (i, 0)),
            pl.BlockSpec((tm, 2), lambda i, *_: (i, 0)),
            pl.BlockSpec((tm, 2), lambda i, *_: (i, 0)),
            pl.BlockSpec(memory_space=pl.ANY),
            pl.BlockSpec((1, D_MODEL), lambda i, *_: (0, 0)),
        ],
        out_specs=pl.BlockSpec((tm, D_MODEL), lambda i, *_: (i, 0)),
        scratch_shapes=[pltpu.VMEM((2, LOCAL_ROWS, D_MODEL), BF16), pltpu.SemaphoreType.DMA((2,))],
    )
    return pl.pallas_call(
        functools.partial(_combine_kernel, final_norm=final_norm),
        grid_spec=grid_spec,
        out_shape=jax.ShapeDtypeStruct((t, D_MODEL), F32),
        compiler_params=_cparams(("arbitrary",)),
        name="combine",
    )(seg_len, seg_off, seg_dst, h, gate.T, lpos.T, y, fn.reshape(1, D_MODEL))


def _moe(h, routing, g, w_gate, w_up, w_down, layer, fn, tm, final_norm):
    t = h.shape[0]
    nt = t // tm
    lpos, gate, meta, tot = routing
    n_blk = (2 * t + nt * N_EXPERTS * (SEG_ALIGN - 1)) // EXPERT_ROWS + N_EXPERTS
    total = tot[:, 0]
    region = (total + EXPERT_ROWS - 1) // EXPERT_ROWS * EXPERT_ROWS
    ends = jnp.cumsum(region)
    starts = ends - region
    seg_len = meta[:, :, 0].reshape(-1)
    seg_off = meta[:, :, 1].reshape(-1)
    seg_dst = (meta[:, :, 2] + starts[None, :]).reshape(-1)
    blk_start = jnp.arange(n_blk, dtype=jnp.int32) * EXPERT_ROWS
    blk_expert = jnp.minimum(jnp.sum(blk_start[:, None] >= ends[None, :], axis=1), N_EXPERTS - 1).astype(jnp.int32)
    n_used = (ends[-1:] // EXPERT_ROWS).astype(jnp.int32)
    buf = _dispatch(h, g, lpos, seg_len, seg_off, seg_dst, starts + total, region - total, n_used, n_blk, tm)
    y = _experts(buf, blk_expert, n_used, w_gate, w_up, w_down, layer)
    return _combine(h, y, seg_len, seg_off, seg_dst, lpos, gate, fn, tm, final_norm)


def _kv_kernel(h_ref, g_ref, wk_ref, wvt_ref, cos_ref, sa_ref, sb_ref, k_ref, vt_ref):
    hn = _rms(h_ref[...], g_ref[...]).astype(BF16)
    k = jnp.dot(hn, wk_ref[...], preferred_element_type=F32)
    cos, sa, sb = cos_ref[...], sa_ref[...], sb_ref[...]
    for c in range(D_MODEL // LANES):
        kc = k[:, c * LANES:(c + 1) * LANES]
        rot = kc * cos + pltpu.roll(kc, LANES - ROT_HALF, 1) * sa + pltpu.roll(kc, ROT_HALF, 1) * sb
        k_ref[0, :, c * LANES:(c + 1) * LANES] = rot.astype(BF16)
    vt = lax.dot_general(wvt_ref[...], hn, (((1,), (1,)), ((), ())), preferred_element_type=F32)
    for c in range(vt_ref.shape[1]):
        vt_ref[0, c] = vt[:, c * LANES:(c + 1) * LANES].astype(BF16)


def _kv(h, g, w_kv, bsz, lp):
    tpb = lp // TILE
    kblk = TILE // LANES
    pos = jnp.maximum(jnp.arange(lp) - PAD_ROWS, 0).astype(F32)
    inv_freq = ROPE_THETA ** (-jnp.arange(ROT_HALF, dtype=F32) * 2.0 / (2 * ROT_HALF))
    ang = pos[:, None] * inv_freq[None, :]
    r = jnp.arange(LANES) % HEAD_DIM
    cos_t = jnp.where(r < 2 * ROT_HALF, jnp.cos(ang)[:, r % ROT_HALF], 1.0)
    sin_t = jnp.sin(ang)[:, r % ROT_HALF]
    sa = jnp.where(r < ROT_HALF, -sin_t, 0.0)
    sb = jnp.where((r >= ROT_HALF) & (r < 2 * ROT_HALF), sin_t, 0.0)
    tab = pl.BlockSpec((TILE, LANES), lambda i: (i % tpb, 0))
    return pl.pallas_call(
        _kv_kernel,
        grid=(bsz * tpb,),
        in_specs=[
            pl.BlockSpec((TILE, D_MODEL), lambda i: (i, 0)),
            pl.BlockSpec((1, D_MODEL), lambda i: (0, 0)),
            pl.BlockSpec((D_MODEL, D_MODEL), lambda i: (0, 0)),
            pl.BlockSpec((D_MODEL, D_MODEL), lambda i: (0, 0)),
            tab, tab, tab,
        ],
        out_specs=[
            pl.BlockSpec((1, TILE, D_MODEL), lambda i: (i // tpb, i % tpb, 0)),
            pl.BlockSpec((1, kblk, D_MODEL, LANES), lambda i: (i // tpb, i % tpb, 0, 0)),
        ],
        out_shape=[
            jax.ShapeDtypeStruct((bsz, lp, D_MODEL), BF16),
            jax.ShapeDtypeStruct((bsz, lp // LANES, D_MODEL, LANES), BF16),
        ],
        compiler_params=_cparams(("arbitrary",)),
        name="kv_proj",
    )(h, g.reshape(1, D_MODEL), w_kv[:, :D_MODEL].astype(BF16), w_kv[:, D_MODEL:].T.astype(BF16), cos_t, sa, sb)


def _q_kernel(h_ref, g_ref, wqt_ref, cos_ref, sin_ref, qt_ref):
    hn = _rms(h_ref[...], g_ref[...]).astype(BF16)
    qt = lax.dot_general(wqt_ref[...], hn, (((1,), (1,)), ((), ())), preferred_element_type=F32)
    cos, sin = cos_ref[...], sin_ref[...]
    scale = HEAD_DIM ** -0.5 * math.log2(math.e)
    for c in range(D_MODEL // HEAD_DIM):
        lo = c * HEAD_DIM
        x1 = qt[lo:lo + ROT_HALF]
        x2 = qt[lo + ROT_HALF:lo + 2 * ROT_HALF]
        blk = jnp.concatenate([x1 * cos - x2 * sin, x2 * cos + x1 * sin, qt[lo + 2 * ROT_HALF:lo + HEAD_DIM]], axis=0)
        qt_ref[0, lo:lo + HEAD_DIM, :] = (blk * scale).astype(BF16)


def _real_row_map(tiles_real, tiles_padded):
    front = FRONT // TILE
    return lambda i: ((i // tiles_real) * tiles_padded + front + i % tiles_real, 0)


def _q_proj(h_pad, g, w_q, bsz, seq, lp):
    tiles_real = seq // TILE
    pos = (N_META + jnp.arange(seq)).astype(F32)
    inv_freq = ROPE_THETA ** (-jnp.arange(ROT_HALF, dtype=F32) * 2.0 / (2 * ROT_HALF))
    ang = inv_freq[:, None] * pos[None, :]
    tab = pl.BlockSpec((ROT_HALF, TILE), lambda i: (0, i % tiles_real))
    return pl.pallas_call(
        _q_kernel,
        grid=(bsz * tiles_real,),
        in_specs=[
            pl.BlockSpec((TILE, D_MODEL), _real_row_map(tiles_real, lp // TILE)),
            pl.BlockSpec((1, D_MODEL), lambda i: (0, 0)),
            pl.BlockSpec((D_MODEL, D_MODEL), lambda i: (0, 0)),
            tab, tab,
        ],
        out_specs=pl.BlockSpec((1, D_MODEL, TILE), lambda i: (i, 0, 0)),
        out_shape=jax.ShapeDtypeStruct((bsz * tiles_real, D_MODEL, TILE), BF16),
        compiler_params=_cparams(("arbitrary",)),
        name="q_proj",
    )(h_pad, g.reshape(1, D_MODEL), w_q.T.astype(BF16), jnp.cos(ang), jnp.sin(ang))


STEP_FULL = 0
STEP_EDGE = (1, 2)


def _attn_schedule(nq):
    per_k = ATT_TK // ATT_TQ
    steps = [(i, t, STEP_FULL if t < i // per_k else STEP_EDGE[i % per_k]) for i in range(nq) for t in range(i // per_k + 1)]
    return [list(col) for col in zip(*steps)]


def _attn_kernel(qi_ref, kt_ref, kind_ref, qt_ref, k_ref, vt_ref, lam_ref, g_ref, o_ref, m_sc, l_sc, acc_sc, sa_sc, sb_sc,
                 *, lam_init, n_steps, block_kinds):
    tq = ATT_TQ
    zero = jnp.zeros((HEAD_DIM, tq), BF16)
    n_sub = ATT_TK // ATT_SUB

    def q_maps(s):
        qt = qt_ref[0, qi_ref[s]]
        return (jnp.concatenate([qt[0:HEAD_DIM], zero], axis=0), jnp.concatenate([zero, qt[HEAD_DIM:]], axis=0))

    def update(n, sc, vtt, fresh=None):
        m_old, l_old, acc_old = m_sc[n], l_sc[n], acc_sc[n]
        if fresh is not None:
            m_old = jnp.where(fresh, NEG, m_old)
            l_old = jnp.where(fresh, 0.0, l_old)
            acc_old = jnp.where(fresh, 0.0, acc_old)
        m_new = jnp.maximum(m_old, jnp.max(sc, axis=0, keepdims=True))
        alpha = jnp.exp2(m_old - m_new)
        p = jnp.exp2(sc - m_new)
        l_sc[n] = alpha * l_old + jnp.sum(p, axis=0, keepdims=True)
        acc_sc[n] = alpha * acc_old + jnp.dot(vtt, p.astype(BF16), preferred_element_type=F32)
        m_sc[n] = m_new

    def values(row0, nkeys):
        blk0 = row0 // LANES
        return jnp.concatenate([vt_ref[0, blk0 + c] for c in range(nkeys // LANES)], axis=1)

    def key_row(s, u):
        return pl.multiple_of(FRONT + kt_ref[s] * ATT_TK + u * ATT_SUB, LANES)

    def scores_into(s, buf):
        qs = q_maps(s)
        for u in range(n_sub):
            kt = k_ref[0, pl.ds(key_row(s, u), ATT_SUB), :]
            for n in range(2):
                buf[u * 2 + n] = jnp.dot(kt, qs[n], preferred_element_type=F32)

    def consume(s, buf, kind):
        fresh = kt_ref[s] == 0
        edge = STEP_EDGE.index(kind) if kind in STEP_EDGE else None
        for u in range(n_sub) if edge is None else range(edge + 1):
            vtt = values(key_row(s, u), ATT_SUB)
            for n in range(2):
                sc = buf[u * 2 + n]
                if u == edge:
                    r = lax.broadcasted_iota(jnp.int32, (ATT_SUB, tq), 0) // CHUNK
                    c = lax.broadcasted_iota(jnp.int32, (ATT_SUB, tq), 1) // CHUNK
                    sc = jnp.where(r <= c, sc, NEG)
                update(n, sc, vtt, fresh if u == 0 else None)

    def finish(s):
        qs = q_maps(s)
        kt = k_ref[0, FRONT - N_META:FRONT, :]
        vtt = vt_ref[0, FRONT // LANES - 1][:, LANES - N_META:]
        for n in range(2):
            update(n, jnp.dot(kt, qs[n], preferred_element_type=F32), vtt)
        lp = lam_ref[...]
        lam = (jnp.exp(jnp.sum(lp[0:1] * lp[1:2], axis=1, keepdims=True))
               - jnp.exp(jnp.sum(lp[2:3] * lp[3:4], axis=1, keepdims=True)) + lam_init)
        o = acc_sc[0] / l_sc[0] - lam * (acc_sc[1] / l_sc[1])
        o = o * lax.rsqrt(jnp.mean(o * o, axis=0, keepdims=True) + EPS) * g_ref[...] * (1.0 - lam_init)
        o_ref[pl.ds(pl.multiple_of(qi_ref[s] * tq, tq), tq), :] = o.T.astype(BF16)

    def block(p, kind_a, kind_b):
        s_a, s_b = 2 * p, 2 * p + 1
        scores_into(s_b, sb_sc)
        consume(s_a, sa_sc, kind_a)
        if kind_a in STEP_EDGE:
            finish(s_a)
        scores_into(jnp.minimum(s_b + 1, n_steps - 1), sa_sc)
        consume(s_b, sb_sc, kind_b)
        if kind_b in STEP_EDGE:
            finish(s_b)

    scores_into(0, sa_sc)

    def body(p, c):
        kind_a, kind_b = kind_ref[2 * p], kind_ref[2 * p + 1]
        for va, vb in block_kinds:
            pl.when(jnp.logical_and(kind_a == va, kind_b == vb))(functools.partial(block, p, va, vb))
        return c

    lax.fori_loop(0, n_steps // 2, body, 0)


def _attention(qt, k, vt4, lam_p, subln_g, bsz, seq, lam_init):
    nq = seq // ATT_TQ
    lp = k.shape[1]
    q_idx, k_idx, kinds = _attn_schedule(nq)
    n_steps = len(kinds)
    assert n_steps % 2 == 0
    block_kinds = sorted(set(zip(kinds[0::2], kinds[1::2])))
    grid_spec = pltpu.PrefetchScalarGridSpec(
        num_scalar_prefetch=3,
        grid=(bsz, N_HEADS),
        in_specs=[
            pl.BlockSpec((1, nq, 2 * HEAD_DIM, ATT_TQ), lambda b, h, *_: (b, 0, h, 0)),
            pl.BlockSpec((1, lp, 2 * HEAD_DIM), lambda b, h, *_: (b, 0, h)),
            pl.BlockSpec((1, lp // LANES, 2 * HEAD_DIM, LANES), lambda b, h, *_: (b, 0, h, 0)),
            pl.BlockSpec((4, HEAD_DIM), lambda b, h, *_: (0, 0)),
            pl.BlockSpec((2 * HEAD_DIM, 1), lambda b, h, *_: (0, 0)),
        ],
        out_specs=pl.BlockSpec((seq, 2 * HEAD_DIM), lambda b, h, *_: (b, h)),
        scratch_shapes=[pltpu.VMEM((2, 1, ATT_TQ), F32), pltpu.VMEM((2, 1, ATT_TQ), F32),
                        pltpu.VMEM((2, 2 * HEAD_DIM, ATT_TQ), F32),
                        pltpu.VMEM((2 * ATT_TK // ATT_SUB, ATT_SUB, ATT_TQ), F32),
                        pltpu.VMEM((2 * ATT_TK // ATT_SUB, ATT_SUB, ATT_TQ), F32)],
    )
    return pl.pallas_call(
        functools.partial(_attn_kernel, lam_init=lam_init, n_steps=n_steps, block_kinds=block_kinds),
        grid_spec=grid_spec,
        out_shape=jax.ShapeDtypeStruct((bsz * seq, D_MODEL), BF16),
        compiler_params=_cparams(("arbitrary", "arbitrary")),
        name="diff_attention",
    )(*(jnp.asarray(col, jnp.int32) for col in (q_idx, k_idx, kinds)),
      qt.reshape(bsz, nq, D_MODEL, ATT_TQ), k, vt4, lam_p, subln_g.reshape(2 * HEAD_DIM, 1))


def _oproj_kernel(h_ref, o_ref, w_ref, rg_ref, wrt_ref, br_ref, tri_ref, out_ref, lpos_ref, gate_ref, meta_ref, tot_ref,
                  seg_sc):
    _route_start(seg_sc)
    h = h_ref[...] + jnp.dot(o_ref[...], w_ref[...], preferred_element_type=F32)
    out_ref[...] = h
    _route(h, rg_ref, wrt_ref, br_ref, tri_ref, lpos_ref, gate_ref, meta_ref, tot_ref, seg_sc)


def _o_proj(h_pad, o, w_out, router, bsz, seq, lp):
    tiles_real = seq // TILE
    r_args, r_specs = _router_operands(*router, TILE)
    ro_specs, ro_shapes, r_scratch = _router_outputs(bsz * seq, TILE)
    return pl.pallas_call(
        _oproj_kernel,
        grid=(bsz * tiles_real,),
        in_specs=[
            pl.BlockSpec((TILE, D_MODEL), _real_row_map(tiles_real, lp // TILE)),
            pl.BlockSpec((TILE, D_MODEL), lambda i: (i, 0)),
            pl.BlockSpec((D_MODEL, D_MODEL), lambda i: (0, 0)),
        ] + r_specs,
        out_specs=[pl.BlockSpec((TILE, D_MODEL), lambda i: (i, 0))] + ro_specs,
        out_shape=[jax.ShapeDtypeStruct((bsz * seq, D_MODEL), F32)] + ro_shapes,
        scratch_shapes=[r_scratch],
        compiler_params=_cparams(("arbitrary",)),
        name="o_proj",
    )(h_pad, o, w_out.astype(BF16), *r_args)


def kernel(x, meta_tokens, a_norm, a_w_in, a_conv, a_w_out, kv_norm, w_kv, b_norm, b_w_q, b_lambda, b_subln, b_w_out, ffn_norm, r_group, r_group_b, r_expert, r_expert_b, e_gate, e_up, e_down, final_norm):
    bsz, seq, d = x.shape
    assert d == D_MODEL and a_norm.shape[0] == 1 and b_norm.shape[0] == 1
    lp = FRONT + seq
    assert seq % ATT_TK == 0 and ATT_TK == 2 * ATT_TQ and ATT_TQ == TILE and ATT_TK % ATT_SUB == 0
    front = jnp.concatenate([jnp.zeros((PAD_ROWS, d), x.dtype), meta_tokens.astype(x.dtype)], axis=0)
    routers = [(ffn_norm[j], r_group[j], r_group_b[j], r_expert[j], r_expert_b[j]) for j in range(2)]
    h, *routing = _mixer(x.reshape(bsz * seq, d), front, a_norm[0], a_w_in[0], a_conv[0], a_w_out[0], routers[0],
                         bsz, lp // TILE)
    h = _moe(h, routing, ffn_norm[0], e_gate, e_up, e_down, 0, final_norm, TILE, False)

    k, vt4 = _kv(h, kv_norm, w_kv, bsz, lp)
    qt = _q_proj(h, b_norm[0], b_w_q[0], bsz, seq, lp)
    lam_init = 0.8 - 0.6 * math.exp(-0.3 * a_norm.shape[0])
    o = _attention(qt, k, vt4, b_lambda[0], b_subln[0], bsz, seq, lam_init)
    h, *routing = _o_proj(h, o, b_w_out[0], routers[1], bsz, seq, lp)
    h = _moe(h, routing, ffn_norm[1], e_gate, e_up, e_down, 1, final_norm, TILE, True)
    return h.reshape(bsz, seq, d)
```

```python
import functools
import math

import jax
import jax.numpy as jnp
from jax import lax
from jax.experimental import pallas as pl
from jax.experimental.pallas import tpu as pltpu

D_MODEL = 1024
CHUNK = 64
N_META = 16
HEAD_DIM = 64
N_HEADS = 8
ROT_HALF = 8
ROPE_THETA = 500000.0
N_GROUPS = 4
EXPERTS_PER_GROUP = 8
N_EXPERTS = 32
D_EXPERT = 512
EPS = 1e-6

LANES = 128
SUBLANES = 8
TILE = 512
FRONT = TILE
PAD_ROWS = FRONT - N_META
EXPERT_ROWS = 512
ATT_TQ = 512
ATT_TK = 1024
ATT_SUB = 512
SEG_ALIGN = 16
STAGE_SLOTS = 2
LOCAL_ROWS = 1536
VMEM_LIMIT = 56 * 1024 * 1024
NEG = float(jnp.finfo(jnp.float32).min)

BF16 = jnp.bfloat16
F32 = jnp.float32


def _rms(x, g):
    return x * lax.rsqrt(jnp.mean(x * x, axis=-1, keepdims=True) + EPS) * g


def _cparams(sem):
    return pltpu.CompilerParams(dimension_semantics=sem, vmem_limit_bytes=VMEM_LIMIT)


def _mixer_kernel(x_ref, front_ref, g_ref, win_ref, cw_ref, wout_ref, rg_ref, wrt_ref, br_ref, tri_ref,
                  o_ref, lpos_ref, gate_ref, meta_ref, tot_ref, cu_ref, seg_sc, *, tiles_per_batch):
    i = pl.program_id(0)
    tm = x_ref.shape[0]
    _route_start(seg_sc)

    @pl.when(i % tiles_per_batch == 0)
    def _():
        cu_ref[0:SUBLANES, :] = jnp.zeros((SUBLANES, D_MODEL), F32)

    x = jnp.where(i % tiles_per_batch == 0, front_ref[...], x_ref[...])
    hn = _rms(x, g_ref[...]).astype(BF16)
    proj = jnp.dot(hn, win_ref[...], preferred_element_type=F32)
    b_gate = proj[:, :D_MODEL]
    cu = proj[:, D_MODEL:2 * D_MODEL] * proj[:, 2 * D_MODEL:]
    cu_ref[SUBLANES:SUBLANES + tm, :] = cu
    cw = cw_ref[...]
    z = (cw[2:3] * cu + cw[1:2] * cu_ref[SUBLANES - 1:SUBLANES - 1 + tm, :]
         + cw[0:1] * cu_ref[SUBLANES - 2:SUBLANES - 2 + tm, :])
    cu_ref[0:SUBLANES, :] = cu_ref[tm:tm + SUBLANES, :]
    y = jnp.dot((b_gate * z).astype(BF16), wout_ref[...], preferred_element_type=F32)
    h = x + y
    o_ref[...] = h
    _route(h, rg_ref, wrt_ref, br_ref, tri_ref, lpos_ref, gate_ref, meta_ref, tot_ref, seg_sc)


def _mixer(x, front, g, w_in, conv_w, w_out, router, bsz, tiles_per_batch):
    t = bsz * tiles_per_batch * TILE
    r_args, r_specs = _router_operands(*router, TILE)
    ro_specs, ro_shapes, r_scratch = _router_outputs(t, TILE)
    tiles_real = tiles_per_batch - FRONT // TILE

    def x_map(i):
        return ((i // tiles_per_batch) * tiles_real + jnp.maximum(i % tiles_per_batch - FRONT // TILE, 0), 0)

    return pl.pallas_call(
        functools.partial(_mixer_kernel, tiles_per_batch=tiles_per_batch),
        grid=(t // TILE,),
        in_specs=[
            pl.BlockSpec((TILE, D_MODEL), x_map),
            pl.BlockSpec((TILE, D_MODEL), lambda i: (0, 0)),
            pl.BlockSpec((1, D_MODEL), lambda i: (0, 0)),
            pl.BlockSpec((D_MODEL, 3 * D_MODEL), lambda i: (0, 0)),
            pl.BlockSpec((3, D_MODEL), lambda i: (0, 0)),
            pl.BlockSpec((D_MODEL, D_MODEL), lambda i: (0, 0)),
        ] + r_specs,
        out_specs=[pl.BlockSpec((TILE, D_MODEL), lambda i: (i, 0))] + ro_specs,
        out_shape=[jax.ShapeDtypeStruct((t, D_MODEL), F32)] + ro_shapes,
        scratch_shapes=[pltpu.VMEM((TILE + 2 * SUBLANES, D_MODEL), F32), r_scratch],
        compiler_params=_cparams(("arbitrary",)),
        name="mixer_a",
    )(x, front, g.reshape(1, D_MODEL), w_in.astype(BF16), conv_w, w_out.astype(BF16), *r_args)


ROUTER_ROWS = SUBLANES + N_EXPERTS


def _round_up(v, m):
    return jnp.floor((v + (m - 1)) * (1.0 / m)) * m


def _route_start(seg_sc):
    @pl.when(pl.program_id(0) == 0)
    def _():
        seg_sc[...] = jnp.zeros_like(seg_sc)


def _route(h, g_ref, wrt_ref, br_ref, tri_ref, lpos_ref, gate_ref, meta_ref, tot_ref, seg_sc):
    i = pl.program_id(0)
    tm = h.shape[0]
    xn = _rms(h, g_ref[...])
    lt = lax.dot_general(wrt_ref[...], xn, (((1,), (1,)), ((), ())),
                         precision=lax.Precision.HIGHEST, preferred_element_type=F32) + br_ref[...]
    lg = lt[0:N_GROUPS]
    gmax = jnp.max(lg, axis=0, keepdims=True)
    iota_g = lax.broadcasted_iota(jnp.int32, lg.shape, 0)
    grp = jnp.min(jnp.where(lg == gmax, iota_g, N_GROUPS), axis=0, keepdims=True)
    p_grp = 1.0 / jnp.sum(jnp.exp(lg - gmax), axis=0, keepdims=True)
    le = lt[SUBLANES:SUBLANES + EXPERTS_PER_GROUP]
    for g in range(1, N_GROUPS):
        lo = SUBLANES + g * EXPERTS_PER_GROUP
        le = jnp.where(grp == g, lt[lo:lo + EXPERTS_PER_GROUP], le)
    iota_e = lax.broadcasted_iota(jnp.int32, le.shape, 0)
    v1 = jnp.max(le, axis=0, keepdims=True)
    i1 = jnp.min(jnp.where(le == v1, iota_e, EXPERTS_PER_GROUP), axis=0, keepdims=True)
    le2 = jnp.where(iota_e == i1, -jnp.inf, le)
    v2 = jnp.max(le2, axis=0, keepdims=True)
    i2 = jnp.min(jnp.where(le2 == v2, iota_e, EXPERTS_PER_GROUP), axis=0, keepdims=True)
    e2 = jnp.exp(v2 - v1)
    den = 1.0 + e2
    gate_ref[...] = jnp.concatenate([p_grp * (1.0 / den), p_grp * (e2 / den)], axis=0)
    experts = (grp * EXPERTS_PER_GROUP + i1, grp * EXPERTS_PER_GROUP + i2)

    iota_x = lax.broadcasted_iota(jnp.int32, (N_EXPERTS, tm), 0)
    hits = [iota_x == e_k for e_k in experts]
    hits_f = [jnp.where(hit, 1.0, 0.0) for hit in hits]
    hits_b = [hf.astype(BF16) for hf in hits_f]
    excl = [jnp.dot(hb, tri_ref[...], preferred_element_type=F32) for hb in hits_b]
    tot_col = [jnp.sum(hf, axis=1, keepdims=True) for hf in hits_f]
    ones = jnp.ones((SUBLANES, tm), BF16)
    n_lane = sum(lax.dot_general(ones, hb, (((1,), (1,)), ((), ())), preferred_element_type=F32) for hb in hits_b)
    seg_len_lane = _round_up(n_lane[0:1], SEG_ALIGN)
    seg_len = _round_up(tot_col[0] + tot_col[1], SEG_ALIGN)
    before = (lax.broadcasted_iota(jnp.int32, (N_EXPERTS, N_EXPERTS), 1)
              < lax.broadcasted_iota(jnp.int32, (N_EXPERTS, N_EXPERTS), 0))
    local_off = jnp.sum(jnp.where(before, seg_len_lane, 0.0), axis=1, keepdims=True)

    lpos_a = jnp.sum(jnp.where(hits[0], excl[0] + local_off, 0.0), axis=0, keepdims=True).astype(jnp.int32)
    lpos_b = jnp.sum(jnp.where(hits[1], excl[1] + tot_col[0] + local_off, 0.0), axis=0, keepdims=True).astype(jnp.int32)
    lpos_ref[...] = jnp.concatenate([lpos_a, lpos_b], axis=0)

    seg_before = seg_sc[...]
    seg_sc[...] = seg_before + seg_len
    lane = lax.broadcasted_iota(jnp.int32, (N_EXPERTS, LANES), 1)
    meta = jnp.where(lane == 0, seg_len, jnp.where(lane == 1, local_off, jnp.where(lane == 2, seg_before, 0.0)))
    meta_ref[0] = meta.astype(jnp.int32)
    tot_ref[...] = jnp.broadcast_to(seg_before + seg_len, tot_ref.shape).astype(jnp.int32)


def _router_operands(g, w_rg, b_rg, w_re, b_re, tm):
    wrt = jnp.zeros((ROUTER_ROWS, D_MODEL), F32)
    wrt = wrt.at[0:N_GROUPS].set(w_rg.T).at[SUBLANES:].set(w_re.T)
    br = jnp.zeros((ROUTER_ROWS, 1), F32)
    br = br.at[0:N_GROUPS, 0].set(b_rg).at[SUBLANES:, 0].set(b_re)
    tri = (lax.broadcasted_iota(jnp.int32, (tm, tm), 0) < lax.broadcasted_iota(jnp.int32, (tm, tm), 1)).astype(BF16)
    specs = [
        pl.BlockSpec((1, D_MODEL), lambda i: (0, 0)),
        pl.BlockSpec((ROUTER_ROWS, D_MODEL), lambda i: (0, 0)),
        pl.BlockSpec((ROUTER_ROWS, 1), lambda i: (0, 0)),
        pl.BlockSpec((tm, tm), lambda i: (0, 0)),
    ]
    return [g.reshape(1, D_MODEL), wrt, br, tri], specs


def _router_outputs(t, tm):
    nt = t // tm
    specs = [
        pl.BlockSpec((2, tm), lambda i: (0, i)),
        pl.BlockSpec((2, tm), lambda i: (0, i)),
        pl.BlockSpec((1, N_EXPERTS, LANES), lambda i: (i, 0, 0)),
        pl.BlockSpec((N_EXPERTS, LANES), lambda i: (0, 0)),
    ]
    shapes = [
        jax.ShapeDtypeStruct((2, t), jnp.int32),
        jax.ShapeDtypeStruct((2, t), F32),
        jax.ShapeDtypeStruct((nt, N_EXPERTS, LANES), jnp.int32),
        jax.ShapeDtypeStruct((N_EXPERTS, LANES), jnp.int32),
    ]
    return specs, shapes, pltpu.VMEM((N_EXPERTS, 1), F32)


def _if_rows(n, fn):
    pl.when(n > 0)(lambda: fn(pl.multiple_of(n, SEG_ALIGN)))


def _rows(ref, start, size):
    return ref.at[pl.ds(pl.multiple_of(start, SEG_ALIGN), size), :]


def _dispatch_kernel(len_ref, off_ref, dst_ref, zs_ref, zn_ref, nu_ref, h_ref, g_ref, lpos_ref, buf_ref, stage, zero_sc,
                     sem_out, sem_fill, *, n_blk):
    t = pl.program_id(0)
    nt = pl.num_programs(0)
    tm = h_ref.shape[0]
    slot = t % STAGE_SLOTS

    def tile_out(t, s, op):
        def body(e, c):
            idx = t * N_EXPERTS + e
            lo, dst = off_ref[idx], dst_ref[idx]
            _if_rows(len_ref[idx], lambda n: op(pltpu.make_async_copy(
                _rows(stage.at[s], lo, n), _rows(buf_ref, dst, n), sem_out.at[s])))
            return c

        lax.fori_loop(0, N_EXPERTS, body, 0)

    def fill_copies(op):
        def tails(e, c):
            _if_rows(zn_ref[e], lambda n: op(pltpu.make_async_copy(
                zero_sc.at[pl.ds(0, n), :], _rows(buf_ref, zs_ref[e], n), sem_fill)))
            return c

        lax.fori_loop(0, N_EXPERTS, tails, 0)

        def blocks(b, c):
            op(pltpu.make_async_copy(zero_sc, _rows(buf_ref, b * EXPERT_ROWS, EXPERT_ROWS), sem_fill))
            return c

        lax.fori_loop(nu_ref[0], n_blk, blocks, 0)

    start = lambda cp: cp.start()
    wait = lambda cp: cp.wait()

    @pl.when(t == 0)
    def _():
        zero_sc[...] = jnp.zeros_like(zero_sc)
        fill_copies(start)

    @pl.when(t >= STAGE_SLOTS)
    def _():
        tile_out(t - STAGE_SLOTS, slot, wait)

    xn = _rms(h_ref[...], g_ref[...]).astype(BF16)
    lpos = lpos_ref[...]
    row = lax.broadcasted_iota(jnp.int32, (LOCAL_ROWS, tm), 0)
    perm = jnp.where(row == lpos[0:1], 1.0, jnp.where(row == lpos[1:2], 1.0, 0.0)).astype(BF16)
    stage[slot] = jnp.dot(perm, xn, preferred_element_type=F32).astype(BF16)
    tile_out(t, slot, start)

    @pl.when(t == nt - 1)
    def _():
        for back in range(STAGE_SLOTS - 1, -1, -1):
            @pl.when(t >= back)
            def _(back=back):
                tile_out(t - back, (t - back) % STAGE_SLOTS, wait)

        fill_copies(wait)


def _dispatch(h, g, lpos, seg_len, seg_off, seg_dst, tail_start, tail_len, n_used, n_blk, tm):
    t = h.shape[0]
    grid_spec = pltpu.PrefetchScalarGridSpec(
        num_scalar_prefetch=6,
        grid=(t // tm,),
        in_specs=[
            pl.BlockSpec((tm, D_MODEL), lambda i, *_: (i, 0)),
            pl.BlockSpec((1, D_MODEL), lambda i, *_: (0, 0)),
            pl.BlockSpec((2, tm), lambda i, *_: (0, i)),
        ],
        out_specs=pl.BlockSpec(memory_space=pl.ANY),
        scratch_shapes=[pltpu.VMEM((STAGE_SLOTS, LOCAL_ROWS, D_MODEL), BF16), pltpu.VMEM((EXPERT_ROWS, D_MODEL), BF16),
                        pltpu.SemaphoreType.DMA((STAGE_SLOTS,)), pltpu.SemaphoreType.DMA],
    )
    return pl.pallas_call(
        functools.partial(_dispatch_kernel, n_blk=n_blk),
        grid_spec=grid_spec,
        out_shape=jax.ShapeDtypeStruct((n_blk * EXPERT_ROWS, D_MODEL), BF16),
        compiler_params=_cparams(("arbitrary",)),
        name="dispatch",
    )(seg_len, seg_off, seg_dst, tail_start, tail_len, n_used, h, g.reshape(1, D_MODEL), lpos)


def _expert_kernel(be_ref, nu_ref, x_ref, wg_ref, wu_ref, wd_ref, y_ref, wgu_sc, wd_sc):
    i = pl.program_id(0)
    e = be_ref[i]
    prev = be_ref[jnp.maximum(i - 1, 0)]

    @pl.when(jnp.logical_or(i == 0, e != prev))
    def _():
        wgu_sc[:, :D_EXPERT] = wg_ref[0, 0].astype(BF16)
        wgu_sc[:, D_EXPERT:] = wu_ref[0, 0].astype(BF16)
        wd_sc[...] = wd_ref[0, 0].astype(BF16)

    @pl.when(i < nu_ref[0])
    def _():
        gu = jnp.dot(x_ref[...], wgu_sc[...], preferred_element_type=F32)
        a = jax.nn.silu(gu[:, :D_EXPERT]) * gu[:, D_EXPERT:]
        y_ref[...] = jnp.dot(a.astype(BF16), wd_sc[...], preferred_element_type=F32).astype(BF16)

    @pl.when(i >= nu_ref[0])
    def _():
        y_ref[...] = jnp.zeros_like(y_ref)


def _experts(buf, blk_expert, n_used, w_gate, w_up, w_down, layer):
    n_blk = blk_expert.shape[0]

    def xmap(i, be, nu):
        return (jnp.maximum(jnp.minimum(i, nu[0] - 1), 0), 0)

    grid_spec = pltpu.PrefetchScalarGridSpec(
        num_scalar_prefetch=2,
        grid=(n_blk,),
        in_specs=[
            pl.BlockSpec((EXPERT_ROWS, D_MODEL), xmap),
            pl.BlockSpec((1, 1, D_MODEL, D_EXPERT), lambda i, be, nu: (layer, be[i], 0, 0)),
            pl.BlockSpec((1, 1, D_MODEL, D_EXPERT), lambda i, be, nu: (layer, be[i], 0, 0)),
            pl.BlockSpec((1, 1, D_EXPERT, D_MODEL), lambda i, be, nu: (layer, be[i], 0, 0)),
        ],
        out_specs=pl.BlockSpec((EXPERT_ROWS, D_MODEL), lambda i, be, nu: (i, 0)),
        scratch_shapes=[pltpu.VMEM((D_MODEL, 2 * D_EXPERT), BF16), pltpu.VMEM((D_EXPERT, D_MODEL), BF16)],
    )
    return pl.pallas_call(
        _expert_kernel,
        grid_spec=grid_spec,
        out_shape=jax.ShapeDtypeStruct(buf.shape, BF16),
        compiler_params=_cparams(("arbitrary",)),
        name="experts",
    )(blk_expert, n_used, buf, w_gate, w_up, w_down)


def _combine_kernel(len_ref, off_ref, dst_ref, h_ref, gate_ref, lpos_ref, y_ref, fn_ref, o_ref, yl, sem, *, final_norm):
    i = pl.program_id(0)
    nt = pl.num_programs(0)
    tm = h_ref.shape[0]
    slot = i % 2

    def tile_copies(t, s, op):
        def body(e, c):
            idx = t * N_EXPERTS + e
            lo, src = off_ref[idx], dst_ref[idx]
            _if_rows(len_ref[idx], lambda n: op(pltpu.make_async_copy(
                _rows(y_ref, src, n), _rows(yl.at[s], lo, n), sem.at[s])))
            return c

        lax.fori_loop(0, N_EXPERTS, body, 0)

    @pl.when(i == 0)
    def _():
        yl[...] = jnp.zeros_like(yl)
        tile_copies(0, 0, lambda cp: cp.start())

    @pl.when(i + 1 < nt)
    def _():
        tile_copies(i + 1, 1 - slot, lambda cp: cp.start())

    tile_copies(i, slot, lambda cp: cp.wait())

    rows = yl[slot]
    lpos = lpos_ref[...]
    col = lax.broadcasted_iota(jnp.int32, (tm, LOCAL_ROWS), 1)
    g = gate_ref[...]
    weights = jnp.where(col == lpos[:, 0:1], g[:, 0:1], jnp.where(col == lpos[:, 1:2], g[:, 1:2], 0.0)).astype(BF16)
    out = h_ref[...] + jnp.dot(weights, rows, preferred_element_type=F32)
    if final_norm:
        out = _rms(out, fn_ref[...])
    o_ref[...] = out


def _combine(h, y, seg_len, seg_off, seg_dst, lpos, gate, fn, tm, final_norm):
    t = h.shape[0]
    grid_spec = pltpu.PrefetchScalarGridSpec(
        num_scalar_prefetch=3,
        grid=(t // tm,),
        in_specs=[
            pl.BlockSpec((tm, D_MODEL), lambda i, *_: (i, 0)),
            pl.BlockSpec((tm, 2), lambda i, *_: (i, 0)),
            pl.BlockSpec((tm, 2), lambda i, *_: (i, 0)),
            pl.BlockSpec(memory_space=pl.ANY),
            pl.BlockSpec((1, D_MODEL), lambda i, *_: (0, 0)),
        ],
        out_specs=pl.BlockSpec((tm, D_MODEL), lambda i, *_: (i, 0)),
        scratch_shapes=[pltpu.VMEM((2, LOCAL_ROWS, D_MODEL), BF16), pltpu.SemaphoreType.DMA((2,))],
    )
    return pl.pallas_call(
        functools.partial(_combine_kernel, final_norm=final_norm),
        grid_spec=grid_spec,
        out_shape=jax.ShapeDtypeStruct((t, D_MODEL), F32),
        compiler_params=_cparams(("arbitrary",)),
        name="combine",
    )(seg_len, seg_off, seg_dst, h, gate.T, lpos.T, y, fn.reshape(1, D_MODEL))


def _moe(h, routing, g, w_gate, w_up, w_down, layer, fn, tm, final_norm):
    t = h.shape[0]
    nt = t // tm
    lpos, gate, meta, tot = routing
    n_blk = (2 * t + nt * N_EXPERTS * (SEG_ALIGN - 1)) // EXPERT_ROWS + N_EXPERTS
    total = tot[:, 0]
    region = (total + EXPERT_ROWS - 1) // EXPERT_ROWS * EXPERT_ROWS
    ends = jnp.cumsum(region)
    starts = ends - region
    seg_len = meta[:, :, 0].reshape(-1)
    seg_off = meta[:, :, 1].reshape(-1)
    seg_dst = (meta[:, :, 2] + starts[None, :]).reshape(-1)
    blk_start = jnp.arange(n_blk, dtype=jnp.int32) * EXPERT_ROWS
    blk_expert = jnp.minimum(jnp.sum(blk_start[:, None] >= ends[None, :], axis=1), N_EXPERTS - 1).astype(jnp.int32)
    n_used = (ends[-1:] // EXPERT_ROWS).astype(jnp.int32)
    buf = _dispatch(h, g, lpos, seg_len, seg_off, seg_dst, starts + total, region - total, n_used, n_blk, tm)
    y = _experts(buf, blk_expert, n_used, w_gate, w_up, w_down, layer)
    return _combine(h, y, seg_len, seg_off, seg_dst, lpos, gate, fn, tm, final_norm)


def _kv_kernel(h_ref, g_ref, wk_ref, wvt_ref, cos_ref, sa_ref, sb_ref, k_ref, vt_ref):
    hn = _rms(h_ref[...], g_ref[...]).astype(BF16)
    k = jnp.dot(hn, wk_ref[...], preferred_element_type=F32)
    cos, sa, sb = cos_ref[...], sa_ref[...], sb_ref[...]
    for c in range(D_MODEL // LANES):
        kc = k[:, c * LANES:(c + 1) * LANES]
        rot = kc * cos + pltpu.roll(kc, LANES - ROT_HALF, 1) * sa + pltpu.roll(kc, ROT_HALF, 1) * sb
        k_ref[0, :, c * LANES:(c + 1) * LANES] = rot.astype(BF16)
    vt = lax.dot_general(wvt_ref[...], hn, (((1,), (1,)), ((), ())), preferred_element_type=F32)
    for c in range(vt_ref.shape[1]):
        vt_ref[0, c] = vt[:, c * LANES:(c + 1) * LANES].astype(BF16)


def _q_kernel(h_ref, g_ref, wqt_ref, cos_ref, sin_ref, qt_ref):
    hn = _rms(h_ref[...], g_ref[...]).astype(BF16)
    qt = lax.dot_general(wqt_ref[...], hn, (((1,), (1,)), ((), ())), preferred_element_type=F32)
    cos, sin = cos_ref[...], sin_ref[...]
    scale = HEAD_DIM ** -0.5 * math.log2(math.e)
    for c in range(D_MODEL // HEAD_DIM):
        lo = c * HEAD_DIM
        x1 = qt[lo:lo + ROT_HALF]
        x2 = qt[lo + ROT_HALF:lo + 2 * ROT_HALF]
        blk = jnp.concatenate([x1 * cos - x2 * sin, x2 * cos + x1 * sin, qt[lo + 2 * ROT_HALF:lo + HEAD_DIM]], axis=0)
        qt_ref[0, lo:lo + HEAD_DIM, :] = (blk * scale).astype(BF16)


def _real_row_map(tiles_real, tiles_padded):
    front = FRONT // TILE
    return lambda i: ((i // tiles_real) * tiles_padded + front + i % tiles_real, 0)


def _kvq_kernel(h_ref, gk_ref, wk_ref, wvt_ref, cos_ref, sa_ref, sb_ref, gq_ref, wqt_ref, qcos_ref, qsin_ref,
                k_ref, vt_ref, qt_ref, *, tiles_per_batch):
    _kv_kernel(h_ref, gk_ref, wk_ref, wvt_ref, cos_ref, sa_ref, sb_ref, k_ref, vt_ref)

    @pl.when(pl.program_id(0) % tiles_per_batch >= FRONT // TILE)
    def _():
        _q_kernel(h_ref, gq_ref, wqt_ref, qcos_ref, qsin_ref, qt_ref)


def _kvq_proj(h, g_kv, w_kv, g_q, w_q, bsz, seq, lp):
    tpb = lp // TILE
    tiles_real = seq // TILE
    kblk = TILE // LANES
    inv_freq = ROPE_THETA ** (-jnp.arange(ROT_HALF, dtype=F32) * 2.0 / (2 * ROT_HALF))
    pos = jnp.maximum(jnp.arange(lp) - PAD_ROWS, 0).astype(F32)
    ang = pos[:, None] * inv_freq[None, :]
    r = jnp.arange(LANES) % HEAD_DIM
    cos_t = jnp.where(r < 2 * ROT_HALF, jnp.cos(ang)[:, r % ROT_HALF], 1.0)
    sin_t = jnp.sin(ang)[:, r % ROT_HALF]
    sa = jnp.where(r < ROT_HALF, -sin_t, 0.0)
    sb = jnp.where((r >= ROT_HALF) & (r < 2 * ROT_HALF), sin_t, 0.0)
    tab = pl.BlockSpec((TILE, LANES), lambda i: (i % tpb, 0))
    qang = inv_freq[:, None] * (N_META + jnp.arange(seq)).astype(F32)[None, :]

    def real_tile(i):
        return jnp.maximum(i % tpb - FRONT // TILE, 0)

    qtab = pl.BlockSpec((ROT_HALF, TILE), lambda i: (0, real_tile(i)))
    weight = pl.BlockSpec((D_MODEL, D_MODEL), lambda i: (0, 0))
    gain = pl.BlockSpec((1, D_MODEL), lambda i: (0, 0))
    return pl.pallas_call(
        functools.partial(_kvq_kernel, tiles_per_batch=tpb),
        grid=(bsz * tpb,),
        in_specs=[pl.BlockSpec((TILE, D_MODEL), lambda i: (i, 0)), gain, weight, weight, tab, tab, tab,
                  gain, weight, qtab, qtab],
        out_specs=[
            pl.BlockSpec((1, TILE, D_MODEL), lambda i: (i // tpb, i % tpb, 0)),
            pl.BlockSpec((1, kblk, D_MODEL, LANES), lambda i: (i // tpb, i % tpb, 0, 0)),
            pl.BlockSpec((1, D_MODEL, TILE), lambda i: ((i // tpb) * tiles_real + real_tile(i), 0, 0)),
        ],
        out_shape=[
            jax.ShapeDtypeStruct((bsz, lp, D_MODEL), BF16),
            jax.ShapeDtypeStruct((bsz, lp // LANES, D_MODEL, LANES), BF16),
            jax.ShapeDtypeStruct((bsz * tiles_real, D_MODEL, TILE), BF16),
        ],
        compiler_params=_cparams(("arbitrary",)),
        name="kvq_proj",
    )(h, g_kv.reshape(1, D_MODEL), w_kv[:, :D_MODEL].astype(BF16), w_kv[:, D_MODEL:].T.astype(BF16), cos_t, sa, sb,
      g_q.reshape(1, D_MODEL), w_q.T.astype(BF16), jnp.cos(qang), jnp.sin(qang))


STEP_FULL = 0
STEP_EDGE = (1, 2)


def _attn_schedule(nq):
    per_k = ATT_TK // ATT_TQ
    steps = [(i, t, STEP_FULL if t < i // per_k else STEP_EDGE[i % per_k]) for i in range(nq) for t in range(i // per_k + 1)]
    return [list(col) for col in zip(*steps)]


def _attn_kernel(qi_ref, kt_ref, kind_ref, qt_ref, k_ref, vt_ref, lam_ref, g_ref, o_ref, m_sc, l_sc, acc_sc, sa_sc, sb_sc,
                 *, lam_init, n_steps, block_kinds):
    tq = ATT_TQ
    zero = jnp.zeros((HEAD_DIM, tq), BF16)
    n_sub = ATT_TK // ATT_SUB

    def q_maps(s):
        qt = qt_ref[0, qi_ref[s]]
        return (jnp.concatenate([qt[0:HEAD_DIM], zero], axis=0), jnp.concatenate([zero, qt[HEAD_DIM:]], axis=0))

    def update(n, sc, vtt, fresh=None):
        m_old, l_old, acc_old = m_sc[n], l_sc[n], acc_sc[n]
        if fresh is not None:
            m_old = jnp.where(fresh, NEG, m_old)
            l_old = jnp.where(fresh, 0.0, l_old)
            acc_old = jnp.where(fresh, 0.0, acc_old)
        m_new = jnp.maximum(m_old, jnp.max(sc, axis=0, keepdims=True))
        alpha = jnp.exp2(m_old - m_new)
        p = jnp.exp2(sc - m_new)
        l_sc[n] = alpha * l_old + jnp.sum(p, axis=0, keepdims=True)
        acc_sc[n] = alpha * acc_old + jnp.dot(vtt, p.astype(BF16), preferred_element_type=F32)
        m_sc[n] = m_new

    def values(row0, nkeys):
        blk0 = row0 // LANES
        return jnp.concatenate([vt_ref[0, blk0 + c] for c in range(nkeys // LANES)], axis=1)

    def key_row(s, u):
        return pl.multiple_of(FRONT + kt_ref[s] * ATT_TK + u * ATT_SUB, LANES)

    def visible_parts(kind):
        return STEP_EDGE.index(kind) + 1 if kind in STEP_EDGE else n_sub

    def scores_into(s, buf, parts=n_sub):
        qs = q_maps(s)
        for u in range(parts):
            kt = k_ref[0, pl.ds(key_row(s, u), ATT_SUB), :]
            for n in range(2):
                buf[u * 2 + n] = jnp.dot(kt, qs[n], preferred_element_type=F32)

    def consume(s, buf, kind):
        fresh = kt_ref[s] == 0
        edge = STEP_EDGE.index(kind) if kind in STEP_EDGE else None
        for u in range(n_sub) if edge is None else range(edge + 1):
            vtt = values(key_row(s, u), ATT_SUB)
            for n in range(2):
                sc = buf[u * 2 + n]
                if u == edge:
                    r = lax.broadcasted_iota(jnp.int32, (ATT_SUB, tq), 0) // CHUNK
                    c = lax.broadcasted_iota(jnp.int32, (ATT_SUB, tq), 1) // CHUNK
                    sc = jnp.where(r <= c, sc, NEG)
                update(n, sc, vtt, fresh if u == 0 else None)

    def finish(s):
        qs = q_maps(s)
        kt = k_ref[0, FRONT - N_META:FRONT, :]
        vtt = vt_ref[0, FRONT // LANES - 1][:, LANES - N_META:]
        for n in range(2):
            update(n, jnp.dot(kt, qs[n], preferred_element_type=F32), vtt)
        lp = lam_ref[...]
        lam = (jnp.exp(jnp.sum(lp[0:1] * lp[1:2], axis=1, keepdims=True))
               - jnp.exp(jnp.sum(lp[2:3] * lp[3:4], axis=1, keepdims=True)) + lam_init)
        o = acc_sc[0] / l_sc[0] - lam * (acc_sc[1] / l_sc[1])
        o = o * lax.rsqrt(jnp.mean(o * o, axis=0, keepdims=True) + EPS) * g_ref[...] * (1.0 - lam_init)
        o_ref[pl.ds(pl.multiple_of(qi_ref[s] * tq, tq), tq), :] = o.T.astype(BF16)

    def block(p, kind_a, kind_b):
        s_a, s_b = 2 * p, 2 * p + 1
        scores_into(s_b, sb_sc, visible_parts(kind_b))
        consume(s_a, sa_sc, kind_a)
        if kind_a in STEP_EDGE:
            finish(s_a)
        scores_into(jnp.minimum(s_b + 1, n_steps - 1), sa_sc)
        consume(s_b, sb_sc, kind_b)
        if kind_b in STEP_EDGE:
            finish(s_b)

    scores_into(0, sa_sc)

    def body(p, c):
        kind_a, kind_b = kind_ref[2 * p], kind_ref[2 * p + 1]
        for va, vb in block_kinds:
            pl.when(jnp.logical_and(kind_a == va, kind_b == vb))(functools.partial(block, p, va, vb))
        return c

    lax.fori_loop(0, n_steps // 2, body, 0)


def _attention(qt, k, vt4, lam_p, subln_g, bsz, seq, lam_init):
    nq = seq // ATT_TQ
    lp = k.shape[1]
    q_idx, k_idx, kinds = _attn_schedule(nq)
    n_steps = len(kinds)
    assert n_steps % 2 == 0
    block_kinds = sorted(set(zip(kinds[0::2], kinds[1::2])))
    grid_spec = pltpu.PrefetchScalarGridSpec(
        num_scalar_prefetch=3,
        grid=(bsz, N_HEADS),
        in_specs=[
            pl.BlockSpec((1, nq, 2 * HEAD_DIM, ATT_TQ), lambda b, h, *_: (b, 0, h, 0)),
            pl.BlockSpec((1, lp, 2 * HEAD_DIM), lambda b, h, *_: (b, 0, h)),
            pl.BlockSpec((1, lp // LANES, 2 * HEAD_DIM, LANES), lambda b, h, *_: (b, 0, h, 0)),
            pl.BlockSpec((4, HEAD_DIM), lambda b, h, *_: (0, 0)),
            pl.BlockSpec((2 * HEAD_DIM, 1), lambda b, h, *_: (0, 0)),
        ],
        out_specs=pl.BlockSpec((seq, 2 * HEAD_DIM), lambda b, h, *_: (b, h)),
        scratch_shapes=[pltpu.VMEM((2, 1, ATT_TQ), F32), pltpu.VMEM((2, 1, ATT_TQ), F32),
                        pltpu.VMEM((2, 2 * HEAD_DIM, ATT_TQ), F32),
                        pltpu.VMEM((2 * ATT_TK // ATT_SUB, ATT_SUB, ATT_TQ), F32),
                        pltpu.VMEM((2 * ATT_TK // ATT_SUB, ATT_SUB, ATT_TQ), F32)],
    )
    return pl.pallas_call(
        functools.partial(_attn_kernel, lam_init=lam_init, n_steps=n_steps, block_kinds=block_kinds),
        grid_spec=grid_spec,
        out_shape=jax.ShapeDtypeStruct((bsz * seq, D_MODEL), BF16),
        compiler_params=_cparams(("arbitrary", "arbitrary")),
        name="diff_attention",
    )(*(jnp.asarray(col, jnp.int32) for col in (q_idx, k_idx, kinds)),
      qt.reshape(bsz, nq, D_MODEL, ATT_TQ), k, vt4, lam_p, subln_g.reshape(2 * HEAD_DIM, 1))


def _oproj_kernel(h_ref, o_ref, w_ref, rg_ref, wrt_ref, br_ref, tri_ref, out_ref, lpos_ref, gate_ref, meta_ref, tot_ref,
                  seg_sc):
    _route_start(seg_sc)
    h = h_ref[...] + jnp.dot(o_ref[...], w_ref[...], preferred_element_type=F32)
    out_ref[...] = h
    _route(h, rg_ref, wrt_ref, br_ref, tri_ref, lpos_ref, gate_ref, meta_ref, tot_ref, seg_sc)


def _o_proj(h_pad, o, w_out, router, bsz, seq, lp):
    tiles_real = seq // TILE
    r_args, r_specs = _router_operands(*router, TILE)
    ro_specs, ro_shapes, r_scratch = _router_outputs(bsz * seq, TILE)
    return pl.pallas_call(
        _oproj_kernel,
        grid=(bsz * tiles_real,),
        in_specs=[
            pl.BlockSpec((TILE, D_MODEL), _real_row_map(tiles_real, lp // TILE)),
            pl.BlockSpec((TILE, D_MODEL), lambda i: (i, 0)),
            pl.BlockSpec((D_MODEL, D_MODEL), lambda i: (0, 0)),
        ] + r_specs,
        out_specs=[pl.BlockSpec((TILE, D_MODEL), lambda i: (i, 0))] + ro_specs,
        out_shape=[jax.ShapeDtypeStruct((bsz * seq, D_MODEL), F32)] + ro_shapes,
        scratch_shapes=[r_scratch],
        compiler_params=_cparams(("arbitrary",)),
        name="o_proj",
    )(h_pad, o, w_out.astype(BF16), *r_args)


def kernel(x, meta_tokens, a_norm, a_w_in, a_conv, a_w_out, kv_norm, w_kv, b_norm, b_w_q, b_lambda, b_subln, b_w_out, ffn_norm, r_group, r_group_b, r_expert, r_expert_b, e_gate, e_up, e_down, final_norm):
    bsz, seq, d = x.shape
    assert d == D_MODEL and a_norm.shape[0] == 1 and b_norm.shape[0] == 1
    lp = FRONT + seq
    assert seq % ATT_TK == 0 and ATT_TK == 2 * ATT_TQ and ATT_TQ == TILE and ATT_TK % ATT_SUB == 0
    front = jnp.concatenate([jnp.zeros((PAD_ROWS, d), x.dtype), meta_tokens.astype(x.dtype)], axis=0)
    routers = [(ffn_norm[j], r_group[j], r_group_b[j], r_expert[j], r_expert_b[j]) for j in range(2)]
    h, *routing = _mixer(x.reshape(bsz * seq, d), front, a_norm[0], a_w_in[0], a_conv[0], a_w_out[0], routers[0],
                         bsz, lp // TILE)
    h = _moe(h, routing, ffn_norm[0], e_gate, e_up, e_down, 0, final_norm, TILE, False)

    k, vt4, qt = _kvq_proj(h, kv_norm, w_kv, b_norm[0], b_w_q[0], bsz, seq, lp)
    lam_init = 0.8 - 0.6 * math.exp(-0.3 * a_norm.shape[0])
    o = _attention(qt, k, vt4, b_lambda[0], b_subln[0], bsz, seq, lam_init)
    h, *routing = _o_proj(h, o, b_w_out[0], routers[1], bsz, seq, lp)
    h = _moe(h, routing, ffn_norm[1], e_gate, e_up, e_down, 1, final_norm, TILE, True)
    return h.reshape(bsz, seq, d)
```

```python
import functools
import math

import jax
import jax.numpy as jnp
from jax import lax
from jax.experimental import pallas as pl
from jax.experimental.pallas import tpu as pltpu

D_MODEL = 1024
CHUNK = 64
N_META = 16
HEAD_DIM = 64
N_HEADS = 8
ROT_HALF = 8
ROPE_THETA = 500000.0
N_GROUPS = 4
EXPERTS_PER_GROUP = 8
N_EXPERTS = 32
D_EXPERT = 512
EPS = 1e-6

LANES = 128
SUBLANES = 8
TILE = 512
FRONT = TILE
PAD_ROWS = FRONT - N_META
EXPERT_ROWS = 512
ATT_TQ = 512
ATT_TK = 1024
ATT_SUB = 512
SEG_ALIGN = 16
STAGE_SLOTS = 2
LOCAL_ROWS = 1536
VMEM_LIMIT = 56 * 1024 * 1024
NEG = float(jnp.finfo(jnp.float32).min)

BF16 = jnp.bfloat16
F32 = jnp.float32


def _rms(x, g):
    return x * lax.rsqrt(jnp.mean(x * x, axis=-1, keepdims=True) + EPS) * g


def _cparams(sem):
    return pltpu.CompilerParams(dimension_semantics=sem, vmem_limit_bytes=VMEM_LIMIT)


def _mixer_kernel(x_ref, front_ref, g_ref, win_ref, cw_ref, wout_ref, rg_ref, wrt_ref, br_ref, tri_ref,
                  o_ref, lpos_ref, gate_ref, meta_ref, tot_ref, cu_ref, seg_sc, *, tiles_per_batch):
    i = pl.program_id(0)
    tm = x_ref.shape[0]
    _route_start(seg_sc)

    @pl.when(i % tiles_per_batch == 0)
    def _():
        cu_ref[0:SUBLANES, :] = jnp.zeros((SUBLANES, D_MODEL), F32)

    x = jnp.where(i % tiles_per_batch == 0, front_ref[...], x_ref[...])
    hn = _rms(x, g_ref[...]).astype(BF16)
    proj = jnp.dot(hn, win_ref[...], preferred_element_type=F32)
    b_gate = proj[:, :D_MODEL]
    cu = proj[:, D_MODEL:2 * D_MODEL] * proj[:, 2 * D_MODEL:]
    cu_ref[SUBLANES:SUBLANES + tm, :] = cu
    cw = cw_ref[...]
    z = (cw[2:3] * cu + cw[1:2] * cu_ref[SUBLANES - 1:SUBLANES - 1 + tm, :]
         + cw[0:1] * cu_ref[SUBLANES - 2:SUBLANES - 2 + tm, :])
    cu_ref[0:SUBLANES, :] = cu_ref[tm:tm + SUBLANES, :]
    y = jnp.dot((b_gate * z).astype(BF16), wout_ref[...], preferred_element_type=F32)
    h = x + y
    o_ref[...] = h
    _route(h, rg_ref, wrt_ref, br_ref, tri_ref, lpos_ref, gate_ref, meta_ref, tot_ref, seg_sc)


def _mixer(x, front, g, w_in, conv_w, w_out, router, bsz, tiles_per_batch):
    t = bsz * tiles_per_batch * TILE
    r_args, r_specs = _router_operands(*router, TILE)
    ro_specs, ro_shapes, r_scratch = _router_outputs(t, TILE)
    tiles_real = tiles_per_batch - FRONT // TILE

    def x_map(i):
        return ((i // tiles_per_batch) * tiles_real + jnp.maximum(i % tiles_per_batch - FRONT // TILE, 0), 0)

    return pl.pallas_call(
        functools.partial(_mixer_kernel, tiles_per_batch=tiles_per_batch),
        grid=(t // TILE,),
        in_specs=[
            pl.BlockSpec((TILE, D_MODEL), x_map),
            pl.BlockSpec((TILE, D_MODEL), lambda i: (0, 0)),
            pl.BlockSpec((1, D_MODEL), lambda i: (0, 0)),
            pl.BlockSpec((D_MODEL, 3 * D_MODEL), lambda i: (0, 0)),
            pl.BlockSpec((3, D_MODEL), lambda i: (0, 0)),
            pl.BlockSpec((D_MODEL, D_MODEL), lambda i: (0, 0)),
        ] + r_specs,
        out_specs=[pl.BlockSpec((TILE, D_MODEL), lambda i: (i, 0))] + ro_specs,
        out_shape=[jax.ShapeDtypeStruct((t, D_MODEL), F32)] + ro_shapes,
        scratch_shapes=[pltpu.VMEM((TILE + 2 * SUBLANES, D_MODEL), F32), r_scratch],
        compiler_params=_cparams(("arbitrary",)),
        name="mixer_a",
    )(x, front, g.reshape(1, D_MODEL), w_in.astype(BF16), conv_w, w_out.astype(BF16), *r_args)


ROUTER_ROWS = SUBLANES + N_EXPERTS


def _round_up(v, m):
    return jnp.floor((v + (m - 1)) * (1.0 / m)) * m


def _route_start(seg_sc):
    @pl.when(pl.program_id(0) == 0)
    def _():
        seg_sc[...] = jnp.zeros_like(seg_sc)


def _route(h, g_ref, wrt_ref, br_ref, tri_ref, lpos_ref, gate_ref, meta_ref, tot_ref, seg_sc):
    i = pl.program_id(0)
    tm = h.shape[0]
    xn = _rms(h, g_ref[...])
    lt = lax.dot_general(wrt_ref[...], xn, (((1,), (1,)), ((), ())),
                         precision=lax.Precision.HIGHEST, preferred_element_type=F32) + br_ref[...]
    lg = lt[0:N_GROUPS]
    gmax = jnp.max(lg, axis=0, keepdims=True)
    iota_g = lax.broadcasted_iota(jnp.int32, lg.shape, 0)
    grp = jnp.min(jnp.where(lg == gmax, iota_g, N_GROUPS), axis=0, keepdims=True)
    p_grp = 1.0 / jnp.sum(jnp.exp(lg - gmax), axis=0, keepdims=True)
    le = lt[SUBLANES:SUBLANES + EXPERTS_PER_GROUP]
    for g in range(1, N_GROUPS):
        lo = SUBLANES + g * EXPERTS_PER_GROUP
        le = jnp.where(grp == g, lt[lo:lo + EXPERTS_PER_GROUP], le)
    iota_e = lax.broadcasted_iota(jnp.int32, le.shape, 0)
    v1 = jnp.max(le, axis=0, keepdims=True)
    i1 = jnp.min(jnp.where(le == v1, iota_e, EXPERTS_PER_GROUP), axis=0, keepdims=True)
    le2 = jnp.where(iota_e == i1, -jnp.inf, le)
    v2 = jnp.max(le2, axis=0, keepdims=True)
    i2 = jnp.min(jnp.where(le2 == v2, iota_e, EXPERTS_PER_GROUP), axis=0, keepdims=True)
    e2 = jnp.exp(v2 - v1)
    den = 1.0 + e2
    gate_ref[...] = jnp.concatenate([p_grp * (1.0 / den), p_grp * (e2 / den)], axis=0)
    experts = (grp * EXPERTS_PER_GROUP + i1, grp * EXPERTS_PER_GROUP + i2)

    iota_x = lax.broadcasted_iota(jnp.int32, (N_EXPERTS, tm), 0)
    hits = [iota_x == e_k for e_k in experts]
    hits_f = [jnp.where(hit, 1.0, 0.0) for hit in hits]
    hits_b = [hf.astype(BF16) for hf in hits_f]
    excl = [jnp.dot(hb, tri_ref[...], preferred_element_type=F32) for hb in hits_b]
    tot_col = [jnp.sum(hf, axis=1, keepdims=True) for hf in hits_f]
    ones = jnp.ones((SUBLANES, tm), BF16)
    n_lane = sum(lax.dot_general(ones, hb, (((1,), (1,)), ((), ())), preferred_element_type=F32) for hb in hits_b)
    seg_len_lane = _round_up(n_lane[0:1], SEG_ALIGN)
    seg_len = _round_up(tot_col[0] + tot_col[1], SEG_ALIGN)
    before = (lax.broadcasted_iota(jnp.int32, (N_EXPERTS, N_EXPERTS), 1)
              < lax.broadcasted_iota(jnp.int32, (N_EXPERTS, N_EXPERTS), 0))
    local_off = jnp.sum(jnp.where(before, seg_len_lane, 0.0), axis=1, keepdims=True)

    lpos_a = jnp.sum(jnp.where(hits[0], excl[0] + local_off, 0.0), axis=0, keepdims=True).astype(jnp.int32)
    lpos_b = jnp.sum(jnp.where(hits[1], excl[1] + tot_col[0] + local_off, 0.0), axis=0, keepdims=True).astype(jnp.int32)
    lpos_ref[...] = jnp.concatenate([lpos_a, lpos_b], axis=0)

    seg_before = seg_sc[...]
    seg_sc[...] = seg_before + seg_len
    lane = lax.broadcasted_iota(jnp.int32, (N_EXPERTS, LANES), 1)
    meta = jnp.where(lane == 0, seg_len, jnp.where(lane == 1, local_off, jnp.where(lane == 2, seg_before, 0.0)))
    meta_ref[0] = meta.astype(jnp.int32)
    tot_ref[...] = jnp.broadcast_to(seg_before + seg_len, tot_ref.shape).astype(jnp.int32)


def _router_operands(g, w_rg, b_rg, w_re, b_re, tm):
    wrt = jnp.zeros((ROUTER_ROWS, D_MODEL), F32)
    wrt = wrt.at[0:N_GROUPS].set(w_rg.T).at[SUBLANES:].set(w_re.T)
    br = jnp.zeros((ROUTER_ROWS, 1), F32)
    br = br.at[0:N_GROUPS, 0].set(b_rg).at[SUBLANES:, 0].set(b_re)
    tri = (lax.broadcasted_iota(jnp.int32, (tm, tm), 0) < lax.broadcasted_iota(jnp.int32, (tm, tm), 1)).astype(BF16)
    specs = [
        pl.BlockSpec((1, D_MODEL), lambda i: (0, 0)),
        pl.BlockSpec((ROUTER_ROWS, D_MODEL), lambda i: (0, 0)),
        pl.BlockSpec((ROUTER_ROWS, 1), lambda i: (0, 0)),
        pl.BlockSpec((tm, tm), lambda i: (0, 0)),
    ]
    return [g.reshape(1, D_MODEL), wrt, br, tri], specs


def _router_outputs(t, tm):
    nt = t // tm
    specs = [
        pl.BlockSpec((2, tm), lambda i: (0, i)),
        pl.BlockSpec((2, tm), lambda i: (0, i)),
        pl.BlockSpec((1, N_EXPERTS, LANES), lambda i: (i, 0, 0)),
        pl.BlockSpec((N_EXPERTS, LANES), lambda i: (0, 0)),
    ]
    shapes = [
        jax.ShapeDtypeStruct((2, t), jnp.int32),
        jax.ShapeDtypeStruct((2, t), F32),
        jax.ShapeDtypeStruct((nt, N_EXPERTS, LANES), jnp.int32),
        jax.ShapeDtypeStruct((N_EXPERTS, LANES), jnp.int32),
    ]
    return specs, shapes, pltpu.VMEM((N_EXPERTS, 1), F32)


def _if_rows(n, fn):
    pl.when(n > 0)(lambda: fn(pl.multiple_of(n, SEG_ALIGN)))


def _rows(ref, start, size):
    return ref.at[pl.ds(pl.multiple_of(start, SEG_ALIGN), size), :]


def _dispatch_kernel(len_ref, off_ref, dst_ref, rows_ref, zs_ref, zn_ref, nu_ref, h_ref, g_ref, lpos_ref, buf_ref, stage,
                     zero_sc, sem_out, sem_fill, *, n_blk):
    t = pl.program_id(0)
    nt = pl.num_programs(0)
    tm = h_ref.shape[0]
    slot = t % STAGE_SLOTS

    def tile_out(t, s):
        def body(e, c):
            idx = t * N_EXPERTS + e
            lo, dst = off_ref[idx], dst_ref[idx]
            _if_rows(len_ref[idx], lambda n: pltpu.make_async_copy(
                _rows(stage.at[s], lo, n), _rows(buf_ref, dst, n), sem_out.at[s]).start())
            return c

        lax.fori_loop(0, N_EXPERTS, body, 0)

    def tile_out_wait(t, s):
        _if_rows(rows_ref[t], lambda n: pltpu.make_async_copy(
            _rows(stage.at[s], 0, n), _rows(buf_ref, 0, n), sem_out.at[s]).wait())

    def fill_copies(op):
        def tails(e, c):
            _if_rows(zn_ref[e], lambda n: op(pltpu.make_async_copy(
                zero_sc.at[pl.ds(0, n), :], _rows(buf_ref, zs_ref[e], n), sem_fill)))
            return c

        lax.fori_loop(0, N_EXPERTS, tails, 0)

        def blocks(b, c):
            op(pltpu.make_async_copy(zero_sc, _rows(buf_ref, b * EXPERT_ROWS, EXPERT_ROWS), sem_fill))
            return c

        lax.fori_loop(nu_ref[0], n_blk, blocks, 0)

    start = lambda cp: cp.start()
    wait = lambda cp: cp.wait()

    @pl.when(t == 0)
    def _():
        zero_sc[...] = jnp.zeros_like(zero_sc)
        fill_copies(start)

    @pl.when(t >= STAGE_SLOTS)
    def _():
        tile_out_wait(t - STAGE_SLOTS, slot)

    xn = _rms(h_ref[...], g_ref[...]).astype(BF16)
    lpos = lpos_ref[...]
    row = lax.broadcasted_iota(jnp.int32, (LOCAL_ROWS, tm), 0)
    perm = jnp.where(row == lpos[0:1], 1.0, jnp.where(row == lpos[1:2], 1.0, 0.0)).astype(BF16)
    stage[slot] = jnp.dot(perm, xn, preferred_element_type=F32).astype(BF16)
    tile_out(t, slot)

    @pl.when(t == nt - 1)
    def _():
        for back in range(STAGE_SLOTS - 1, -1, -1):
            @pl.when(t >= back)
            def _(back=back):
                tile_out_wait(t - back, (t - back) % STAGE_SLOTS)

        fill_copies(wait)


def _dispatch(h, g, lpos, seg_len, seg_off, seg_dst, tile_rows, tail_start, tail_len, n_used, n_blk, tm):
    t = h.shape[0]
    grid_spec = pltpu.PrefetchScalarGridSpec(
        num_scalar_prefetch=7,
        grid=(t // tm,),
        in_specs=[
            pl.BlockSpec((tm, D_MODEL), lambda i, *_: (i, 0)),
            pl.BlockSpec((1, D_MODEL), lambda i, *_: (0, 0)),
            pl.BlockSpec((2, tm), lambda i, *_: (0, i)),
        ],
        out_specs=pl.BlockSpec(memory_space=pl.ANY),
        scratch_shapes=[pltpu.VMEM((STAGE_SLOTS, LOCAL_ROWS, D_MODEL), BF16), pltpu.VMEM((EXPERT_ROWS, D_MODEL), BF16),
                        pltpu.SemaphoreType.DMA((STAGE_SLOTS,)), pltpu.SemaphoreType.DMA],
    )
    return pl.pallas_call(
        functools.partial(_dispatch_kernel, n_blk=n_blk),
        grid_spec=grid_spec,
        out_shape=jax.ShapeDtypeStruct((n_blk * EXPERT_ROWS, D_MODEL), BF16),
        compiler_params=_cparams(("arbitrary",)),
        name="dispatch",
    )(seg_len, seg_off, seg_dst, tile_rows, tail_start, tail_len, n_used, h, g.reshape(1, D_MODEL), lpos)


def _expert_kernel(be_ref, nu_ref, x_ref, wg_ref, wu_ref, wd_ref, y_ref, wgu_sc, wd_sc):
    i = pl.program_id(0)
    e = be_ref[i]
    prev = be_ref[jnp.maximum(i - 1, 0)]

    @pl.when(jnp.logical_or(i == 0, e != prev))
    def _():
        wgu_sc[:, :D_EXPERT] = wg_ref[0, 0].astype(BF16)
        wgu_sc[:, D_EXPERT:] = wu_ref[0, 0].astype(BF16)
        wd_sc[...] = wd_ref[0, 0].astype(BF16)

    @pl.when(i < nu_ref[0])
    def _():
        gu = jnp.dot(x_ref[...], wgu_sc[...], preferred_element_type=F32)
        a = jax.nn.silu(gu[:, :D_EXPERT]) * gu[:, D_EXPERT:]
        y_ref[...] = jnp.dot(a.astype(BF16), wd_sc[...], preferred_element_type=F32).astype(BF16)

    @pl.when(i >= nu_ref[0])
    def _():
        y_ref[...] = jnp.zeros_like(y_ref)


def _experts(buf, blk_expert, n_used, w_gate, w_up, w_down, layer):
    n_blk = blk_expert.shape[0]

    def xmap(i, be, nu):
        return (jnp.maximum(jnp.minimum(i, nu[0] - 1), 0), 0)

    grid_spec = pltpu.PrefetchScalarGridSpec(
        num_scalar_prefetch=2,
        grid=(n_blk,),
        in_specs=[
            pl.BlockSpec((EXPERT_ROWS, D_MODEL), xmap),
            pl.BlockSpec((1, 1, D_MODEL, D_EXPERT), lambda i, be, nu: (layer, be[i], 0, 0)),
            pl.BlockSpec((1, 1, D_MODEL, D_EXPERT), lambda i, be, nu: (layer, be[i], 0, 0)),
            pl.BlockSpec((1, 1, D_EXPERT, D_MODEL), lambda i, be, nu: (layer, be[i], 0, 0)),
        ],
        out_specs=pl.BlockSpec((EXPERT_ROWS, D_MODEL), lambda i, be, nu: (i, 0)),
        scratch_shapes=[pltpu.VMEM((D_MODEL, 2 * D_EXPERT), BF16), pltpu.VMEM((D_EXPERT, D_MODEL), BF16)],
    )
    return pl.pallas_call(
        _expert_kernel,
        grid_spec=grid_spec,
        out_shape=jax.ShapeDtypeStruct(buf.shape, BF16),
        compiler_params=_cparams(("arbitrary",)),
        name="experts",
    )(blk_expert, n_used, buf, w_gate, w_up, w_down)


def _combine_kernel(len_ref, off_ref, dst_ref, rows_ref, h_ref, gate_ref, lpos_ref, y_ref, fn_ref, o_ref, yl, sem,
                    *, final_norm):
    i = pl.program_id(0)
    nt = pl.num_programs(0)
    tm = h_ref.shape[0]
    slot = i % 2

    def tile_copies(t, s):
        def body(e, c):
            idx = t * N_EXPERTS + e
            lo, src = off_ref[idx], dst_ref[idx]
            _if_rows(len_ref[idx], lambda n: pltpu.make_async_copy(
                _rows(y_ref, src, n), _rows(yl.at[s], lo, n), sem.at[s]).start())
            return c

        lax.fori_loop(0, N_EXPERTS, body, 0)

    @pl.when(i == 0)
    def _():
        yl[...] = jnp.zeros_like(yl)
        tile_copies(0, 0)

    @pl.when(i + 1 < nt)
    def _():
        tile_copies(i + 1, 1 - slot)

    _if_rows(rows_ref[i], lambda n: pltpu.make_async_copy(
        _rows(y_ref, 0, n), _rows(yl.at[slot], 0, n), sem.at[slot]).wait())

    rows = yl[slot]
    lpos = lpos_ref[...]
    col = lax.broadcasted_iota(jnp.int32, (tm, LOCAL_ROWS), 1)
    g = gate_ref[...]
    weights = jnp.where(col == lpos[:, 0:1], g[:, 0:1], jnp.where(col == lpos[:, 1:2], g[:, 1:2], 0.0)).astype(BF16)
    out = h_ref[...] + jnp.dot(weights, rows, preferred_element_type=F32)
    if final_norm:
        out = _rms(out, fn_ref[...])
    o_ref[...] = out


def _combine(h, y, seg_len, seg_off, seg_dst, tile_rows, lpos, gate, fn, tm, final_norm):
    t = h.shape[0]
    grid_spec = pltpu.PrefetchScalarGridSpec(
        num_scalar_prefetch=4,
        grid=(t // tm,),
        in_specs=[
            pl.BlockSpec((tm, D_MODEL), lambda i, *_: (i, 0)),
            pl.BlockSpec((tm, 2), lambda i, *_: (i, 0)),
            pl.BlockSpec((tm, 2), lambda i, *_: (i, 0)),
            pl.BlockSpec(memory_space=pl.ANY),
            pl.BlockSpec((1, D_MODEL), lambda i, *_: (0, 0)),
        ],
        out_specs=pl.BlockSpec((tm, D_MODEL), lambda i, *_: (i, 0)),
        scratch_shapes=[pltpu.VMEM((2, LOCAL_ROWS, D_MODEL), BF16), pltpu.SemaphoreType.DMA((2,))],
    )
    return pl.pallas_call(
        functools.partial(_combine_kernel, final_norm=final_norm),
        grid_spec=grid_spec,
        out_shape=jax.ShapeDtypeStruct((t, D_MODEL), F32),
        compiler_params=_cparams(("arbitrary",)),
        name="combine",
    )(seg_len, seg_off, seg_dst, tile_rows, h, gate.T, lpos.T, y, fn.reshape(1, D_MODEL))


def _moe(h, routing, g, w_gate, w_up, w_down, layer, fn, tm, final_norm):
    t = h.shape[0]
    nt = t // tm
    lpos, gate, meta, tot = routing
    n_blk = (2 * t + nt * N_EXPERTS * (SEG_ALIGN - 1)) // EXPERT_ROWS + N_EXPERTS
    total = tot[:, 0]
    region = (total + EXPERT_ROWS - 1) // EXPERT_ROWS * EXPERT_ROWS
    ends = jnp.cumsum(region)
    starts = ends - region
    seg_len = meta[:, :, 0].reshape(-1)
    seg_off = meta[:, :, 1].reshape(-1)
    seg_dst = (meta[:, :, 2] + starts[None, :]).reshape(-1)
    tile_rows = jnp.sum(meta[:, :, 0], axis=1)
    blk_start = jnp.arange(n_blk, dtype=jnp.int32) * EXPERT_ROWS
    blk_expert = jnp.minimum(jnp.sum(blk_start[:, None] >= ends[None, :], axis=1), N_EXPERTS - 1).astype(jnp.int32)
    n_used = (ends[-1:] // EXPERT_ROWS).astype(jnp.int32)
    buf = _dispatch(h, g, lpos, seg_len, seg_off, seg_dst, tile_rows, starts + total, region - total, n_used, n_blk, tm)
    y = _experts(buf, blk_expert, n_used, w_gate, w_up, w_down, layer)
    return _combine(h, y, seg_len, seg_off, seg_dst, tile_rows, lpos, gate, fn, tm, final_norm)


def _kv_kernel(h_ref, g_ref, wk_ref, wvt_ref, cos_ref, sa_ref, sb_ref, k_ref, vt_ref):
    hn = _rms(h_ref[...], g_ref[...]).astype(BF16)
    k = jnp.dot(hn, wk_ref[...], preferred_element_type=F32)
    cos, sa, sb = cos_ref[...], sa_ref[...], sb_ref[...]
    for c in range(D_MODEL // LANES):
        kc = k[:, c * LANES:(c + 1) * LANES]
        rot = kc * cos + pltpu.roll(kc, LANES - ROT_HALF, 1) * sa + pltpu.roll(kc, ROT_HALF, 1) * sb
        k_ref[0, :, c * LANES:(c + 1) * LANES] = rot.astype(BF16)
    vt = lax.dot_general(wvt_ref[...], hn, (((1,), (1,)), ((), ())), preferred_element_type=F32)
    for c in range(vt_ref.shape[1]):
        vt_ref[0, c] = vt[:, c * LANES:(c + 1) * LANES].astype(BF16)


def _q_kernel(h_ref, g_ref, wqt_ref, cos_ref, sin_ref, qt_ref):
    hn = _rms(h_ref[...], g_ref[...]).astype(BF16)
    qt = lax.dot_general(wqt_ref[...], hn, (((1,), (1,)), ((), ())), preferred_element_type=F32)
    cos, sin = cos_ref[...], sin_ref[...]
    scale = HEAD_DIM ** -0.5 * math.log2(math.e)
    for c in range(D_MODEL // HEAD_DIM):
        lo = c * HEAD_DIM
        x1 = qt[lo:lo + ROT_HALF]
        x2 = qt[lo + ROT_HALF:lo + 2 * ROT_HALF]
        blk = jnp.concatenate([x1 * cos - x2 * sin, x2 * cos + x1 * sin, qt[lo + 2 * ROT_HALF:lo + HEAD_DIM]], axis=0)
        qt_ref[0, lo:lo + HEAD_DIM, :] = (blk * scale).astype(BF16)


def _real_row_map(tiles_real, tiles_padded):
    front = FRONT // TILE
    return lambda i: ((i // tiles_real) * tiles_padded + front + i % tiles_real, 0)


def _kvq_kernel(h_ref, gk_ref, wk_ref, wvt_ref, cos_ref, sa_ref, sb_ref, gq_ref, wqt_ref, qcos_ref, qsin_ref,
                k_ref, vt_ref, qt_ref, *, tiles_per_batch):
    _kv_kernel(h_ref, gk_ref, wk_ref, wvt_ref, cos_ref, sa_ref, sb_ref, k_ref, vt_ref)

    @pl.when(pl.program_id(0) % tiles_per_batch >= FRONT // TILE)
    def _():
        _q_kernel(h_ref, gq_ref, wqt_ref, qcos_ref, qsin_ref, qt_ref)


def _kvq_proj(h, g_kv, w_kv, g_q, w_q, bsz, seq, lp):
    tpb = lp // TILE
    tiles_real = seq // TILE
    kblk = TILE // LANES
    inv_freq = ROPE_THETA ** (-jnp.arange(ROT_HALF, dtype=F32) * 2.0 / (2 * ROT_HALF))
    pos = jnp.maximum(jnp.arange(lp) - PAD_ROWS, 0).astype(F32)
    ang = pos[:, None] * inv_freq[None, :]
    r = jnp.arange(LANES) % HEAD_DIM
    cos_t = jnp.where(r < 2 * ROT_HALF, jnp.cos(ang)[:, r % ROT_HALF], 1.0)
    sin_t = jnp.sin(ang)[:, r % ROT_HALF]
    sa = jnp.where(r < ROT_HALF, -sin_t, 0.0)
    sb = jnp.where((r >= ROT_HALF) & (r < 2 * ROT_HALF), sin_t, 0.0)
    tab = pl.BlockSpec((TILE, LANES), lambda i: (i % tpb, 0))
    qang = inv_freq[:, None] * (N_META + jnp.arange(seq)).astype(F32)[None, :]

    def real_tile(i):
        return jnp.maximum(i % tpb - FRONT // TILE, 0)

    qtab = pl.BlockSpec((ROT_HALF, TILE), lambda i: (0, real_tile(i)))
    weight = pl.BlockSpec((D_MODEL, D_MODEL), lambda i: (0, 0))
    gain = pl.BlockSpec((1, D_MODEL), lambda i: (0, 0))
    return pl.pallas_call(
        functools.partial(_kvq_kernel, tiles_per_batch=tpb),
        grid=(bsz * tpb,),
        in_specs=[pl.BlockSpec((TILE, D_MODEL), lambda i: (i, 0)), gain, weight, weight, tab, tab, tab,
                  gain, weight, qtab, qtab],
        out_specs=[
            pl.BlockSpec((1, TILE, D_MODEL), lambda i: (i // tpb, i % tpb, 0)),
            pl.BlockSpec((1, kblk, D_MODEL, LANES), lambda i: (i // tpb, i % tpb, 0, 0)),
            pl.BlockSpec((1, D_MODEL, TILE), lambda i: ((i // tpb) * tiles_real + real_tile(i), 0, 0)),
        ],
        out_shape=[
            jax.ShapeDtypeStruct((bsz, lp, D_MODEL), BF16),
            jax.ShapeDtypeStruct((bsz, lp // LANES, D_MODEL, LANES), BF16),
            jax.ShapeDtypeStruct((bsz * tiles_real, D_MODEL, TILE), BF16),
        ],
        compiler_params=_cparams(("arbitrary",)),
        name="kvq_proj",
    )(h, g_kv.reshape(1, D_MODEL), w_kv[:, :D_MODEL].astype(BF16), w_kv[:, D_MODEL:].T.astype(BF16), cos_t, sa, sb,
      g_q.reshape(1, D_MODEL), w_q.T.astype(BF16), jnp.cos(qang), jnp.sin(qang))


STEP_FULL = 0
STEP_EDGE = (1, 2)


def _attn_schedule(nq):
    per_k = ATT_TK // ATT_TQ
    steps = [(i, t, STEP_FULL if t < i // per_k else STEP_EDGE[i % per_k]) for i in range(nq) for t in range(i // per_k + 1)]
    return [list(col) for col in zip(*steps)]


def _attn_kernel(qi_ref, kt_ref, kind_ref, qt_ref, k_ref, vt_ref, lam_ref, g_ref, o_ref, m_sc, l_sc, acc_sc, sa_sc, sb_sc,
                 *, lam_init, n_steps, block_kinds):
    tq = ATT_TQ
    zero = jnp.zeros((HEAD_DIM, tq), BF16)
    n_sub = ATT_TK // ATT_SUB

    def q_maps(s):
        qt = qt_ref[0, qi_ref[s]]
        return (jnp.concatenate([qt[0:HEAD_DIM], zero], axis=0), jnp.concatenate([zero, qt[HEAD_DIM:]], axis=0))

    def update(n, sc, vtt, fresh=None):
        m_old, l_old, acc_old = m_sc[n], l_sc[n], acc_sc[n]
        if fresh is not None:
            m_old = jnp.where(fresh, NEG, m_old)
            l_old = jnp.where(fresh, 0.0, l_old)
            acc_old = jnp.where(fresh, 0.0, acc_old)
        m_new = jnp.maximum(m_old, jnp.max(sc, axis=0, keepdims=True))
        alpha = jnp.exp2(m_old - m_new)
        p = jnp.exp2(sc - m_new)
        l_sc[n] = alpha * l_old + jnp.sum(p, axis=0, keepdims=True)
        acc_sc[n] = alpha * acc_old + jnp.dot(vtt, p.astype(BF16), preferred_element_type=F32)
        m_sc[n] = m_new

    def values(row0, nkeys):
        blk0 = row0 // LANES
        return jnp.concatenate([vt_ref[0, blk0 + c] for c in range(nkeys // LANES)], axis=1)

    def key_row(s, u):
        return pl.multiple_of(FRONT + kt_ref[s] * ATT_TK + u * ATT_SUB, LANES)

    def visible_parts(kind):
        return STEP_EDGE.index(kind) + 1 if kind in STEP_EDGE else n_sub

    def scores_into(s, buf, parts=n_sub):
        qs = q_maps(s)
        for u in range(parts):
            kt = k_ref[0, pl.ds(key_row(s, u), ATT_SUB), :]
            for n in range(2):
                buf[u * 2 + n] = jnp.dot(kt, qs[n], preferred_element_type=F32)

    def consume(s, buf, kind):
        fresh = kt_ref[s] == 0
        edge = STEP_EDGE.index(kind) if kind in STEP_EDGE else None
        for u in range(n_sub) if edge is None else range(edge + 1):
            vtt = values(key_row(s, u), ATT_SUB)
            for n in range(2):
                sc = buf[u * 2 + n]
                if u == edge:
                    r = lax.broadcasted_iota(jnp.int32, (ATT_SUB, tq), 0) // CHUNK
                    c = lax.broadcasted_iota(jnp.int32, (ATT_SUB, tq), 1) // CHUNK
                    sc = jnp.where(r <= c, sc, NEG)
                update(n, sc, vtt, fresh if u == 0 else None)

    def finish(s):
        qs = q_maps(s)
        kt = k_ref[0, FRONT - N_META:FRONT, :]
        vtt = vt_ref[0, FRONT // LANES - 1][:, LANES - N_META:]
        for n in range(2):
            update(n, jnp.dot(kt, qs[n], preferred_element_type=F32), vtt)
        lp = lam_ref[...]
        lam = (jnp.exp(jnp.sum(lp[0:1] * lp[1:2], axis=1, keepdims=True))
               - jnp.exp(jnp.sum(lp[2:3] * lp[3:4], axis=1, keepdims=True)) + lam_init)
        o = acc_sc[0] / l_sc[0] - lam * (acc_sc[1] / l_sc[1])
        o = o * lax.rsqrt(jnp.mean(o * o, axis=0, keepdims=True) + EPS) * g_ref[...] * (1.0 - lam_init)
        o_ref[pl.ds(pl.multiple_of(qi_ref[s] * tq, tq), tq), :] = o.T.astype(BF16)

    def block(p, kind_a, kind_b):
        s_a, s_b = 2 * p, 2 * p + 1
        scores_into(s_b, sb_sc, visible_parts(kind_b))
        consume(s_a, sa_sc, kind_a)
        if kind_a in STEP_EDGE:
            finish(s_a)
        scores_into(jnp.minimum(s_b + 1, n_steps - 1), sa_sc)
        consume(s_b, sb_sc, kind_b)
        if kind_b in STEP_EDGE:
            finish(s_b)

    scores_into(0, sa_sc)

    def body(p, c):
        kind_a, kind_b = kind_ref[2 * p], kind_ref[2 * p + 1]
        for va, vb in block_kinds:
            pl.when(jnp.logical_and(kind_a == va, kind_b == vb))(functools.partial(block, p, va, vb))
        return c

    lax.fori_loop(0, n_steps // 2, body, 0)


def _attention(qt, k, vt4, lam_p, subln_g, bsz, seq, lam_init):
    nq = seq // ATT_TQ
    lp = k.shape[1]
    q_idx, k_idx, kinds = _attn_schedule(nq)
    n_steps = len(kinds)
    assert n_steps % 2 == 0
    block_kinds = sorted(set(zip(kinds[0::2], kinds[1::2])))
    grid_spec = pltpu.PrefetchScalarGridSpec(
        num_scalar_prefetch=3,
        grid=(bsz, N_HEADS),
        in_specs=[
            pl.BlockSpec((1, nq, 2 * HEAD_DIM, ATT_TQ), lambda b, h, *_: (b, 0, h, 0)),
            pl.BlockSpec((1, lp, 2 * HEAD_DIM), lambda b, h, *_: (b, 0, h)),
            pl.BlockSpec((1, lp // LANES, 2 * HEAD_DIM, LANES), lambda b, h, *_: (b, 0, h, 0)),
            pl.BlockSpec((4, HEAD_DIM), lambda b, h, *_: (0, 0)),
            pl.BlockSpec((2 * HEAD_DIM, 1), lambda b, h, *_: (0, 0)),
        ],
        out_specs=pl.BlockSpec((seq, 2 * HEAD_DIM), lambda b, h, *_: (b, h)),
        scratch_shapes=[pltpu.VMEM((2, 1, ATT_TQ), F32), pltpu.VMEM((2, 1, ATT_TQ), F32),
                        pltpu.VMEM((2, 2 * HEAD_DIM, ATT_TQ), F32),
                        pltpu.VMEM((2 * ATT_TK // ATT_SUB, ATT_SUB, ATT_TQ), F32),
                        pltpu.VMEM((2 * ATT_TK // ATT_SUB, ATT_SUB, ATT_TQ), F32)],
    )
    return pl.pallas_call(
        functools.partial(_attn_kernel, lam_init=lam_init, n_steps=n_steps, block_kinds=block_kinds),
        grid_spec=grid_spec,
        out_shape=jax.ShapeDtypeStruct((bsz * seq, D_MODEL), BF16),
        compiler_params=_cparams(("arbitrary", "arbitrary")),
        name="diff_attention",
    )(*(jnp.asarray(col, jnp.int32) for col in (q_idx, k_idx, kinds)),
      qt.reshape(bsz, nq, D_MODEL, ATT_TQ), k, vt4, lam_p, subln_g.reshape(2 * HEAD_DIM, 1))


def _oproj_kernel(h_ref, o_ref, w_ref, rg_ref, wrt_ref, br_ref, tri_ref, out_ref, lpos_ref, gate_ref, meta_ref, tot_ref,
                  seg_sc):
    _route_start(seg_sc)
    h = h_ref[...] + jnp.dot(o_ref[...], w_ref[...], preferred_element_type=F32)
    out_ref[...] = h
    _route(h, rg_ref, wrt_ref, br_ref, tri_ref, lpos_ref, gate_ref, meta_ref, tot_ref, seg_sc)


def _o_proj(h_pad, o, w_out, router, bsz, seq, lp):
    tiles_real = seq // TILE
    r_args, r_specs = _router_operands(*router, TILE)
    ro_specs, ro_shapes, r_scratch = _router_outputs(bsz * seq, TILE)
    return pl.pallas_call(
        _oproj_kernel,
        grid=(bsz * tiles_real,),
        in_specs=[
            pl.BlockSpec((TILE, D_MODEL), _real_row_map(tiles_real, lp // TILE)),
            pl.BlockSpec((TILE, D_MODEL), lambda i: (i, 0)),
            pl.BlockSpec((D_MODEL, D_MODEL), lambda i: (0, 0)),
        ] + r_specs,
        out_specs=[pl.BlockSpec((TILE, D_MODEL), lambda i: (i, 0))] + ro_specs,
        out_shape=[jax.ShapeDtypeStruct((bsz * seq, D_MODEL), F32)] + ro_shapes,
        scratch_shapes=[r_scratch],
        compiler_params=_cparams(("arbitrary",)),
        name="o_proj",
    )(h_pad, o, w_out.astype(BF16), *r_args)


def kernel(x, meta_tokens, a_norm, a_w_in, a_conv, a_w_out, kv_norm, w_kv, b_norm, b_w_q, b_lambda, b_subln, b_w_out, ffn_norm, r_group, r_group_b, r_expert, r_expert_b, e_gate, e_up, e_down, final_norm):
    bsz, seq, d = x.shape
    assert d == D_MODEL and a_norm.shape[0] == 1 and b_norm.shape[0] == 1
    lp = FRONT + seq
    assert seq % ATT_TK == 0 and ATT_TK == 2 * ATT_TQ and ATT_TQ == TILE and ATT_TK % ATT_SUB == 0
    front = jnp.concatenate([jnp.zeros((PAD_ROWS, d), x.dtype), meta_tokens.astype(x.dtype)], axis=0)
    routers = [(ffn_norm[j], r_group[j], r_group_b[j], r_expert[j], r_expert_b[j]) for j in range(2)]
    h, *routing = _mixer(x.reshape(bsz * seq, d), front, a_norm[0], a_w_in[0], a_conv[0], a_w_out[0], routers[0],
                         bsz, lp // TILE)
    h = _moe(h, routing, ffn_norm[0], e_gate, e_up, e_down, 0, final_norm, TILE, False)

    k, vt4, qt = _kvq_proj(h, kv_norm, w_kv, b_norm[0], b_w_q[0], bsz, seq, lp)
    lam_init = 0.8 - 0.6 * math.exp(-0.3 * a_norm.shape[0])
    o = _attention(qt, k, vt4, b_lambda[0], b_subln[0], bsz, seq, lam_init)
    h, *routing = _o_proj(h, o, b_w_out[0], routers[1], bsz, seq, lp)
    h = _moe(h, routing, ffn_norm[1], e_gate, e_up, e_down, 1, final_norm, TILE, True)
    return h.reshape(bsz, seq, d)
```

```python
import functools
import math

import jax
import jax.numpy as jnp
from jax import lax
from jax.experimental import pallas as pl
from jax.experimental.pallas import tpu as pltpu

D_MODEL = 1024
CHUNK = 64
N_META = 16
HEAD_DIM = 64
N_HEADS = 8
ROT_HALF = 8
ROPE_THETA = 500000.0
N_GROUPS = 4
EXPERTS_PER_GROUP = 8
N_EXPERTS = 32
D_EXPERT = 512
EPS = 1e-6

LANES = 128
SUBLANES = 8
TILE = 512
FRONT = TILE
PAD_ROWS = FRONT - N_META
EXPERT_ROWS = 512
ATT_TQ = 512
ATT_TK = 1024
ATT_SUB = 512
SEG_ALIGN = 16
SEG_UNROLL = 4
STAGE_SLOTS = 2
LOCAL_ROWS = 1536
VMEM_LIMIT = 56 * 1024 * 1024
NEG = float(jnp.finfo(jnp.float32).min)

BF16 = jnp.bfloat16
F32 = jnp.float32


def _rms(x, g):
    return x * lax.rsqrt(jnp.mean(x * x, axis=-1, keepdims=True) + EPS) * g


def _cparams(sem):
    return pltpu.CompilerParams(dimension_semantics=sem, vmem_limit_bytes=VMEM_LIMIT)


def _mixer_kernel(x_ref, front_ref, g_ref, win_ref, cw_ref, wout_ref, rg_ref, wrt_ref, br_ref, tri_ref,
                  o_ref, lpos_ref, gate_ref, meta_ref, tot_ref, cu_ref, seg_sc, *, tiles_per_batch):
    i = pl.program_id(0)
    tm = x_ref.shape[0]
    _route_start(seg_sc)

    @pl.when(i % tiles_per_batch == 0)
    def _():
        cu_ref[0:SUBLANES, :] = jnp.zeros((SUBLANES, D_MODEL), F32)

    x = jnp.where(i % tiles_per_batch == 0, front_ref[...], x_ref[...])
    hn = _rms(x, g_ref[...]).astype(BF16)
    proj = jnp.dot(hn, win_ref[...], preferred_element_type=F32)
    b_gate = proj[:, :D_MODEL]
    cu = proj[:, D_MODEL:2 * D_MODEL] * proj[:, 2 * D_MODEL:]
    cu_ref[SUBLANES:SUBLANES + tm, :] = cu
    cw = cw_ref[...]
    z = (cw[2:3] * cu + cw[1:2] * cu_ref[SUBLANES - 1:SUBLANES - 1 + tm, :]
         + cw[0:1] * cu_ref[SUBLANES - 2:SUBLANES - 2 + tm, :])
    cu_ref[0:SUBLANES, :] = cu_ref[tm:tm + SUBLANES, :]
    y = jnp.dot((b_gate * z).astype(BF16), wout_ref[...], preferred_element_type=F32)
    h = x + y
    o_ref[...] = h
    _route(h, rg_ref, wrt_ref, br_ref, tri_ref, lpos_ref, gate_ref, meta_ref, tot_ref, seg_sc)


def _mixer(x, front, g, w_in, conv_w, w_out, router, bsz, tiles_per_batch):
    t = bsz * tiles_per_batch * TILE
    r_args, r_specs = _router_operands(*router, TILE)
    ro_specs, ro_shapes, r_scratch = _router_outputs(t, TILE)
    tiles_real = tiles_per_batch - FRONT // TILE

    def x_map(i):
        return ((i // tiles_per_batch) * tiles_real + jnp.maximum(i % tiles_per_batch - FRONT // TILE, 0), 0)

    return pl.pallas_call(
        functools.partial(_mixer_kernel, tiles_per_batch=tiles_per_batch),
        grid=(t // TILE,),
        in_specs=[
            pl.BlockSpec((TILE, D_MODEL), x_map),
            pl.BlockSpec((TILE, D_MODEL), lambda i: (0, 0)),
            pl.BlockSpec((1, D_MODEL), lambda i: (0, 0)),
            pl.BlockSpec((D_MODEL, 3 * D_MODEL), lambda i: (0, 0)),
            pl.BlockSpec((3, D_MODEL), lambda i: (0, 0)),
            pl.BlockSpec((D_MODEL, D_MODEL), lambda i: (0, 0)),
        ] + r_specs,
        out_specs=[pl.BlockSpec((TILE, D_MODEL), lambda i: (i, 0))] + ro_specs,
        out_shape=[jax.ShapeDtypeStruct((t, D_MODEL), F32)] + ro_shapes,
        scratch_shapes=[pltpu.VMEM((TILE + 2 * SUBLANES, D_MODEL), F32), r_scratch],
        compiler_params=_cparams(("arbitrary",)),
        name="mixer_a",
    )(x, front, g.reshape(1, D_MODEL), w_in.astype(BF16), conv_w, w_out.astype(BF16), *r_args)


ROUTER_ROWS = SUBLANES + N_EXPERTS


def _round_up(v, m):
    return jnp.floor((v + (m - 1)) * (1.0 / m)) * m


def _route_start(seg_sc):
    @pl.when(pl.program_id(0) == 0)
    def _():
        seg_sc[...] = jnp.zeros_like(seg_sc)


def _route(h, g_ref, wrt_ref, br_ref, tri_ref, lpos_ref, gate_ref, meta_ref, tot_ref, seg_sc):
    i = pl.program_id(0)
    tm = h.shape[0]
    xn = _rms(h, g_ref[...])
    lt = lax.dot_general(wrt_ref[...], xn, (((1,), (1,)), ((), ())),
                         precision=lax.Precision.HIGHEST, preferred_element_type=F32) + br_ref[...]
    lg = lt[0:N_GROUPS]
    gmax = jnp.max(lg, axis=0, keepdims=True)
    iota_g = lax.broadcasted_iota(jnp.int32, lg.shape, 0)
    grp = jnp.min(jnp.where(lg == gmax, iota_g, N_GROUPS), axis=0, keepdims=True)
    p_grp = 1.0 / jnp.sum(jnp.exp(lg - gmax), axis=0, keepdims=True)
    le = lt[SUBLANES:SUBLANES + EXPERTS_PER_GROUP]
    for g in range(1, N_GROUPS):
        lo = SUBLANES + g * EXPERTS_PER_GROUP
        le = jnp.where(grp == g, lt[lo:lo + EXPERTS_PER_GROUP], le)
    iota_e = lax.broadcasted_iota(jnp.int32, le.shape, 0)
    v1 = jnp.max(le, axis=0, keepdims=True)
    i1 = jnp.min(jnp.where(le == v1, iota_e, EXPERTS_PER_GROUP), axis=0, keepdims=True)
    le2 = jnp.where(iota_e == i1, -jnp.inf, le)
    v2 = jnp.max(le2, axis=0, keepdims=True)
    i2 = jnp.min(jnp.where(le2 == v2, iota_e, EXPERTS_PER_GROUP), axis=0, keepdims=True)
    e2 = jnp.exp(v2 - v1)
    den = 1.0 + e2
    gate_ref[...] = jnp.concatenate([p_grp * (1.0 / den), p_grp * (e2 / den)], axis=0)
    experts = (grp * EXPERTS_PER_GROUP + i1, grp * EXPERTS_PER_GROUP + i2)

    iota_x = lax.broadcasted_iota(jnp.int32, (N_EXPERTS, tm), 0)
    hits = [iota_x == e_k for e_k in experts]
    hits_f = [jnp.where(hit, 1.0, 0.0) for hit in hits]
    hits_b = [hf.astype(BF16) for hf in hits_f]
    excl = [jnp.dot(hb, tri_ref[...], preferred_element_type=F32) for hb in hits_b]
    tot_col = [jnp.sum(hf, axis=1, keepdims=True) for hf in hits_f]
    ones = jnp.ones((SUBLANES, tm), BF16)
    n_lane = sum(lax.dot_general(ones, hb, (((1,), (1,)), ((), ())), preferred_element_type=F32) for hb in hits_b)
    seg_len_lane = _round_up(n_lane[0:1], SEG_ALIGN)
    seg_len = _round_up(tot_col[0] + tot_col[1], SEG_ALIGN)
    before = (lax.broadcasted_iota(jnp.int32, (N_EXPERTS, N_EXPERTS), 1)
              < lax.broadcasted_iota(jnp.int32, (N_EXPERTS, N_EXPERTS), 0))
    local_off = jnp.sum(jnp.where(before, seg_len_lane, 0.0), axis=1, keepdims=True)

    lpos_a = jnp.sum(jnp.where(hits[0], excl[0] + local_off, 0.0), axis=0, keepdims=True).astype(jnp.int32)
    lpos_b = jnp.sum(jnp.where(hits[1], excl[1] + tot_col[0] + local_off, 0.0), axis=0, keepdims=True).astype(jnp.int32)
    lpos_ref[...] = jnp.concatenate([lpos_a, lpos_b], axis=0)

    seg_before = seg_sc[...]
    seg_sc[...] = seg_before + seg_len
    lane = lax.broadcasted_iota(jnp.int32, (N_EXPERTS, LANES), 1)
    meta = jnp.where(lane == 0, seg_len, jnp.where(lane == 1, local_off, jnp.where(lane == 2, seg_before, 0.0)))
    meta_ref[0] = meta.astype(jnp.int32)
    tot_ref[...] = jnp.broadcast_to(seg_before + seg_len, tot_ref.shape).astype(jnp.int32)


def _router_operands(g, w_rg, b_rg, w_re, b_re, tm):
    wrt = jnp.zeros((ROUTER_ROWS, D_MODEL), F32)
    wrt = wrt.at[0:N_GROUPS].set(w_rg.T).at[SUBLANES:].set(w_re.T)
    br = jnp.zeros((ROUTER_ROWS, 1), F32)
    br = br.at[0:N_GROUPS, 0].set(b_rg).at[SUBLANES:, 0].set(b_re)
    tri = (lax.broadcasted_iota(jnp.int32, (tm, tm), 0) < lax.broadcasted_iota(jnp.int32, (tm, tm), 1)).astype(BF16)
    specs = [
        pl.BlockSpec((1, D_MODEL), lambda i: (0, 0)),
        pl.BlockSpec((ROUTER_ROWS, D_MODEL), lambda i: (0, 0)),
        pl.BlockSpec((ROUTER_ROWS, 1), lambda i: (0, 0)),
        pl.BlockSpec((tm, tm), lambda i: (0, 0)),
    ]
    return [g.reshape(1, D_MODEL), wrt, br, tri], specs


def _router_outputs(t, tm):
    nt = t // tm
    specs = [
        pl.BlockSpec((2, tm), lambda i: (0, i)),
        pl.BlockSpec((2, tm), lambda i: (0, i)),
        pl.BlockSpec((1, N_EXPERTS, LANES), lambda i: (i, 0, 0)),
        pl.BlockSpec((N_EXPERTS, LANES), lambda i: (0, 0)),
    ]
    shapes = [
        jax.ShapeDtypeStruct((2, t), jnp.int32),
        jax.ShapeDtypeStruct((2, t), F32),
        jax.ShapeDtypeStruct((nt, N_EXPERTS, LANES), jnp.int32),
        jax.ShapeDtypeStruct((N_EXPERTS, LANES), jnp.int32),
    ]
    return specs, shapes, pltpu.VMEM((N_EXPERTS, 1), F32)


def _if_rows(n, fn):
    pl.when(n > 0)(lambda: fn(pl.multiple_of(n, SEG_ALIGN)))


def _rows(ref, start, size):
    return ref.at[pl.ds(pl.multiple_of(start, SEG_ALIGN), size), :]


def _for_each_segment(t, fn):
    def body(j, c):
        for k in range(SEG_UNROLL):
            fn(t * N_EXPERTS + j * SEG_UNROLL + k)
        return c

    lax.fori_loop(0, N_EXPERTS // SEG_UNROLL, body, 0)


def _dispatch_kernel(len_ref, off_ref, dst_ref, rows_ref, zs_ref, zn_ref, nu_ref, h_ref, g_ref, lpos_ref, buf_ref, stage,
                     zero_sc, sem_out, sem_fill, *, n_blk):
    t = pl.program_id(0)
    nt = pl.num_programs(0)
    tm = h_ref.shape[0]
    slot = t % STAGE_SLOTS

    def tile_out(t, s):
        def start_segment(idx):
            lo, dst = off_ref[idx], dst_ref[idx]
            _if_rows(len_ref[idx], lambda n: pltpu.make_async_copy(
                _rows(stage.at[s], lo, n), _rows(buf_ref, dst, n), sem_out.at[s]).start())

        _for_each_segment(t, start_segment)

    def tile_out_wait(t, s):
        _if_rows(rows_ref[t], lambda n: pltpu.make_async_copy(
            _rows(stage.at[s], 0, n), _rows(buf_ref, 0, n), sem_out.at[s]).wait())

    def fill_copies(op):
        def tails(e, c):
            _if_rows(zn_ref[e], lambda n: op(pltpu.make_async_copy(
                zero_sc.at[pl.ds(0, n), :], _rows(buf_ref, zs_ref[e], n), sem_fill)))
            return c

        lax.fori_loop(0, N_EXPERTS, tails, 0)

        def blocks(b, c):
            op(pltpu.make_async_copy(zero_sc, _rows(buf_ref, b * EXPERT_ROWS, EXPERT_ROWS), sem_fill))
            return c

        lax.fori_loop(nu_ref[0], n_blk, blocks, 0)

    start = lambda cp: cp.start()
    wait = lambda cp: cp.wait()

    @pl.when(t == 0)
    def _():
        zero_sc[...] = jnp.zeros_like(zero_sc)
        fill_copies(start)

    @pl.when(t >= STAGE_SLOTS)
    def _():
        tile_out_wait(t - STAGE_SLOTS, slot)

    xn = _rms(h_ref[...], g_ref[...]).astype(BF16)
    lpos = lpos_ref[...]
    row = lax.broadcasted_iota(jnp.int32, (LOCAL_ROWS, tm), 0)
    perm = jnp.where(row == lpos[0:1], 1.0, jnp.where(row == lpos[1:2], 1.0, 0.0)).astype(BF16)
    stage[slot] = jnp.dot(perm, xn, preferred_element_type=F32).astype(BF16)
    tile_out(t, slot)

    @pl.when(t == nt - 1)
    def _():
        for back in range(STAGE_SLOTS - 1, -1, -1):
            @pl.when(t >= back)
            def _(back=back):
                tile_out_wait(t - back, (t - back) % STAGE_SLOTS)

        fill_copies(wait)


def _dispatch(h, g, lpos, seg_len, seg_off, seg_dst, tile_rows, tail_start, tail_len, n_used, n_blk, tm):
    t = h.shape[0]
    grid_spec = pltpu.PrefetchScalarGridSpec(
        num_scalar_prefetch=7,
        grid=(t // tm,),
        in_specs=[
            pl.BlockSpec((tm, D_MODEL), lambda i, *_: (i, 0)),
            pl.BlockSpec((1, D_MODEL), lambda i, *_: (0, 0)),
            pl.BlockSpec((2, tm), lambda i, *_: (0, i)),
        ],
        out_specs=pl.BlockSpec(memory_space=pl.ANY),
        scratch_shapes=[pltpu.VMEM((STAGE_SLOTS, LOCAL_ROWS, D_MODEL), BF16), pltpu.VMEM((EXPERT_ROWS, D_MODEL), BF16),
                        pltpu.SemaphoreType.DMA((STAGE_SLOTS,)), pltpu.SemaphoreType.DMA],
    )
    return pl.pallas_call(
        functools.partial(_dispatch_kernel, n_blk=n_blk),
        grid_spec=grid_spec,
        out_shape=jax.ShapeDtypeStruct((n_blk * EXPERT_ROWS, D_MODEL), BF16),
        compiler_params=_cparams(("arbitrary",)),
        name="dispatch",
    )(seg_len, seg_off, seg_dst, tile_rows, tail_start, tail_len, n_used, h, g.reshape(1, D_MODEL), lpos)


def _expert_kernel(be_ref, nu_ref, x_ref, wg_ref, wu_ref, wd_ref, y_ref, wgu_sc, wd_sc):
    i = pl.program_id(0)
    e = be_ref[i]
    prev = be_ref[jnp.maximum(i - 1, 0)]

    @pl.when(jnp.logical_or(i == 0, e != prev))
    def _():
        wgu_sc[:, :D_EXPERT] = wg_ref[0, 0].astype(BF16)
        wgu_sc[:, D_EXPERT:] = wu_ref[0, 0].astype(BF16)
        wd_sc[...] = wd_ref[0, 0].astype(BF16)

    @pl.when(i < nu_ref[0])
    def _():
        gu = jnp.dot(x_ref[...], wgu_sc[...], preferred_element_type=F32)
        a = jax.nn.silu(gu[:, :D_EXPERT]) * gu[:, D_EXPERT:]
        y_ref[...] = jnp.dot(a.astype(BF16), wd_sc[...], preferred_element_type=F32).astype(BF16)

    @pl.when(i >= nu_ref[0])
    def _():
        y_ref[...] = jnp.zeros_like(y_ref)


def _experts(buf, blk_expert, n_used, w_gate, w_up, w_down, layer):
    n_blk = blk_expert.shape[0]

    def xmap(i, be, nu):
        return (jnp.maximum(jnp.minimum(i, nu[0] - 1), 0), 0)

    grid_spec = pltpu.PrefetchScalarGridSpec(
        num_scalar_prefetch=2,
        grid=(n_blk,),
        in_specs=[
            pl.BlockSpec((EXPERT_ROWS, D_MODEL), xmap),
            pl.BlockSpec((1, 1, D_MODEL, D_EXPERT), lambda i, be, nu: (layer, be[i], 0, 0)),
            pl.BlockSpec((1, 1, D_MODEL, D_EXPERT), lambda i, be, nu: (layer, be[i], 0, 0)),
            pl.BlockSpec((1, 1, D_EXPERT, D_MODEL), lambda i, be, nu: (layer, be[i], 0, 0)),
        ],
        out_specs=pl.BlockSpec((EXPERT_ROWS, D_MODEL), lambda i, be, nu: (i, 0)),
        scratch_shapes=[pltpu.VMEM((D_MODEL, 2 * D_EXPERT), BF16), pltpu.VMEM((D_EXPERT, D_MODEL), BF16)],
    )
    return pl.pallas_call(
        _expert_kernel,
        grid_spec=grid_spec,
        out_shape=jax.ShapeDtypeStruct(buf.shape, BF16),
        compiler_params=_cparams(("arbitrary",)),
        name="experts",
    )(blk_expert, n_used, buf, w_gate, w_up, w_down)


def _combine_kernel(len_ref, off_ref, dst_ref, rows_ref, h_ref, gate_ref, lpos_ref, y_ref, fn_ref, o_ref, yl, sem,
                    *, final_norm):
    i = pl.program_id(0)
    nt = pl.num_programs(0)
    tm = h_ref.shape[0]
    slot = i % 2

    def tile_copies(t, s):
        def start_segment(idx):
            lo, src = off_ref[idx], dst_ref[idx]
            _if_rows(len_ref[idx], lambda n: pltpu.make_async_copy(
                _rows(y_ref, src, n), _rows(yl.at[s], lo, n), sem.at[s]).start())

        _for_each_segment(t, start_segment)

    @pl.when(i == 0)
    def _():
        yl[...] = jnp.zeros_like(yl)
        tile_copies(0, 0)

    @pl.when(i + 1 < nt)
    def _():
        tile_copies(i + 1, 1 - slot)

    _if_rows(rows_ref[i], lambda n: pltpu.make_async_copy(
        _rows(y_ref, 0, n), _rows(yl.at[slot], 0, n), sem.at[slot]).wait())

    rows = yl[slot]
    lpos = lpos_ref[...]
    col = lax.broadcasted_iota(jnp.int32, (tm, LOCAL_ROWS), 1)
    g = gate_ref[...]
    weights = jnp.where(col == lpos[:, 0:1], g[:, 0:1], jnp.where(col == lpos[:, 1:2], g[:, 1:2], 0.0)).astype(BF16)
    out = h_ref[...] + jnp.dot(weights, rows, preferred_element_type=F32)
    if final_norm:
        out = _rms(out, fn_ref[...])
    o_ref[...] = out


def _combine(h, y, seg_len, seg_off, seg_dst, tile_rows, lpos, gate, fn, tm, final_norm):
    t = h.shape[0]
    grid_spec = pltpu.PrefetchScalarGridSpec(
        num_scalar_prefetch=4,
        grid=(t // tm,),
        in_specs=[
            pl.BlockSpec((tm, D_MODEL), lambda i, *_: (i, 0)),
            pl.BlockSpec((tm, 2), lambda i, *_: (i, 0)),
            pl.BlockSpec((tm, 2), lambda i, *_: (i, 0)),
            pl.BlockSpec(memory_space=pl.ANY),
            pl.BlockSpec((1, D_MODEL), lambda i, *_: (0, 0)),
        ],
        out_specs=pl.BlockSpec((tm, D_MODEL), lambda i, *_: (i, 0)),
        scratch_shapes=[pltpu.VMEM((2, LOCAL_ROWS, D_MODEL), BF16), pltpu.SemaphoreType.DMA((2,))],
    )
    return pl.pallas_call(
        functools.partial(_combine_kernel, final_norm=final_norm),
        grid_spec=grid_spec,
        out_shape=jax.ShapeDtypeStruct((t, D_MODEL), F32),
        compiler_params=_cparams(("arbitrary",)),
        name="combine",
    )(seg_len, seg_off, seg_dst, tile_rows, h, gate.T, lpos.T, y, fn.reshape(1, D_MODEL))


def _moe(h, routing, g, w_gate, w_up, w_down, layer, fn, tm, final_norm):
    t = h.shape[0]
    nt = t // tm
    lpos, gate, meta, tot = routing
    n_blk = (2 * t + nt * N_EXPERTS * (SEG_ALIGN - 1)) // EXPERT_ROWS + N_EXPERTS
    total = tot[:, 0]
    region = (total + EXPERT_ROWS - 1) // EXPERT_ROWS * EXPERT_ROWS
    ends = jnp.cumsum(region)
    starts = ends - region
    seg_len = meta[:, :, 0].reshape(-1)
    seg_off = meta[:, :, 1].reshape(-1)
    seg_dst = (meta[:, :, 2] + starts[None, :]).reshape(-1)
    tile_rows = jnp.sum(meta[:, :, 0], axis=1)
    blk_start = jnp.arange(n_blk, dtype=jnp.int32) * EXPERT_ROWS
    blk_expert = jnp.minimum(jnp.sum(blk_start[:, None] >= ends[None, :], axis=1), N_EXPERTS - 1).astype(jnp.int32)
    n_used = (ends[-1:] // EXPERT_ROWS).astype(jnp.int32)
    buf = _dispatch(h, g, lpos, seg_len, seg_off, seg_dst, tile_rows, starts + total, region - total, n_used, n_blk, tm)
    y = _experts(buf, blk_expert, n_used, w_gate, w_up, w_down, layer)
    return _combine(h, y, seg_len, seg_off, seg_dst, tile_rows, lpos, gate, fn, tm, final_norm)


def _kv_kernel(h_ref, g_ref, wk_ref, wvt_ref, cos_ref, sa_ref, sb_ref, k_ref, vt_ref):
    hn = _rms(h_ref[...], g_ref[...]).astype(BF16)
    k = jnp.dot(hn, wk_ref[...], preferred_element_type=F32)
    cos, sa, sb = cos_ref[...], sa_ref[...], sb_ref[...]
    for c in range(D_MODEL // LANES):
        kc = k[:, c * LANES:(c + 1) * LANES]
        rot = kc * cos + pltpu.roll(kc, LANES - ROT_HALF, 1) * sa + pltpu.roll(kc, ROT_HALF, 1) * sb
        k_ref[0, :, c * LANES:(c + 1) * LANES] = rot.astype(BF16)
    vt = lax.dot_general(wvt_ref[...], hn, (((1,), (1,)), ((), ())), preferred_element_type=F32)
    for c in range(vt_ref.shape[1]):
        vt_ref[0, c] = vt[:, c * LANES:(c + 1) * LANES].astype(BF16)


def _q_kernel(h_ref, g_ref, wqt_ref, cos_ref, sin_ref, qt_ref):
    hn = _rms(h_ref[...], g_ref[...]).astype(BF16)
    qt = lax.dot_general(wqt_ref[...], hn, (((1,), (1,)), ((), ())), preferred_element_type=F32)
    cos, sin = cos_ref[...], sin_ref[...]
    scale = HEAD_DIM ** -0.5 * math.log2(math.e)
    for c in range(D_MODEL // HEAD_DIM):
        lo = c * HEAD_DIM
        x1 = qt[lo:lo + ROT_HALF]
        x2 = qt[lo + ROT_HALF:lo + 2 * ROT_HALF]
        blk = jnp.concatenate([x1 * cos - x2 * sin, x2 * cos + x1 * sin, qt[lo + 2 * ROT_HALF:lo + HEAD_DIM]], axis=0)
        qt_ref[0, lo:lo + HEAD_DIM, :] = (blk * scale).astype(BF16)


def _real_row_map(tiles_real, tiles_padded):
    front = FRONT // TILE
    return lambda i: ((i // tiles_real) * tiles_padded + front + i % tiles_real, 0)


def _kvq_kernel(h_ref, gk_ref, wk_ref, wvt_ref, cos_ref, sa_ref, sb_ref, gq_ref, wqt_ref, qcos_ref, qsin_ref,
                k_ref, vt_ref, qt_ref, *, tiles_per_batch):
    _kv_kernel(h_ref, gk_ref, wk_ref, wvt_ref, cos_ref, sa_ref, sb_ref, k_ref, vt_ref)

    @pl.when(pl.program_id(0) % tiles_per_batch >= FRONT // TILE)
    def _():
        _q_kernel(h_ref, gq_ref, wqt_ref, qcos_ref, qsin_ref, qt_ref)


def _kvq_proj(h, g_kv, w_kv, g_q, w_q, bsz, seq, lp):
    tpb = lp // TILE
    tiles_real = seq // TILE
    kblk = TILE // LANES
    inv_freq = ROPE_THETA ** (-jnp.arange(ROT_HALF, dtype=F32) * 2.0 / (2 * ROT_HALF))
    pos = jnp.maximum(jnp.arange(lp) - PAD_ROWS, 0).astype(F32)
    ang = pos[:, None] * inv_freq[None, :]
    r = jnp.arange(LANES) % HEAD_DIM
    cos_t = jnp.where(r < 2 * ROT_HALF, jnp.cos(ang)[:, r % ROT_HALF], 1.0)
    sin_t = jnp.sin(ang)[:, r % ROT_HALF]
    sa = jnp.where(r < ROT_HALF, -sin_t, 0.0)
    sb = jnp.where((r >= ROT_HALF) & (r < 2 * ROT_HALF), sin_t, 0.0)
    tab = pl.BlockSpec((TILE, LANES), lambda i: (i % tpb, 0))
    qang = inv_freq[:, None] * (N_META + jnp.arange(seq)).astype(F32)[None, :]

    def real_tile(i):
        return jnp.maximum(i % tpb - FRONT // TILE, 0)

    qtab = pl.BlockSpec((ROT_HALF, TILE), lambda i: (0, real_tile(i)))
    weight = pl.BlockSpec((D_MODEL, D_MODEL), lambda i: (0, 0))
    gain = pl.BlockSpec((1, D_MODEL), lambda i: (0, 0))
    return pl.pallas_call(
        functools.partial(_kvq_kernel, tiles_per_batch=tpb),
        grid=(bsz * tpb,),
        in_specs=[pl.BlockSpec((TILE, D_MODEL), lambda i: (i, 0)), gain, weight, weight, tab, tab, tab,
                  gain, weight, qtab, qtab],
        out_specs=[
            pl.BlockSpec((1, TILE, D_MODEL), lambda i: (i // tpb, i % tpb, 0)),
            pl.BlockSpec((1, kblk, D_MODEL, LANES), lambda i: (i // tpb, i % tpb, 0, 0)),
            pl.BlockSpec((1, D_MODEL, TILE), lambda i: ((i // tpb) * tiles_real + real_tile(i), 0, 0)),
        ],
        out_shape=[
            jax.ShapeDtypeStruct((bsz, lp, D_MODEL), BF16),
            jax.ShapeDtypeStruct((bsz, lp // LANES, D_MODEL, LANES), BF16),
            jax.ShapeDtypeStruct((bsz * tiles_real, D_MODEL, TILE), BF16),
        ],
        compiler_params=_cparams(("arbitrary",)),
        name="kvq_proj",
    )(h, g_kv.reshape(1, D_MODEL), w_kv[:, :D_MODEL].astype(BF16), w_kv[:, D_MODEL:].T.astype(BF16), cos_t, sa, sb,
      g_q.reshape(1, D_MODEL), w_q.T.astype(BF16), jnp.cos(qang), jnp.sin(qang))


STEP_FULL = 0
STEP_EDGE = (1, 2)


def _attn_schedule(nq):
    per_k = ATT_TK // ATT_TQ
    steps = [(i, t, STEP_FULL if t < i // per_k else STEP_EDGE[i % per_k]) for i in range(nq) for t in range(i // per_k + 1)]
    return [list(col) for col in zip(*steps)]


def _attn_kernel(qi_ref, kt_ref, kind_ref, qt_ref, k_ref, vt_ref, lam_ref, g_ref, o_ref, m_sc, l_sc, acc_sc, sa_sc, sb_sc,
                 *, lam_init, n_steps, block_kinds):
    tq = ATT_TQ
    zero = jnp.zeros((HEAD_DIM, tq), BF16)
    n_sub = ATT_TK // ATT_SUB

    def q_maps(s):
        qt = qt_ref[0, qi_ref[s]]
        return (jnp.concatenate([qt[0:HEAD_DIM], zero], axis=0), jnp.concatenate([zero, qt[HEAD_DIM:]], axis=0))

    def update(n, sc, vtt, fresh=None):
        m_old, l_old, acc_old = m_sc[n], l_sc[n], acc_sc[n]
        if fresh is not None:
            m_old = jnp.where(fresh, NEG, m_old)
            l_old = jnp.where(fresh, 0.0, l_old)
            acc_old = jnp.where(fresh, 0.0, acc_old)
        m_new = jnp.maximum(m_old, jnp.max(sc, axis=0, keepdims=True))
        alpha = jnp.exp2(m_old - m_new)
        p = jnp.exp2(sc - m_new)
        l_sc[n] = alpha * l_old + jnp.sum(p, axis=0, keepdims=True)
        acc_sc[n] = alpha * acc_old + jnp.dot(vtt, p.astype(BF16), preferred_element_type=F32)
        m_sc[n] = m_new

    def values(row0, nkeys):
        blk0 = row0 // LANES
        return jnp.concatenate([vt_ref[0, blk0 + c] for c in range(nkeys // LANES)], axis=1)

    def key_row(s, u):
        return pl.multiple_of(FRONT + kt_ref[s] * ATT_TK + u * ATT_SUB, LANES)

    def visible_parts(kind):
        return STEP_EDGE.index(kind) + 1 if kind in STEP_EDGE else n_sub

    def scores_into(s, buf, parts=n_sub):
        qs = q_maps(s)
        for u in range(parts):
            kt = k_ref[0, pl.ds(key_row(s, u), ATT_SUB), :]
            for n in range(2):
                buf[u * 2 + n] = jnp.dot(kt, qs[n], preferred_element_type=F32)

    def consume(s, buf, kind):
        fresh = kt_ref[s] == 0
        edge = STEP_EDGE.index(kind) if kind in STEP_EDGE else None
        for u in range(n_sub) if edge is None else range(edge + 1):
            vtt = values(key_row(s, u), ATT_SUB)
            for n in range(2):
                sc = buf[u * 2 + n]
                if u == edge:
                    r = lax.broadcasted_iota(jnp.int32, (ATT_SUB, tq), 0) // CHUNK
                    c = lax.broadcasted_iota(jnp.int32, (ATT_SUB, tq), 1) // CHUNK
                    sc = jnp.where(r <= c, sc, NEG)
                update(n, sc, vtt, fresh if u == 0 else None)

    def finish(s):
        qs = q_maps(s)
        kt = k_ref[0, FRONT - N_META:FRONT, :]
        vtt = vt_ref[0, FRONT // LANES - 1][:, LANES - N_META:]
        for n in range(2):
            update(n, jnp.dot(kt, qs[n], preferred_element_type=F32), vtt)
        lp = lam_ref[...]
        lam = (jnp.exp(jnp.sum(lp[0:1] * lp[1:2], axis=1, keepdims=True))
               - jnp.exp(jnp.sum(lp[2:3] * lp[3:4], axis=1, keepdims=True)) + lam_init)
        o = acc_sc[0] / l_sc[0] - lam * (acc_sc[1] / l_sc[1])
        o = o * lax.rsqrt(jnp.mean(o * o, axis=0, keepdims=True) + EPS) * g_ref[...] * (1.0 - lam_init)
        o_ref[pl.ds(pl.multiple_of(qi_ref[s] * tq, tq), tq), :] = o.T.astype(BF16)

    def block(p, kind_a, kind_b):
        s_a, s_b = 2 * p, 2 * p + 1
        scores_into(s_b, sb_sc, visible_parts(kind_b))
        consume(s_a, sa_sc, kind_a)
        if kind_a in STEP_EDGE:
            finish(s_a)
        scores_into(jnp.minimum(s_b + 1, n_steps - 1), sa_sc)
        consume(s_b, sb_sc, kind_b)
        if kind_b in STEP_EDGE:
            finish(s_b)

    scores_into(0, sa_sc)

    def body(p, c):
        kind_a, kind_b = kind_ref[2 * p], kind_ref[2 * p + 1]
        for va, vb in block_kinds:
            pl.when(jnp.logical_and(kind_a == va, kind_b == vb))(functools.partial(block, p, va, vb))
        return c

    lax.fori_loop(0, n_steps // 2, body, 0)


def _attention(qt, k, vt4, lam_p, subln_g, bsz, seq, lam_init):
    nq = seq // ATT_TQ
    lp = k.shape[1]
    q_idx, k_idx, kinds = _attn_schedule(nq)
    n_steps = len(kinds)
    assert n_steps % 2 == 0
    block_kinds = sorted(set(zip(kinds[0::2], kinds[1::2])))
    grid_spec = pltpu.PrefetchScalarGridSpec(
        num_scalar_prefetch=3,
        grid=(bsz, N_HEADS),
        in_specs=[
            pl.BlockSpec((1, nq, 2 * HEAD_DIM, ATT_TQ), lambda b, h, *_: (b, 0, h, 0)),
            pl.BlockSpec((1, lp, 2 * HEAD_DIM), lambda b, h, *_: (b, 0, h)),
            pl.BlockSpec((1, lp // LANES, 2 * HEAD_DIM, LANES), lambda b, h, *_: (b, 0, h, 0)),
            pl.BlockSpec((4, HEAD_DIM), lambda b, h, *_: (0, 0)),
            pl.BlockSpec((2 * HEAD_DIM, 1), lambda b, h, *_: (0, 0)),
        ],
        out_specs=pl.BlockSpec((seq, 2 * HEAD_DIM), lambda b, h, *_: (b, h)),
        scratch_shapes=[pltpu.VMEM((2, 1, ATT_TQ), F32), pltpu.VMEM((2, 1, ATT_TQ), F32),
                        pltpu.VMEM((2, 2 * HEAD_DIM, ATT_TQ), F32),
                        pltpu.VMEM((2 * ATT_TK // ATT_SUB, ATT_SUB, ATT_TQ), F32),
                        pltpu.VMEM((2 * ATT_TK // ATT_SUB, ATT_SUB, ATT_TQ), F32)],
    )
    return pl.pallas_call(
        functools.partial(_attn_kernel, lam_init=lam_init, n_steps=n_steps, block_kinds=block_kinds),
        grid_spec=grid_spec,
        out_shape=jax.ShapeDtypeStruct((bsz * seq, D_MODEL), BF16),
        compiler_params=_cparams(("arbitrary", "arbitrary")),
        name="diff_attention",
    )(*(jnp.asarray(col, jnp.int32) for col in (q_idx, k_idx, kinds)),
      qt.reshape(bsz, nq, D_MODEL, ATT_TQ), k, vt4, lam_p, subln_g.reshape(2 * HEAD_DIM, 1))


def _oproj_kernel(h_ref, o_ref, w_ref, rg_ref, wrt_ref, br_ref, tri_ref, out_ref, lpos_ref, gate_ref, meta_ref, tot_ref,
                  seg_sc):
    _route_start(seg_sc)
    h = h_ref[...] + jnp.dot(o_ref[...], w_ref[...], preferred_element_type=F32)
    out_ref[...] = h
    _route(h, rg_ref, wrt_ref, br_ref, tri_ref, lpos_ref, gate_ref, meta_ref, tot_ref, seg_sc)


def _o_proj(h_pad, o, w_out, router, bsz, seq, lp):
    tiles_real = seq // TILE
    r_args, r_specs = _router_operands(*router, TILE)
    ro_specs, ro_shapes, r_scratch = _router_outputs(bsz * seq, TILE)
    return pl.pallas_call(
        _oproj_kernel,
        grid=(bsz * tiles_real,),
        in_specs=[
            pl.BlockSpec((TILE, D_MODEL), _real_row_map(tiles_real, lp // TILE)),
            pl.BlockSpec((TILE, D_MODEL), lambda i: (i, 0)),
            pl.BlockSpec((D_MODEL, D_MODEL), lambda i: (0, 0)),
        ] + r_specs,
        out_specs=[pl.BlockSpec((TILE, D_MODEL), lambda i: (i, 0))] + ro_specs,
        out_shape=[jax.ShapeDtypeStruct((bsz * seq, D_MODEL), F32)] + ro_shapes,
        scratch_shapes=[r_scratch],
        compiler_params=_cparams(("arbitrary",)),
        name="o_proj",
    )(h_pad, o, w_out.astype(BF16), *r_args)


def kernel(x, meta_tokens, a_norm, a_w_in, a_conv, a_w_out, kv_norm, w_kv, b_norm, b_w_q, b_lambda, b_subln, b_w_out, ffn_norm, r_group, r_group_b, r_expert, r_expert_b, e_gate, e_up, e_down, final_norm):
    bsz, seq, d = x.shape
    assert d == D_MODEL and a_norm.shape[0] == 1 and b_norm.shape[0] == 1
    lp = FRONT + seq
    assert seq % ATT_TK == 0 and ATT_TK == 2 * ATT_TQ and ATT_TQ == TILE and ATT_TK % ATT_SUB == 0
    front = jnp.concatenate([jnp.zeros((PAD_ROWS, d), x.dtype), meta_tokens.astype(x.dtype)], axis=0)
    routers = [(ffn_norm[j], r_group[j], r_group_b[j], r_expert[j], r_expert_b[j]) for j in range(2)]
    h, *routing = _mixer(x.reshape(bsz * seq, d), front, a_norm[0], a_w_in[0], a_conv[0], a_w_out[0], routers[0],
                         bsz, lp // TILE)
    h = _moe(h, routing, ffn_norm[0], e_gate, e_up, e_down, 0, final_norm, TILE, False)

    k, vt4, qt = _kvq_proj(h, kv_norm, w_kv, b_norm[0], b_w_q[0], bsz, seq, lp)
    lam_init = 0.8 - 0.6 * math.exp(-0.3 * a_norm.shape[0])
    o = _attention(qt, k, vt4, b_lambda[0], b_subln[0], bsz, seq, lam_init)
    h, *routing = _o_proj(h, o, b_w_out[0], routers[1], bsz, seq, lp)
    h = _moe(h, routing, ffn_norm[1], e_gate, e_up, e_down, 1, final_norm, TILE, True)
    return h.reshape(bsz, seq, d)
```
